```python
import jax
import jax.numpy as jnp
from jax import lax
import numpy as np


D_MODEL = 1024
BATCH = 4
SEQ = 4096
DEPTH = 4

GRID_W = 64
CTX_LEN = 256
EPS = 1e-6
HG_HEADS = 4
HG_DK = 128
HG_DV = 128
HG_WIDTH = HG_HEADS * HG_DV
HG_CHUNK = 64
MLA_HEADS = 8
MLA_NOPE = 64
MLA_ROPE = 32
MLA_V = 64
MLA_Q_RANK = 384
MLA_KV_RANK = 256
MLA_WIDTH = MLA_HEADS * MLA_V
MLA_SCALE = (MLA_NOPE + MLA_ROPE) ** -0.5
Q_BLOCK = 128
ROPE_THETA = 10000.0
N_EXPERTS = 16
N_GROUPS = 4
EXPERTS_PER_GROUP = N_EXPERTS // N_GROUPS
TOP_K = 2
D_EXPERT = 512
MOE_BLOCK = 128
IN_SPLITS = (HG_HEADS * HG_DK, HG_WIDTH, HG_HEADS * HG_DK, HG_HEADS * HG_DK, HG_WIDTH,
             MLA_Q_RANK, MLA_KV_RANK + MLA_ROPE, D_MODEL, D_MODEL)
IN_DIM = sum(IN_SPLITS)

kernel_name = 'hybrid_hgrn2_mla_grouped_moe_dit'


def rmsnorm(x, g):
    xf = x.astype(jnp.float32)
    y = xf * lax.rsqrt(jnp.mean(xf * xf, axis=-1, keepdims=True) + EPS)
    return (y * g.astype(jnp.float32)).astype(x.dtype)


def modulate(h, shift, scale):
    return h * (1 + scale) + shift


def split_in(p):
    offs = [int(o) for o in np.cumsum(IN_SPLITS)[:-1]]
    return jnp.split(p, offs, axis=-1)


def to_heads(t, n_heads):
    b_, l_, w = t.shape
    return t.reshape(b_, l_, n_heads, w // n_heads).transpose(0, 2, 1, 3)


def from_heads(t):
    b_, h_, l_, d_ = t.shape
    return t.transpose(0, 2, 1, 3).reshape(b_, l_, h_ * d_)


def axial_angles(n_tok):
    rows = n_tok // GRID_W
    row = jnp.broadcast_to(jnp.arange(rows, dtype=jnp.float32)[:, None], (rows, GRID_W)).reshape(-1)
    col = jnp.broadcast_to(jnp.arange(GRID_W, dtype=jnp.float32)[None, :], (rows, GRID_W)).reshape(-1)
    axis_dim = MLA_ROPE // 2
    inv_freq = ROPE_THETA ** (-jnp.arange(0, axis_dim, 2, dtype=jnp.float32) / axis_dim)
    return row[:, None] * inv_freq, col[:, None] * inv_freq


def rotate(x, ang):
    m = ang.shape[-1]
    cos, sin = jnp.cos(ang), jnp.sin(ang)
    x1, x2 = x[..., :m], x[..., m:]
    return jnp.concatenate([x1 * cos - x2 * sin, x1 * sin + x2 * cos], axis=-1)


def rope_2d(x, ang):
    ang_r, ang_c = ang
    a = MLA_ROPE // 2
    xf = x.astype(jnp.float32)
    return jnp.concatenate([rotate(xf[..., :a], ang_r), rotate(xf[..., a:], ang_c)], axis=-1).astype(x.dtype)


def lower_bound(gamma, layer):
    p = jnp.cumsum(jax.nn.softmax(gamma.astype(jnp.float32), axis=0), axis=0)
    return (p[layer] - p[0]).reshape(HG_HEADS, 1, HG_DK)


def forget_gate(f_pre, lb):
    f = lb + (1.0 - lb) * jax.nn.sigmoid(f_pre)
    return jnp.log(f), 1.0 - f


def gla_chunk_scan(q, k, v, logf, s0):
    b_, h_, L, _ = q.shape
    dv = v.shape[-1]
    nc = L // HG_CHUNK

    def to_chunks(t):
        return jnp.moveaxis(t.reshape(b_, h_, nc, HG_CHUNK, t.shape[-1]), 2, 0)

    mask = jnp.tril(jnp.ones((HG_CHUNK, HG_CHUNK), dtype=bool))

    def step(state, inp):
        qc, kc, vc, lc = inp
        bcum = jnp.cumsum(lc, axis=2)
        o_inter = jnp.einsum('bhtk,bhkv->bhtv', qc * jnp.exp(bcum), state)
        diff = bcum[:, :, :, None, :] - bcum[:, :, None, :, :]
        decay = jnp.exp(jnp.where(mask[:, :, None], diff, -jnp.inf))
        att = jnp.einsum('bhtk,bhtsk,bhsk->bhts', qc, decay, kc)
        o = o_inter + jnp.einsum('bhts,bhsv->bhtv', att, vc)
        blast = bcum[:, :, -1:, :]
        new_state = (jnp.exp(bcum[:, :, -1, :])[..., None] * state
                     + jnp.einsum('bhsk,bhsv->bhkv', kc * jnp.exp(blast - bcum), vc))
        return new_state, o

    s_final, o = lax.scan(step, s0, (to_chunks(q), to_chunks(k), to_chunks(v), to_chunks(logf)))
    return jnp.moveaxis(o, 0, 2).reshape(b_, h_, L, dv), s_final


def hgrn2_direction(q_c, v_c, f_c, q_x, v_x, f_x, lb, reverse):
    if reverse:
        q_c, v_c, f_c, q_x, v_x, f_x = [jnp.flip(t, axis=2) for t in (q_c, v_c, f_c, q_x, v_x, f_x)]
    logf_c, k_c = forget_gate(f_c, lb)
    logf_x, k_x = forget_gate(f_x, lb)
    s0 = jnp.zeros((q_c.shape[0], HG_HEADS, HG_DK, HG_DV), jnp.float32)
    o_c, s_c = gla_chunk_scan(q_c, k_c, v_c, logf_c, s0)
    o_x, _ = gla_chunk_scan(q_x, k_x, v_x, logf_x, s_c)
    if reverse:
        o_c, o_x = jnp.flip(o_c, axis=2), jnp.flip(o_x, axis=2)
    return o_c, o_x


def hgrn2_readout(o, g, gain):
    b_, h_, L, dv = o.shape
    o = o.transpose(0, 2, 1, 3)
    o = o * lax.rsqrt(jnp.mean(o * o, axis=-1, keepdims=True) + EPS) * gain.astype(jnp.float32).reshape(h_, dv)
    return (o.reshape(b_, L, h_ * dv) * jax.nn.silu(g.astype(jnp.float32))).astype(g.dtype)


def mla_qkv(dq, dkv, q_norm_g, kv_norm_g, w_uq, w_ukv, ang):
    b_, L, _ = dq.shape
    q = (rmsnorm(dq, q_norm_g) @ w_uq).reshape(b_, L, MLA_HEADS, MLA_NOPE + MLA_ROPE).transpose(0, 2, 1, 3)
    kv = (rmsnorm(dkv[..., :MLA_KV_RANK], kv_norm_g) @ w_ukv).reshape(b_, L, MLA_HEADS, MLA_NOPE + MLA_V).transpose(0, 2, 1, 3)
    q_nope, q_pe = q[..., :MLA_NOPE], q[..., MLA_NOPE:]
    k_nope, v = kv[..., :MLA_NOPE], kv[..., MLA_NOPE:]
    k_pe = dkv[..., MLA_KV_RANK:][:, None]
    if ang is not None:
        q_pe = rope_2d(q_pe, ang)
        k_pe = rope_2d(k_pe, ang)
    q = jnp.concatenate([q_nope, q_pe], axis=-1)
    k = jnp.concatenate([k_nope, jnp.broadcast_to(k_pe, (b_, MLA_HEADS, L, MLA_ROPE))], axis=-1)
    return q, k, v


def softmax_attend(q, k, v):
    s = jnp.einsum('bhqe,bhke->bhqk', q, k, preferred_element_type=jnp.float32) * MLA_SCALE
    p = jax.nn.softmax(s, axis=-1)
    return jnp.einsum('bhqk,bhkv->bhqv', p.astype(v.dtype), v)


def blocked_attend(q, k, v):
    b_, h_, L, e = q.shape
    nb = L // Q_BLOCK
    qb = jnp.moveaxis(q.reshape(b_, h_, nb, Q_BLOCK, e), 2, 0)
    o = lax.map(lambda qq: softmax_attend(qq, k, v), qb)
    return jnp.moveaxis(o, 0, 2).reshape(b_, h_, L, v.shape[-1])


def merge_branches(a, m, gate_a, gate_b, w_pa, w_pb, w_o):
    mixed = jax.nn.sigmoid(gate_a) * (a @ w_pa) + jax.nn.sigmoid(gate_b) * (m @ w_pb)
    return mixed @ w_o


def token_mixer(hc, hx, ang, lb_f, lb_b, w_in, hg_norm_g, q_norm_g, kv_norm_g, w_uq, w_ukv, w_pa, w_pb, w_o, need_ctx):
    qc, ic, ffc, fbc, gc, dqc, dkvc, gac, gbc = split_in(hc @ w_in)
    qx, ix, ffx, fbx, gx, dqx, dkvx, gax, gbx = split_in(hx @ w_in)
    hg = lambda t: to_heads(t, HG_HEADS).astype(jnp.float32)
    qc4, ic4, qx4, ix4 = hg(qc), hg(ic), hg(qx), hg(ix)
    ocf, oxf = hgrn2_direction(qc4, ic4, hg(ffc), qx4, ix4, hg(ffx), lb_f, False)
    ocb, oxb = hgrn2_direction(qc4, ic4, hg(fbc), qx4, ix4, hg(fbx), lb_b, True)
    a_x = hgrn2_readout(oxf + oxb, gx, hg_norm_g)
    q_c, k_c, v_c = mla_qkv(dqc, dkvc, q_norm_g, kv_norm_g, w_uq, w_ukv, None)
    q_x, k_x, v_x = mla_qkv(dqx, dkvx, q_norm_g, kv_norm_g, w_uq, w_ukv, ang)
    m_x = from_heads(blocked_attend(q_x, jnp.concatenate([k_c, k_x], axis=2), jnp.concatenate([v_c, v_x], axis=2)))
    y_x = merge_branches(a_x, m_x, gax, gbx, w_pa, w_pb, w_o)
    if not need_ctx:
        return None, y_x
    a_c = hgrn2_readout(ocf + ocb, gc, hg_norm_g)
    m_c = from_heads(softmax_attend(q_c, k_c, v_c))
    y_c = merge_branches(a_c, m_c, gac, gbc, w_pa, w_pb, w_o)
    return y_c, y_x


def moe_ffn(h, w_router, router_bias, w_gate, w_up, w_down):
    n, d = h.shape
    scores = jax.nn.sigmoid(jnp.dot(h, w_router, preferred_element_type=jnp.float32))
    sel = (scores + router_bias.astype(jnp.float32)).reshape(n, N_GROUPS, EXPERTS_PER_GROUP)
    group_score = lax.top_k(sel, TOP_K)[0].sum(-1)
    g_idx = jnp.argmax(group_score, axis=-1).astype(jnp.int32)
    in_group = jnp.take_along_axis(sel, g_idx[:, None, None], axis=1)[:, 0]
    _, local = lax.top_k(in_group, TOP_K)
    experts = g_idx[:, None] * EXPERTS_PER_GROUP + local
    wts = jnp.take_along_axis(scores, experts, axis=1)
    wts = wts / jnp.sum(wts, axis=-1, keepdims=True)
    n_assign = n * TOP_K
    e_flat = experts.reshape(-1)
    tok = jnp.repeat(jnp.arange(n, dtype=jnp.int32), TOP_K)
    order = jnp.argsort(e_flat)
    e_s, tok_s, w_s = e_flat[order], tok[order], wts.reshape(-1)[order]
    counts = jnp.zeros((N_EXPERTS,), jnp.int32).at[e_flat].add(1)
    starts = jnp.cumsum(counts) - counts
    padded = (counts + MOE_BLOCK - 1) // MOE_BLOCK * MOE_BLOCK
    pad_ends = jnp.cumsum(padded)
    pad_starts = pad_ends - padded
    dest = pad_starts[e_s] + jnp.arange(n_assign, dtype=jnp.int32) - starts[e_s]
    n_blocks = -(-n_assign // MOE_BLOCK) + N_EXPERTS
    buf = jnp.zeros((n_blocks * MOE_BLOCK, d), h.dtype).at[dest].set(h[tok_s])
    block_expert = jnp.minimum(
        jnp.searchsorted(pad_ends, jnp.arange(n_blocks, dtype=jnp.int32) * MOE_BLOCK, side='right'),
        N_EXPERTS - 1)

    def expert_block(args):
        xb, e = args
        return (jax.nn.silu(xb @ w_gate[e]) * (xb @ w_up[e])) @ w_down[e]

    yb = lax.map(expert_block, (buf.reshape(n_blocks, MOE_BLOCK, d), block_expert))
    y = yb.reshape(-1, d)[dest] * w_s[:, None].astype(h.dtype)
    return jnp.zeros_like(h).at[tok_s].add(y)


def setup_inputs(seed: int = 0) -> dict:
    key = jax.random.key(seed)
    ks = jax.random.split(key, 25)
    nrm = lambda k, shape, scale: jax.random.normal(k, shape, jnp.float32) * scale
    gain = lambda k, shape: 1.0 + 0.1 * jax.random.normal(k, shape, jnp.float32)
    return {
        'x': nrm(ks[0], (BATCH, SEQ, D_MODEL), 1.0),
        'c': nrm(ks[1], (BATCH, D_MODEL), 1.0),
        'ctx': nrm(ks[2], (BATCH, CTX_LEN, D_MODEL), 1.0),
        'c_ctx': nrm(ks[3], (D_MODEL,), 1.0),
        'w_mod': nrm(ks[4], (DEPTH, D_MODEL, 6 * D_MODEL), 0.5 * D_MODEL ** -0.5),
        'b_mod': nrm(ks[5], (DEPTH, 6 * D_MODEL), 0.02),
        'norm1_g': gain(ks[6], (DEPTH, D_MODEL)),
        'norm2_g': gain(ks[7], (DEPTH, D_MODEL)),
        'w_in': nrm(ks[8], (DEPTH, D_MODEL, IN_DIM), D_MODEL ** -0.5),
        'gamma_fwd': nrm(ks[9], (DEPTH, HG_HEADS * HG_DK), 1.0),
        'gamma_bwd': nrm(ks[10], (DEPTH, HG_HEADS * HG_DK), 1.0),
        'hg_norm_g': gain(ks[11], (DEPTH, HG_WIDTH)),
        'q_norm_g': gain(ks[12], (DEPTH, MLA_Q_RANK)),
        'kv_norm_g': gain(ks[13], (DEPTH, MLA_KV_RANK)),
        'w_uq': nrm(ks[14], (DEPTH, MLA_Q_RANK, MLA_HEADS * (MLA_NOPE + MLA_ROPE)), MLA_Q_RANK ** -0.5),
        'w_ukv': nrm(ks[15], (DEPTH, MLA_KV_RANK, MLA_HEADS * (MLA_NOPE + MLA_V)), MLA_KV_RANK ** -0.5),
        'w_pa': nrm(ks[16], (DEPTH, HG_WIDTH, D_MODEL), HG_WIDTH ** -0.5),
        'w_pb': nrm(ks[17], (DEPTH, MLA_WIDTH, D_MODEL), MLA_WIDTH ** -0.5),
        'w_o': nrm(ks[18], (DEPTH, D_MODEL, D_MODEL), D_MODEL ** -0.5),
        'w_router': nrm(ks[19], (D_MODEL, N_EXPERTS), D_MODEL ** -0.5),
        'router_bias': nrm(ks[20], (N_EXPERTS,), 0.01),
        'w_gate_e': nrm(ks[21], (DEPTH, N_EXPERTS, D_MODEL, D_EXPERT), D_MODEL ** -0.5),
        'w_up_e': nrm(ks[22], (DEPTH, N_EXPERTS, D_MODEL, D_EXPERT), D_MODEL ** -0.5),
        'w_down_e': nrm(ks[23], (DEPTH, N_EXPERTS, D_EXPERT, D_MODEL), D_EXPERT ** -0.5),
        'final_g': gain(ks[24], (D_MODEL,)),
    }


def reference(x, c, ctx, c_ctx, w_mod, b_mod, norm1_g, norm2_g, w_in, gamma_fwd, gamma_bwd, hg_norm_g,
              q_norm_g, kv_norm_g, w_uq, w_ukv, w_pa, w_pb, w_o, w_router, router_bias,
              w_gate_e, w_up_e, w_down_e, final_g):
    b_, n_lat, d = x.shape
    n_ctx = ctx.shape[1]
    ang = axial_angles(n_lat)
    zx, zc = x, ctx
    for l in range(DEPTH):
        last = l == DEPTH - 1
        mx = jnp.split(jax.nn.silu(c) @ w_mod[l] + b_mod[l], 6, axis=-1)
        mc = jnp.split(jax.nn.silu(c_ctx) @ w_mod[l] + b_mod[l], 6, axis=-1)
        hx = modulate(rmsnorm(zx, norm1_g[l]), mx[0][:, None], mx[1][:, None])
        hc = modulate(rmsnorm(zc, norm1_g[l]), mc[0], mc[1])
        yc, yx = token_mixer(hc, hx, ang, lower_bound(gamma_fwd, l), lower_bound(gamma_bwd, l), w_in[l],
                             hg_norm_g[l], q_norm_g[l], kv_norm_g[l], w_uq[l], w_ukv[l], w_pa[l], w_pb[l], w_o[l],
                             not last)
        zx = zx + mx[2][:, None] * yx
        hx = modulate(rmsnorm(zx, norm2_g[l]), mx[3][:, None], mx[4][:, None])
        if last:
            fx = moe_ffn(hx.reshape(-1, d), w_router, router_bias, w_gate_e[l], w_up_e[l], w_down_e[l]).reshape(b_, n_lat, d)
        else:
            zc = zc + mc[2] * yc
            hc = modulate(rmsnorm(zc, norm2_g[l]), mc[3], mc[4])
            f = moe_ffn(jnp.concatenate([hc, hx], axis=1).reshape(-1, d), w_router, router_bias,
                        w_gate_e[l], w_up_e[l], w_down_e[l]).reshape(b_, n_ctx + n_lat, d)
            zc = zc + mc[5] * f[:, :n_ctx]
            fx = f[:, n_ctx:]
        zx = zx + mx[5][:, None] * fx
    return rmsnorm(zx, final_g)
```

```python
import functools

import numpy as np
import jax
import jax.numpy as jnp
from jax import lax
from jax.experimental import pallas as pl
from jax.experimental.pallas import tpu as pltpu

F32 = jnp.float32
BF16 = jnp.bfloat16

EPS = 1e-6
GRID_W = 64
ROPE_THETA = 10000.0
HG_HEADS = 4
HG_DK = 128
HG_WIDTH = 512
MLA_HEADS = 8
MLA_NOPE = 64
MLA_ROPE = 32
MLA_V = 64
MLA_Q_RANK = 384
MLA_KV_RANK = 256
MLA_SCALE = (MLA_NOPE + MLA_ROPE) ** -0.5
N_EXPERTS = 16
N_GROUPS = 4
EPG = 4
N_CLASSES = N_GROUPS * 6

LANE = 128
SUB = 8
TM = 256
CHUNK = 64
MOE_BM = 256
VMEM_LIMIT = 56 * 1024 * 1024

C_Q, C_I, C_FF, C_FB, C_G, C_DQ, C_DKV, C_GA, C_GB, C_KPE, C_END = (
    0, 512, 1024, 1536, 2048, 2560, 2944, 3200, 4224, 5248, 5376)


def _cparams(n_axes):
    return pltpu.CompilerParams(dimension_semantics=("arbitrary",) * n_axes,
                                vmem_limit_bytes=VMEM_LIMIT)


def _rms(x, g):
    y = x * lax.rsqrt(jnp.mean(x * x, axis=-1, keepdims=True) + EPS)
    return y * g


def _dot(a, b):
    return jnp.dot(a, b, preferred_element_type=F32)


def _dot_nt(a, b):
    return lax.dot_general(a, b, (((1,), (1,)), ((), ())), preferred_element_type=F32)


def _dot_tn(a, b):
    return lax.dot_general(a, b, (((0,), (0,)), ((), ())), preferred_element_type=F32)


def _split3(x):
    hi = x.astype(BF16)
    r1 = x - hi.astype(F32)
    mid = r1.astype(BF16)
    lo = (r1 - mid.astype(F32)).astype(BF16)
    return hi, mid, lo


def _dot_f32(a, b):
    a0, a1, a2 = _split3(a)
    b0, b1, b2 = _split3(b)
    return (_dot(a0, b0) + (_dot(a0, b1) + _dot(a1, b0))
            + (_dot(a0, b2) + _dot(a1, b1) + _dot(a2, b0)))


def _mod_kernel(c_ref, w_ref, b_ref, o_ref):
    c = c_ref[...]
    s = c * jax.nn.sigmoid(c)
    o_ref[0] = _dot_f32(s, w_ref[0]) + b_ref[0]


def _modulation(cc, w_mod, b_mod):
    depth, d, n6 = w_mod.shape
    nb = 1536
    return pl.pallas_call(
        _mod_kernel,
        grid=(depth, n6 // nb),
        in_specs=[pl.BlockSpec((SUB, d), lambda l, j: (0, 0)),
                  pl.BlockSpec((1, d, nb), lambda l, j: (l, 0, j)),
                  pl.BlockSpec((1, 1, nb), lambda l, j: (l, 0, j))],
        out_specs=pl.BlockSpec((1, SUB, nb), lambda l, j: (l, 0, j)),
        out_shape=jax.ShapeDtypeStruct((depth, SUB, n6), F32),
        compiler_params=_cparams(2),
        name="modulation",
    )(cc, w_mod, b_mod.reshape(depth, 1, n6))


def _inproj_kernel(z_ref, mod_ref, g1_ref, w_ref, lbf_ref, lbb_ref, qg_ref, kvg_ref,
                   wqa_ref, wqb_ref, wka_ref, wv_ref, e2_ref, cq_ref, sq_ref, kc_ref,
                   q_o, v_o, lff_o, kf_o, lfb_o, kb_o, g_o, gas_o, gbs_o, qq_o, kk_o, vv_o):
    mod = mod_ref[0, 0]
    h = _rms(z_ref[0], g1_ref[...]) * (1.0 + mod[1:2]) + mod[0:1]
    hb = h.astype(BF16)

    def seg(a, b):
        return _dot(hb, w_ref[:, a:b])

    q_o[0] = seg(C_Q, C_I)
    v_o[0] = seg(C_I, C_FF)
    for a, lb_ref, lf_o, k_o in ((C_FF, lbf_ref, lff_o, kf_o), (C_FB, lbb_ref, lfb_o, kb_o)):
        lb = lb_ref[...]
        f = lb + (1.0 - lb) * jax.nn.sigmoid(seg(a, a + HG_WIDTH))
        lf_o[0] = jnp.log(f)
        k_o[0] = 1.0 - f
    g_o[0] = seg(C_G, C_DQ)
    gas_o[0] = jax.nn.sigmoid(seg(C_GA, C_GB))
    gbs_o[0] = jax.nn.sigmoid(seg(C_GB, C_KPE))

    qn = _rms(seg(C_DQ, C_DKV), qg_ref[...]).astype(BF16)
    qa = _dot(qn, wqa_ref[...])
    qb = _dot(qn, wqb_ref[...])
    cq = cq_ref[...]
    sq = sq_ref[...]
    for hh in range(MLA_HEADS):
        sl = slice(hh * LANE, (hh + 1) * LANE)
        qq_o[0, :, sl] = (qa[:, sl] * cq + qb[:, sl] * sq).astype(BF16)
    kvn = _rms(seg(C_DKV, C_GA), kvg_ref[...]).astype(BF16)
    kp = (seg(C_KPE, C_END) * kc_ref[...]).astype(BF16)
    kk_o[0] = (_dot(kvn, wka_ref[...]) + _dot(kp, e2_ref[...])).astype(BF16)
    vv_o[0] = _dot(kvn, wv_ref[...]).astype(BF16)


def _inproj(z, modl, g1, w, lbf, lbb, qg, kvg, wqa, wqb, wka, wv, e2, cq, sq, kc, nct):
    b_, lt, d = z.shape
    nt = lt // TM
    row = lambda w_: pl.BlockSpec((1, TM, w_), lambda b, i: (b, i, 0))
    full = lambda a: pl.BlockSpec(a.shape, lambda b, i: (0,) * a.ndim)
    tab = pl.BlockSpec((TM, LANE), lambda b, i: (i, 0))
    f32o = lambda w_: jax.ShapeDtypeStruct((b_, lt, w_), F32)
    bf16o = lambda w_: jax.ShapeDtypeStruct((b_, lt, w_), BF16)
    return pl.pallas_call(
        _inproj_kernel,
        grid=(b_, nt),
        in_specs=[row(d),
                  pl.BlockSpec((1, 1, SUB, d), lambda b, i: (b, jnp.minimum(i // nct, 1), 0, 0)),
                  full(g1), full(w), full(lbf), full(lbb), full(qg), full(kvg),
                  full(wqa), full(wqb), full(wka), full(wv), full(e2), tab, tab, tab],
        out_specs=[row(512)] * 7 + [row(1024)] * 4 + [row(512)],
        out_shape=[f32o(512)] * 7 + [f32o(1024)] * 2 + [bf16o(1024)] * 2 + [bf16o(512)],
        compiler_params=_cparams(2),
        name="inproj",
    )(z, modl, g1, w, lbf, lbb, qg, kvg, wqa, wqb, wka, wv, e2, cq, sq, kc)


def _gla_chunk(q, k, v, lf, st, tri, row_i, col_i, reverse):
    c = CHUNK
    nblk = c // SUB
    h0, h1, h2 = _split3(lf)
    b = _dot(tri, h0) + _dot(tri, h1) + _dot(tri, h2)
    btot = b[0:1] if reverse else b[c - 1:c]
    o = _dot_nt((q * jnp.exp(b)).astype(BF16), st.astype(BF16))
    att = jnp.zeros((c, c), F32)
    rl = row_i & (SUB - 1)
    for dd in range(SUB):
        if dd == 0:
            w = q * k
        else:
            sh = (c - dd) if reverse else dd
            kd = pltpu.roll(k, sh, 0)
            bd = pltpu.roll(b, sh, 0)
            w = q * kd * jnp.exp(jnp.minimum(b - bd, 0.0))
        colsum = jnp.sum(w, axis=-1, keepdims=True)
        if reverse:
            msk = (col_i == row_i + dd) & (rl <= SUB - 1 - dd)
        else:
            msk = (col_i == row_i - dd) & (rl >= dd)
        att = jnp.where(msk, colsum, att)
    rows = []
    cm = lax.broadcasted_iota(jnp.int32, (SUB, c), 1)
    for ib in range(nblk):
        lo = ib * SUB
        a_rows = att[lo:lo + SUB]
        if (not reverse and ib == 0) or (reverse and ib == nblk - 1):
            rows.append(a_rows)
            continue
        r = b[lo + SUB:lo + SUB + 1] if reverse else b[lo - 1:lo]
        qt = q[lo:lo + SUB] * jnp.exp(b[lo:lo + SUB] - r)
        kt = k * jnp.exp(jnp.minimum(r - b, 0.0))
        blk = _dot_nt(qt.astype(BF16), kt.astype(BF16))
        keep = (cm >= lo + SUB) if reverse else (cm < lo)
        rows.append(jnp.where(keep, blk, a_rows))
    att = jnp.concatenate(rows, axis=0)
    o = o + _dot(att.astype(BF16), v.astype(BF16))
    kdec = k * jnp.exp(btot - b)
    st_new = st * jnp.exp(btot) + _dot_tn(v.astype(BF16), kdec.astype(BF16))
    return o, st_new


def _hgrn_kernel(*refs, reverse, readout):
    if readout:
        q_ref, v_ref, lf_ref, k_ref, of_ref, g_ref, gain_ref, o_ref, s_ref = refs
    else:
        q_ref, v_ref, lf_ref, k_ref, o_ref, s_ref = refs

    @pl.when(pl.program_id(1) == 0)
    def _():
        s_ref[...] = jnp.zeros_like(s_ref)

    c = CHUNK
    row_i = lax.broadcasted_iota(jnp.int32, (c, c), 0)
    col_i = lax.broadcasted_iota(jnp.int32, (c, c), 1)
    tri = ((col_i >= row_i) if reverse else (col_i <= row_i)).astype(BF16)
    nch = TM // c

    def body(ci, carry):
        cidx = (nch - 1 - ci) if reverse else ci
        r0 = pl.multiple_of(cidx * c, c)
        for hh in range(HG_HEADS):
            ix = (0, pl.ds(r0, c), slice(hh * HG_DK, (hh + 1) * HG_DK))
            o, st_new = _gla_chunk(q_ref[ix], k_ref[ix], v_ref[ix], lf_ref[ix], s_ref[hh],
                                   tri, row_i, col_i, reverse)
            s_ref[hh] = st_new
            if readout:
                o = o + of_ref[ix]
                o = o * lax.rsqrt(jnp.mean(o * o, axis=-1, keepdims=True) + EPS)
                o = o * gain_ref[:, hh * HG_DK:(hh + 1) * HG_DK]
                g = g_ref[ix]
                o = o * (g * jax.nn.sigmoid(g))
            o_ref[ix] = o
        return carry

    lax.fori_loop(0, nch, body, 0)


def _hgrn(q, v, lf, k, nct, reverse, extra=None):
    b_, lt, w = q.shape
    nt = lt // TM
    if reverse:
        tile = lambda i: jnp.where(i < nct, nct - 1 - i, nt - 1 - (i - nct))
    else:
        tile = lambda i: i
    row = pl.BlockSpec((1, TM, w), lambda b, i: (b, tile(i), 0))
    ins = [q, v, lf, k]
    specs = [row] * 4
    if extra is not None:
        o_f, g, gain = extra
        ins += [o_f, g, gain]
        specs += [row, row, pl.BlockSpec(gain.shape, lambda b, i: (0, 0))]
    return pl.pallas_call(
        functools.partial(_hgrn_kernel, reverse=reverse, readout=extra is not None),
        grid=(b_, nt),
        in_specs=specs,
        out_specs=row,
        out_shape=jax.ShapeDtypeStruct((b_, lt, w), F32),
        scratch_shapes=[pltpu.VMEM((HG_HEADS, HG_DK, HG_DK), F32)],
        compiler_params=_cparams(2),
        name="hgrn_bwd" if reverse else "hgrn_fwd",
    )(*ins)


def _attn_kernel(q_ref, k_ref, v_ref, o_ref, *, n_ctx, nct):
    i = pl.program_id(2)
    lane = lax.broadcasted_iota(jnp.int32, (TM, LANE), 1)

    def run(nk):
        outs = []
        for hh in range(2):
            sl = slice(hh * LANE, (hh + 1) * LANE)
            s = _dot_nt(q_ref[0, :, sl], k_ref[0, :nk, sl])
            p = jnp.exp(s - jnp.max(s, axis=-1, keepdims=True))
            l = jnp.sum(p, axis=-1, keepdims=True)
            outs.append(_dot(p.astype(BF16), v_ref[0, :nk, :]) / l)
        o_ref[0] = jnp.where(lane < MLA_V, outs[0], outs[1])

    @pl.when(i < nct)
    def _():
        run(n_ctx)

    @pl.when(i >= nct)
    def _():
        run(k_ref.shape[1])


def _attention(qq, kk, vv, n_ctx):
    b_, lt, _ = qq.shape
    nt = lt // TM
    npair = MLA_HEADS // 2
    return pl.pallas_call(
        functools.partial(_attn_kernel, n_ctx=n_ctx, nct=n_ctx // TM),
        grid=(b_, npair, nt),
        in_specs=[pl.BlockSpec((1, TM, 2 * LANE), lambda b, j, i: (b, i, j)),
                  pl.BlockSpec((1, lt, 2 * LANE), lambda b, j, i: (b, 0, j)),
                  pl.BlockSpec((1, lt, LANE), lambda b, j, i: (b, 0, j))],
        out_specs=pl.BlockSpec((1, TM, LANE), lambda b, j, i: (b, i, j)),
        out_shape=jax.ShapeDtypeStruct((b_, lt, npair * LANE), F32),
        compiler_params=_cparams(3),
        name="mla_attention",
    )(qq, kk, vv)


def _route(sc, sel):
    r = lambda x, e: x[e:e + 1]
    best = None
    for g in range(N_GROUPS):
        u = [r(sel, EPG * g + j) for j in range(EPG)]
        gs = None
        for a in range(EPG):
            for b in range(a + 1, EPG):
                pr = u[a] + u[b]
                gs = pr if gs is None else jnp.maximum(gs, pr)
        if best is None:
            best, gi = gs, jnp.zeros_like(gs, dtype=jnp.int32)
        else:
            upd = gs > best
            best = jnp.where(upd, gs, best)
            gi = jnp.where(upd, g, gi)

    def pick(x, j):
        out = r(x, j)
        for g in range(1, N_GROUPS):
            out = jnp.where(gi == g, r(x, EPG * g + j), out)
        return out

    u = [pick(sel, j) for j in range(EPG)]
    s = [pick(sc, j) for j in range(EPG)]

    def argmax4(vals):
        bv, bi = vals[0], jnp.zeros_like(gi)
        for j in range(1, EPG):
            upd = vals[j] > bv
            bv = jnp.where(upd, vals[j], bv)
            bi = jnp.where(upd, j, bi)
        return bi

    l1 = argmax4(u)
    l2 = argmax4([jnp.where(l1 == j, -jnp.inf, u[j]) for j in range(EPG)])
    lo = jnp.minimum(l1, l2)
    hi = jnp.maximum(l1, l2)
    s_lo = sum(jnp.where(lo == j, s[j], 0.0) for j in range(EPG))
    s_hi = sum(jnp.where(hi == j, s[j], 0.0) for j in range(EPG))
    den = s_lo + s_hi
    pair = jnp.where(lo == 0, hi - 1, jnp.where(lo == 1, hi + 1, 5))
    cls = gi * 6 + pair
    z = jnp.zeros_like(den)
    rows = jnp.concatenate([cls.astype(F32), s_lo / den, s_hi / den, z, z, z, z, z], axis=0)
    return rows, cls


def _merge_kernel(a_ref, m_ref, gas_ref, gbs_ref, z_ref, mod_ref, wpa_ref, wpb_ref, wo_ref,
                  g2_ref, wr_ref, rb_ref, zn_o, hw_o, cls_o):
    mod = mod_ref[0, 0]
    d = z_ref.shape[-1]
    pa = _dot(a_ref[0].astype(BF16), wpa_ref[...])
    pb = _dot(m_ref[0].astype(BF16), wpb_ref[...])
    mixed = gas_ref[0] * pa + gbs_ref[0] * pb
    zn = z_ref[0] + mod[2:3] * _dot(mixed.astype(BF16), wo_ref[...])
    zn_o[0] = zn
    h2 = _rms(zn, g2_ref[...]) * (1.0 + mod[4:5]) + mod[3:4]
    hw_o[0, :, :d] = h2
    logits = _dot_f32(h2, wr_ref[...])
    sc = jax.nn.sigmoid(logits.T[:N_EXPERTS])
    rows, cls = _route(sc, sc + rb_ref[...])
    pad = jnp.zeros((LANE - SUB, rows.shape[1]), F32)
    hw_o[0, :, d:] = jnp.concatenate([rows, pad], axis=0).T
    cls_o[0] = cls


def _merge(a, m, gas, gbs, z, modl, wpa, wpb, wo, g2, wr, rbb, nct):
    b_, lt, d = z.shape
    nt = lt // TM
    row = lambda w_: pl.BlockSpec((1, TM, w_), lambda b, i: (b, i, 0))
    full = lambda x: pl.BlockSpec(x.shape, lambda b, i: (0,) * x.ndim)
    return pl.pallas_call(
        _merge_kernel,
        grid=(b_, nt),
        in_specs=[row(512), row(512), row(d), row(d), row(d),
                  pl.BlockSpec((1, 1, SUB, d), lambda b, i: (b, jnp.minimum(i // nct, 1), 0, 0)),
                  full(wpa), full(wpb), full(wo), full(g2), full(wr), full(rbb)],
        out_specs=[row(d), row(d + LANE), pl.BlockSpec((1, 1, TM), lambda b, i: (b, 0, i))],
        out_shape=[jax.ShapeDtypeStruct((b_, lt, d), F32),
                   jax.ShapeDtypeStruct((b_, lt, d + LANE), F32),
                   jax.ShapeDtypeStruct((b_, 1, lt), jnp.int32)],
        compiler_params=_cparams(2),
        name="merge_route",
    )(a, m, gas, gbs, z, modl, wpa, wpb, wo, g2, wr, rbb)


def _row_copy(src_hbm, idx_ref, buf, sem, slot, r):
    return pltpu.make_async_copy(src_hbm.at[pl.ds(idx_ref[0, 0, r], 1)],
                                 buf.at[slot, pl.ds(r, 1)], sem.at[slot])


def _gather_start(src_hbm, idx_ref, buf, sem, slot, n):
    def body(r, c):
        _row_copy(src_hbm, idx_ref, buf, sem, slot, r).start()
        return c
    lax.fori_loop(0, n, body, 0)


def _gather_wait(src_hbm, idx_ref, buf, sem, slot, n):
    def body(r, c):
        _row_copy(src_hbm, idx_ref, buf, sem, slot, r).wait()
        return c
    lax.fori_loop(0, n, body, 0)


def _moe_kernel(ea_ref, eb_ref, nu_ref, idx_ref, idxn_ref, hw_hbm,
                wga_ref, wua_ref, wda_ref, wgb_ref, wub_ref, wdb_ref, y_ref, buf, sem):
    j = pl.program_id(0)
    nblk = pl.num_programs(0)
    slot = j % 2
    d = y_ref.shape[-1]

    @pl.when(j == 0)
    def _():
        _gather_start(hw_hbm, idx_ref, buf, sem, 0, MOE_BM)

    @pl.when(j + 1 < nblk)
    def _():
        _gather_start(hw_hbm, idxn_ref, buf, sem, 1 - slot, MOE_BM)

    _gather_wait(hw_hbm, idx_ref, buf, sem, slot, MOE_BM)

    @pl.when(j < nu_ref[0])
    def _():
        xb = buf[slot]
        x = xb[:, :d].astype(BF16)

        def ffn(wg, wu, wd):
            act = jax.nn.silu(_dot(x, wg[0])) * _dot(x, wu[0])
            return _dot(act.astype(BF16), wd[0])

        y_ref[...] = (xb[:, d + 1:d + 2] * ffn(wga_ref, wua_ref, wda_ref)
                      + xb[:, d + 2:d + 3] * ffn(wgb_ref, wub_ref, wdb_ref))

    @pl.when(j >= nu_ref[0])
    def _():
        y_ref[...] = jnp.zeros_like(y_ref)


def _moe(hw, src, blk_ea, blk_eb, n_used, wg, wu, wd):
    n, wdt = hw.shape
    d = wdt - LANE
    nblk = src.shape[0]
    de = wg.shape[-1]
    idx_spec = lambda f: pl.BlockSpec((1, 1, MOE_BM), f, memory_space=pltpu.SMEM)
    wspec = lambda shp, which: pl.BlockSpec(
        (1,) + shp, (lambda j, ea, eb, nu: (ea[j], 0, 0)) if which == 0 else (lambda j, ea, eb, nu: (eb[j], 0, 0)))
    return pl.pallas_call(
        _moe_kernel,
        grid_spec=pltpu.PrefetchScalarGridSpec(
            num_scalar_prefetch=3,
            grid=(nblk,),
            in_specs=[idx_spec(lambda j, ea, eb, nu: (j, 0, 0)),
                      idx_spec(lambda j, ea, eb, nu: (jnp.minimum(j + 1, nblk - 1), 0, 0)),
                      pl.BlockSpec(memory_space=pl.ANY),
                      wspec((d, de), 0), wspec((d, de), 0), wspec((de, d), 0),
                      wspec((d, de), 1), wspec((d, de), 1), wspec((de, d), 1)],
            out_specs=pl.BlockSpec((MOE_BM, d), lambda j, ea, eb, nu: (j, 0)),
            scratch_shapes=[pltpu.VMEM((2, MOE_BM, wdt), F32), pltpu.SemaphoreType.DMA((2,))]),
        out_shape=jax.ShapeDtypeStruct((nblk * MOE_BM, d), F32),
        compiler_params=_cparams(1),
        name="moe_ffn",
    )(blk_ea, blk_eb, n_used, src, src, hw, wg, wu, wd, wg, wu, wd)


def _post_kernel(*refs, final):
    if final:
        idx_ref, idxn_ref, y_hbm, zn_ref, mod_ref, fg_ref, o_ref, buf, sem = refs
    else:
        idx_ref, idxn_ref, y_hbm, zn_ref, mod_ref, o_ref, buf, sem = refs
    t = pl.program_id(0)
    nstep = pl.num_programs(0)
    slot = t % 2

    @pl.when(t == 0)
    def _():
        _gather_start(y_hbm, idx_ref, buf, sem, 0, TM)

    @pl.when(t + 1 < nstep)
    def _():
        _gather_start(y_hbm, idxn_ref, buf, sem, 1 - slot, TM)

    _gather_wait(y_hbm, idx_ref, buf, sem, slot, TM)
    z = zn_ref[0] + mod_ref[0, 0][5:6] * buf[slot]
    if final:
        z = _rms(z, fg_ref[...])
    o_ref[0] = z


def _post(y, pos, zn, modl, nct, final_g=None):
    b_, lt, d = zn.shape
    nt = lt // TM
    final = final_g is not None
    t0 = nct if final else 0
    ntl = nt - t0
    nstep = b_ * ntl
    idx_spec = lambda f: pl.BlockSpec((1, 1, TM), f, memory_space=pltpu.SMEM)
    ins = [pos, pos, y, zn, modl]
    specs = [idx_spec(lambda t: (t, 0, 0)),
             idx_spec(lambda t: (jnp.minimum(t + 1, nstep - 1), 0, 0)),
             pl.BlockSpec(memory_space=pl.ANY),
             pl.BlockSpec((1, TM, d), lambda t: (t // ntl, t % ntl + t0, 0)),
             pl.BlockSpec((1, 1, SUB, d), lambda t: (t // ntl, jnp.minimum((t % ntl + t0) // nct, 1), 0, 0))]
    if final:
        ins.append(final_g)
        specs.append(pl.BlockSpec(final_g.shape, lambda t: (0, 0)))
    return pl.pallas_call(
        functools.partial(_post_kernel, final=final),
        grid=(nstep,),
        in_specs=specs,
        out_specs=pl.BlockSpec((1, TM, d), lambda t: (t // ntl, t % ntl, 0)),
        out_shape=jax.ShapeDtypeStruct((b_, ntl * TM, d), F32),
        scratch_shapes=[pltpu.VMEM((2, TM, d), F32), pltpu.SemaphoreType.DMA((2,))],
        compiler_params=_cparams(1),
        name="unsort_residual",
    )(*ins)


_PAIRS = [(0, 1), (0, 2), (0, 3), (1, 2), (1, 3), (2, 3)]
_CLS_EA = np.array([EPG * g + p[0] for g in range(N_GROUPS) for p in _PAIRS], np.int32)
_CLS_EB = np.array([EPG * g + p[1] for g in range(N_GROUPS) for p in _PAIRS], np.int32)


def _sort_plan(cls):
    n = cls.shape[0]
    nblk = n // MOE_BM + N_CLASSES
    onehot = (cls[:, None] == jnp.arange(N_CLASSES, dtype=jnp.int32)[None, :]).astype(jnp.int32)
    csum = jnp.cumsum(onehot, axis=0)
    counts = csum[-1]
    rank = jnp.take_along_axis(csum, cls[:, None], axis=1)[:, 0] - 1
    padded = (counts + MOE_BM - 1) // MOE_BM * MOE_BM
    pad_end = jnp.cumsum(padded)
    pos = (pad_end - padded)[cls] + rank
    src = jnp.zeros((nblk * MOE_BM,), jnp.int32).at[pos].set(jnp.arange(n, dtype=jnp.int32))
    blk_cls = jnp.minimum(
        jnp.searchsorted(pad_end, jnp.arange(nblk, dtype=jnp.int32) * MOE_BM, side='right'),
        N_CLASSES - 1).astype(jnp.int32)
    n_used = (pad_end[-1] // MOE_BM).astype(jnp.int32).reshape(1)
    return pos, src.reshape(nblk, 1, MOE_BM), jnp.asarray(_CLS_EA)[blk_cls], jnp.asarray(_CLS_EB)[blk_cls], n_used


def _rot_cols(w):
    return jnp.concatenate([-w[:, 8:16], w[:, 0:8], -w[:, 24:32], w[:, 16:24]], axis=1)


def _prep_w_in(w):
    kpe = w[:, 3200:3232]
    pad = jnp.zeros((w.shape[0], C_END - C_KPE - 2 * MLA_ROPE), w.dtype)
    return jnp.concatenate([w[:, :3200], w[:, 3232:], kpe, _rot_cols(kpe), pad], axis=1).astype(BF16)


def _prep_mla(w_uq, w_ukv):
    qh = w_uq.reshape(MLA_Q_RANK, MLA_HEADS, MLA_NOPE + MLA_ROPE)
    zq = jnp.zeros((MLA_Q_RANK, MLA_HEADS, LANE - MLA_NOPE - MLA_ROPE), w_uq.dtype)
    wqa = jnp.concatenate([qh, zq], axis=2).reshape(MLA_Q_RANK, MLA_HEADS * LANE)
    pe = qh[:, :, MLA_NOPE:].reshape(MLA_Q_RANK * MLA_HEADS, MLA_ROPE)
    pe_sw = _rot_cols(pe).reshape(MLA_Q_RANK, MLA_HEADS, MLA_ROPE)
    wqb = jnp.concatenate([jnp.zeros((MLA_Q_RANK, MLA_HEADS, MLA_NOPE), w_uq.dtype), pe_sw, zq],
                          axis=2).reshape(MLA_Q_RANK, MLA_HEADS * LANE)
    kvh = w_ukv.reshape(MLA_KV_RANK, MLA_HEADS, MLA_NOPE + MLA_V)
    wka = jnp.concatenate([kvh[:, :, :MLA_NOPE], jnp.zeros((MLA_KV_RANK, MLA_HEADS, LANE - MLA_NOPE), w_ukv.dtype)],
                          axis=2).reshape(MLA_KV_RANK, MLA_HEADS * LANE)
    wv = kvh[:, :, MLA_NOPE:].reshape(MLA_KV_RANK, MLA_HEADS * MLA_V)
    return wqa.astype(BF16), wqb.astype(BF16), wka.astype(BF16), wv.astype(BF16)


def _rope_tables(n_ctx, n_lat):
    rows = n_lat // GRID_W
    rowp = jnp.broadcast_to(jnp.arange(rows, dtype=F32)[:, None], (rows, GRID_W)).reshape(-1)
    colp = jnp.broadcast_to(jnp.arange(GRID_W, dtype=F32)[None, :], (rows, GRID_W)).reshape(-1)
    axis_dim = MLA_ROPE // 2
    inv_freq = ROPE_THETA ** (-jnp.arange(0, axis_dim, 2, dtype=F32) / axis_dim)
    ar, ac = rowp[:, None] * inv_freq, colp[:, None] * inv_freq
    cos32 = jnp.concatenate([jnp.cos(ar), jnp.cos(ar), jnp.cos(ac), jnp.cos(ac)], axis=1)
    sin32 = jnp.concatenate([jnp.sin(ar), jnp.sin(ar), jnp.sin(ac), jnp.sin(ac)], axis=1)
    cos32 = jnp.concatenate([jnp.ones((n_ctx, MLA_ROPE), F32), cos32], axis=0)
    sin32 = jnp.concatenate([jnp.zeros((n_ctx, MLA_ROPE), F32), sin32], axis=0)
    lt = n_ctx + n_lat
    one, zero = jnp.ones((lt, MLA_NOPE), F32), jnp.zeros((lt, MLA_NOPE), F32)
    z32 = jnp.zeros((lt, LANE - MLA_NOPE - MLA_ROPE), F32)
    cq = jnp.concatenate([one, cos32, z32], axis=1) * MLA_SCALE
    sq = jnp.concatenate([zero, sin32, z32], axis=1) * MLA_SCALE
    kc = jnp.concatenate([cos32, sin32, zero], axis=1)
    e2 = np.zeros((LANE, MLA_HEADS * LANE), np.float32)
    for hh in range(MLA_HEADS):
        for j in range(MLA_ROPE):
            e2[j, hh * LANE + MLA_NOPE + j] = 1.0
            e2[MLA_ROPE + j, hh * LANE + MLA_NOPE + j] = 1.0
    return cq, sq, kc, jnp.asarray(e2, dtype=BF16)


def _lower_bounds(gamma):
    p = jnp.cumsum(jax.nn.softmax(gamma.astype(F32), axis=0), axis=0)
    return p - p[0:1]


def kernel(x, c, ctx, c_ctx, w_mod, b_mod, norm1_g, norm2_g, w_in, gamma_fwd, gamma_bwd, hg_norm_g,
           q_norm_g, kv_norm_g, w_uq, w_ukv, w_pa, w_pb, w_o, w_router, router_bias,
           w_gate_e, w_up_e, w_down_e, final_g):
    b_, n_lat, d = x.shape
    n_ctx = ctx.shape[1]
    depth = w_mod.shape[0]
    assert n_ctx % TM == 0 and n_lat % TM == 0 and n_lat % GRID_W == 0 and b_ + 1 <= SUB
    nct = n_ctx // TM
    lt = n_ctx + n_lat
    nt = lt // TM

    cc = jnp.concatenate([c, c_ctx[None, :], jnp.zeros((SUB - b_ - 1, d), F32)], axis=0)
    mod = _modulation(cc, w_mod, b_mod).reshape(depth, SUB, 6, d)
    mod_x = mod[:, :b_]
    mod_c = jnp.broadcast_to(mod[:, b_:b_ + 1], mod_x.shape)
    mod_t = jnp.stack([mod_c, mod_x], axis=2)
    mod_t = jnp.concatenate([mod_t, jnp.zeros((depth, b_, 2, SUB - 6, d), F32)], axis=3)

    cq, sq, kc, e2 = _rope_tables(n_ctx, n_lat)
    lbf, lbb = _lower_bounds(gamma_fwd), _lower_bounds(gamma_bwd)
    wr = jnp.concatenate([w_router, jnp.zeros((d, LANE - N_EXPERTS), F32)], axis=1)
    rbb = jnp.broadcast_to(router_bias.astype(F32)[:, None], (N_EXPERTS, TM))

    z = jnp.concatenate([ctx, x], axis=1)
    out = None
    for l in range(depth):
        last = l == depth - 1
        wqa, wqb, wka, wv = _prep_mla(w_uq[l], w_ukv[l])
        (q, v, lff, kf, lfb, kb, g, gas, gbs, qq, kk, vv) = _inproj(
            z, mod_t[l], norm1_g[l][None], _prep_w_in(w_in[l]), lbf[l][None], lbb[l][None],
            q_norm_g[l][None], kv_norm_g[l][None], wqa, wqb, wka, wv, e2, cq, sq, kc, nct)
        o_f = _hgrn(q, v, lff, kf, nct, reverse=False)
        a = _hgrn(q, v, lfb, kb, nct, reverse=True, extra=(o_f, g, hg_norm_g[l][None]))
        m = _attention(qq, kk, vv, n_ctx)
        zn, hw, cls = _merge(a, m, gas, gbs, z, mod_t[l], w_pa[l].astype(BF16), w_pb[l].astype(BF16),
                             w_o[l].astype(BF16), norm2_g[l][None], wr, rbb, nct)
        pos, src, blk_ea, blk_eb, n_used = _sort_plan(cls.reshape(-1))
        y = _moe(hw.reshape(b_ * lt, d + LANE), src, blk_ea, blk_eb, n_used,
                 w_gate_e[l].astype(BF16), w_up_e[l].astype(BF16), w_down_e[l].astype(BF16))
        pos = pos.reshape(b_, nt, 1, TM)
        if last:
            out = _post(y, pos[:, nct:].reshape(-1, 1, TM), zn, mod_t[l], nct, final_g[None])
        else:
            z = _post(y, pos.reshape(-1, 1, TM), zn, mod_t[l], nct)
    return out
```

```python
import functools

import numpy as np
import jax
import jax.numpy as jnp
from jax import lax
from jax.experimental import pallas as pl
from jax.experimental.pallas import tpu as pltpu

F32 = jnp.float32
BF16 = jnp.bfloat16

EPS = 1e-6
GRID_W = 64
ROPE_THETA = 10000.0
HG_HEADS = 4
HG_DK = 128
HG_WIDTH = 512
MLA_HEADS = 8
MLA_NOPE = 64
MLA_ROPE = 32
MLA_V = 64
MLA_Q_RANK = 384
MLA_KV_RANK = 256
MLA_SCALE = (MLA_NOPE + MLA_ROPE) ** -0.5
LOG2E = 1.4426950408889634
N_EXPERTS = 16
N_GROUPS = 4
EPG = 4
N_CLASSES = N_GROUPS * 6

LANE = 128
SUB = 8
TM = 256
CHUNK = 64
MOE_BM = 256
VMEM_LIMIT = 56 * 1024 * 1024

C_Q, C_I, C_FF, C_FB, C_G, C_DQ, C_DKV, C_GA, C_GB, C_KPE, C_END = (
    0, 512, 1024, 1536, 2048, 2560, 2944, 3200, 4224, 5248, 5376)


def _cparams(n_axes):
    return pltpu.CompilerParams(dimension_semantics=("arbitrary",) * n_axes,
                                vmem_limit_bytes=VMEM_LIMIT)


def _rms(x, g):
    y = x * lax.rsqrt(jnp.mean(x * x, axis=-1, keepdims=True) + EPS)
    return y * g


def _dot(a, b):
    return jnp.dot(a, b, preferred_element_type=F32)


def _dot_nt(a, b):
    return lax.dot_general(a, b, (((1,), (1,)), ((), ())), preferred_element_type=F32)


def _dot_tn(a, b):
    return lax.dot_general(a, b, (((0,), (0,)), ((), ())), preferred_element_type=F32)


def _split3(x):
    hi = x.astype(BF16)
    r1 = x - hi.astype(F32)
    mid = r1.astype(BF16)
    lo = (r1 - mid.astype(F32)).astype(BF16)
    return hi, mid, lo


def _dot_f32(a, b):
    a0, a1, a2 = _split3(a)
    b0, b1, b2 = _split3(b)
    return (_dot(a0, b0) + (_dot(a0, b1) + _dot(a1, b0))
            + (_dot(a0, b2) + _dot(a1, b1) + _dot(a2, b0)))


def _mod_kernel(c_ref, w_ref, b_ref, o_ref):
    c = c_ref[...]
    s = c * jax.nn.sigmoid(c)
    o_ref[0] = _dot_f32(s, w_ref[0]) + b_ref[0]


def _modulation(cc, w_mod, b_mod):
    depth, d, n6 = w_mod.shape
    nb = 1536
    return pl.pallas_call(
        _mod_kernel,
        grid=(depth, n6 // nb),
        in_specs=[pl.BlockSpec((SUB, d), lambda l, j: (0, 0)),
                  pl.BlockSpec((1, d, nb), lambda l, j: (l, 0, j)),
                  pl.BlockSpec((1, 1, nb), lambda l, j: (l, 0, j))],
        out_specs=pl.BlockSpec((1, SUB, nb), lambda l, j: (l, 0, j)),
        out_shape=jax.ShapeDtypeStruct((depth, SUB, n6), F32),
        compiler_params=_cparams(2),
        name="modulation",
    )(cc, w_mod, b_mod.reshape(depth, 1, n6))


def _inproj_kernel(z_ref, mod_ref, g1_ref, w_ref, lbf_ref, lbb_ref, qg_ref, kvg_ref,
                   wqa_ref, wqb_ref, wka_ref, wv_ref, vone_ref, e2_ref, cq_ref, sq_ref, kc_ref,
                   q_o, v_o, lff_o, kf_o, lfb_o, kb_o, g_o, gas_o, gbs_o, qq_o, kk_o, vv_o):
    mod = mod_ref[0, 0]
    h = _rms(z_ref[0], g1_ref[...]) * (1.0 + mod[1:2]) + mod[0:1]
    hb = h.astype(BF16)

    def seg(a, b):
        return _dot(hb, w_ref[:, a:b])

    q_o[0] = seg(C_Q, C_I)
    v_o[0] = seg(C_I, C_FF)
    for a, lb_ref, lf_o, k_o in ((C_FF, lbf_ref, lff_o, kf_o), (C_FB, lbb_ref, lfb_o, kb_o)):
        lb = lb_ref[...]
        f = lb + (1.0 - lb) * jax.nn.sigmoid(seg(a, a + HG_WIDTH))
        lf_o[0] = jnp.log(f)
        k_o[0] = 1.0 - f
    g_o[0] = seg(C_G, C_DQ)
    gas_o[0] = jax.nn.sigmoid(seg(C_GA, C_GB))
    gbs_o[0] = jax.nn.sigmoid(seg(C_GB, C_KPE))

    qn = _rms(seg(C_DQ, C_DKV), qg_ref[...]).astype(BF16)
    qa = _dot(qn, wqa_ref[...])
    qb = _dot(qn, wqb_ref[...])
    cq = cq_ref[...]
    sq = sq_ref[...]
    for hh in range(MLA_HEADS):
        sl = slice(hh * LANE, (hh + 1) * LANE)
        qq_o[0, :, sl] = (qa[:, sl] * cq + qb[:, sl] * sq).astype(BF16)
    kvn = _rms(seg(C_DKV, C_GA), kvg_ref[...]).astype(BF16)
    kp = (seg(C_KPE, C_END) * kc_ref[...]).astype(BF16)
    kk_o[0] = (_dot(kvn, wka_ref[...]) + _dot(kp, e2_ref[...])).astype(BF16)
    vv_o[0] = (_dot(kvn, wv_ref[...]) + vone_ref[...]).astype(BF16)


def _inproj(z, modl, g1, w, lbf, lbb, qg, kvg, wqa, wqb, wka, wv, vone, e2, cq, sq, kc, nct):
    b_, lt, d = z.shape
    nt = lt // TM
    row = lambda w_: pl.BlockSpec((1, TM, w_), lambda b, i: (b, i, 0))
    full = lambda a: pl.BlockSpec(a.shape, lambda b, i: (0,) * a.ndim)
    tab = pl.BlockSpec((TM, LANE), lambda b, i: (i, 0))
    f32o = lambda w_: jax.ShapeDtypeStruct((b_, lt, w_), F32)
    bf16o = lambda w_: jax.ShapeDtypeStruct((b_, lt, w_), BF16)
    return pl.pallas_call(
        _inproj_kernel,
        grid=(b_, nt),
        in_specs=[row(d),
                  pl.BlockSpec((1, 1, SUB, d), lambda b, i: (b, jnp.minimum(i // nct, 1), 0, 0)),
                  full(g1), full(w), full(lbf), full(lbb), full(qg), full(kvg),
                  full(wqa), full(wqb), full(wka), full(wv), full(vone), full(e2), tab, tab, tab],
        out_specs=[row(512)] * 7 + [row(1024)] * 5,
        out_shape=[f32o(512)] * 7 + [f32o(1024)] * 2 + [bf16o(1024)] * 3,
        compiler_params=_cparams(2),
        name="inproj",
    )(z, modl, g1, w, lbf, lbb, qg, kvg, wqa, wqb, wka, wv, vone, e2, cq, sq, kc)


N_LEVELS = 6


def _chunk_tables(reverse):
    c = CHUNK
    t = np.arange(c)[:, None]
    u = np.arange(c)[None, :]
    mats = [(u >= t) if reverse else (u <= t)]
    lvl = np.full((c, c), N_LEVELS + 1, np.int32)
    lvl[t == u] = 0
    for l in range(N_LEVELS, 0, -1):
        m = 1 << l
        mid = (t // m) * m + m // 2
        mats.insert(1, (u >= mid) if reverse else (u <= mid - 1))
        same = (t // m) == (u // m)
        lvl[same & ((u > t) if reverse else (u < t))] = l
    tall = np.concatenate(mats, axis=0).astype(np.float32)
    return jnp.asarray(tall, dtype=BF16), jnp.asarray(lvl)


def _gla_chunk(q, k, v, cs, st, lvl, reverse):
    c = CHUNK
    b = cs[0:c]
    btot = b[0:1] if reverse else b[c - 1:c]
    o = _dot_nt((q * jnp.exp(b)).astype(BF16), st.astype(BF16))
    att = jnp.where(lvl == 0, _dot_nt(q.astype(BF16), k.astype(BF16)), 0.0)
    for l in range(1, N_LEVELS + 1):
        r = cs[l * c:(l + 1) * c]
        qt = q * jnp.exp(jnp.minimum(b - r, 0.0))
        kt = k * jnp.exp(jnp.minimum(r - b, 0.0))
        att = jnp.where(lvl == l, _dot_nt(qt.astype(BF16), kt.astype(BF16)), att)
    o = o + _dot(att.astype(BF16), v.astype(BF16))
    kdec = k * jnp.exp(btot - b)
    st_new = st * jnp.exp(btot) + _dot_tn(v.astype(BF16), kdec.astype(BF16))
    return o, st_new


def _hgrn_kernel(*refs, reverse, readout):
    if readout:
        q_ref, v_ref, lf_ref, k_ref, tall_ref, lvl_ref, of_ref, g_ref, gain_ref, o_ref, s_ref = refs
    else:
        q_ref, v_ref, lf_ref, k_ref, tall_ref, lvl_ref, o_ref, s_ref = refs

    @pl.when(pl.program_id(1) == 0)
    def _():
        s_ref[...] = jnp.zeros_like(s_ref)

    c = CHUNK
    nch = TM // c

    def body(ci, carry):
        cidx = (nch - 1 - ci) if reverse else ci
        rows = pl.ds(pl.multiple_of(cidx * c, c), c)
        lf = lf_ref[0, rows, :]
        hi = lf.astype(BF16)
        lo = (lf - hi.astype(F32)).astype(BF16)
        cs = _dot(tall_ref[...], hi) + _dot(tall_ref[...], lo)
        lvl = lvl_ref[...]
        for hh in range(HG_HEADS):
            hsl = slice(hh * HG_DK, (hh + 1) * HG_DK)
            ix = (0, rows, hsl)
            o, st_new = _gla_chunk(q_ref[ix], k_ref[ix], v_ref[ix], cs[:, hsl], s_ref[hh], lvl, reverse)
            s_ref[hh] = st_new
            if readout:
                o = o + of_ref[ix]
                o = o * lax.rsqrt(jnp.mean(o * o, axis=-1, keepdims=True) + EPS)
                o = o * gain_ref[:, hsl]
                g = g_ref[ix]
                o = o * (g * jax.nn.sigmoid(g))
            o_ref[ix] = o
        return carry

    lax.fori_loop(0, nch, body, 0)


def _hgrn(q, v, lf, k, nct, reverse, extra=None):
    b_, lt, w = q.shape
    nt = lt // TM
    if reverse:
        tile = lambda i: jnp.where(i < nct, nct - 1 - i, nt - 1 - (i - nct))
    else:
        tile = lambda i: i
    row = pl.BlockSpec((1, TM, w), lambda b, i: (b, tile(i), 0))
    full = lambda a: pl.BlockSpec(a.shape, lambda b, i: (0,) * a.ndim)
    tall, lvl = _chunk_tables(reverse)
    ins = [q, v, lf, k, tall, lvl]
    specs = [row] * 4 + [full(tall), full(lvl)]
    if extra is not None:
        o_f, g, gain = extra
        ins += [o_f, g, gain]
        specs += [row, row, full(gain)]
    return pl.pallas_call(
        functools.partial(_hgrn_kernel, reverse=reverse, readout=extra is not None),
        grid=(b_, nt),
        in_specs=specs,
        out_specs=row,
        out_shape=jax.ShapeDtypeStruct((b_, lt, w), F32),
        scratch_shapes=[pltpu.VMEM((HG_HEADS, HG_DK, HG_DK), F32)],
        compiler_params=_cparams(2),
        name="hgrn_bwd" if reverse else "hgrn_fwd",
    )(*ins)


KV_CHUNK = 256


def _attn_kernel(q_ref, k_ref, v_ref, o_ref, *, n_ctx, nct):
    i = pl.program_id(2)
    lane = lax.broadcasted_iota(jnp.int32, (TM, LANE), 1)

    def run(nk):
        outs = []
        for hh in range(2):
            sl = slice(hh * LANE, (hh + 1) * LANE)
            q = q_ref[0, :, sl]
            m = acc = None
            for c in range(nk // KV_CHUNK):
                ks = slice(c * KV_CHUNK, (c + 1) * KV_CHUNK)
                s = _dot_nt(q, k_ref[0, ks, sl])
                mc = jnp.max(s, axis=-1, keepdims=True)
                if c == 0:
                    m = mc
                    acc = _dot(jnp.exp2(s - m).astype(BF16), v_ref[0, ks, sl])
                else:
                    m_new = jnp.maximum(m, mc)
                    acc = acc * jnp.exp2(m - m_new) + _dot(jnp.exp2(s - m_new).astype(BF16), v_ref[0, ks, sl])
                    m = m_new
            den = acc[:, MLA_V:MLA_V + 1] if hh == 0 else acc[:, 0:1]
            outs.append(acc / den)
        o_ref[0] = jnp.where(lane < MLA_V, outs[0], outs[1])

    @pl.when(i < nct)
    def _():
        run(n_ctx)

    @pl.when(i >= nct)
    def _():
        run(k_ref.shape[1])


def _attention(qq, kk, vv, n_ctx):
    b_, lt, _ = qq.shape
    nt = lt // TM
    npair = MLA_HEADS // 2
    return pl.pallas_call(
        functools.partial(_attn_kernel, n_ctx=n_ctx, nct=n_ctx // TM),
        grid=(b_, npair, nt),
        in_specs=[pl.BlockSpec((1, TM, 2 * LANE), lambda b, j, i: (b, i, j)),
                  pl.BlockSpec((1, lt, 2 * LANE), lambda b, j, i: (b, 0, j)),
                  pl.BlockSpec((1, lt, 2 * LANE), lambda b, j, i: (b, 0, j))],
        out_specs=pl.BlockSpec((1, TM, LANE), lambda b, j, i: (b, i, j)),
        out_shape=jax.ShapeDtypeStruct((b_, lt, npair * LANE), F32),
        compiler_params=_cparams(3),
        name="mla_attention",
    )(qq, kk, vv)


def _route(sc, sel):
    r = lambda x, e: x[e:e + 1]
    best = None
    for g in range(N_GROUPS):
        u = [r(sel, EPG * g + j) for j in range(EPG)]
        gs = None
        for a in range(EPG):
            for b in range(a + 1, EPG):
                pr = u[a] + u[b]
                gs = pr if gs is None else jnp.maximum(gs, pr)
        if best is None:
            best, gi = gs, jnp.zeros_like(gs, dtype=jnp.int32)
        else:
            upd = gs > best
            best = jnp.where(upd, gs, best)
            gi = jnp.where(upd, g, gi)

    def pick(x, j):
        out = r(x, j)
        for g in range(1, N_GROUPS):
            out = jnp.where(gi == g, r(x, EPG * g + j), out)
        return out

    u = [pick(sel, j) for j in range(EPG)]
    s = [pick(sc, j) for j in range(EPG)]

    def argmax4(vals):
        bv, bi = vals[0], jnp.zeros_like(gi)
        for j in range(1, EPG):
            upd = vals[j] > bv
            bv = jnp.where(upd, vals[j], bv)
            bi = jnp.where(upd, j, bi)
        return bi

    l1 = argmax4(u)
    l2 = argmax4([jnp.where(l1 == j, -jnp.inf, u[j]) for j in range(EPG)])
    lo = jnp.minimum(l1, l2)
    hi = jnp.maximum(l1, l2)
    s_lo = sum(jnp.where(lo == j, s[j], 0.0) for j in range(EPG))
    s_hi = sum(jnp.where(hi == j, s[j], 0.0) for j in range(EPG))
    den = s_lo + s_hi
    pair = jnp.where(lo == 0, hi - 1, jnp.where(lo == 1, hi + 1, 5))
    cls = gi * 6 + pair
    z = jnp.zeros_like(den)
    return jnp.concatenate([cls.astype(F32), s_lo / den, s_hi / den, z, z, z, z, z], axis=0)


def _merge_kernel(a_ref, m_ref, gas_ref, gbs_ref, z_ref, mod_ref, wpa_ref, wpb_ref, wo_ref,
                  g2_ref, wr_ref, rb_ref, zn_o, ht_o, route_o):
    mod = mod_ref[0, 0]
    pa = _dot(a_ref[0].astype(BF16), wpa_ref[...])
    pb = _dot(m_ref[0].astype(BF16), wpb_ref[...])
    mixed = gas_ref[0] * pa + gbs_ref[0] * pb
    zn = z_ref[0] + mod[2:3] * _dot(mixed.astype(BF16), wo_ref[...])
    zn_o[0] = zn
    h2 = _rms(zn, g2_ref[...]) * (1.0 + mod[4:5]) + mod[3:4]
    _tiles_store(ht_o, h2)
    logits = _dot_f32(h2, wr_ref[...])
    sc = jax.nn.sigmoid(logits.T[:N_EXPERTS])
    route_o[0] = _route(sc, sc + rb_ref[...])


def _merge(a, m, gas, gbs, z, modl, wpa, wpb, wo, g2, wr, rbb, nct):
    b_, lt, d = z.shape
    nt = lt // TM
    row = lambda w_: pl.BlockSpec((1, TM, w_), lambda b, i: (b, i, 0))
    full = lambda x: pl.BlockSpec(x.shape, lambda b, i: (0,) * x.ndim)
    return pl.pallas_call(
        _merge_kernel,
        grid=(b_, nt),
        in_specs=[row(512), row(512), row(d), row(d), row(d),
                  pl.BlockSpec((1, 1, SUB, d), lambda b, i: (b, jnp.minimum(i // nct, 1), 0, 0)),
                  full(wpa), full(wpb), full(wo), full(g2), full(wr), full(rbb)],
        out_specs=[row(d), pl.BlockSpec((TM * SUB, LANE), lambda b, i: (b * nt + i, 0)),
                   pl.BlockSpec((1, SUB, TM), lambda b, i: (b, 0, i))],
        out_shape=[jax.ShapeDtypeStruct((b_, lt, d), F32),
                   jax.ShapeDtypeStruct((b_ * lt * SUB, LANE), F32),
                   jax.ShapeDtypeStruct((b_, SUB, lt), F32)],
        compiler_params=_cparams(2),
        name="merge_route",
    )(a, m, gas, gbs, z, modl, wpa, wpb, wo, g2, wr, rbb)


def _tiles_store(ref, x, lead=()):
    n = x.shape[0]
    for j in range(SUB):
        ref[lead + (pl.ds(j, n, stride=SUB), slice(None))] = x[:, j * LANE:(j + 1) * LANE]


def _tiles_load(ref, n, lead=()):
    return jnp.concatenate([ref[lead + (pl.ds(j, n, stride=SUB), slice(None))] for j in range(SUB)], axis=1)


def _tile_copy(src_hbm, idx_ref, buf, sem, slot, r):
    return pltpu.make_async_copy(src_hbm.at[pl.ds(pl.multiple_of(idx_ref[0, 0, r] * SUB, SUB), SUB)],
                                 buf.at[slot, pl.ds(pl.multiple_of(r * SUB, SUB), SUB)], sem.at[slot])


GATHER_UNROLL = 8


def _gather_start(src_hbm, idx_ref, buf, sem, slot, n):
    def body(r8, c):
        for u in range(GATHER_UNROLL):
            _tile_copy(src_hbm, idx_ref, buf, sem, slot, r8 * GATHER_UNROLL + u).start()
        return c
    lax.fori_loop(0, n // GATHER_UNROLL, body, 0)


def _gather_wait(src_hbm, idx_ref, buf, sem, slot, n):
    def body(r8, c):
        for u in range(GATHER_UNROLL):
            _tile_copy(src_hbm, idx_ref, buf, sem, slot, r8 * GATHER_UNROLL + u).wait()
        return c
    lax.fori_loop(0, n // GATHER_UNROLL, body, 0)


def _moe_kernel(ea_ref, eb_ref, nu_ref, idx_ref, idxn_ref, ht_hbm, w_ref,
                wga_ref, wua_ref, wda_ref, wgb_ref, wub_ref, wdb_ref, y_ref, buf, sem):
    j = pl.program_id(0)
    nblk = pl.num_programs(0)
    slot = j % 2

    @pl.when(j == 0)
    def _():
        _gather_start(ht_hbm, idx_ref, buf, sem, 0, MOE_BM)

    @pl.when(j + 1 < nblk)
    def _():
        _gather_start(ht_hbm, idxn_ref, buf, sem, 1 - slot, MOE_BM)

    _gather_wait(ht_hbm, idx_ref, buf, sem, slot, MOE_BM)

    @pl.when(j < nu_ref[0])
    def _():
        x = _tiles_load(buf, MOE_BM, (slot,)).astype(BF16)
        pad = jnp.zeros((LANE - SUB, MOE_BM), F32)
        wcol = jnp.concatenate([w_ref[0], pad], axis=0).T

        def ffn(wg, wu, wd):
            act = jax.nn.silu(_dot(x, wg[0])) * _dot(x, wu[0])
            return _dot(act.astype(BF16), wd[0])

        _tiles_store(y_ref, wcol[:, 1:2] * ffn(wga_ref, wua_ref, wda_ref)
                     + wcol[:, 2:3] * ffn(wgb_ref, wub_ref, wdb_ref))

    @pl.when(j >= nu_ref[0])
    def _():
        y_ref[...] = jnp.zeros_like(y_ref)


def _moe(ht, src, wsort, blk_ea, blk_eb, n_used, wg, wu, wd):
    nblk = src.shape[0]
    d, de = wg.shape[1:]
    idx_spec = lambda f: pl.BlockSpec((1, 1, MOE_BM), f, memory_space=pltpu.SMEM)
    wspec = lambda shp, which: pl.BlockSpec(
        (1,) + shp, (lambda j, ea, eb, nu: (ea[j], 0, 0)) if which == 0 else (lambda j, ea, eb, nu: (eb[j], 0, 0)))
    return pl.pallas_call(
        _moe_kernel,
        grid_spec=pltpu.PrefetchScalarGridSpec(
            num_scalar_prefetch=3,
            grid=(nblk,),
            in_specs=[idx_spec(lambda j, ea, eb, nu: (j, 0, 0)),
                      idx_spec(lambda j, ea, eb, nu: (jnp.minimum(j + 1, nblk - 1), 0, 0)),
                      pl.BlockSpec(memory_space=pl.ANY),
                      pl.BlockSpec((1, SUB, MOE_BM), lambda j, ea, eb, nu: (j, 0, 0)),
                      wspec((d, de), 0), wspec((d, de), 0), wspec((de, d), 0),
                      wspec((d, de), 1), wspec((d, de), 1), wspec((de, d), 1)],
            out_specs=pl.BlockSpec((MOE_BM * SUB, LANE), lambda j, ea, eb, nu: (j, 0)),
            scratch_shapes=[pltpu.VMEM((2, MOE_BM * SUB, LANE), F32), pltpu.SemaphoreType.DMA((2,))]),
        out_shape=jax.ShapeDtypeStruct((nblk * MOE_BM * SUB, LANE), F32),
        compiler_params=_cparams(1),
        name="moe_ffn",
    )(blk_ea, blk_eb, n_used, src, src, ht, wsort, wg, wu, wd, wg, wu, wd)


def _post_kernel(*refs, final):
    if final:
        idx_ref, idxn_ref, y_hbm, zn_ref, mod_ref, fg_ref, o_ref, buf, sem = refs
    else:
        idx_ref, idxn_ref, y_hbm, zn_ref, mod_ref, o_ref, buf, sem = refs
    t = pl.program_id(0)
    nstep = pl.num_programs(0)
    slot = t % 2

    @pl.when(t == 0)
    def _():
        _gather_start(y_hbm, idx_ref, buf, sem, 0, TM)

    @pl.when(t + 1 < nstep)
    def _():
        _gather_start(y_hbm, idxn_ref, buf, sem, 1 - slot, TM)

    _gather_wait(y_hbm, idx_ref, buf, sem, slot, TM)
    z = zn_ref[0] + mod_ref[0, 0][5:6] * _tiles_load(buf, TM, (slot,))
    if final:
        z = _rms(z, fg_ref[...])
    o_ref[0] = z


def _post(y, pos, zn, modl, nct, final_g=None):
    b_, lt, d = zn.shape
    nt = lt // TM
    final = final_g is not None
    t0 = nct if final else 0
    ntl = nt - t0
    nstep = b_ * ntl
    idx_spec = lambda f: pl.BlockSpec((1, 1, TM), f, memory_space=pltpu.SMEM)
    ins = [pos, pos, y, zn, modl]
    specs = [idx_spec(lambda t: (t, 0, 0)),
             idx_spec(lambda t: (jnp.minimum(t + 1, nstep - 1), 0, 0)),
             pl.BlockSpec(memory_space=pl.ANY),
             pl.BlockSpec((1, TM, d), lambda t: (t // ntl, t % ntl + t0, 0)),
             pl.BlockSpec((1, 1, SUB, d), lambda t: (t // ntl, jnp.minimum((t % ntl + t0) // nct, 1), 0, 0))]
    if final:
        ins.append(final_g)
        specs.append(pl.BlockSpec(final_g.shape, lambda t: (0, 0)))
    return pl.pallas_call(
        functools.partial(_post_kernel, final=final),
        grid=(nstep,),
        in_specs=specs,
        out_specs=pl.BlockSpec((1, TM, d), lambda t: (t // ntl, t % ntl, 0)),
        out_shape=jax.ShapeDtypeStruct((b_, ntl * TM, d), F32),
        scratch_shapes=[pltpu.VMEM((2, TM * SUB, LANE), F32), pltpu.SemaphoreType.DMA((2,))],
        compiler_params=_cparams(1),
        name="unsort_residual",
    )(*ins)


_PAIRS = [(0, 1), (0, 2), (0, 3), (1, 2), (1, 3), (2, 3)]
_CLS_EA = np.array([EPG * g + p[0] for g in range(N_GROUPS) for p in _PAIRS], np.int32)
_CLS_EB = np.array([EPG * g + p[1] for g in range(N_GROUPS) for p in _PAIRS], np.int32)


def _sort_plan(cls, w2):
    n = cls.shape[0]
    nblk = n // MOE_BM + N_CLASSES
    onehot = (cls[:, None] == jnp.arange(N_CLASSES, dtype=jnp.int32)[None, :]).astype(jnp.int32)
    csum = jnp.cumsum(onehot, axis=0)
    counts = csum[-1]
    rank = jnp.take_along_axis(csum, cls[:, None], axis=1)[:, 0] - 1
    padded = (counts + MOE_BM - 1) // MOE_BM * MOE_BM
    pad_end = jnp.cumsum(padded)
    pos = (pad_end - padded)[cls] + rank
    src = jnp.zeros((nblk * MOE_BM,), jnp.int32).at[pos].set(jnp.arange(n, dtype=jnp.int32))
    blk_cls = jnp.minimum(
        jnp.searchsorted(pad_end, jnp.arange(nblk, dtype=jnp.int32) * MOE_BM, side='right'),
        N_CLASSES - 1).astype(jnp.int32)
    n_used = (pad_end[-1] // MOE_BM).astype(jnp.int32).reshape(1)
    ws = w2[:, src].reshape(2, nblk, MOE_BM).transpose(1, 0, 2)
    zr = jnp.zeros((nblk, 1, MOE_BM), F32)
    wsort = jnp.concatenate([zr, ws] + [zr] * (SUB - 3), axis=1)
    return (pos, src.reshape(nblk, 1, MOE_BM), wsort,
            jnp.asarray(_CLS_EA)[blk_cls], jnp.asarray(_CLS_EB)[blk_cls], n_used)


def _rot_cols(w):
    return jnp.concatenate([-w[:, 8:16], w[:, 0:8], -w[:, 24:32], w[:, 16:24]], axis=1)


def _prep_w_in(w):
    kpe = w[:, 3200:3232]
    pad = jnp.zeros((w.shape[0], C_END - C_KPE - 2 * MLA_ROPE), w.dtype)
    return jnp.concatenate([w[:, :3200], w[:, 3232:], kpe, _rot_cols(kpe), pad], axis=1).astype(BF16)


def _prep_mla(w_uq, w_ukv):
    qh = w_uq.reshape(MLA_Q_RANK, MLA_HEADS, MLA_NOPE + MLA_ROPE)
    zq = jnp.zeros((MLA_Q_RANK, MLA_HEADS, LANE - MLA_NOPE - MLA_ROPE), w_uq.dtype)
    wqa = jnp.concatenate([qh, zq], axis=2).reshape(MLA_Q_RANK, MLA_HEADS * LANE)
    pe = qh[:, :, MLA_NOPE:].reshape(MLA_Q_RANK * MLA_HEADS, MLA_ROPE)
    pe_sw = _rot_cols(pe).reshape(MLA_Q_RANK, MLA_HEADS, MLA_ROPE)
    wqb = jnp.concatenate([jnp.zeros((MLA_Q_RANK, MLA_HEADS, MLA_NOPE), w_uq.dtype), pe_sw, zq],
                          axis=2).reshape(MLA_Q_RANK, MLA_HEADS * LANE)
    kvh = w_ukv.reshape(MLA_KV_RANK, MLA_HEADS, MLA_NOPE + MLA_V)
    wka = jnp.concatenate([kvh[:, :, :MLA_NOPE], jnp.zeros((MLA_KV_RANK, MLA_HEADS, LANE - MLA_NOPE), w_ukv.dtype)],
                          axis=2).reshape(MLA_KV_RANK, MLA_HEADS * LANE)
    vh = kvh[:, :, MLA_NOPE:]
    zv = jnp.zeros_like(vh)
    odd = (jnp.arange(MLA_HEADS) % 2 == 1)[None, :, None]
    wv = jnp.concatenate([jnp.where(odd, zv, vh), jnp.where(odd, vh, zv)], axis=2).reshape(MLA_KV_RANK, MLA_HEADS * LANE)
    return wqa.astype(BF16), wqb.astype(BF16), wka.astype(BF16), wv.astype(BF16)


def _rope_tables(n_ctx, n_lat):
    rows = n_lat // GRID_W
    rowp = jnp.broadcast_to(jnp.arange(rows, dtype=F32)[:, None], (rows, GRID_W)).reshape(-1)
    colp = jnp.broadcast_to(jnp.arange(GRID_W, dtype=F32)[None, :], (rows, GRID_W)).reshape(-1)
    axis_dim = MLA_ROPE // 2
    inv_freq = ROPE_THETA ** (-jnp.arange(0, axis_dim, 2, dtype=F32) / axis_dim)
    ar, ac = rowp[:, None] * inv_freq, colp[:, None] * inv_freq
    cos32 = jnp.concatenate([jnp.cos(ar), jnp.cos(ar), jnp.cos(ac), jnp.cos(ac)], axis=1)
    sin32 = jnp.concatenate([jnp.sin(ar), jnp.sin(ar), jnp.sin(ac), jnp.sin(ac)], axis=1)
    cos32 = jnp.concatenate([jnp.ones((n_ctx, MLA_ROPE), F32), cos32], axis=0)
    sin32 = jnp.concatenate([jnp.zeros((n_ctx, MLA_ROPE), F32), sin32], axis=0)
    lt = n_ctx + n_lat
    one, zero = jnp.ones((lt, MLA_NOPE), F32), jnp.zeros((lt, MLA_NOPE), F32)
    z32 = jnp.zeros((lt, LANE - MLA_NOPE - MLA_ROPE), F32)
    cq = jnp.concatenate([one, cos32, z32], axis=1) * (MLA_SCALE * LOG2E)
    sq = jnp.concatenate([zero, sin32, z32], axis=1) * (MLA_SCALE * LOG2E)
    kc = jnp.concatenate([cos32, sin32, zero], axis=1)
    e2 = np.zeros((LANE, MLA_HEADS * LANE), np.float32)
    for hh in range(MLA_HEADS):
        for j in range(MLA_ROPE):
            e2[j, hh * LANE + MLA_NOPE + j] = 1.0
            e2[MLA_ROPE + j, hh * LANE + MLA_NOPE + j] = 1.0
    vone = np.zeros((1, MLA_HEADS * LANE), np.float32)
    for hh in range(MLA_HEADS):
        vone[0, hh * LANE + (MLA_V if hh % 2 == 0 else 0)] = 1.0
    return cq, sq, kc, jnp.asarray(e2, dtype=BF16), jnp.asarray(vone)


def _lower_bounds(gamma):
    p = jnp.cumsum(jax.nn.softmax(gamma.astype(F32), axis=0), axis=0)
    return p - p[0:1]


def kernel(x, c, ctx, c_ctx, w_mod, b_mod, norm1_g, norm2_g, w_in, gamma_fwd, gamma_bwd, hg_norm_g,
           q_norm_g, kv_norm_g, w_uq, w_ukv, w_pa, w_pb, w_o, w_router, router_bias,
           w_gate_e, w_up_e, w_down_e, final_g):
    b_, n_lat, d = x.shape
    n_ctx = ctx.shape[1]
    depth = w_mod.shape[0]
    assert n_ctx % TM == 0 and n_lat % TM == 0 and n_lat % GRID_W == 0 and b_ + 1 <= SUB
    nct = n_ctx // TM
    lt = n_ctx + n_lat
    nt = lt // TM

    cc = jnp.concatenate([c, c_ctx[None, :], jnp.zeros((SUB - b_ - 1, d), F32)], axis=0)
    mod = _modulation(cc, w_mod, b_mod).reshape(depth, SUB, 6, d)
    mod_x = mod[:, :b_]
    mod_c = jnp.broadcast_to(mod[:, b_:b_ + 1], mod_x.shape)
    mod_t = jnp.stack([mod_c, mod_x], axis=2)
    mod_t = jnp.concatenate([mod_t, jnp.zeros((depth, b_, 2, SUB - 6, d), F32)], axis=3)

    cq, sq, kc, e2, vone = _rope_tables(n_ctx, n_lat)
    lbf, lbb = _lower_bounds(gamma_fwd), _lower_bounds(gamma_bwd)
    wr = jnp.concatenate([w_router, jnp.zeros((d, LANE - N_EXPERTS), F32)], axis=1)
    rbb = jnp.broadcast_to(router_bias.astype(F32)[:, None], (N_EXPERTS, TM))

    z = jnp.concatenate([ctx, x], axis=1)
    out = None
    for l in range(depth):
        last = l == depth - 1
        wqa, wqb, wka, wv = _prep_mla(w_uq[l], w_ukv[l])
        (q, v, lff, kf, lfb, kb, g, gas, gbs, qq, kk, vv) = _inproj(
            z, mod_t[l], norm1_g[l][None], _prep_w_in(w_in[l]), lbf[l][None], lbb[l][None],
            q_norm_g[l][None], kv_norm_g[l][None], wqa, wqb, wka, wv, vone, e2, cq, sq, kc, nct)
        o_f = _hgrn(q, v, lff, kf, nct, reverse=False)
        a = _hgrn(q, v, lfb, kb, nct, reverse=True, extra=(o_f, g, hg_norm_g[l][None]))
        m = _attention(qq, kk, vv, n_ctx)
        zn, ht, route = _merge(a, m, gas, gbs, z, mod_t[l], w_pa[l].astype(BF16), w_pb[l].astype(BF16),
                               w_o[l].astype(BF16), norm2_g[l][None], wr, rbb, nct)
        route = route.transpose(1, 0, 2).reshape(SUB, b_ * lt)
        pos, src, wsort, blk_ea, blk_eb, n_used = _sort_plan(route[0].astype(jnp.int32), route[1:3])
        y = _moe(ht, src, wsort, blk_ea, blk_eb, n_used,
                 w_gate_e[l].astype(BF16), w_up_e[l].astype(BF16), w_down_e[l].astype(BF16))
        pos = pos.reshape(b_, nt, 1, TM)
        if last:
            out = _post(y, pos[:, nct:].reshape(-1, 1, TM), zn, mod_t[l], nct, final_g[None])
        else:
            z = _post(y, pos.reshape(-1, 1, TM), zn, mod_t[l], nct)
    return out
```

```python
import functools

import numpy as np
import jax
import jax.numpy as jnp
from jax import lax
from jax.experimental import pallas as pl
from jax.experimental.pallas import tpu as pltpu

F32 = jnp.float32
BF16 = jnp.bfloat16

EPS = 1e-6
GRID_W = 64
ROPE_THETA = 10000.0
HG_HEADS = 4
HG_DK = 128
HG_WIDTH = 512
MLA_HEADS = 8
MLA_NOPE = 64
MLA_ROPE = 32
MLA_V = 64
MLA_Q_RANK = 384
MLA_KV_RANK = 256
MLA_SCALE = (MLA_NOPE + MLA_ROPE) ** -0.5
LOG2E = 1.4426950408889634
N_EXPERTS = 16
N_GROUPS = 4
EPG = 4
N_CLASSES = N_GROUPS * 6

LANE = 128
SUB = 8
TM = 256
CHUNK = 128
MOE_BM = 256
VMEM_LIMIT = 56 * 1024 * 1024

C_Q, C_I, C_FF, C_FB, C_G, C_DQ, C_DKV, C_GA, C_GB, C_KPE, C_END = (
    0, 512, 1024, 1536, 2048, 2560, 2944, 3200, 4224, 5248, 5376)


def _cparams(n_axes):
    return pltpu.CompilerParams(dimension_semantics=("arbitrary",) * n_axes,
                                vmem_limit_bytes=VMEM_LIMIT)


def _rms(x, g):
    y = x * lax.rsqrt(jnp.mean(x * x, axis=-1, keepdims=True) + EPS)
    return y * g


def _dot(a, b):
    return jnp.dot(a, b, preferred_element_type=F32)


def _dot_nt(a, b):
    return lax.dot_general(a, b, (((1,), (1,)), ((), ())), preferred_element_type=F32)


def _dot_tn(a, b):
    return lax.dot_general(a, b, (((0,), (0,)), ((), ())), preferred_element_type=F32)


def _split3(x):
    hi = x.astype(BF16)
    r1 = x - hi.astype(F32)
    mid = r1.astype(BF16)
    lo = (r1 - mid.astype(F32)).astype(BF16)
    return hi, mid, lo


def _dot_f32(a, b):
    a0, a1, a2 = _split3(a)
    b0, b1, b2 = _split3(b)
    return (_dot(a0, b0) + (_dot(a0, b1) + _dot(a1, b0))
            + (_dot(a0, b2) + _dot(a1, b1) + _dot(a2, b0)))


def _mod_kernel(c_ref, w_ref, b_ref, o_ref):
    c = c_ref[...]
    s = c * jax.nn.sigmoid(c)
    o_ref[0] = _dot_f32(s, w_ref[0]) + b_ref[0]


def _modulation(cc, w_mod, b_mod):
    depth, d, n6 = w_mod.shape
    nb = 1536
    return pl.pallas_call(
        _mod_kernel,
        grid=(depth, n6 // nb),
        in_specs=[pl.BlockSpec((SUB, d), lambda l, j: (0, 0)),
                  pl.BlockSpec((1, d, nb), lambda l, j: (l, 0, j)),
                  pl.BlockSpec((1, 1, nb), lambda l, j: (l, 0, j))],
        out_specs=pl.BlockSpec((1, SUB, nb), lambda l, j: (l, 0, j)),
        out_shape=jax.ShapeDtypeStruct((depth, SUB, n6), F32),
        compiler_params=_cparams(2),
        name="modulation",
    )(cc, w_mod, b_mod.reshape(depth, 1, n6))


def _inproj_kernel(z_ref, mod_ref, g1_ref, w_ref, lbf_ref, lbb_ref, qg_ref, kvg_ref,
                   wqa_ref, wqb_ref, wka_ref, wv_ref, vone_ref, e2_ref, cq_ref, sq_ref, kc_ref,
                   q_o, v_o, lff_o, kf_o, lfb_o, kb_o, g_o, gas_o, gbs_o, qq_o, kk_o, vv_o):
    mod = mod_ref[0, 0]
    h = _rms(z_ref[0], g1_ref[...]) * (1.0 + mod[1:2]) + mod[0:1]
    hb = h.astype(BF16)

    def seg(a, b):
        return _dot(hb, w_ref[:, a:b])

    q_o[0] = seg(C_Q, C_I).astype(BF16)
    v_o[0] = seg(C_I, C_FF).astype(BF16)
    for a, lb_ref, lf_o, k_o in ((C_FF, lbf_ref, lff_o, kf_o), (C_FB, lbb_ref, lfb_o, kb_o)):
        lb = lb_ref[...]
        f = lb + (1.0 - lb) * jax.nn.sigmoid(seg(a, a + HG_WIDTH))
        lf_o[0] = jnp.log(f) * LOG2E
        k_o[0] = (1.0 - f).astype(BF16)
    g_o[0] = seg(C_G, C_DQ).astype(BF16)
    gas_o[0] = jax.nn.sigmoid(seg(C_GA, C_GB)).astype(BF16)
    gbs_o[0] = jax.nn.sigmoid(seg(C_GB, C_KPE)).astype(BF16)

    qn = _rms(seg(C_DQ, C_DKV), qg_ref[...]).astype(BF16)
    qa = _dot(qn, wqa_ref[...])
    qb = _dot(qn, wqb_ref[...])
    cq = cq_ref[...]
    sq = sq_ref[...]
    for hh in range(MLA_HEADS):
        sl = slice(hh * LANE, (hh + 1) * LANE)
        qq_o[0, :, sl] = (qa[:, sl] * cq + qb[:, sl] * sq).astype(BF16)
    kvn = _rms(seg(C_DKV, C_GA), kvg_ref[...]).astype(BF16)
    kp = (seg(C_KPE, C_END) * kc_ref[...]).astype(BF16)
    kk_o[0] = (_dot(kvn, wka_ref[...]) + _dot(kp, e2_ref[...])).astype(BF16)
    vv_o[0] = (_dot(kvn, wv_ref[...]) + vone_ref[...]).astype(BF16)


def _inproj(z, modl, g1, w, lbf, lbb, qg, kvg, wqa, wqb, wka, wv, vone, e2, cq, sq, kc, nct):
    b_, lt, d = z.shape
    nt = lt // TM
    row = lambda w_: pl.BlockSpec((1, TM, w_), lambda b, i: (b, i, 0))
    full = lambda a: pl.BlockSpec(a.shape, lambda b, i: (0,) * a.ndim)
    tab = pl.BlockSpec((TM, LANE), lambda b, i: (i, 0))
    f32o = lambda w_: jax.ShapeDtypeStruct((b_, lt, w_), F32)
    bf16o = lambda w_: jax.ShapeDtypeStruct((b_, lt, w_), BF16)
    return pl.pallas_call(
        _inproj_kernel,
        grid=(b_, nt),
        in_specs=[row(d),
                  pl.BlockSpec((1, 1, SUB, d), lambda b, i: (b, jnp.minimum(i // nct, 1), 0, 0)),
                  full(g1), full(w), full(lbf), full(lbb), full(qg), full(kvg),
                  full(wqa), full(wqb), full(wka), full(wv), full(vone), full(e2), tab, tab, tab],
        out_specs=[row(512)] * 7 + [row(1024)] * 5,
        out_shape=[bf16o(512), bf16o(512), f32o(512), bf16o(512), f32o(512), bf16o(512), bf16o(512)]
        + [bf16o(1024)] * 5,
        compiler_params=_cparams(2),
        name="inproj",
    )(z, modl, g1, w, lbf, lbb, qg, kvg, wqa, wqb, wka, wv, vone, e2, cq, sq, kc)


N_LEVELS = 7


def _chunk_tables(reverse):
    c = CHUNK
    t = np.arange(c)[:, None]
    u = np.arange(c)[None, :]
    tri = ((u >= t) if reverse else (u <= t)).astype(np.float32)
    lvl = np.full((c, c), N_LEVELS + 1, np.int32)
    lvl[t == u] = 0
    for l in range(N_LEVELS, 0, -1):
        m = 1 << l
        same = (t // m) == (u // m)
        lvl[same & ((u > t) if reverse else (u < t))] = l
    return jnp.asarray(tri, dtype=BF16), jnp.asarray(lvl)


def _level_ref(bs_ref, l, reverse):
    w = bs_ref.shape[1]
    sub = lax.broadcasted_iota(jnp.int32, (SUB, w), 0)
    off = 0 if reverse else -1
    m = 1 << l

    def row(r):
        return jnp.broadcast_to(bs_ref[r:r + 1, :], (SUB, w))

    pieces = []
    for j in range(CHUNK // SUB):
        base = j * SUB
        if m >= SUB:
            pieces.append(row((base // m) * m + m // 2 + off))
        else:
            p = row(base + m // 2 + off)
            for i in range(1, SUB // m):
                p = jnp.where(sub < i * m, p, row(base + i * m + m // 2 + off))
            pieces.append(p)
    return jnp.concatenate(pieces, axis=0)


def _hgrn_kernel(*refs, reverse, readout):
    if readout:
        (q_ref, v_ref, lf_ref, k_ref, tri_ref, lvl_ref, of_ref, g_ref, gain_ref, o_ref,
         s_ref, bs_ref, qs_ref, ks_ref) = refs
    else:
        q_ref, v_ref, lf_ref, k_ref, tri_ref, lvl_ref, o_ref, s_ref, bs_ref, qs_ref, ks_ref = refs

    @pl.when(pl.program_id(1) == 0)
    def _():
        s_ref[...] = jnp.zeros_like(s_ref)

    c = CHUNK
    nch = TM // c
    e = lambda x: jnp.exp2(x).astype(BF16)

    lvl = lvl_ref[...]
    for ci in range(nch):
        cidx = (nch - 1 - ci) if reverse else ci
        rows = slice(cidx * c, (cidx + 1) * c)
        lf = lf_ref[0, rows, :]
        hi = lf.astype(BF16)
        lo = (lf - hi.astype(F32)).astype(BF16)
        bsc = bs_ref.at[ci]
        bsc[...] = _dot(tri_ref[...], hi) + _dot(tri_ref[...], lo)
        b = bsc[...]
        btot = b[0:1] if reverse else b[c - 1:c]
        q = q_ref[0, rows, :]
        k = k_ref[0, rows, :]
        qs_ref[ci, 0] = q * e(b)
        ks_ref[ci, 0] = k * e(btot - b)
        for l in range(1, N_LEVELS + 1):
            d = b - _level_ref(bsc, l, reverse)
            qs_ref[ci, l] = q * e(d)
            ks_ref[ci, l] = k * e(-d)
        for hh in range(HG_HEADS):
            hsl = slice(hh * HG_DK, (hh + 1) * HG_DK)
            ix = (0, rows, hsl)
            v = v_ref[ix]
            st = s_ref[hh]
            att = jnp.where(lvl == 0, _dot_nt(q[:, hsl], k[:, hsl]), 0.0)
            for l in range(1, N_LEVELS + 1):
                att = jnp.where(lvl == l, _dot_nt(qs_ref[ci, l, :, hsl], ks_ref[ci, l, :, hsl]), att)
            o = _dot_nt(qs_ref[ci, 0, :, hsl], st.astype(BF16)) + _dot(att.astype(BF16), v)
            s_ref[hh] = st * jnp.exp2(btot[:, hsl]) + _dot_tn(v, ks_ref[ci, 0, :, hsl])
            if readout:
                o = o + of_ref[ix]
                o = o * lax.rsqrt(jnp.mean(o * o, axis=-1, keepdims=True) + EPS)
                o = o * gain_ref[:, hsl]
                g = g_ref[ix].astype(F32)
                o = o * (g * jax.nn.sigmoid(g))
            o_ref[ix] = o.astype(o_ref.dtype)


def _hgrn(q, v, lf, k, nct, reverse, extra=None):
    b_, lt, w = q.shape
    nt = lt // TM
    if reverse:
        tile = lambda i: jnp.where(i < nct, nct - 1 - i, nt - 1 - (i - nct))
    else:
        tile = lambda i: i
    row = pl.BlockSpec((1, TM, w), lambda b, i: (b, tile(i), 0))
    full = lambda a: pl.BlockSpec(a.shape, lambda b, i: (0,) * a.ndim)
    tri, lvl = _chunk_tables(reverse)
    ins = [q, v, lf, k, tri, lvl]
    specs = [row] * 4 + [full(tri), full(lvl)]
    if extra is not None:
        o_f, g, gain = extra
        ins += [o_f, g, gain]
        specs += [row, row, full(gain)]
    return pl.pallas_call(
        functools.partial(_hgrn_kernel, reverse=reverse, readout=extra is not None),
        grid=(b_, nt),
        in_specs=specs,
        out_specs=row,
        out_shape=jax.ShapeDtypeStruct((b_, lt, w), BF16),
        scratch_shapes=[pltpu.VMEM((HG_HEADS, HG_DK, HG_DK), F32), pltpu.VMEM((TM // CHUNK, CHUNK, w), F32),
                        pltpu.VMEM((TM // CHUNK, N_LEVELS + 1, CHUNK, w), BF16),
                        pltpu.VMEM((TM // CHUNK, N_LEVELS + 1, CHUNK, w), BF16)],
        compiler_params=_cparams(2),
        name="hgrn_bwd" if reverse else "hgrn_fwd",
    )(*ins)


KV_CHUNK = 256


def _attn_kernel(q_ref, k_ref, v_ref, o_ref, *, n_ctx, nct):
    i = pl.program_id(2)
    lane = lax.broadcasted_iota(jnp.int32, (TM, LANE), 1)

    def run(nk):
        outs = []
        for hh in range(2):
            sl = slice(hh * LANE, (hh + 1) * LANE)
            q = q_ref[0, :, sl]
            m = acc = None
            for c in range(nk // KV_CHUNK):
                ks = slice(c * KV_CHUNK, (c + 1) * KV_CHUNK)
                s = _dot_nt(q, k_ref[0, ks, sl])
                mc = jnp.max(s, axis=-1, keepdims=True)
                if c == 0:
                    m = mc
                    acc = _dot(jnp.exp2(s - m).astype(BF16), v_ref[0, ks, sl])
                else:
                    m_new = jnp.maximum(m, mc)
                    acc = acc * jnp.exp2(m - m_new) + _dot(jnp.exp2(s - m_new).astype(BF16), v_ref[0, ks, sl])
                    m = m_new
            den = acc[:, MLA_V:MLA_V + 1] if hh == 0 else acc[:, 0:1]
            outs.append(acc / den)
        o_ref[0] = jnp.where(lane < MLA_V, outs[0], outs[1]).astype(o_ref.dtype)

    @pl.when(i < nct)
    def _():
        run(n_ctx)

    @pl.when(i >= nct)
    def _():
        run(k_ref.shape[1])


def _attention(qq, kk, vv, n_ctx):
    b_, lt, _ = qq.shape
    nt = lt // TM
    npair = MLA_HEADS // 2
    return pl.pallas_call(
        functools.partial(_attn_kernel, n_ctx=n_ctx, nct=n_ctx // TM),
        grid=(b_, npair, nt),
        in_specs=[pl.BlockSpec((1, TM, 2 * LANE), lambda b, j, i: (b, i, j)),
                  pl.BlockSpec((1, lt, 2 * LANE), lambda b, j, i: (b, 0, j)),
                  pl.BlockSpec((1, lt, 2 * LANE), lambda b, j, i: (b, 0, j))],
        out_specs=pl.BlockSpec((1, TM, LANE), lambda b, j, i: (b, i, j)),
        out_shape=jax.ShapeDtypeStruct((b_, lt, npair * LANE), BF16),
        compiler_params=_cparams(3),
        name="mla_attention",
    )(qq, kk, vv)


def _route(sel):
    r = lambda x, e: x[e:e + 1]
    best = None
    for g in range(N_GROUPS):
        u = [r(sel, EPG * g + j) for j in range(EPG)]
        gs = None
        for a in range(EPG):
            for b in range(a + 1, EPG):
                pr = u[a] + u[b]
                gs = pr if gs is None else jnp.maximum(gs, pr)
        if best is None:
            best, gi = gs, jnp.zeros_like(gs, dtype=jnp.int32)
        else:
            upd = gs > best
            best = jnp.where(upd, gs, best)
            gi = jnp.where(upd, g, gi)

    def pick(x, j):
        out = r(x, j)
        for g in range(1, N_GROUPS):
            out = jnp.where(gi == g, r(x, EPG * g + j), out)
        return out

    u = [pick(sel, j) for j in range(EPG)]

    def argmax4(vals):
        bv, bi = vals[0], jnp.zeros_like(gi)
        for j in range(1, EPG):
            upd = vals[j] > bv
            bv = jnp.where(upd, vals[j], bv)
            bi = jnp.where(upd, j, bi)
        return bi

    l1 = argmax4(u)
    l2 = argmax4([jnp.where(l1 == j, -jnp.inf, u[j]) for j in range(EPG)])
    lo = jnp.minimum(l1, l2)
    hi = jnp.maximum(l1, l2)
    pair = jnp.where(lo == 0, hi - 1, jnp.where(lo == 1, hi + 1, 5))
    cls = (gi * 6 + pair).astype(F32)
    return jnp.concatenate([cls] + [jnp.zeros_like(cls)] * (SUB - 1), axis=0)


def _merge_kernel(a_ref, m_ref, gas_ref, gbs_ref, z_ref, mod_ref, wpa_ref, wpb_ref, wo_ref,
                  g2_ref, wr_ref, rb_ref, zn_o, ht_o, route_o):
    mod = mod_ref[0, 0]
    pa = _dot(a_ref[0].astype(BF16), wpa_ref[...])
    pb = _dot(m_ref[0].astype(BF16), wpb_ref[...])
    mixed = gas_ref[0] * pa + gbs_ref[0] * pb
    zn = z_ref[0] + mod[2:3] * _dot(mixed.astype(BF16), wo_ref[...])
    zn_o[0] = zn
    h2 = _rms(zn, g2_ref[...]) * (1.0 + mod[4:5]) + mod[3:4]
    _tiles_store(ht_o, h2)
    logits = _dot_f32(h2, wr_ref[...])
    sc = jax.nn.sigmoid(logits.T[:N_EXPERTS])
    route_o[0] = _route(sc + rb_ref[...])


def _merge(a, m, gas, gbs, z, modl, wpa, wpb, wo, g2, wr, rbb, nct):
    b_, lt, d = z.shape
    nt = lt // TM
    row = lambda w_: pl.BlockSpec((1, TM, w_), lambda b, i: (b, i, 0))
    full = lambda x: pl.BlockSpec(x.shape, lambda b, i: (0,) * x.ndim)
    return pl.pallas_call(
        _merge_kernel,
        grid=(b_, nt),
        in_specs=[row(512), row(512), row(d), row(d), row(d),
                  pl.BlockSpec((1, 1, SUB, d), lambda b, i: (b, jnp.minimum(i // nct, 1), 0, 0)),
                  full(wpa), full(wpb), full(wo), full(g2), full(wr), full(rbb)],
        out_specs=[row(d), pl.BlockSpec((TM * SUB, LANE), lambda b, i: (b * nt + i, 0)),
                   pl.BlockSpec((1, SUB, TM), lambda b, i: (b, 0, i))],
        out_shape=[jax.ShapeDtypeStruct((b_, lt, d), F32),
                   jax.ShapeDtypeStruct((b_ * lt * SUB, LANE), F32),
                   jax.ShapeDtypeStruct((b_, SUB, lt), F32)],
        compiler_params=_cparams(2),
        name="merge_route",
    )(a, m, gas, gbs, z, modl, wpa, wpb, wo, g2, wr, rbb)


def _tiles_store(ref, x, lead=()):
    n = x.shape[0]
    for j in range(SUB):
        ref[lead + (pl.ds(j, n, stride=SUB), slice(None))] = x[:, j * LANE:(j + 1) * LANE]


def _tiles_load(ref, n, lead=()):
    return jnp.concatenate([ref[lead + (pl.ds(j, n, stride=SUB), slice(None))] for j in range(SUB)], axis=1)


def _tile_copy(src_hbm, idx_ref, buf, sem, slot, r):
    return pltpu.make_async_copy(src_hbm.at[pl.ds(pl.multiple_of(idx_ref[0, 0, r] * SUB, SUB), SUB)],
                                 buf.at[slot, pl.ds(pl.multiple_of(r * SUB, SUB), SUB)], sem.at[slot])


GATHER_UNROLL = 8


def _gather_start(src_hbm, idx_ref, buf, sem, slot, n):
    def body(r8, c):
        for u in range(GATHER_UNROLL):
            _tile_copy(src_hbm, idx_ref, buf, sem, slot, r8 * GATHER_UNROLL + u).start()
        return c
    lax.fori_loop(0, n // GATHER_UNROLL, body, 0)


def _gather_wait(src_hbm, buf, sem, slot):
    pltpu.make_async_copy(src_hbm.at[pl.ds(0, buf.shape[1])], buf.at[slot], sem.at[slot]).wait()


def _moe_kernel(ea_ref, eb_ref, idx_ref, idxn_ref, ht_hbm, wrt_ref,
                wga_ref, wua_ref, wda_ref, wgb_ref, wub_ref, wdb_ref, y_ref, buf, sem):
    j = pl.program_id(0)
    nblk = pl.num_programs(0)
    slot = j % 2

    @pl.when(j == 0)
    def _():
        _gather_start(ht_hbm, idx_ref, buf, sem, 0, MOE_BM)

    _gather_wait(ht_hbm, buf, sem, slot)
    x32 = _tiles_load(buf, MOE_BM, (slot,))
    x = x32.astype(BF16)

    n_stage = 6
    per = MOE_BM // n_stage + 1

    def prefetch(stage):
        for r in range(stage * per, min((stage + 1) * per, MOE_BM)):
            _tile_copy(ht_hbm, idxn_ref, buf, sem, 1 - slot, r).start()

    def ffn(wg, wu, wd, e, stage):
        sc = jax.nn.sigmoid(jnp.sum(x32 * wrt_ref[pl.ds(e, 1), :], axis=-1, keepdims=True))
        prefetch(stage)
        gate = _dot(x, wg[0])
        prefetch(stage + 1)
        act = jax.nn.silu(gate) * _dot(x, wu[0])
        prefetch(stage + 2)
        return sc, _dot(act.astype(BF16), wd[0])

    s_lo, y_lo = ffn(wga_ref, wua_ref, wda_ref, ea_ref[j], 0)
    s_hi, y_hi = ffn(wgb_ref, wub_ref, wdb_ref, eb_ref[j], 3)
    inv = 1.0 / (s_lo + s_hi)
    _tiles_store(y_ref, (s_lo * inv) * y_lo + (s_hi * inv) * y_hi)

    @pl.when(j == nblk - 1)
    def _():
        _gather_wait(ht_hbm, buf, sem, 1 - slot)


def _moe(ht, src, wrt, blk_ea, blk_eb, wg, wu, wd):
    nblk = src.shape[0]
    d, de = wg.shape[1:]
    idx_spec = lambda f: pl.BlockSpec((1, 1, MOE_BM), f, memory_space=pltpu.SMEM)
    wspec = lambda shp, which: pl.BlockSpec(
        (1,) + shp, (lambda j, ea, eb: (ea[j], 0, 0)) if which == 0 else (lambda j, ea, eb: (eb[j], 0, 0)))
    return pl.pallas_call(
        _moe_kernel,
        grid_spec=pltpu.PrefetchScalarGridSpec(
            num_scalar_prefetch=2,
            grid=(nblk,),
            in_specs=[idx_spec(lambda j, ea, eb: (j, 0, 0)),
                      idx_spec(lambda j, ea, eb: (jnp.minimum(j + 1, nblk - 1), 0, 0)),
                      pl.BlockSpec(memory_space=pl.ANY),
                      pl.BlockSpec(wrt.shape, lambda j, ea, eb: (0, 0)),
                      wspec((d, de), 0), wspec((d, de), 0), wspec((de, d), 0),
                      wspec((d, de), 1), wspec((d, de), 1), wspec((de, d), 1)],
            out_specs=pl.BlockSpec((MOE_BM * SUB, LANE), lambda j, ea, eb: (j, 0)),
            scratch_shapes=[pltpu.VMEM((2, MOE_BM * SUB, LANE), F32), pltpu.SemaphoreType.DMA((2,))]),
        out_shape=jax.ShapeDtypeStruct((nblk * MOE_BM * SUB, LANE), F32),
        compiler_params=_cparams(1),
        name="moe_ffn",
    )(blk_ea, blk_eb, src, src, ht, wrt, wg, wu, wd, wg, wu, wd)


def _post_kernel(*refs, final):
    if final:
        idx_ref, idxn_ref, y_hbm, zn_ref, mod_ref, fg_ref, o_ref, buf, sem = refs
    else:
        idx_ref, idxn_ref, y_hbm, zn_ref, mod_ref, o_ref, buf, sem = refs
    t = pl.program_id(0)
    nstep = pl.num_programs(0)
    slot = t % 2

    @pl.when(t == 0)
    def _():
        _gather_start(y_hbm, idx_ref, buf, sem, 0, TM)

    @pl.when(t + 1 < nstep)
    def _():
        _gather_start(y_hbm, idxn_ref, buf, sem, 1 - slot, TM)

    _gather_wait(y_hbm, buf, sem, slot)
    z = zn_ref[0] + mod_ref[0, 0][5:6] * _tiles_load(buf, TM, (slot,))
    if final:
        z = _rms(z, fg_ref[...])
    o_ref[0] = z


def _post(y, pos, zn, modl, nct, final_g=None):
    b_, lt, d = zn.shape
    nt = lt // TM
    final = final_g is not None
    t0 = nct if final else 0
    ntl = nt - t0
    nstep = b_ * ntl
    idx_spec = lambda f: pl.BlockSpec((1, 1, TM), f, memory_space=pltpu.SMEM)
    ins = [pos, pos, y, zn, modl]
    specs = [idx_spec(lambda t: (t, 0, 0)),
             idx_spec(lambda t: (jnp.minimum(t + 1, nstep - 1), 0, 0)),
             pl.BlockSpec(memory_space=pl.ANY),
             pl.BlockSpec((1, TM, d), lambda t: (t // ntl, t % ntl + t0, 0)),
             pl.BlockSpec((1, 1, SUB, d), lambda t: (t // ntl, jnp.minimum((t % ntl + t0) // nct, 1), 0, 0))]
    if final:
        ins.append(final_g)
        specs.append(pl.BlockSpec(final_g.shape, lambda t: (0, 0)))
    return pl.pallas_call(
        functools.partial(_post_kernel, final=final),
        grid=(nstep,),
        in_specs=specs,
        out_specs=pl.BlockSpec((1, TM, d), lambda t: (t // ntl, t % ntl, 0)),
        out_shape=jax.ShapeDtypeStruct((b_, ntl * TM, d), F32),
        scratch_shapes=[pltpu.VMEM((2, TM * SUB, LANE), F32), pltpu.SemaphoreType.DMA((2,))],
        compiler_params=_cparams(1),
        name="unsort_residual",
    )(*ins)


_PAIRS = [(0, 1), (0, 2), (0, 3), (1, 2), (1, 3), (2, 3)]
_CLS_EA = np.array([EPG * g + p[0] for g in range(N_GROUPS) for p in _PAIRS], np.int32)
_CLS_EB = np.array([EPG * g + p[1] for g in range(N_GROUPS) for p in _PAIRS], np.int32)


def _sort_plan(cls):
    n = cls.shape[0]
    nblk = n // MOE_BM + N_CLASSES
    onehot = (cls[:, None] == jnp.arange(N_CLASSES, dtype=jnp.int32)[None, :]).astype(jnp.int32)
    csum = jnp.cumsum(onehot, axis=0)
    counts = csum[-1]
    rank = jnp.take_along_axis(csum, cls[:, None], axis=1)[:, 0] - 1
    padded = (counts + MOE_BM - 1) // MOE_BM * MOE_BM
    pad_end = jnp.cumsum(padded)
    pos = (pad_end - padded)[cls] + rank
    src = jnp.zeros((nblk * MOE_BM,), jnp.int32).at[pos].set(jnp.arange(n, dtype=jnp.int32))
    blk_cls = jnp.minimum(
        jnp.searchsorted(pad_end, jnp.arange(nblk, dtype=jnp.int32) * MOE_BM, side='right'),
        N_CLASSES - 1).astype(jnp.int32)
    return pos, src.reshape(nblk, 1, MOE_BM), jnp.asarray(_CLS_EA)[blk_cls], jnp.asarray(_CLS_EB)[blk_cls]


def _rot_cols(w):
    return jnp.concatenate([-w[:, 8:16], w[:, 0:8], -w[:, 24:32], w[:, 16:24]], axis=1)


def _prep_w_in(w):
    kpe = w[:, 3200:3232]
    pad = jnp.zeros((w.shape[0], C_END - C_KPE - 2 * MLA_ROPE), w.dtype)
    return jnp.concatenate([w[:, :3200], w[:, 3232:], kpe, _rot_cols(kpe), pad], axis=1).astype(BF16)


def _prep_mla(w_uq, w_ukv):
    qh = w_uq.reshape(MLA_Q_RANK, MLA_HEADS, MLA_NOPE + MLA_ROPE)
    zq = jnp.zeros((MLA_Q_RANK, MLA_HEADS, LANE - MLA_NOPE - MLA_ROPE), w_uq.dtype)
    wqa = jnp.concatenate([qh, zq], axis=2).reshape(MLA_Q_RANK, MLA_HEADS * LANE)
    pe = qh[:, :, MLA_NOPE:].reshape(MLA_Q_RANK * MLA_HEADS, MLA_ROPE)
    pe_sw = _rot_cols(pe).reshape(MLA_Q_RANK, MLA_HEADS, MLA_ROPE)
    wqb = jnp.concatenate([jnp.zeros((MLA_Q_RANK, MLA_HEADS, MLA_NOPE), w_uq.dtype), pe_sw, zq],
                          axis=2).reshape(MLA_Q_RANK, MLA_HEADS * LANE)
    kvh = w_ukv.reshape(MLA_KV_RANK, MLA_HEADS, MLA_NOPE + MLA_V)
    wka = jnp.concatenate([kvh[:, :, :MLA_NOPE], jnp.zeros((MLA_KV_RANK, MLA_HEADS, LANE - MLA_NOPE), w_ukv.dtype)],
                          axis=2).reshape(MLA_KV_RANK, MLA_HEADS * LANE)
    vh = kvh[:, :, MLA_NOPE:]
    zv = jnp.zeros_like(vh)
    odd = (jnp.arange(MLA_HEADS) % 2 == 1)[None, :, None]
    wv = jnp.concatenate([jnp.where(odd, zv, vh), jnp.where(odd, vh, zv)], axis=2).reshape(MLA_KV_RANK, MLA_HEADS * LANE)
    return wqa.astype(BF16), wqb.astype(BF16), wka.astype(BF16), wv.astype(BF16)


def _rope_tables(n_ctx, n_lat):
    rows = n_lat // GRID_W
    rowp = jnp.broadcast_to(jnp.arange(rows, dtype=F32)[:, None], (rows, GRID_W)).reshape(-1)
    colp = jnp.broadcast_to(jnp.arange(GRID_W, dtype=F32)[None, :], (rows, GRID_W)).reshape(-1)
    axis_dim = MLA_ROPE // 2
    inv_freq = ROPE_THETA ** (-jnp.arange(0, axis_dim, 2, dtype=F32) / axis_dim)
    ar, ac = rowp[:, None] * inv_freq, colp[:, None] * inv_freq
    cos32 = jnp.concatenate([jnp.cos(ar), jnp.cos(ar), jnp.cos(ac), jnp.cos(ac)], axis=1)
    sin32 = jnp.concatenate([jnp.sin(ar), jnp.sin(ar), jnp.sin(ac), jnp.sin(ac)], axis=1)
    cos32 = jnp.concatenate([jnp.ones((n_ctx, MLA_ROPE), F32), cos32], axis=0)
    sin32 = jnp.concatenate([jnp.zeros((n_ctx, MLA_ROPE), F32), sin32], axis=0)
    lt = n_ctx + n_lat
    one, zero = jnp.ones((lt, MLA_NOPE), F32), jnp.zeros((lt, MLA_NOPE), F32)
    z32 = jnp.zeros((lt, LANE - MLA_NOPE - MLA_ROPE), F32)
    cq = jnp.concatenate([one, cos32, z32], axis=1) * (MLA_SCALE * LOG2E)
    sq = jnp.concatenate([zero, sin32, z32], axis=1) * (MLA_SCALE * LOG2E)
    kc = jnp.concatenate([cos32, sin32, zero], axis=1)
    e2 = np.zeros((LANE, MLA_HEADS * LANE), np.float32)
    for hh in range(MLA_HEADS):
        for j in range(MLA_ROPE):
            e2[j, hh * LANE + MLA_NOPE + j] = 1.0
            e2[MLA_ROPE + j, hh * LANE + MLA_NOPE + j] = 1.0
    vone = np.zeros((1, MLA_HEADS * LANE), np.float32)
    for hh in range(MLA_HEADS):
        vone[0, hh * LANE + (MLA_V if hh % 2 == 0 else 0)] = 1.0
    return cq, sq, kc, jnp.asarray(e2, dtype=BF16), jnp.asarray(vone)


def _lower_bounds(gamma):
    p = jnp.cumsum(jax.nn.softmax(gamma.astype(F32), axis=0), axis=0)
    return p - p[0:1]


def kernel(x, c, ctx, c_ctx, w_mod, b_mod, norm1_g, norm2_g, w_in, gamma_fwd, gamma_bwd, hg_norm_g,
           q_norm_g, kv_norm_g, w_uq, w_ukv, w_pa, w_pb, w_o, w_router, router_bias,
           w_gate_e, w_up_e, w_down_e, final_g):
    b_, n_lat, d = x.shape
    n_ctx = ctx.shape[1]
    depth = w_mod.shape[0]
    assert n_ctx % TM == 0 and n_lat % TM == 0 and n_lat % GRID_W == 0 and b_ + 1 <= SUB
    nct = n_ctx // TM
    lt = n_ctx + n_lat
    nt = lt // TM

    cc = jnp.concatenate([c, c_ctx[None, :], jnp.zeros((SUB - b_ - 1, d), F32)], axis=0)
    mod = _modulation(cc, w_mod, b_mod).reshape(depth, SUB, 6, d)
    mod_x = mod[:, :b_]
    mod_c = jnp.broadcast_to(mod[:, b_:b_ + 1], mod_x.shape)
    mod_t = jnp.stack([mod_c, mod_x], axis=2)
    mod_t = jnp.concatenate([mod_t, jnp.zeros((depth, b_, 2, SUB - 6, d), F32)], axis=3)

    cq, sq, kc, e2, vone = _rope_tables(n_ctx, n_lat)
    lbf, lbb = _lower_bounds(gamma_fwd), _lower_bounds(gamma_bwd)
    wr = jnp.concatenate([w_router, jnp.zeros((d, LANE - N_EXPERTS), F32)], axis=1)
    rbb = jnp.broadcast_to(router_bias.astype(F32)[:, None], (N_EXPERTS, TM))

    z = jnp.concatenate([ctx, x], axis=1)
    out = None
    for l in range(depth):
        last = l == depth - 1
        wqa, wqb, wka, wv = _prep_mla(w_uq[l], w_ukv[l])
        (q, v, lff, kf, lfb, kb, g, gas, gbs, qq, kk, vv) = _inproj(
            z, mod_t[l], norm1_g[l][None], _prep_w_in(w_in[l]), lbf[l][None], lbb[l][None],
            q_norm_g[l][None], kv_norm_g[l][None], wqa, wqb, wka, wv, vone, e2, cq, sq, kc, nct)
        o_f = _hgrn(q, v, lff, kf, nct, reverse=False)
        a = _hgrn(q, v, lfb, kb, nct, reverse=True, extra=(o_f, g, hg_norm_g[l][None]))
        m = _attention(qq, kk, vv, n_ctx)
        zn, ht, route = _merge(a, m, gas, gbs, z, mod_t[l], w_pa[l].astype(BF16), w_pb[l].astype(BF16),
                               w_o[l].astype(BF16), norm2_g[l][None], wr, rbb, nct)
        pos, src, blk_ea, blk_eb = _sort_plan(route[:, 0, :].reshape(-1).astype(jnp.int32))
        y = _moe(ht, src, w_router.T, blk_ea, blk_eb,
                 w_gate_e[l].astype(BF16), w_up_e[l].astype(BF16), w_down_e[l].astype(BF16))
        pos = pos.reshape(b_, nt, 1, TM)
        if last:
            out = _post(y, pos[:, nct:].reshape(-1, 1, TM), zn, mod_t[l], nct, final_g[None])
        else:
            z = _post(y, pos.reshape(-1, 1, TM), zn, mod_t[l], nct)
    return out
```

```python
import functools

import numpy as np
import jax
import jax.numpy as jnp
from jax import lax
from jax.experimental import pallas as pl
from jax.experimental.pallas import tpu as pltpu

F32 = jnp.float32
BF16 = jnp.bfloat16

EPS = 1e-6
GRID_W = 64
ROPE_THETA = 10000.0
HG_HEADS = 4
HG_DK = 128
HG_WIDTH = 512
MLA_HEADS = 8
MLA_NOPE = 64
MLA_ROPE = 32
MLA_V = 64
MLA_Q_RANK = 384
MLA_KV_RANK = 256
MLA_SCALE = (MLA_NOPE + MLA_ROPE) ** -0.5
LOG2E = 1.4426950408889634
N_EXPERTS = 16
N_GROUPS = 4
EPG = 4
N_CLASSES = N_GROUPS * 6

LANE = 128
SUB = 8
TM = 256
CHUNK = 128
MOE_BM = 256
VMEM_LIMIT = 56 * 1024 * 1024

C_Q, C_I, C_FF, C_FB, C_G, C_DQ, C_DKV, C_GA, C_GB, C_KPE, C_END = (
    0, 512, 1024, 1536, 2048, 2560, 2944, 3200, 4224, 5248, 5376)


def _cparams(n_axes):
    return pltpu.CompilerParams(dimension_semantics=("arbitrary",) * n_axes,
                                vmem_limit_bytes=VMEM_LIMIT)


def _rms(x, g):
    y = x * lax.rsqrt(jnp.mean(x * x, axis=-1, keepdims=True) + EPS)
    return y * g


def _dot(a, b):
    return jnp.dot(a, b, preferred_element_type=F32)


def _dot_nt(a, b):
    return lax.dot_general(a, b, (((1,), (1,)), ((), ())), preferred_element_type=F32)


def _dot_tn(a, b):
    return lax.dot_general(a, b, (((0,), (0,)), ((), ())), preferred_element_type=F32)


def _split3(x):
    hi = x.astype(BF16)
    r1 = x - hi.astype(F32)
    mid = r1.astype(BF16)
    lo = (r1 - mid.astype(F32)).astype(BF16)
    return hi, mid, lo


def _dot_f32(a, b, passes=6):
    a0, a1, a2 = _split3(a)
    b0, b1, b2 = _split3(b)
    out = _dot(a0, b0) + (_dot(a0, b1) + _dot(a1, b0))
    if passes == 6:
        out = out + (_dot(a0, b2) + _dot(a1, b1) + _dot(a2, b0))
    return out


def _mod_kernel(c_ref, w_ref, b_ref, o_ref):
    c = c_ref[...]
    s = c * jax.nn.sigmoid(c)
    o_ref[0] = _dot_f32(s, w_ref[0]) + b_ref[0]


def _modulation(cc, w_mod, b_mod):
    depth, d, n6 = w_mod.shape
    nb = 1536
    return pl.pallas_call(
        _mod_kernel,
        grid=(depth, n6 // nb),
        in_specs=[pl.BlockSpec((SUB, d), lambda l, j: (0, 0)),
                  pl.BlockSpec((1, d, nb), lambda l, j: (l, 0, j)),
                  pl.BlockSpec((1, 1, nb), lambda l, j: (l, 0, j))],
        out_specs=pl.BlockSpec((1, SUB, nb), lambda l, j: (l, 0, j)),
        out_shape=jax.ShapeDtypeStruct((depth, SUB, n6), F32),
        compiler_params=_cparams(2),
        name="modulation",
    )(cc, w_mod, b_mod.reshape(depth, 1, n6))


def _inproj_kernel(z_ref, mod_ref, g1_ref, w_ref, lbf_ref, lbb_ref, qg_ref, kvg_ref,
                   wq_ref, wkv_ref, cqe_ref, sqe_ref, cqo_ref, sqo_ref, kc_ref,
                   q_o, v_o, lff_o, kf_o, lfb_o, kb_o, g_o, gas_o, gbs_o, qq_o, kk_o, vv_o):
    mod = mod_ref[0, 0]
    h = _rms(z_ref[0], g1_ref[...]) * (1.0 + mod[1:2]) + mod[0:1]
    hb = h.astype(BF16)

    def seg(a, b):
        return _dot(hb, w_ref[:, a:b])

    q_o[0] = seg(C_Q, C_I).astype(BF16)
    v_o[0] = seg(C_I, C_FF).astype(BF16)
    for a, lb_ref, lf_o, k_o in ((C_FF, lbf_ref, lff_o, kf_o), (C_FB, lbb_ref, lfb_o, kb_o)):
        lb = lb_ref[...]
        f = lb + (1.0 - lb) * jax.nn.sigmoid(seg(a, a + HG_WIDTH))
        lf_o[0] = jnp.log(f) * LOG2E
        k_o[0] = (1.0 - f).astype(BF16)
    g_o[0] = seg(C_G, C_DQ).astype(BF16)
    gas_o[0] = jax.nn.sigmoid(seg(C_GA, C_GB)).astype(BF16)
    gbs_o[0] = jax.nn.sigmoid(seg(C_GB, C_KPE)).astype(BF16)

    lane = lax.broadcasted_iota(jnp.int32, (TM, LANE), 1)
    qn = _rms(seg(C_DQ, C_DKV), qg_ref[...]).astype(BF16)
    qa = _dot(qn, wq_ref[...])
    tabs = ((cqe_ref[...], sqe_ref[...]), (cqo_ref[...], sqo_ref[...]))
    for hh in range(MLA_HEADS):
        sl = slice(hh * LANE, (hh + 1) * LANE)
        cq, sq = tabs[hh % 2]
        blk = qa[:, sl]
        qq_o[0, :, sl] = (blk * cq + pltpu.roll(blk, LANE - MLA_ROPE, 1) * sq).astype(BF16)
    kvn = _rms(seg(C_DKV, C_GA), kvg_ref[...]).astype(BF16)
    kv = _dot(kvn, wkv_ref[...])
    kp = seg(C_KPE, C_END) * kc_ref[...]
    kpe_e = jnp.where(lane < MLA_ROPE, kp + pltpu.roll(kp, LANE - MLA_ROPE, 1), 0.0)
    kpe_o = pltpu.roll(kpe_e, MLA_NOPE, 1)
    one_e = jnp.where(lane == MLA_V, 1.0, 0.0)
    one_o = jnp.where(lane == 0, 1.0, 0.0)
    for hh in range(MLA_HEADS):
        sl = slice(hh * LANE, (hh + 1) * LANE)
        blk = kv[:, sl]
        if hh % 2 == 0:
            kk_o[0, :, sl] = jnp.where(lane >= MLA_V, blk, kpe_e).astype(BF16)
            vv_o[0, :, sl] = jnp.where(lane < MLA_V, blk, one_e).astype(BF16)
        else:
            kk_o[0, :, sl] = jnp.where(lane < MLA_NOPE, blk, kpe_o).astype(BF16)
            vv_o[0, :, sl] = jnp.where(lane >= MLA_NOPE, blk, one_o).astype(BF16)


def _inproj(z, modl, g1, w, lbf, lbb, qg, kvg, wq, wkv, cqe, sqe, cqo, sqo, kc, nct):
    b_, lt, d = z.shape
    nt = lt // TM
    row = lambda w_: pl.BlockSpec((1, TM, w_), lambda b, i: (b, i, 0))
    full = lambda a: pl.BlockSpec(a.shape, lambda b, i: (0,) * a.ndim)
    tab = pl.BlockSpec((TM, LANE), lambda b, i: (i, 0))
    f32o = lambda w_: jax.ShapeDtypeStruct((b_, lt, w_), F32)
    bf16o = lambda w_: jax.ShapeDtypeStruct((b_, lt, w_), BF16)
    return pl.pallas_call(
        _inproj_kernel,
        grid=(b_, nt),
        in_specs=[row(d),
                  pl.BlockSpec((1, 1, SUB, d), lambda b, i: (b, jnp.minimum(i // nct, 1), 0, 0)),
                  full(g1), full(w), full(lbf), full(lbb), full(qg), full(kvg),
                  full(wq), full(wkv), tab, tab, tab, tab, tab],
        out_specs=[row(512)] * 7 + [row(1024)] * 5,
        out_shape=[bf16o(512), bf16o(512), f32o(512), bf16o(512), f32o(512), bf16o(512), bf16o(512)]
        + [bf16o(1024)] * 5,
        compiler_params=_cparams(2),
        name="inproj",
    )(z, modl, g1, w, lbf, lbb, qg, kvg, wq, wkv, cqe, sqe, cqo, sqo, kc)


N_LEVELS = 7


def _chunk_tables(reverse):
    c = CHUNK
    t = np.arange(c)[:, None]
    u = np.arange(c)[None, :]
    tri = ((u >= t) if reverse else (u <= t)).astype(np.float32)
    lvl = np.full((c, c), N_LEVELS + 1, np.int32)
    lvl[t == u] = 0
    for l in range(N_LEVELS, 0, -1):
        m = 1 << l
        same = (t // m) == (u // m)
        lvl[same & ((u > t) if reverse else (u < t))] = l
    return jnp.asarray(tri, dtype=BF16), jnp.asarray(lvl)


def _level_ref(bs_ref, l, reverse):
    w = bs_ref.shape[1]
    sub = lax.broadcasted_iota(jnp.int32, (SUB, w), 0)
    off = 0 if reverse else -1
    m = 1 << l

    def row(r):
        return jnp.broadcast_to(bs_ref[r:r + 1, :], (SUB, w))

    pieces = []
    for j in range(CHUNK // SUB):
        base = j * SUB
        if m >= SUB:
            pieces.append(row((base // m) * m + m // 2 + off))
        else:
            p = row(base + m // 2 + off)
            for i in range(1, SUB // m):
                p = jnp.where(sub < i * m, p, row(base + i * m + m // 2 + off))
            pieces.append(p)
    return jnp.concatenate(pieces, axis=0)


def _hgrn_kernel(*refs, reverse, readout):
    if readout:
        (q_ref, v_ref, lf_ref, k_ref, tri_ref, lvl_ref, of_ref, g_ref, gain_ref, o_ref,
         s_ref, bs_ref, qs_ref, ks_ref) = refs
    else:
        q_ref, v_ref, lf_ref, k_ref, tri_ref, lvl_ref, o_ref, s_ref, bs_ref, qs_ref, ks_ref = refs

    @pl.when(pl.program_id(1) == 0)
    def _():
        s_ref[...] = jnp.zeros_like(s_ref)

    c = CHUNK
    nch = TM // c
    e = lambda x: jnp.exp2(x).astype(BF16)

    lvl = lvl_ref[...]
    for ci in range(nch):
        cidx = (nch - 1 - ci) if reverse else ci
        rows = slice(cidx * c, (cidx + 1) * c)
        lf = lf_ref[0, rows, :]
        hi = lf.astype(BF16)
        lo = (lf - hi.astype(F32)).astype(BF16)
        bsc = bs_ref.at[ci]
        bsc[...] = _dot(tri_ref[...], hi) + _dot(tri_ref[...], lo)
        b = bsc[...]
        btot = b[0:1] if reverse else b[c - 1:c]
        q = q_ref[0, rows, :]
        k = k_ref[0, rows, :]
        qs_ref[ci, 0] = q * e(b)
        ks_ref[ci, 0] = k * e(btot - b)
        for l in range(1, N_LEVELS + 1):
            d = b - _level_ref(bsc, l, reverse)
            qs_ref[ci, l] = q * e(d)
            ks_ref[ci, l] = k * e(-d)
        for hh in range(HG_HEADS):
            hsl = slice(hh * HG_DK, (hh + 1) * HG_DK)
            ix = (0, rows, hsl)
            v = v_ref[ix]
            st = s_ref[hh]
            att = jnp.where(lvl == 0, _dot_nt(q[:, hsl], k[:, hsl]), 0.0)
            for l in range(1, N_LEVELS + 1):
                att = jnp.where(lvl == l, _dot_nt(qs_ref[ci, l, :, hsl], ks_ref[ci, l, :, hsl]), att)
            o = _dot_nt(qs_ref[ci, 0, :, hsl], st.astype(BF16)) + _dot(att.astype(BF16), v)
            s_ref[hh] = st * jnp.exp2(btot[:, hsl]) + _dot_tn(v, ks_ref[ci, 0, :, hsl])
            if readout:
                o = o + of_ref[ix]
                o = o * lax.rsqrt(jnp.mean(o * o, axis=-1, keepdims=True) + EPS)
                o = o * gain_ref[:, hsl]
                g = g_ref[ix].astype(F32)
                o = o * (g * jax.nn.sigmoid(g))
            o_ref[ix] = o.astype(o_ref.dtype)


def _hgrn(q, v, lf, k, nct, reverse, extra=None):
    b_, lt, w = q.shape
    nt = lt // TM
    if reverse:
        tile = lambda i: jnp.where(i < nct, nct - 1 - i, nt - 1 - (i - nct))
    else:
        tile = lambda i: i
    row = pl.BlockSpec((1, TM, w), lambda b, i: (b, tile(i), 0))
    full = lambda a: pl.BlockSpec(a.shape, lambda b, i: (0,) * a.ndim)
    tri, lvl = _chunk_tables(reverse)
    ins = [q, v, lf, k, tri, lvl]
    specs = [row] * 4 + [full(tri), full(lvl)]
    if extra is not None:
        o_f, g, gain = extra
        ins += [o_f, g, gain]
        specs += [row, row, full(gain)]
    return pl.pallas_call(
        functools.partial(_hgrn_kernel, reverse=reverse, readout=extra is not None),
        grid=(b_, nt),
        in_specs=specs,
        out_specs=row,
        out_shape=jax.ShapeDtypeStruct((b_, lt, w), BF16),
        scratch_shapes=[pltpu.VMEM((HG_HEADS, HG_DK, HG_DK), F32), pltpu.VMEM((TM // CHUNK, CHUNK, w), F32),
                        pltpu.VMEM((TM // CHUNK, N_LEVELS + 1, CHUNK, w), BF16),
                        pltpu.VMEM((TM // CHUNK, N_LEVELS + 1, CHUNK, w), BF16)],
        compiler_params=_cparams(2),
        name="hgrn_bwd" if reverse else "hgrn_fwd",
    )(*ins)


KV_CHUNK = 256


def _attn_kernel(q_ref, k_ref, v_ref, o_ref, *, n_ctx, nct):
    i = pl.program_id(2)
    lane = lax.broadcasted_iota(jnp.int32, (TM, LANE), 1)

    def run(nk):
        outs = []
        for hh in range(2):
            sl = slice(hh * LANE, (hh + 1) * LANE)
            q = q_ref[0, :, sl]
            m = acc = None
            edges = [0] + list(range(n_ctx, nk + 1, KV_CHUNK))
            for c in range(len(edges) - 1):
                ks = slice(edges[c], edges[c + 1])
                s = _dot_nt(q, k_ref[0, ks, sl])
                mc = jnp.max(s, axis=-1, keepdims=True)
                if c == 0:
                    m = mc
                    acc = _dot(jnp.exp2(s - m).astype(BF16), v_ref[0, ks, sl])
                else:
                    m_new = jnp.maximum(m, mc)
                    acc = acc * jnp.exp2(m - m_new) + _dot(jnp.exp2(s - m_new).astype(BF16), v_ref[0, ks, sl])
                    m = m_new
            den = acc[:, MLA_V:MLA_V + 1] if hh == 0 else acc[:, 0:1]
            outs.append(acc / den)
        o_ref[0] = jnp.where(lane < MLA_V, outs[0], outs[1]).astype(o_ref.dtype)

    @pl.when(i < nct)
    def _():
        run(n_ctx)

    @pl.when(i >= nct)
    def _():
        run(k_ref.shape[1])


def _attention(qq, kk, vv, n_ctx):
    b_, lt, _ = qq.shape
    nt = lt // TM
    npair = MLA_HEADS // 2
    return pl.pallas_call(
        functools.partial(_attn_kernel, n_ctx=n_ctx, nct=n_ctx // TM),
        grid=(b_, npair, nt),
        in_specs=[pl.BlockSpec((1, TM, 2 * LANE), lambda b, j, i: (b, i, j)),
                  pl.BlockSpec((1, lt, 2 * LANE), lambda b, j, i: (b, 0, j)),
                  pl.BlockSpec((1, lt, 2 * LANE), lambda b, j, i: (b, 0, j))],
        out_specs=pl.BlockSpec((1, TM, LANE), lambda b, j, i: (b, i, j)),
        out_shape=jax.ShapeDtypeStruct((b_, lt, npair * LANE), BF16),
        compiler_params=_cparams(3),
        name="mla_attention",
    )(qq, kk, vv)


def _route(sel):
    r = lambda x, e: x[e:e + 1]
    best = None
    for g in range(N_GROUPS):
        u = [r(sel, EPG * g + j) for j in range(EPG)]
        gs = None
        for a in range(EPG):
            for b in range(a + 1, EPG):
                pr = u[a] + u[b]
                gs = pr if gs is None else jnp.maximum(gs, pr)
        if best is None:
            best, gi = gs, jnp.zeros_like(gs, dtype=jnp.int32)
        else:
            upd = gs > best
            best = jnp.where(upd, gs, best)
            gi = jnp.where(upd, g, gi)

    def pick(x, j):
        out = r(x, j)
        for g in range(1, N_GROUPS):
            out = jnp.where(gi == g, r(x, EPG * g + j), out)
        return out

    u = [pick(sel, j) for j in range(EPG)]

    def argmax4(vals):
        bv, bi = vals[0], jnp.zeros_like(gi)
        for j in range(1, EPG):
            upd = vals[j] > bv
            bv = jnp.where(upd, vals[j], bv)
            bi = jnp.where(upd, j, bi)
        return bi

    l1 = argmax4(u)
    l2 = argmax4([jnp.where(l1 == j, -jnp.inf, u[j]) for j in range(EPG)])
    lo = jnp.minimum(l1, l2)
    hi = jnp.maximum(l1, l2)
    pair = jnp.where(lo == 0, hi - 1, jnp.where(lo == 1, hi + 1, 5))
    cls = (gi * 6 + pair).astype(F32)
    return jnp.concatenate([cls] + [jnp.zeros_like(cls)] * (SUB - 1), axis=0)


def _merge_kernel(a_ref, m_ref, gas_ref, gbs_ref, z_ref, mod_ref, wpa_ref, wpb_ref, wo_ref,
                  g2_ref, wr_ref, rb_ref, zn_o, ht_o, route_o):
    mod = mod_ref[0, 0]
    pa = _dot(a_ref[0].astype(BF16), wpa_ref[...])
    pb = _dot(m_ref[0].astype(BF16), wpb_ref[...])
    mixed = gas_ref[0] * pa + gbs_ref[0] * pb
    zn = z_ref[0] + mod[2:3] * _dot(mixed.astype(BF16), wo_ref[...])
    zn_o[0] = zn
    h2 = _rms(zn, g2_ref[...]) * (1.0 + mod[4:5]) + mod[3:4]
    _tiles_store(ht_o, h2)
    logits = _dot_f32(h2, wr_ref[...], passes=3)
    sc = jax.nn.sigmoid(logits.T[:N_EXPERTS])
    route_o[0] = _route(sc + rb_ref[...])


def _merge(a, m, gas, gbs, z, modl, wpa, wpb, wo, g2, wr, rbb, nct):
    b_, lt, d = z.shape
    nt = lt // TM
    row = lambda w_: pl.BlockSpec((1, TM, w_), lambda b, i: (b, i, 0))
    full = lambda x: pl.BlockSpec(x.shape, lambda b, i: (0,) * x.ndim)
    return pl.pallas_call(
        _merge_kernel,
        grid=(b_, nt),
        in_specs=[row(512), row(512), row(d), row(d), row(d),
                  pl.BlockSpec((1, 1, SUB, d), lambda b, i: (b, jnp.minimum(i // nct, 1), 0, 0)),
                  full(wpa), full(wpb), full(wo), full(g2), full(wr), full(rbb)],
        out_specs=[row(d), pl.BlockSpec((TM * SUB, LANE), lambda b, i: (b * nt + i, 0)),
                   pl.BlockSpec((1, SUB, TM), lambda b, i: (b, 0, i))],
        out_shape=[jax.ShapeDtypeStruct((b_, lt, d), F32),
                   jax.ShapeDtypeStruct((b_ * lt * SUB, LANE), F32),
                   jax.ShapeDtypeStruct((b_, SUB, lt), F32)],
        compiler_params=_cparams(2),
        name="merge_route",
    )(a, m, gas, gbs, z, modl, wpa, wpb, wo, g2, wr, rbb)


def _tiles_store(ref, x, lead=()):
    n = x.shape[0]
    for j in range(SUB):
        ref[lead + (pl.ds(j, n, stride=SUB), slice(None))] = x[:, j * LANE:(j + 1) * LANE]


def _tiles_load(ref, n, lead=()):
    return jnp.concatenate([ref[lead + (pl.ds(j, n, stride=SUB), slice(None))] for j in range(SUB)], axis=1)


def _tile_copy(src_hbm, idx_ref, buf, sem, slot, r):
    return pltpu.make_async_copy(src_hbm.at[pl.ds(pl.multiple_of(idx_ref[0, 0, r] * SUB, SUB), SUB)],
                                 buf.at[slot, pl.ds(pl.multiple_of(r * SUB, SUB), SUB)], sem.at[slot])


GATHER_UNROLL = 8


def _gather_start(src_hbm, idx_ref, buf, sem, slot, n):
    def body(r8, c):
        for u in range(GATHER_UNROLL):
            _tile_copy(src_hbm, idx_ref, buf, sem, slot, r8 * GATHER_UNROLL + u).start()
        return c
    lax.fori_loop(0, n // GATHER_UNROLL, body, 0)


def _gather_wait(src_hbm, buf, sem, slot):
    pltpu.make_async_copy(src_hbm.at[pl.ds(0, buf.shape[1])], buf.at[slot], sem.at[slot]).wait()


def _moe_kernel(ea_ref, eb_ref, nu_ref, idx_ref, idxn_ref, ht_hbm, wrt_ref,
                wga_ref, wua_ref, wda_ref, wgb_ref, wub_ref, wdb_ref, y_ref, buf, sem):
    j = pl.program_id(0)
    nblk = pl.num_programs(0)
    slot = j % 2

    @pl.when(j == 0)
    def _():
        _gather_start(ht_hbm, idx_ref, buf, sem, 0, MOE_BM)

    @pl.when(j + 1 < nblk)
    def _():
        _gather_start(ht_hbm, idxn_ref, buf, sem, 1 - slot, MOE_BM)

    _gather_wait(ht_hbm, buf, sem, slot)

    @pl.when(j < nu_ref[0])
    def _():
        x32 = _tiles_load(buf, MOE_BM, (slot,))
        x = x32.astype(BF16)

        def ffn(wg, wu, wd, e):
            sc = jax.nn.sigmoid(jnp.sum(x32 * wrt_ref[pl.ds(e, 1), :], axis=-1, keepdims=True))
            act = jax.nn.silu(_dot(x, wg[0])) * _dot(x, wu[0])
            return sc, _dot(act.astype(BF16), wd[0])

        s_lo, y_lo = ffn(wga_ref, wua_ref, wda_ref, ea_ref[j])
        s_hi, y_hi = ffn(wgb_ref, wub_ref, wdb_ref, eb_ref[j])
        inv = 1.0 / (s_lo + s_hi)
        _tiles_store(y_ref, (s_lo * inv) * y_lo + (s_hi * inv) * y_hi)

    @pl.when(j >= nu_ref[0])
    def _():
        y_ref[...] = jnp.zeros_like(y_ref)


def _moe(ht, src, wrt, blk_ea, blk_eb, n_used, wg, wu, wd):
    nblk = src.shape[0]
    d, de = wg.shape[1:]
    idx_spec = lambda f: pl.BlockSpec((1, 1, MOE_BM), f, memory_space=pltpu.SMEM)
    wspec = lambda shp, which: pl.BlockSpec(
        (1,) + shp, (lambda j, ea, eb, nu: (ea[j], 0, 0)) if which == 0 else (lambda j, ea, eb, nu: (eb[j], 0, 0)))
    return pl.pallas_call(
        _moe_kernel,
        grid_spec=pltpu.PrefetchScalarGridSpec(
            num_scalar_prefetch=3,
            grid=(nblk,),
            in_specs=[idx_spec(lambda j, ea, eb, nu: (j, 0, 0)),
                      idx_spec(lambda j, ea, eb, nu: (jnp.minimum(j + 1, nblk - 1), 0, 0)),
                      pl.BlockSpec(memory_space=pl.ANY),
                      pl.BlockSpec(wrt.shape, lambda j, ea, eb, nu: (0, 0)),
                      wspec((d, de), 0), wspec((d, de), 0), wspec((de, d), 0),
                      wspec((d, de), 1), wspec((d, de), 1), wspec((de, d), 1)],
            out_specs=pl.BlockSpec((MOE_BM * SUB, LANE), lambda j, ea, eb, nu: (j, 0)),
            scratch_shapes=[pltpu.VMEM((2, MOE_BM * SUB, LANE), F32), pltpu.SemaphoreType.DMA((2,))]),
        out_shape=jax.ShapeDtypeStruct((nblk * MOE_BM * SUB, LANE), F32),
        compiler_params=_cparams(1),
        name="moe_ffn",
    )(blk_ea, blk_eb, n_used, src, src, ht, wrt, wg, wu, wd, wg, wu, wd)


def _post_kernel(*refs, final):
    if final:
        idx_ref, idxn_ref, y_hbm, zn_ref, mod_ref, fg_ref, o_ref, buf, sem = refs
    else:
        idx_ref, idxn_ref, y_hbm, zn_ref, mod_ref, o_ref, buf, sem = refs
    t = pl.program_id(0)
    nstep = pl.num_programs(0)
    slot = t % 2

    @pl.when(t == 0)
    def _():
        _gather_start(y_hbm, idx_ref, buf, sem, 0, TM)

    @pl.when(t + 1 < nstep)
    def _():
        _gather_start(y_hbm, idxn_ref, buf, sem, 1 - slot, TM)

    _gather_wait(y_hbm, buf, sem, slot)
    z = zn_ref[0] + mod_ref[0, 0][5:6] * _tiles_load(buf, TM, (slot,))
    if final:
        z = _rms(z, fg_ref[...])
    o_ref[0] = z


def _post(y, pos, zn, modl, nct, final_g=None):
    b_, lt, d = zn.shape
    nt = lt // TM
    final = final_g is not None
    t0 = nct if final else 0
    ntl = nt - t0
    nstep = b_ * ntl
    idx_spec = lambda f: pl.BlockSpec((1, 1, TM), f, memory_space=pltpu.SMEM)
    ins = [pos, pos, y, zn, modl]
    specs = [idx_spec(lambda t: (t, 0, 0)),
             idx_spec(lambda t: (jnp.minimum(t + 1, nstep - 1), 0, 0)),
             pl.BlockSpec(memory_space=pl.ANY),
             pl.BlockSpec((1, TM, d), lambda t: (t // ntl, t % ntl + t0, 0)),
             pl.BlockSpec((1, 1, SUB, d), lambda t: (t // ntl, jnp.minimum((t % ntl + t0) // nct, 1), 0, 0))]
    if final:
        ins.append(final_g)
        specs.append(pl.BlockSpec(final_g.shape, lambda t: (0, 0)))
    return pl.pallas_call(
        functools.partial(_post_kernel, final=final),
        grid=(nstep,),
        in_specs=specs,
        out_specs=pl.BlockSpec((1, TM, d), lambda t: (t // ntl, t % ntl, 0)),
        out_shape=jax.ShapeDtypeStruct((b_, ntl * TM, d), F32),
        scratch_shapes=[pltpu.VMEM((2, TM * SUB, LANE), F32), pltpu.SemaphoreType.DMA((2,))],
        compiler_params=_cparams(1),
        name="unsort_residual",
    )(*ins)


_PAIRS = [(0, 1), (0, 2), (0, 3), (1, 2), (1, 3), (2, 3)]
_CLS_EA = np.array([EPG * g + p[0] for g in range(N_GROUPS) for p in _PAIRS], np.int32)
_CLS_EB = np.array([EPG * g + p[1] for g in range(N_GROUPS) for p in _PAIRS], np.int32)


def _sort_plan(cls):
    n = cls.shape[0]
    nblk = n // MOE_BM + N_CLASSES
    onehot = (cls[:, None] == jnp.arange(N_CLASSES, dtype=jnp.int32)[None, :]).astype(jnp.int32)
    csum = jnp.cumsum(onehot, axis=0)
    counts = csum[-1]
    rank = jnp.take_along_axis(csum, cls[:, None], axis=1)[:, 0] - 1
    padded = (counts + MOE_BM - 1) // MOE_BM * MOE_BM
    pad_end = jnp.cumsum(padded)
    pos = (pad_end - padded)[cls] + rank
    src = jnp.zeros((nblk * MOE_BM,), jnp.int32).at[pos].set(jnp.arange(n, dtype=jnp.int32))
    blk_cls = jnp.minimum(
        jnp.searchsorted(pad_end, jnp.arange(nblk, dtype=jnp.int32) * MOE_BM, side='right'),
        N_CLASSES - 1).astype(jnp.int32)
    n_used = (pad_end[-1] // MOE_BM).astype(jnp.int32).reshape(1)
    return (pos, src.reshape(nblk, 1, MOE_BM), jnp.asarray(_CLS_EA)[blk_cls], jnp.asarray(_CLS_EB)[blk_cls],
            n_used)


def _rot_cols(w):
    return jnp.concatenate([-w[:, 8:16], w[:, 0:8], -w[:, 24:32], w[:, 16:24]], axis=1)


def _prep_w_in(w):
    kpe = w[:, 3200:3232]
    pad = jnp.zeros((w.shape[0], C_END - C_KPE - 2 * MLA_ROPE), w.dtype)
    return jnp.concatenate([w[:, :3200], w[:, 3232:], kpe, _rot_cols(kpe), pad], axis=1).astype(BF16)


def _prep_mla(w_uq, w_ukv):
    odd = (jnp.arange(MLA_HEADS) % 2 == 1)[None, :, None]
    qh = w_uq.reshape(MLA_Q_RANK, MLA_HEADS, MLA_NOPE + MLA_ROPE)
    nope, pe = qh[:, :, :MLA_NOPE], qh[:, :, MLA_NOPE:]
    pe_sw = _rot_cols(pe.reshape(MLA_Q_RANK * MLA_HEADS, MLA_ROPE)).reshape(MLA_Q_RANK, MLA_HEADS, MLA_ROPE)
    wq = jnp.where(odd, jnp.concatenate([nope, pe, pe_sw], axis=2), jnp.concatenate([pe, pe_sw, nope], axis=2))
    kvh = w_ukv.reshape(MLA_KV_RANK, MLA_HEADS, MLA_NOPE + MLA_V)
    kn, vh = kvh[:, :, :MLA_NOPE], kvh[:, :, MLA_NOPE:]
    wkv = jnp.where(odd, jnp.concatenate([kn, vh], axis=2), jnp.concatenate([vh, kn], axis=2))
    return (wq.reshape(MLA_Q_RANK, MLA_HEADS * LANE).astype(BF16),
            wkv.reshape(MLA_KV_RANK, MLA_HEADS * LANE).astype(BF16))


def _rope_tables(n_ctx, n_lat):
    rows = n_lat // GRID_W
    rowp = jnp.broadcast_to(jnp.arange(rows, dtype=F32)[:, None], (rows, GRID_W)).reshape(-1)
    colp = jnp.broadcast_to(jnp.arange(GRID_W, dtype=F32)[None, :], (rows, GRID_W)).reshape(-1)
    axis_dim = MLA_ROPE // 2
    inv_freq = ROPE_THETA ** (-jnp.arange(0, axis_dim, 2, dtype=F32) / axis_dim)
    ar, ac = rowp[:, None] * inv_freq, colp[:, None] * inv_freq
    cos32 = jnp.concatenate([jnp.cos(ar), jnp.cos(ar), jnp.cos(ac), jnp.cos(ac)], axis=1)
    sin32 = jnp.concatenate([jnp.sin(ar), jnp.sin(ar), jnp.sin(ac), jnp.sin(ac)], axis=1)
    cos32 = jnp.concatenate([jnp.ones((n_ctx, MLA_ROPE), F32), cos32], axis=0)
    sin32 = jnp.concatenate([jnp.zeros((n_ctx, MLA_ROPE), F32), sin32], axis=0)
    lt = n_ctx + n_lat
    one, zero = jnp.ones((lt, MLA_NOPE), F32), jnp.zeros((lt, MLA_NOPE), F32)
    z32 = jnp.zeros((lt, MLA_ROPE), F32)
    sc = MLA_SCALE * LOG2E
    cqe = jnp.concatenate([cos32, z32, one], axis=1) * sc
    sqe = jnp.concatenate([sin32, z32, zero], axis=1) * sc
    cqo = jnp.concatenate([one, cos32, z32], axis=1) * sc
    sqo = jnp.concatenate([zero, sin32, z32], axis=1) * sc
    kc = jnp.concatenate([cos32, sin32, zero], axis=1)
    return cqe, sqe, cqo, sqo, kc


def _lower_bounds(gamma):
    p = jnp.cumsum(jax.nn.softmax(gamma.astype(F32), axis=0), axis=0)
    return p - p[0:1]


def kernel(x, c, ctx, c_ctx, w_mod, b_mod, norm1_g, norm2_g, w_in, gamma_fwd, gamma_bwd, hg_norm_g,
           q_norm_g, kv_norm_g, w_uq, w_ukv, w_pa, w_pb, w_o, w_router, router_bias,
           w_gate_e, w_up_e, w_down_e, final_g):
    b_, n_lat, d = x.shape
    n_ctx = ctx.shape[1]
    depth = w_mod.shape[0]
    assert n_ctx % TM == 0 and n_lat % TM == 0 and n_lat % GRID_W == 0 and b_ + 1 <= SUB
    nct = n_ctx // TM
    lt = n_ctx + n_lat
    nt = lt // TM

    cc = jnp.concatenate([c, c_ctx[None, :], jnp.zeros((SUB - b_ - 1, d), F32)], axis=0)
    mod = _modulation(cc, w_mod, b_mod).reshape(depth, SUB, 6, d)
    mod_x = mod[:, :b_]
    mod_c = jnp.broadcast_to(mod[:, b_:b_ + 1], mod_x.shape)
    mod_t = jnp.stack([mod_c, mod_x], axis=2)
    mod_t = jnp.concatenate([mod_t, jnp.zeros((depth, b_, 2, SUB - 6, d), F32)], axis=3)

    cqe, sqe, cqo, sqo, kc = _rope_tables(n_ctx, n_lat)
    lbf, lbb = _lower_bounds(gamma_fwd), _lower_bounds(gamma_bwd)
    wr = jnp.concatenate([w_router, jnp.zeros((d, LANE - N_EXPERTS), F32)], axis=1)
    rbb = jnp.broadcast_to(router_bias.astype(F32)[:, None], (N_EXPERTS, TM))

    z = jnp.concatenate([ctx, x], axis=1)
    out = None
    for l in range(depth):
        last = l == depth - 1
        wq, wkv = _prep_mla(w_uq[l], w_ukv[l])
        (q, v, lff, kf, lfb, kb, g, gas, gbs, qq, kk, vv) = _inproj(
            z, mod_t[l], norm1_g[l][None], _prep_w_in(w_in[l]), lbf[l][None], lbb[l][None],
            q_norm_g[l][None], kv_norm_g[l][None], wq, wkv, cqe, sqe, cqo, sqo, kc, nct)
        o_f = _hgrn(q, v, lff, kf, nct, reverse=False)
        a = _hgrn(q, v, lfb, kb, nct, reverse=True, extra=(o_f, g, hg_norm_g[l][None]))
        m = _attention(qq, kk, vv, n_ctx)
        zn, ht, route = _merge(a, m, gas, gbs, z, mod_t[l], w_pa[l].astype(BF16), w_pb[l].astype(BF16),
                               w_o[l].astype(BF16), norm2_g[l][None], wr, rbb, nct)
        pos, src, blk_ea, blk_eb, n_used = _sort_plan(route[:, 0, :].reshape(-1).astype(jnp.int32))
        y = _moe(ht, src, w_router.T, blk_ea, blk_eb, n_used,
                 w_gate_e[l].astype(BF16), w_up_e[l].astype(BF16), w_down_e[l].astype(BF16))
        pos = pos.reshape(b_, nt, 1, TM)
        if last:
            out = _post(y, pos[:, nct:].reshape(-1, 1, TM), zn, mod_t[l], nct, final_g[None])
        else:
            z = _post(y, pos.reshape(-1, 1, TM), zn, mod_t[l], nct)
    return out
```

```python
import functools

import numpy as np
import jax
import jax.numpy as jnp
from jax import lax
from jax.experimental import pallas as pl
from jax.experimental.pallas import tpu as pltpu

F32 = jnp.float32
BF16 = jnp.bfloat16

EPS = 1e-6
GRID_W = 64
ROPE_THETA = 10000.0
HG_HEADS = 4
HG_DK = 128
HG_WIDTH = 512
MLA_HEADS = 8
MLA_NOPE = 64
MLA_ROPE = 32
MLA_V = 64
MLA_Q_RANK = 384
MLA_KV_RANK = 256
MLA_SCALE = (MLA_NOPE + MLA_ROPE) ** -0.5
LOG2E = 1.4426950408889634
N_EXPERTS = 16
N_GROUPS = 4
EPG = 4
N_CLASSES = N_GROUPS * 6

LANE = 128
SUB = 8
TM = 256
CHUNK = 128
MOE_BM = 256
VMEM_LIMIT = 56 * 1024 * 1024

C_Q, C_I, C_FF, C_FB, C_G, C_DQ, C_DKV, C_GA, C_GB, C_KPE, C_END = (
    0, 512, 1024, 1536, 2048, 2560, 2944, 3200, 4224, 5248, 5376)


def _cparams(n_axes):
    return pltpu.CompilerParams(dimension_semantics=("arbitrary",) * n_axes,
                                vmem_limit_bytes=VMEM_LIMIT)


def _rms(x, g):
    y = x * lax.rsqrt(jnp.mean(x * x, axis=-1, keepdims=True) + EPS)
    return y * g


def _dot(a, b):
    return jnp.dot(a, b, preferred_element_type=F32)


def _dot_nt(a, b):
    return lax.dot_general(a, b, (((1,), (1,)), ((), ())), preferred_element_type=F32)


def _dot_tn(a, b):
    return lax.dot_general(a, b, (((0,), (0,)), ((), ())), preferred_element_type=F32)


def _split3(x):
    hi = x.astype(BF16)
    r1 = x - hi.astype(F32)
    mid = r1.astype(BF16)
    lo = (r1 - mid.astype(F32)).astype(BF16)
    return hi, mid, lo


def _dot_f32(a, b, passes=6):
    a0, a1, a2 = _split3(a)
    b0, b1, b2 = _split3(b)
    out = _dot(a0, b0) + (_dot(a0, b1) + _dot(a1, b0))
    if passes == 6:
        out = out + (_dot(a0, b2) + _dot(a1, b1) + _dot(a2, b0))
    return out


def _mod_kernel(c_ref, w_ref, b_ref, o_ref):
    c = c_ref[...]
    s = c * jax.nn.sigmoid(c)
    o_ref[0] = _dot_f32(s, w_ref[0], passes=3) + b_ref[0]


def _modulation(cc, w_mod, b_mod):
    depth, d, n6 = w_mod.shape
    nb = 1536
    return pl.pallas_call(
        _mod_kernel,
        grid=(depth, n6 // nb),
        in_specs=[pl.BlockSpec((SUB, d), lambda l, j: (0, 0)),
                  pl.BlockSpec((1, d, nb), lambda l, j: (l, 0, j)),
                  pl.BlockSpec((1, 1, nb), lambda l, j: (l, 0, j))],
        out_specs=pl.BlockSpec((1, SUB, nb), lambda l, j: (l, 0, j)),
        out_shape=jax.ShapeDtypeStruct((depth, SUB, n6), F32),
        compiler_params=_cparams(2),
        name="modulation",
    )(cc, w_mod, b_mod.reshape(depth, 1, n6))


def _inproj_kernel(z_ref, mod_ref, g1_ref, w_ref, lbf_ref, lbb_ref, qg_ref, kvg_ref,
                   wq_ref, wkv_ref, cqe_ref, sqe_ref, cqo_ref, sqo_ref, kc_ref,
                   q_o, v_o, lff_o, kf_o, lfb_o, kb_o, g_o, gas_o, gbs_o, qq_o, kk_o, vv_o):
    mod = mod_ref[0, 0]
    h = _rms(z_ref[0], g1_ref[...]) * (1.0 + mod[1:2]) + mod[0:1]
    hb = h.astype(BF16)

    def seg(a, b):
        return _dot(hb, w_ref[:, a:b])

    q_o[0] = seg(C_Q, C_I).astype(BF16)
    v_o[0] = seg(C_I, C_FF).astype(BF16)
    for a, lb_ref, lf_o, k_o in ((C_FF, lbf_ref, lff_o, kf_o), (C_FB, lbb_ref, lfb_o, kb_o)):
        lb = lb_ref[...]
        f = lb + (1.0 - lb) * jax.nn.sigmoid(seg(a, a + HG_WIDTH))
        lf_o[0] = jnp.log(f) * LOG2E
        k_o[0] = (1.0 - f).astype(BF16)
    g_o[0] = seg(C_G, C_DQ).astype(BF16)
    gas_o[0] = jax.nn.sigmoid(seg(C_GA, C_GB)).astype(BF16)
    gbs_o[0] = jax.nn.sigmoid(seg(C_GB, C_KPE)).astype(BF16)

    lane = lax.broadcasted_iota(jnp.int32, (TM, LANE), 1)
    qn = _rms(seg(C_DQ, C_DKV), qg_ref[...]).astype(BF16)
    qa = _dot(qn, wq_ref[...])
    tabs = ((cqe_ref[...], sqe_ref[...]), (cqo_ref[...], sqo_ref[...]))
    for hh in range(MLA_HEADS):
        sl = slice(hh * LANE, (hh + 1) * LANE)
        cq, sq = tabs[hh % 2]
        blk = qa[:, sl]
        qq_o[0, :, sl] = (blk * cq + pltpu.roll(blk, LANE - MLA_ROPE, 1) * sq).astype(BF16)
    kvn = _rms(seg(C_DKV, C_GA), kvg_ref[...]).astype(BF16)
    kv = _dot(kvn, wkv_ref[...])
    kp = seg(C_KPE, C_END) * kc_ref[...]
    kpe_e = jnp.where(lane < MLA_ROPE, kp + pltpu.roll(kp, LANE - MLA_ROPE, 1), 0.0)
    kpe_o = pltpu.roll(kpe_e, MLA_NOPE, 1)
    one_e = jnp.where(lane == MLA_V, 1.0, 0.0)
    one_o = jnp.where(lane == 0, 1.0, 0.0)
    for hh in range(MLA_HEADS):
        sl = slice(hh * LANE, (hh + 1) * LANE)
        blk = kv[:, sl]
        if hh % 2 == 0:
            kk_o[0, :, sl] = jnp.where(lane >= MLA_V, blk, kpe_e).astype(BF16)
            vv_o[0, :, sl] = jnp.where(lane < MLA_V, blk, one_e).astype(BF16)
        else:
            kk_o[0, :, sl] = jnp.where(lane < MLA_NOPE, blk, kpe_o).astype(BF16)
            vv_o[0, :, sl] = jnp.where(lane >= MLA_NOPE, blk, one_o).astype(BF16)


def _inproj(z, modl, g1, w, lbf, lbb, qg, kvg, wq, wkv, cqe, sqe, cqo, sqo, kc, nct):
    b_, lt, d = z.shape
    nt = lt // TM
    row = lambda w_: pl.BlockSpec((1, TM, w_), lambda b, i: (b, i, 0))
    full = lambda a: pl.BlockSpec(a.shape, lambda b, i: (0,) * a.ndim)
    tab = pl.BlockSpec((TM, LANE), lambda b, i: (i, 0))
    f32o = lambda w_: jax.ShapeDtypeStruct((b_, lt, w_), F32)
    bf16o = lambda w_: jax.ShapeDtypeStruct((b_, lt, w_), BF16)
    return pl.pallas_call(
        _inproj_kernel,
        grid=(b_, nt),
        in_specs=[row(d),
                  pl.BlockSpec((1, 1, SUB, d), lambda b, i: (b, jnp.minimum(i // nct, 1), 0, 0)),
                  full(g1), full(w), full(lbf), full(lbb), full(qg), full(kvg),
                  full(wq), full(wkv), tab, tab, tab, tab, tab],
        out_specs=[row(512)] * 7 + [row(1024)] * 5,
        out_shape=[bf16o(512), bf16o(512), f32o(512), bf16o(512), f32o(512), bf16o(512), bf16o(512)]
        + [bf16o(1024)] * 5,
        compiler_params=_cparams(2),
        name="inproj",
    )(z, modl, g1, w, lbf, lbb, qg, kvg, wq, wkv, cqe, sqe, cqo, sqo, kc)


N_LEVELS = 7


def _chunk_tables(reverse):
    c = CHUNK
    t = np.arange(c)[:, None]
    u = np.arange(c)[None, :]
    tri = ((u >= t) if reverse else (u <= t)).astype(np.float32)
    lvl = np.full((c, c), N_LEVELS + 1, np.int32)
    lvl[t == u] = 0
    for l in range(N_LEVELS, 0, -1):
        m = 1 << l
        same = (t // m) == (u // m)
        lvl[same & ((u > t) if reverse else (u < t))] = l
    return jnp.asarray(tri, dtype=BF16), jnp.asarray(lvl)


def _level_ref(bs_ref, l, reverse):
    w = bs_ref.shape[1]
    sub = lax.broadcasted_iota(jnp.int32, (SUB, w), 0)
    off = 0 if reverse else -1
    m = 1 << l

    def row(r):
        return jnp.broadcast_to(bs_ref[r:r + 1, :], (SUB, w))

    pieces = []
    for j in range(CHUNK // SUB):
        base = j * SUB
        if m >= SUB:
            pieces.append(row((base // m) * m + m // 2 + off))
        else:
            p = row(base + m // 2 + off)
            for i in range(1, SUB // m):
                p = jnp.where(sub < i * m, p, row(base + i * m + m // 2 + off))
            pieces.append(p)
    return jnp.concatenate(pieces, axis=0)


def _hgrn_kernel(*refs, reverse, readout):
    if readout:
        (q_ref, v_ref, lf_ref, k_ref, tri_ref, lvl_ref, of_ref, g_ref, gain_ref, o_ref,
         s_ref, bs_ref, qs_ref, ks_ref) = refs
    else:
        q_ref, v_ref, lf_ref, k_ref, tri_ref, lvl_ref, o_ref, s_ref, bs_ref, qs_ref, ks_ref = refs

    @pl.when(pl.program_id(1) == 0)
    def _():
        s_ref[...] = jnp.zeros_like(s_ref)

    c = CHUNK
    nch = TM // c
    e = lambda x: jnp.exp2(x).astype(BF16)

    lvl = lvl_ref[...]
    for ci in range(nch):
        cidx = (nch - 1 - ci) if reverse else ci
        rows = slice(cidx * c, (cidx + 1) * c)
        lf = lf_ref[0, rows, :]
        hi = lf.astype(BF16)
        lo = (lf - hi.astype(F32)).astype(BF16)
        bsc = bs_ref.at[ci]
        bsc[...] = _dot(tri_ref[...], hi) + _dot(tri_ref[...], lo)
        b = bsc[...]
        btot = b[0:1] if reverse else b[c - 1:c]
        q = q_ref[0, rows, :]
        k = k_ref[0, rows, :]
        qs_ref[ci, 0] = q * e(b)
        ks_ref[ci, 0] = k * e(btot - b)
        for l in range(1, N_LEVELS + 1):
            d = b - _level_ref(bsc, l, reverse)
            qs_ref[ci, l] = q * e(d)
            ks_ref[ci, l] = k * e(-d)
        for hh in range(HG_HEADS):
            hsl = slice(hh * HG_DK, (hh + 1) * HG_DK)
            ix = (0, rows, hsl)
            v = v_ref[ix]
            st = s_ref[hh]
            att = jnp.where(lvl == 0, _dot_nt(q[:, hsl], k[:, hsl]), 0.0)
            for l in range(1, N_LEVELS + 1):
                att = jnp.where(lvl == l, _dot_nt(qs_ref[ci, l, :, hsl], ks_ref[ci, l, :, hsl]), att)
            o = _dot_nt(qs_ref[ci, 0, :, hsl], st.astype(BF16)) + _dot(att.astype(BF16), v)
            s_ref[hh] = st * jnp.exp2(btot[:, hsl]) + _dot_tn(v, ks_ref[ci, 0, :, hsl])
            if readout:
                o = o + of_ref[ix]
                o = o * lax.rsqrt(jnp.mean(o * o, axis=-1, keepdims=True) + EPS)
                o = o * gain_ref[:, hsl]
                g = g_ref[ix].astype(F32)
                o = o * (g * jax.nn.sigmoid(g))
            o_ref[ix] = o.astype(o_ref.dtype)


def _hgrn(q, v, lf, k, nct, reverse, extra=None):
    b_, lt, w = q.shape
    nt = lt // TM
    if reverse:
        tile = lambda i: jnp.where(i < nct, nct - 1 - i, nt - 1 - (i - nct))
    else:
        tile = lambda i: i
    row = pl.BlockSpec((1, TM, w), lambda b, i: (b, tile(i), 0))
    full = lambda a: pl.BlockSpec(a.shape, lambda b, i: (0,) * a.ndim)
    tri, lvl = _chunk_tables(reverse)
    ins = [q, v, lf, k, tri, lvl]
    specs = [row] * 4 + [full(tri), full(lvl)]
    if extra is not None:
        o_f, g, gain = extra
        ins += [o_f, g, gain]
        specs += [row, row, full(gain)]
    return pl.pallas_call(
        functools.partial(_hgrn_kernel, reverse=reverse, readout=extra is not None),
        grid=(b_, nt),
        in_specs=specs,
        out_specs=row,
        out_shape=jax.ShapeDtypeStruct((b_, lt, w), BF16),
        scratch_shapes=[pltpu.VMEM((HG_HEADS, HG_DK, HG_DK), F32), pltpu.VMEM((TM // CHUNK, CHUNK, w), F32),
                        pltpu.VMEM((TM // CHUNK, N_LEVELS + 1, CHUNK, w), BF16),
                        pltpu.VMEM((TM // CHUNK, N_LEVELS + 1, CHUNK, w), BF16)],
        compiler_params=_cparams(2),
        name="hgrn_bwd" if reverse else "hgrn_fwd",
    )(*ins)


KV_CHUNK = 256


def _attn_kernel(q_ref, k_ref, v_ref, o_ref, *, n_ctx, nct):
    i = pl.program_id(2)
    lane = lax.broadcasted_iota(jnp.int32, (TM, LANE), 1)

    def run(nk):
        outs = []
        for hh in range(2):
            sl = slice(hh * LANE, (hh + 1) * LANE)
            q = q_ref[0, :, sl]
            m = acc = None
            edges = [0] + list(range(n_ctx, nk + 1, KV_CHUNK))
            for c in range(len(edges) - 1):
                ks = slice(edges[c], edges[c + 1])
                s = _dot_nt(q, k_ref[0, ks, sl])
                mc = jnp.max(s, axis=-1, keepdims=True)
                if c == 0:
                    m = mc
                    acc = _dot(jnp.exp2(s - m).astype(BF16), v_ref[0, ks, sl])
                else:
                    m_new = jnp.maximum(m, mc)
                    acc = acc * jnp.exp2(m - m_new) + _dot(jnp.exp2(s - m_new).astype(BF16), v_ref[0, ks, sl])
                    m = m_new
            den = acc[:, MLA_V:MLA_V + 1] if hh == 0 else acc[:, 0:1]
            outs.append(acc / den)
        o_ref[0] = jnp.where(lane < MLA_V, outs[0], outs[1]).astype(o_ref.dtype)

    @pl.when(i < nct)
    def _():
        run(n_ctx)

    @pl.when(i >= nct)
    def _():
        run(k_ref.shape[1])


def _attention(qq, kk, vv, n_ctx):
    b_, lt, _ = qq.shape
    nt = lt // TM
    npair = MLA_HEADS // 2
    return pl.pallas_call(
        functools.partial(_attn_kernel, n_ctx=n_ctx, nct=n_ctx // TM),
        grid=(b_, npair, nt),
        in_specs=[pl.BlockSpec((1, TM, 2 * LANE), lambda b, j, i: (b, i, j)),
                  pl.BlockSpec((1, lt, 2 * LANE), lambda b, j, i: (b, 0, j)),
                  pl.BlockSpec((1, lt, 2 * LANE), lambda b, j, i: (b, 0, j))],
        out_specs=pl.BlockSpec((1, TM, LANE), lambda b, j, i: (b, i, j)),
        out_shape=jax.ShapeDtypeStruct((b_, lt, npair * LANE), BF16),
        compiler_params=_cparams(3),
        name="mla_attention",
    )(qq, kk, vv)


def _route(sel):
    r = lambda x, e: x[e:e + 1]
    best = None
    for g in range(N_GROUPS):
        u = [r(sel, EPG * g + j) for j in range(EPG)]
        gs = None
        for a in range(EPG):
            for b in range(a + 1, EPG):
                pr = u[a] + u[b]
                gs = pr if gs is None else jnp.maximum(gs, pr)
        if best is None:
            best, gi = gs, jnp.zeros_like(gs, dtype=jnp.int32)
        else:
            upd = gs > best
            best = jnp.where(upd, gs, best)
            gi = jnp.where(upd, g, gi)

    def pick(x, j):
        out = r(x, j)
        for g in range(1, N_GROUPS):
            out = jnp.where(gi == g, r(x, EPG * g + j), out)
        return out

    u = [pick(sel, j) for j in range(EPG)]

    def argmax4(vals):
        bv, bi = vals[0], jnp.zeros_like(gi)
        for j in range(1, EPG):
            upd = vals[j] > bv
            bv = jnp.where(upd, vals[j], bv)
            bi = jnp.where(upd, j, bi)
        return bi

    l1 = argmax4(u)
    l2 = argmax4([jnp.where(l1 == j, -jnp.inf, u[j]) for j in range(EPG)])
    lo = jnp.minimum(l1, l2)
    hi = jnp.maximum(l1, l2)
    pair = jnp.where(lo == 0, hi - 1, jnp.where(lo == 1, hi + 1, 5))
    cls = (gi * 6 + pair).astype(F32)
    return jnp.concatenate([cls] + [jnp.zeros_like(cls)] * (SUB - 1), axis=0)


def _merge_kernel(a_ref, m_ref, gas_ref, gbs_ref, z_ref, mod_ref, wpa_ref, wpb_ref, wo_ref,
                  g2_ref, wr_ref, rb_ref, zn_o, ht_o, route_o):
    mod = mod_ref[0, 0]
    pa = _dot(a_ref[0].astype(BF16), wpa_ref[...])
    pb = _dot(m_ref[0].astype(BF16), wpb_ref[...])
    mixed = gas_ref[0] * pa + gbs_ref[0] * pb
    zn = z_ref[0] + mod[2:3] * _dot(mixed.astype(BF16), wo_ref[...])
    zn_o[0] = zn
    h2 = _rms(zn, g2_ref[...]) * (1.0 + mod[4:5]) + mod[3:4]
    _tiles_store(ht_o, h2)
    logits = _dot_f32(h2, wr_ref[...], passes=3)
    sc = jax.nn.sigmoid(logits.T[:N_EXPERTS])
    route_o[0] = _route(sc + rb_ref[...])


def _merge(a, m, gas, gbs, z, modl, wpa, wpb, wo, g2, wr, rbb, nct):
    b_, lt, d = z.shape
    nt = lt // TM
    row = lambda w_: pl.BlockSpec((1, TM, w_), lambda b, i: (b, i, 0))
    full = lambda x: pl.BlockSpec(x.shape, lambda b, i: (0,) * x.ndim)
    return pl.pallas_call(
        _merge_kernel,
        grid=(b_, nt),
        in_specs=[row(512), row(512), row(d), row(d), row(d),
                  pl.BlockSpec((1, 1, SUB, d), lambda b, i: (b, jnp.minimum(i // nct, 1), 0, 0)),
                  full(wpa), full(wpb), full(wo), full(g2), full(wr), full(rbb)],
        out_specs=[row(d), pl.BlockSpec((TM * SUB, LANE), lambda b, i: (b * nt + i, 0)),
                   pl.BlockSpec((1, SUB, TM), lambda b, i: (b, 0, i))],
        out_shape=[jax.ShapeDtypeStruct((b_, lt, d), F32),
                   jax.ShapeDtypeStruct((b_ * lt * SUB, LANE), F32),
                   jax.ShapeDtypeStruct((b_, SUB, lt), F32)],
        compiler_params=_cparams(2),
        name="merge_route",
    )(a, m, gas, gbs, z, modl, wpa, wpb, wo, g2, wr, rbb)


def _tiles_store(ref, x, lead=()):
    n = x.shape[0]
    for j in range(SUB):
        ref[lead + (pl.ds(j, n, stride=SUB), slice(None))] = x[:, j * LANE:(j + 1) * LANE]


def _tiles_load(ref, n, lead=()):
    return jnp.concatenate([ref[lead + (pl.ds(j, n, stride=SUB), slice(None))] for j in range(SUB)], axis=1)


def _tile_copy(src_hbm, idx_ref, buf, sem, slot, r):
    return pltpu.make_async_copy(src_hbm.at[pl.ds(pl.multiple_of(idx_ref[0, 0, r] * SUB, SUB), SUB)],
                                 buf.at[slot, pl.ds(pl.multiple_of(r * SUB, SUB), SUB)], sem.at[slot])


GATHER_UNROLL = 8


def _gather_start(src_hbm, idx_ref, buf, sem, slot, n):
    def body(r8, c):
        for u in range(GATHER_UNROLL):
            _tile_copy(src_hbm, idx_ref, buf, sem, slot, r8 * GATHER_UNROLL + u).start(priority=1)
        return c
    lax.fori_loop(0, n // GATHER_UNROLL, body, 0)


def _gather_wait(src_hbm, buf, sem, slot):
    pltpu.make_async_copy(src_hbm.at[pl.ds(0, buf.shape[1])], buf.at[slot], sem.at[slot]).wait()


def _moe_kernel(ea_ref, eb_ref, nu_ref, idx_ref, idxn_ref, ht_hbm, wrt_ref,
                wga_ref, wua_ref, wda_ref, wgb_ref, wub_ref, wdb_ref, y_ref, buf, sem):
    j = pl.program_id(0)
    nblk = pl.num_programs(0)
    slot = j % 2

    @pl.when(j == 0)
    def _():
        _gather_start(ht_hbm, idx_ref, buf, sem, 0, MOE_BM)

    @pl.when(j + 1 < nblk)
    def _():
        _gather_start(ht_hbm, idxn_ref, buf, sem, 1 - slot, MOE_BM)

    _gather_wait(ht_hbm, buf, sem, slot)

    @pl.when(j < nu_ref[0])
    def _():
        x32 = _tiles_load(buf, MOE_BM, (slot,))
        x = x32.astype(BF16)

        def ffn(wg, wu, wd, e):
            sc = jax.nn.sigmoid(jnp.sum(x32 * wrt_ref[pl.ds(e, 1), :], axis=-1, keepdims=True))
            act = jax.nn.silu(_dot(x, wg[0])) * _dot(x, wu[0])
            return sc, _dot(act.astype(BF16), wd[0])

        s_lo, y_lo = ffn(wga_ref, wua_ref, wda_ref, ea_ref[j])
        s_hi, y_hi = ffn(wgb_ref, wub_ref, wdb_ref, eb_ref[j])
        inv = 1.0 / (s_lo + s_hi)
        _tiles_store(y_ref, (s_lo * inv) * y_lo + (s_hi * inv) * y_hi)

    @pl.when(j >= nu_ref[0])
    def _():
        y_ref[...] = jnp.zeros_like(y_ref)


def _moe(ht, src, wrt, blk_ea, blk_eb, n_used, wg, wu, wd):
    nblk = src.shape[0]
    d, de = wg.shape[1:]
    idx_spec = lambda f: pl.BlockSpec((1, 1, MOE_BM), f, memory_space=pltpu.SMEM)
    wspec = lambda shp, which: pl.BlockSpec(
        (1,) + shp, (lambda j, ea, eb, nu: (ea[j], 0, 0)) if which == 0 else (lambda j, ea, eb, nu: (eb[j], 0, 0)))
    return pl.pallas_call(
        _moe_kernel,
        grid_spec=pltpu.PrefetchScalarGridSpec(
            num_scalar_prefetch=3,
            grid=(nblk,),
            in_specs=[idx_spec(lambda j, ea, eb, nu: (j, 0, 0)),
                      idx_spec(lambda j, ea, eb, nu: (jnp.minimum(j + 1, nblk - 1), 0, 0)),
                      pl.BlockSpec(memory_space=pl.ANY),
                      pl.BlockSpec(wrt.shape, lambda j, ea, eb, nu: (0, 0)),
                      wspec((d, de), 0), wspec((d, de), 0), wspec((de, d), 0),
                      wspec((d, de), 1), wspec((d, de), 1), wspec((de, d), 1)],
            out_specs=pl.BlockSpec((MOE_BM * SUB, LANE), lambda j, ea, eb, nu: (j, 0)),
            scratch_shapes=[pltpu.VMEM((2, MOE_BM * SUB, LANE), F32), pltpu.SemaphoreType.DMA((2,))]),
        out_shape=jax.ShapeDtypeStruct((nblk * MOE_BM * SUB, LANE), F32),
        compiler_params=_cparams(1),
        name="moe_ffn",
    )(blk_ea, blk_eb, n_used, src, src, ht, wrt, wg, wu, wd, wg, wu, wd)


def _post_kernel(*refs, final):
    if final:
        idx_ref, idxn_ref, y_hbm, zn_ref, mod_ref, fg_ref, o_ref, buf, sem = refs
    else:
        idx_ref, idxn_ref, y_hbm, zn_ref, mod_ref, o_ref, buf, sem = refs
    t = pl.program_id(0)
    nstep = pl.num_programs(0)
    slot = t % 2

    @pl.when(t == 0)
    def _():
        _gather_start(y_hbm, idx_ref, buf, sem, 0, TM)

    @pl.when(t + 1 < nstep)
    def _():
        _gather_start(y_hbm, idxn_ref, buf, sem, 1 - slot, TM)

    _gather_wait(y_hbm, buf, sem, slot)
    z = zn_ref[0] + mod_ref[0, 0][5:6] * _tiles_load(buf, TM, (slot,))
    if final:
        z = _rms(z, fg_ref[...])
    o_ref[0] = z


def _post(y, pos, zn, modl, nct, final_g=None):
    b_, lt, d = zn.shape
    nt = lt // TM
    final = final_g is not None
    t0 = nct if final else 0
    ntl = nt - t0
    nstep = b_ * ntl
    idx_spec = lambda f: pl.BlockSpec((1, 1, TM), f, memory_space=pltpu.SMEM)
    ins = [pos, pos, y, zn, modl]
    specs = [idx_spec(lambda t: (t, 0, 0)),
             idx_spec(lambda t: (jnp.minimum(t + 1, nstep - 1), 0, 0)),
             pl.BlockSpec(memory_space=pl.ANY),
             pl.BlockSpec((1, TM, d), lambda t: (t // ntl, t % ntl + t0, 0)),
             pl.BlockSpec((1, 1, SUB, d), lambda t: (t // ntl, jnp.minimum((t % ntl + t0) // nct, 1), 0, 0))]
    if final:
        ins.append(final_g)
        specs.append(pl.BlockSpec(final_g.shape, lambda t: (0, 0)))
    return pl.pallas_call(
        functools.partial(_post_kernel, final=final),
        grid=(nstep,),
        in_specs=specs,
        out_specs=pl.BlockSpec((1, TM, d), lambda t: (t // ntl, t % ntl, 0)),
        out_shape=jax.ShapeDtypeStruct((b_, ntl * TM, d), F32),
        scratch_shapes=[pltpu.VMEM((2, TM * SUB, LANE), F32), pltpu.SemaphoreType.DMA((2,))],
        compiler_params=_cparams(1),
        name="unsort_residual",
    )(*ins)


_PAIRS = [(0, 1), (0, 2), (0, 3), (1, 2), (1, 3), (2, 3)]
_CLS_EA = np.array([EPG * g + p[0] for g in range(N_GROUPS) for p in _PAIRS], np.int32)
_CLS_EB = np.array([EPG * g + p[1] for g in range(N_GROUPS) for p in _PAIRS], np.int32)


def _sort_plan(cls):
    n = cls.shape[0]
    nblk = n // MOE_BM + N_CLASSES
    onehot = (cls[:, None] == jnp.arange(N_CLASSES, dtype=jnp.int32)[None, :]).astype(jnp.int32)
    csum = jnp.cumsum(onehot, axis=0)
    counts = csum[-1]
    rank = jnp.take_along_axis(csum, cls[:, None], axis=1)[:, 0] - 1
    padded = (counts + MOE_BM - 1) // MOE_BM * MOE_BM
    pad_end = jnp.cumsum(padded)
    pos = (pad_end - padded)[cls] + rank
    src = jnp.zeros((nblk * MOE_BM,), jnp.int32).at[pos].set(jnp.arange(n, dtype=jnp.int32))
    blk_cls = jnp.minimum(
        jnp.searchsorted(pad_end, jnp.arange(nblk, dtype=jnp.int32) * MOE_BM, side='right'),
        N_CLASSES - 1).astype(jnp.int32)
    n_used = (pad_end[-1] // MOE_BM).astype(jnp.int32).reshape(1)
    return (pos, src.reshape(nblk, 1, MOE_BM), jnp.asarray(_CLS_EA)[blk_cls], jnp.asarray(_CLS_EB)[blk_cls],
            n_used)


def _rot_cols(w):
    return jnp.concatenate([-w[:, 8:16], w[:, 0:8], -w[:, 24:32], w[:, 16:24]], axis=1)


def _prep_w_in(w):
    kpe = w[:, 3200:3232]
    pad = jnp.zeros((w.shape[0], C_END - C_KPE - 2 * MLA_ROPE), w.dtype)
    return jnp.concatenate([w[:, :3200], w[:, 3232:], kpe, _rot_cols(kpe), pad], axis=1).astype(BF16)


def _prep_mla(w_uq, w_ukv):
    odd = (jnp.arange(MLA_HEADS) % 2 == 1)[None, :, None]
    qh = w_uq.reshape(MLA_Q_RANK, MLA_HEADS, MLA_NOPE + MLA_ROPE)
    nope, pe = qh[:, :, :MLA_NOPE], qh[:, :, MLA_NOPE:]
    pe_sw = _rot_cols(pe.reshape(MLA_Q_RANK * MLA_HEADS, MLA_ROPE)).reshape(MLA_Q_RANK, MLA_HEADS, MLA_ROPE)
    wq = jnp.where(odd, jnp.concatenate([nope, pe, pe_sw], axis=2), jnp.concatenate([pe, pe_sw, nope], axis=2))
    kvh = w_ukv.reshape(MLA_KV_RANK, MLA_HEADS, MLA_NOPE + MLA_V)
    kn, vh = kvh[:, :, :MLA_NOPE], kvh[:, :, MLA_NOPE:]
    wkv = jnp.where(odd, jnp.concatenate([kn, vh], axis=2), jnp.concatenate([vh, kn], axis=2))
    return (wq.reshape(MLA_Q_RANK, MLA_HEADS * LANE).astype(BF16),
            wkv.reshape(MLA_KV_RANK, MLA_HEADS * LANE).astype(BF16))


def _rope_tables(n_ctx, n_lat):
    rows = n_lat // GRID_W
    rowp = jnp.broadcast_to(jnp.arange(rows, dtype=F32)[:, None], (rows, GRID_W)).reshape(-1)
    colp = jnp.broadcast_to(jnp.arange(GRID_W, dtype=F32)[None, :], (rows, GRID_W)).reshape(-1)
    axis_dim = MLA_ROPE // 2
    inv_freq = ROPE_THETA ** (-jnp.arange(0, axis_dim, 2, dtype=F32) / axis_dim)
    ar, ac = rowp[:, None] * inv_freq, colp[:, None] * inv_freq
    cos32 = jnp.concatenate([jnp.cos(ar), jnp.cos(ar), jnp.cos(ac), jnp.cos(ac)], axis=1)
    sin32 = jnp.concatenate([jnp.sin(ar), jnp.sin(ar), jnp.sin(ac), jnp.sin(ac)], axis=1)
    cos32 = jnp.concatenate([jnp.ones((n_ctx, MLA_ROPE), F32), cos32], axis=0)
    sin32 = jnp.concatenate([jnp.zeros((n_ctx, MLA_ROPE), F32), sin32], axis=0)
    lt = n_ctx + n_lat
    one, zero = jnp.ones((lt, MLA_NOPE), F32), jnp.zeros((lt, MLA_NOPE), F32)
    z32 = jnp.zeros((lt, MLA_ROPE), F32)
    sc = MLA_SCALE * LOG2E
    cqe = jnp.concatenate([cos32, z32, one], axis=1) * sc
    sqe = jnp.concatenate([sin32, z32, zero], axis=1) * sc
    cqo = jnp.concatenate([one, cos32, z32], axis=1) * sc
    sqo = jnp.concatenate([zero, sin32, z32], axis=1) * sc
    kc = jnp.concatenate([cos32, sin32, zero], axis=1)
    return cqe, sqe, cqo, sqo, kc


def _lower_bounds(gamma):
    p = jnp.cumsum(jax.nn.softmax(gamma.astype(F32), axis=0), axis=0)
    return p - p[0:1]


def kernel(x, c, ctx, c_ctx, w_mod, b_mod, norm1_g, norm2_g, w_in, gamma_fwd, gamma_bwd, hg_norm_g,
           q_norm_g, kv_norm_g, w_uq, w_ukv, w_pa, w_pb, w_o, w_router, router_bias,
           w_gate_e, w_up_e, w_down_e, final_g):
    b_, n_lat, d = x.shape
    n_ctx = ctx.shape[1]
    depth = w_mod.shape[0]
    assert n_ctx % TM == 0 and n_lat % TM == 0 and n_lat % GRID_W == 0 and b_ + 1 <= SUB
    nct = n_ctx // TM
    lt = n_ctx + n_lat
    nt = lt // TM

    cc = jnp.concatenate([c, c_ctx[None, :], jnp.zeros((SUB - b_ - 1, d), F32)], axis=0)
    mod = _modulation(cc, w_mod, b_mod).reshape(depth, SUB, 6, d)
    mod_x = mod[:, :b_]
    mod_c = jnp.broadcast_to(mod[:, b_:b_ + 1], mod_x.shape)
    mod_t = jnp.stack([mod_c, mod_x], axis=2)
    mod_t = jnp.concatenate([mod_t, jnp.zeros((depth, b_, 2, SUB - 6, d), F32)], axis=3)

    cqe, sqe, cqo, sqo, kc = _rope_tables(n_ctx, n_lat)
    lbf, lbb = _lower_bounds(gamma_fwd), _lower_bounds(gamma_bwd)
    wr = jnp.concatenate([w_router, jnp.zeros((d, LANE - N_EXPERTS), F32)], axis=1)
    rbb = jnp.broadcast_to(router_bias.astype(F32)[:, None], (N_EXPERTS, TM))

    z = jnp.concatenate([ctx, x], axis=1)
    out = None
    for l in range(depth):
        last = l == depth - 1
        wq, wkv = _prep_mla(w_uq[l], w_ukv[l])
        (q, v, lff, kf, lfb, kb, g, gas, gbs, qq, kk, vv) = _inproj(
            z, mod_t[l], norm1_g[l][None], _prep_w_in(w_in[l]), lbf[l][None], lbb[l][None],
            q_norm_g[l][None], kv_norm_g[l][None], wq, wkv, cqe, sqe, cqo, sqo, kc, nct)
        o_f = _hgrn(q, v, lff, kf, nct, reverse=False)
        a = _hgrn(q, v, lfb, kb, nct, reverse=True, extra=(o_f, g, hg_norm_g[l][None]))
        m = _attention(qq, kk, vv, n_ctx)
        zn, ht, route = _merge(a, m, gas, gbs, z, mod_t[l], w_pa[l].astype(BF16), w_pb[l].astype(BF16),
                               w_o[l].astype(BF16), norm2_g[l][None], wr, rbb, nct)
        pos, src, blk_ea, blk_eb, n_used = _sort_plan(route[:, 0, :].reshape(-1).astype(jnp.int32))
        y = _moe(ht, src, w_router.T, blk_ea, blk_eb, n_used,
                 w_gate_e[l].astype(BF16), w_up_e[l].astype(BF16), w_down_e[l].astype(BF16))
        pos = pos.reshape(b_, nt, 1, TM)
        if last:
            out = _post(y, pos[:, nct:].reshape(-1, 1, TM), zn, mod_t[l], nct, final_g[None])
        else:
            z = _post(y, pos.reshape(-1, 1, TM), zn, mod_t[l], nct)
    return out
```

```python
import functools

import numpy as np
import jax
import jax.numpy as jnp
from jax import lax
from jax.experimental import pallas as pl
from jax.experimental.pallas import tpu as pltpu

F32 = jnp.float32
BF16 = jnp.bfloat16

EPS = 1e-6
GRID_W = 64
ROPE_THETA = 10000.0
HG_HEADS = 4
HG_DK = 128
HG_WIDTH = 512
MLA_HEADS = 8
MLA_NOPE = 64
MLA_ROPE = 32
MLA_V = 64
MLA_Q_RANK = 384
MLA_KV_RANK = 256
MLA_SCALE = (MLA_NOPE + MLA_ROPE) ** -0.5
LOG2E = 1.4426950408889634
N_EXPERTS = 16
N_GROUPS = 4
EPG = 4
N_CLASSES = N_GROUPS * 6

LANE = 128
SUB = 8
TM = 256
CHUNK = 128
MOE_BM = 256
VMEM_LIMIT = 56 * 1024 * 1024

C_Q, C_I, C_FF, C_FB, C_G, C_DQ, C_DKV, C_GA, C_GB, C_KPE, C_END = (
    0, 512, 1024, 1536, 2048, 2560, 2944, 3200, 4224, 5248, 5376)


def _cparams(n_axes):
    return pltpu.CompilerParams(dimension_semantics=("arbitrary",) * n_axes,
                                vmem_limit_bytes=VMEM_LIMIT)


def _rms(x, g):
    y = x * lax.rsqrt(jnp.mean(x * x, axis=-1, keepdims=True) + EPS)
    return y * g


def _dot(a, b):
    return jnp.dot(a, b, preferred_element_type=F32)


def _dot_nt(a, b):
    return lax.dot_general(a, b, (((1,), (1,)), ((), ())), preferred_element_type=F32)


def _dot_tn(a, b):
    return lax.dot_general(a, b, (((0,), (0,)), ((), ())), preferred_element_type=F32)


def _split3(x):
    hi = x.astype(BF16)
    r1 = x - hi.astype(F32)
    mid = r1.astype(BF16)
    lo = (r1 - mid.astype(F32)).astype(BF16)
    return hi, mid, lo


def _dot_f32(a, b, passes=6):
    a0, a1, a2 = _split3(a)
    b0, b1, b2 = _split3(b)
    out = _dot(a0, b0) + (_dot(a0, b1) + _dot(a1, b0))
    if passes == 6:
        out = out + (_dot(a0, b2) + _dot(a1, b1) + _dot(a2, b0))
    return out


def _mod_kernel(c_ref, w_ref, b_ref, o_ref):
    c = c_ref[...]
    s = c * jax.nn.sigmoid(c)
    o_ref[0] = _dot_f32(s, w_ref[0], passes=3) + b_ref[0]


def _modulation(cc, w_mod, b_mod):
    depth, d, n6 = w_mod.shape
    nb = 1536
    return pl.pallas_call(
        _mod_kernel,
        grid=(depth, n6 // nb),
        in_specs=[pl.BlockSpec((SUB, d), lambda l, j: (0, 0)),
                  pl.BlockSpec((1, d, nb), lambda l, j: (l, 0, j)),
                  pl.BlockSpec((1, 1, nb), lambda l, j: (l, 0, j))],
        out_specs=pl.BlockSpec((1, SUB, nb), lambda l, j: (l, 0, j)),
        out_shape=jax.ShapeDtypeStruct((depth, SUB, n6), F32),
        compiler_params=_cparams(2),
        name="modulation",
    )(cc, w_mod, b_mod.reshape(depth, 1, n6))


def _inproj_kernel(z_ref, mod_ref, g1_ref, w_ref, lbf_ref, lbb_ref, qg_ref, kvg_ref,
                   wq_ref, wkv_ref, cqe_ref, sqe_ref, cqo_ref, sqo_ref, kc_ref,
                   q_o, v_o, lff_o, kf_o, lfb_o, kb_o, g_o, gas_o, gbs_o, qq_o, kk_o, vv_o):
    mod = mod_ref[0, 0]
    h = _rms(z_ref[0], g1_ref[...]) * (1.0 + mod[1:2]) + mod[0:1]
    hb = h.astype(BF16)

    p = _dot(hb, w_ref[...])

    def seg(a, b):
        return p[:, a:b]

    q_o[0] = seg(C_Q, C_I).astype(BF16)
    v_o[0] = seg(C_I, C_FF).astype(BF16)
    for a, lb_ref, lf_o, k_o in ((C_FF, lbf_ref, lff_o, kf_o), (C_FB, lbb_ref, lfb_o, kb_o)):
        lb = lb_ref[...]
        f = lb + (1.0 - lb) * jax.nn.sigmoid(seg(a, a + HG_WIDTH))
        lf_o[0] = jnp.log(f) * LOG2E
        k_o[0] = (1.0 - f).astype(BF16)
    g_o[0] = seg(C_G, C_DQ).astype(BF16)
    gas_o[0] = jax.nn.sigmoid(seg(C_GA, C_GB)).astype(BF16)
    gbs_o[0] = jax.nn.sigmoid(seg(C_GB, C_KPE)).astype(BF16)

    lane = lax.broadcasted_iota(jnp.int32, (TM, LANE), 1)
    qn = _rms(seg(C_DQ, C_DKV), qg_ref[...]).astype(BF16)
    qa = _dot(qn, wq_ref[...])
    tabs = ((cqe_ref[...], sqe_ref[...]), (cqo_ref[...], sqo_ref[...]))
    for hh in range(MLA_HEADS):
        sl = slice(hh * LANE, (hh + 1) * LANE)
        cq, sq = tabs[hh % 2]
        blk = qa[:, sl]
        qq_o[0, :, sl] = (blk * cq + pltpu.roll(blk, LANE - MLA_ROPE, 1) * sq).astype(BF16)
    kvn = _rms(seg(C_DKV, C_GA), kvg_ref[...]).astype(BF16)
    kv = _dot(kvn, wkv_ref[...])
    kp = seg(C_KPE, C_END) * kc_ref[...]
    kpe_e = jnp.where(lane < MLA_ROPE, kp + pltpu.roll(kp, LANE - MLA_ROPE, 1), 0.0)
    kpe_o = pltpu.roll(kpe_e, MLA_NOPE, 1)
    one_e = jnp.where(lane == MLA_V, 1.0, 0.0)
    one_o = jnp.where(lane == 0, 1.0, 0.0)
    for hh in range(MLA_HEADS):
        sl = slice(hh * LANE, (hh + 1) * LANE)
        blk = kv[:, sl]
        if hh % 2 == 0:
            kk_o[0, :, sl] = jnp.where(lane >= MLA_V, blk, kpe_e).astype(BF16)
            vv_o[0, :, sl] = jnp.where(lane < MLA_V, blk, one_e).astype(BF16)
        else:
            kk_o[0, :, sl] = jnp.where(lane < MLA_NOPE, blk, kpe_o).astype(BF16)
            vv_o[0, :, sl] = jnp.where(lane >= MLA_NOPE, blk, one_o).astype(BF16)


def _inproj(z, modl, g1, w, lbf, lbb, qg, kvg, wq, wkv, cqe, sqe, cqo, sqo, kc, nct):
    b_, lt, d = z.shape
    nt = lt // TM
    row = lambda w_: pl.BlockSpec((1, TM, w_), lambda b, i: (b, i, 0))
    full = lambda a: pl.BlockSpec(a.shape, lambda b, i: (0,) * a.ndim)
    tab = pl.BlockSpec((TM, LANE), lambda b, i: (i, 0))
    f32o = lambda w_: jax.ShapeDtypeStruct((b_, lt, w_), F32)
    bf16o = lambda w_: jax.ShapeDtypeStruct((b_, lt, w_), BF16)
    return pl.pallas_call(
        _inproj_kernel,
        grid=(b_, nt),
        in_specs=[row(d),
                  pl.BlockSpec((1, 1, SUB, d), lambda b, i: (b, jnp.minimum(i // nct, 1), 0, 0)),
                  full(g1), full(w), full(lbf), full(lbb), full(qg), full(kvg),
                  full(wq), full(wkv), tab, tab, tab, tab, tab],
        out_specs=[row(512)] * 7 + [row(1024)] * 5,
        out_shape=[bf16o(512), bf16o(512), f32o(512), bf16o(512), f32o(512), bf16o(512), bf16o(512)]
        + [bf16o(1024)] * 5,
        compiler_params=_cparams(2),
        name="inproj",
    )(z, modl, g1, w, lbf, lbb, qg, kvg, wq, wkv, cqe, sqe, cqo, sqo, kc)


N_LEVELS = 7


def _chunk_tables(reverse):
    c = CHUNK
    t = np.arange(c)[:, None]
    u = np.arange(c)[None, :]
    tri = ((u >= t) if reverse else (u <= t)).astype(np.float32)
    lvl = np.full((c, c), N_LEVELS + 1, np.int32)
    lvl[t == u] = 0
    for l in range(N_LEVELS, 0, -1):
        m = 1 << l
        same = (t // m) == (u // m)
        lvl[same & ((u > t) if reverse else (u < t))] = l
    return jnp.asarray(tri, dtype=BF16), jnp.asarray(lvl)


def _level_ref(bs_ref, l, reverse):
    w = bs_ref.shape[1]
    sub = lax.broadcasted_iota(jnp.int32, (SUB, w), 0)
    off = 0 if reverse else -1
    m = 1 << l

    def row(r):
        return jnp.broadcast_to(bs_ref[r:r + 1, :], (SUB, w))

    pieces = []
    for j in range(CHUNK // SUB):
        base = j * SUB
        if m >= SUB:
            pieces.append(row((base // m) * m + m // 2 + off))
        else:
            p = row(base + m // 2 + off)
            for i in range(1, SUB // m):
                p = jnp.where(sub < i * m, p, row(base + i * m + m // 2 + off))
            pieces.append(p)
    return jnp.concatenate(pieces, axis=0)


def _hgrn_kernel(*refs, reverse, readout):
    if readout:
        (q_ref, v_ref, lf_ref, k_ref, tri_ref, lvl_ref, of_ref, g_ref, gain_ref, o_ref,
         s_ref, bs_ref, qs_ref, ks_ref) = refs
    else:
        q_ref, v_ref, lf_ref, k_ref, tri_ref, lvl_ref, o_ref, s_ref, bs_ref, qs_ref, ks_ref = refs

    @pl.when(pl.program_id(1) == 0)
    def _():
        s_ref[...] = jnp.zeros_like(s_ref)

    c = CHUNK
    nch = TM // c
    e = lambda x: jnp.exp2(x).astype(BF16)

    lvl = lvl_ref[...]
    for ci in range(nch):
        cidx = (nch - 1 - ci) if reverse else ci
        rows = slice(cidx * c, (cidx + 1) * c)
        lf = lf_ref[0, rows, :]
        hi = lf.astype(BF16)
        lo = (lf - hi.astype(F32)).astype(BF16)
        bsc = bs_ref.at[ci]
        bsc[...] = _dot(tri_ref[...], hi) + _dot(tri_ref[...], lo)
        b = bsc[...]
        btot = b[0:1] if reverse else b[c - 1:c]
        q = q_ref[0, rows, :]
        k = k_ref[0, rows, :]
        qs_ref[ci, 0] = q * e(b)
        ks_ref[ci, 0] = k * e(btot - b)
        for l in range(1, N_LEVELS + 1):
            d = b - _level_ref(bsc, l, reverse)
            qs_ref[ci, l] = q * e(d)
            ks_ref[ci, l] = k * e(-d)
        for hh in range(HG_HEADS):
            hsl = slice(hh * HG_DK, (hh + 1) * HG_DK)
            ix = (0, rows, hsl)
            v = v_ref[ix]
            st = s_ref[hh]
            att = jnp.where(lvl == 0, _dot_nt(q[:, hsl], k[:, hsl]), 0.0)
            for l in range(1, N_LEVELS + 1):
                att = jnp.where(lvl == l, _dot_nt(qs_ref[ci, l, :, hsl], ks_ref[ci, l, :, hsl]), att)
            o = _dot_nt(qs_ref[ci, 0, :, hsl], st.astype(BF16)) + _dot(att.astype(BF16), v)
            s_ref[hh] = st * jnp.exp2(btot[:, hsl]) + _dot_tn(v, ks_ref[ci, 0, :, hsl])
            if readout:
                o = o + of_ref[ix]
                o = o * lax.rsqrt(jnp.mean(o * o, axis=-1, keepdims=True) + EPS)
                o = o * gain_ref[:, hsl]
                g = g_ref[ix].astype(F32)
                o = o * (g * jax.nn.sigmoid(g))
            o_ref[ix] = o.astype(o_ref.dtype)


def _hgrn(q, v, lf, k, nct, reverse, extra=None):
    b_, lt, w = q.shape
    nt = lt // TM
    if reverse:
        tile = lambda i: jnp.where(i < nct, nct - 1 - i, nt - 1 - (i - nct))
    else:
        tile = lambda i: i
    row = pl.BlockSpec((1, TM, w), lambda b, i: (b, tile(i), 0))
    full = lambda a: pl.BlockSpec(a.shape, lambda b, i: (0,) * a.ndim)
    tri, lvl = _chunk_tables(reverse)
    ins = [q, v, lf, k, tri, lvl]
    specs = [row] * 4 + [full(tri), full(lvl)]
    if extra is not None:
        o_f, g, gain = extra
        ins += [o_f, g, gain]
        specs += [row, row, full(gain)]
    return pl.pallas_call(
        functools.partial(_hgrn_kernel, reverse=reverse, readout=extra is not None),
        grid=(b_, nt),
        in_specs=specs,
        out_specs=row,
        out_shape=jax.ShapeDtypeStruct((b_, lt, w), BF16),
        scratch_shapes=[pltpu.VMEM((HG_HEADS, HG_DK, HG_DK), F32), pltpu.VMEM((TM // CHUNK, CHUNK, w), F32),
                        pltpu.VMEM((TM // CHUNK, N_LEVELS + 1, CHUNK, w), BF16),
                        pltpu.VMEM((TM // CHUNK, N_LEVELS + 1, CHUNK, w), BF16)],
        compiler_params=_cparams(2),
        name="hgrn_bwd" if reverse else "hgrn_fwd",
    )(*ins)


KV_CHUNK = 256


def _attn_kernel(q_ref, k_ref, v_ref, o_ref, *, n_ctx, nct):
    i = pl.program_id(2)
    lane = lax.broadcasted_iota(jnp.int32, (TM, LANE), 1)

    def run(nk):
        outs = []
        for hh in range(2):
            sl = slice(hh * LANE, (hh + 1) * LANE)
            q = q_ref[0, :, sl]
            m = acc = None
            edges = [0] + list(range(n_ctx, nk + 1, KV_CHUNK))
            for c in range(len(edges) - 1):
                ks = slice(edges[c], edges[c + 1])
                s = _dot_nt(q, k_ref[0, ks, sl])
                mc = jnp.max(s, axis=-1, keepdims=True)
                if c == 0:
                    m = mc
                    acc = _dot(jnp.exp2(s - m).astype(BF16), v_ref[0, ks, sl])
                else:
                    m_new = jnp.maximum(m, mc)
                    acc = acc * jnp.exp2(m - m_new) + _dot(jnp.exp2(s - m_new).astype(BF16), v_ref[0, ks, sl])
                    m = m_new
            den = acc[:, MLA_V:MLA_V + 1] if hh == 0 else acc[:, 0:1]
            outs.append(acc / den)
        o_ref[0] = jnp.where(lane < MLA_V, outs[0], outs[1]).astype(o_ref.dtype)

    @pl.when(i < nct)
    def _():
        run(n_ctx)

    @pl.when(i >= nct)
    def _():
        run(k_ref.shape[1])


def _attention(qq, kk, vv, n_ctx):
    b_, lt, _ = qq.shape
    nt = lt // TM
    npair = MLA_HEADS // 2
    return pl.pallas_call(
        functools.partial(_attn_kernel, n_ctx=n_ctx, nct=n_ctx // TM),
        grid=(b_, npair, nt),
        in_specs=[pl.BlockSpec((1, TM, 2 * LANE), lambda b, j, i: (b, i, j)),
                  pl.BlockSpec((1, lt, 2 * LANE), lambda b, j, i: (b, 0, j)),
                  pl.BlockSpec((1, lt, 2 * LANE), lambda b, j, i: (b, 0, j))],
        out_specs=pl.BlockSpec((1, TM, LANE), lambda b, j, i: (b, i, j)),
        out_shape=jax.ShapeDtypeStruct((b_, lt, npair * LANE), BF16),
        compiler_params=_cparams(3),
        name="mla_attention",
    )(qq, kk, vv)


def _route(sel):
    r = lambda x, e: x[e:e + 1]
    best = None
    for g in range(N_GROUPS):
        u = [r(sel, EPG * g + j) for j in range(EPG)]
        gs = None
        for a in range(EPG):
            for b in range(a + 1, EPG):
                pr = u[a] + u[b]
                gs = pr if gs is None else jnp.maximum(gs, pr)
        if best is None:
            best, gi = gs, jnp.zeros_like(gs, dtype=jnp.int32)
        else:
            upd = gs > best
            best = jnp.where(upd, gs, best)
            gi = jnp.where(upd, g, gi)

    def pick(x, j):
        out = r(x, j)
        for g in range(1, N_GROUPS):
            out = jnp.where(gi == g, r(x, EPG * g + j), out)
        return out

    u = [pick(sel, j) for j in range(EPG)]

    def argmax4(vals):
        bv, bi = vals[0], jnp.zeros_like(gi)
        for j in range(1, EPG):
            upd = vals[j] > bv
            bv = jnp.where(upd, vals[j], bv)
            bi = jnp.where(upd, j, bi)
        return bi

    l1 = argmax4(u)
    l2 = argmax4([jnp.where(l1 == j, -jnp.inf, u[j]) for j in range(EPG)])
    lo = jnp.minimum(l1, l2)
    hi = jnp.maximum(l1, l2)
    pair = jnp.where(lo == 0, hi - 1, jnp.where(lo == 1, hi + 1, 5))
    cls = (gi * 6 + pair).astype(F32)
    return jnp.concatenate([cls] + [jnp.zeros_like(cls)] * (SUB - 1), axis=0)


def _merge_kernel(a_ref, m_ref, gas_ref, gbs_ref, z_ref, mod_ref, wpa_ref, wpb_ref, wo_ref,
                  g2_ref, wr_ref, rb_ref, zn_o, ht_o, route_o):
    mod = mod_ref[0, 0]
    pa = _dot(a_ref[0].astype(BF16), wpa_ref[...])
    pb = _dot(m_ref[0].astype(BF16), wpb_ref[...])
    mixed = gas_ref[0] * pa + gbs_ref[0] * pb
    zn = z_ref[0] + mod[2:3] * _dot(mixed.astype(BF16), wo_ref[...])
    zn_o[0] = zn
    h2 = _rms(zn, g2_ref[...]) * (1.0 + mod[4:5]) + mod[3:4]
    _tiles_store(ht_o, h2)
    logits = _dot_f32(h2, wr_ref[...], passes=3)
    sc = jax.nn.sigmoid(logits.T[:N_EXPERTS])
    route_o[0] = _route(sc + rb_ref[...])


def _merge(a, m, gas, gbs, z, modl, wpa, wpb, wo, g2, wr, rbb, nct):
    b_, lt, d = z.shape
    nt = lt // TM
    row = lambda w_: pl.BlockSpec((1, TM, w_), lambda b, i: (b, i, 0))
    full = lambda x: pl.BlockSpec(x.shape, lambda b, i: (0,) * x.ndim)
    return pl.pallas_call(
        _merge_kernel,
        grid=(b_, nt),
        in_specs=[row(512), row(512), row(d), row(d), row(d),
                  pl.BlockSpec((1, 1, SUB, d), lambda b, i: (b, jnp.minimum(i // nct, 1), 0, 0)),
                  full(wpa), full(wpb), full(wo), full(g2), full(wr), full(rbb)],
        out_specs=[row(d), pl.BlockSpec((TM * SUB, LANE), lambda b, i: (b * nt + i, 0)),
                   pl.BlockSpec((1, SUB, TM), lambda b, i: (b, 0, i))],
        out_shape=[jax.ShapeDtypeStruct((b_, lt, d), F32),
                   jax.ShapeDtypeStruct((b_ * lt * SUB, LANE), F32),
                   jax.ShapeDtypeStruct((b_, SUB, lt), F32)],
        compiler_params=_cparams(2),
        name="merge_route",
    )(a, m, gas, gbs, z, modl, wpa, wpb, wo, g2, wr, rbb)


def _tiles_store(ref, x, lead=()):
    n = x.shape[0]
    for j in range(SUB):
        ref[lead + (pl.ds(j, n, stride=SUB), slice(None))] = x[:, j * LANE:(j + 1) * LANE]


def _tiles_load(ref, n, lead=()):
    return jnp.concatenate([ref[lead + (pl.ds(j, n, stride=SUB), slice(None))] for j in range(SUB)], axis=1)


def _tile_copy(src_hbm, idx_ref, buf, sem, slot, r):
    return pltpu.make_async_copy(src_hbm.at[pl.ds(pl.multiple_of(idx_ref[0, 0, r] * SUB, SUB), SUB)],
                                 buf.at[slot, pl.ds(pl.multiple_of(r * SUB, SUB), SUB)], sem.at[slot])


GATHER_UNROLL = 8


def _gather_start(src_hbm, idx_ref, buf, sem, slot, n):
    def body(r8, c):
        for u in range(GATHER_UNROLL):
            _tile_copy(src_hbm, idx_ref, buf, sem, slot, r8 * GATHER_UNROLL + u).start()
        return c
    lax.fori_loop(0, n // GATHER_UNROLL, body, 0)


def _gather_wait(src_hbm, buf, sem, slot):
    pltpu.make_async_copy(src_hbm.at[pl.ds(0, buf.shape[1])], buf.at[slot], sem.at[slot]).wait()


def _sort_kernel(pos_ref, ht_hbm, init_hbm, hs_hbm, sem):
    del init_hbm
    t = pl.program_id(0)
    nstep = pl.num_programs(0)
    slot = t % 2

    def wait(s):
        pltpu.make_async_copy(ht_hbm.at[pl.ds(0, TM * SUB)], hs_hbm.at[pl.ds(0, TM * SUB)], sem.at[s]).wait()

    def body(r8, c):
        for u in range(GATHER_UNROLL):
            r = r8 * GATHER_UNROLL + u
            pltpu.make_async_copy(
                ht_hbm.at[pl.ds(pl.multiple_of((t * TM + r) * SUB, SUB), SUB)],
                hs_hbm.at[pl.ds(pl.multiple_of(pos_ref[0, 0, r] * SUB, SUB), SUB)], sem.at[slot]).start()
        return c
    lax.fori_loop(0, TM // GATHER_UNROLL, body, 0)

    @pl.when(t > 0)
    def _():
        wait(1 - slot)

    @pl.when(t == nstep - 1)
    def _():
        wait(slot)


def _sort_rows(ht, pos, n_sorted):
    nstep = pos.shape[0]
    init = jnp.zeros((n_sorted * SUB, LANE), F32)
    return pl.pallas_call(
        _sort_kernel,
        grid=(nstep,),
        in_specs=[pl.BlockSpec((1, 1, TM), lambda t: (t, 0, 0), memory_space=pltpu.SMEM),
                  pl.BlockSpec(memory_space=pl.ANY), pl.BlockSpec(memory_space=pl.ANY)],
        out_specs=pl.BlockSpec(memory_space=pl.ANY),
        out_shape=jax.ShapeDtypeStruct(init.shape, F32),
        scratch_shapes=[pltpu.SemaphoreType.DMA((2,))],
        input_output_aliases={2: 0},
        compiler_params=_cparams(1),
        name="class_sort",
    )(pos, ht, init)


def _moe_kernel(ea_ref, eb_ref, nu_ref, hs_ref, wrt_ref,
                wga_ref, wua_ref, wda_ref, wgb_ref, wub_ref, wdb_ref, y_ref):
    j = pl.program_id(0)

    @pl.when(j < nu_ref[0])
    def _():
        x32 = _tiles_load(hs_ref, MOE_BM)
        x = x32.astype(BF16)

        def ffn(wg, wu, wd, e):
            sc = jax.nn.sigmoid(jnp.sum(x32 * wrt_ref[pl.ds(e, 1), :], axis=-1, keepdims=True))
            act = jax.nn.silu(_dot(x, wg[0])) * _dot(x, wu[0])
            return sc, _dot(act.astype(BF16), wd[0])

        s_lo, y_lo = ffn(wga_ref, wua_ref, wda_ref, ea_ref[j])
        s_hi, y_hi = ffn(wgb_ref, wub_ref, wdb_ref, eb_ref[j])
        inv = 1.0 / (s_lo + s_hi)
        _tiles_store(y_ref, (s_lo * inv) * y_lo + (s_hi * inv) * y_hi)

    @pl.when(j >= nu_ref[0])
    def _():
        y_ref[...] = jnp.zeros_like(y_ref)


def _moe(hs, wrt, blk_ea, blk_eb, n_used, wg, wu, wd):
    nblk = blk_ea.shape[0]
    d, de = wg.shape[1:]
    tiles = pl.BlockSpec((MOE_BM * SUB, LANE), lambda j, ea, eb, nu: (j, 0))
    wspec = lambda shp, which: pl.BlockSpec(
        (1,) + shp, (lambda j, ea, eb, nu: (ea[j], 0, 0)) if which == 0 else (lambda j, ea, eb, nu: (eb[j], 0, 0)))
    return pl.pallas_call(
        _moe_kernel,
        grid_spec=pltpu.PrefetchScalarGridSpec(
            num_scalar_prefetch=3,
            grid=(nblk,),
            in_specs=[tiles,
                      pl.BlockSpec(wrt.shape, lambda j, ea, eb, nu: (0, 0)),
                      wspec((d, de), 0), wspec((d, de), 0), wspec((de, d), 0),
                      wspec((d, de), 1), wspec((d, de), 1), wspec((de, d), 1)],
            out_specs=tiles),
        out_shape=jax.ShapeDtypeStruct(hs.shape, F32),
        compiler_params=_cparams(1),
        name="moe_ffn",
    )(blk_ea, blk_eb, n_used, hs, wrt, wg, wu, wd, wg, wu, wd)


def _post_kernel(*refs, final):
    if final:
        idx_ref, idxn_ref, y_hbm, zn_ref, mod_ref, fg_ref, o_ref, buf, sem = refs
    else:
        idx_ref, idxn_ref, y_hbm, zn_ref, mod_ref, o_ref, buf, sem = refs
    t = pl.program_id(0)
    nstep = pl.num_programs(0)
    slot = t % 2

    @pl.when(t == 0)
    def _():
        _gather_start(y_hbm, idx_ref, buf, sem, 0, TM)

    @pl.when(t + 1 < nstep)
    def _():
        _gather_start(y_hbm, idxn_ref, buf, sem, 1 - slot, TM)

    _gather_wait(y_hbm, buf, sem, slot)
    z = zn_ref[0] + mod_ref[0, 0][5:6] * _tiles_load(buf, TM, (slot,))
    if final:
        z = _rms(z, fg_ref[...])
    o_ref[0] = z


def _post(y, pos, zn, modl, nct, final_g=None):
    b_, lt, d = zn.shape
    nt = lt // TM
    final = final_g is not None
    t0 = nct if final else 0
    ntl = nt - t0
    nstep = b_ * ntl
    idx_spec = lambda f: pl.BlockSpec((1, 1, TM), f, memory_space=pltpu.SMEM)
    ins = [pos, pos, y, zn, modl]
    specs = [idx_spec(lambda t: (t, 0, 0)),
             idx_spec(lambda t: (jnp.minimum(t + 1, nstep - 1), 0, 0)),
             pl.BlockSpec(memory_space=pl.ANY),
             pl.BlockSpec((1, TM, d), lambda t: (t // ntl, t % ntl + t0, 0)),
             pl.BlockSpec((1, 1, SUB, d), lambda t: (t // ntl, jnp.minimum((t % ntl + t0) // nct, 1), 0, 0))]
    if final:
        ins.append(final_g)
        specs.append(pl.BlockSpec(final_g.shape, lambda t: (0, 0)))
    return pl.pallas_call(
        functools.partial(_post_kernel, final=final),
        grid=(nstep,),
        in_specs=specs,
        out_specs=pl.BlockSpec((1, TM, d), lambda t: (t // ntl, t % ntl, 0)),
        out_shape=jax.ShapeDtypeStruct((b_, ntl * TM, d), F32),
        scratch_shapes=[pltpu.VMEM((2, TM * SUB, LANE), F32), pltpu.SemaphoreType.DMA((2,))],
        compiler_params=_cparams(1),
        name="unsort_residual",
    )(*ins)


_PAIRS = [(0, 1), (0, 2), (0, 3), (1, 2), (1, 3), (2, 3)]
_CLS_EA = np.array([EPG * g + p[0] for g in range(N_GROUPS) for p in _PAIRS], np.int32)
_CLS_EB = np.array([EPG * g + p[1] for g in range(N_GROUPS) for p in _PAIRS], np.int32)


def _sort_plan(cls):
    n = cls.shape[0]
    nblk = n // MOE_BM + N_CLASSES
    onehot = (cls[:, None] == jnp.arange(N_CLASSES, dtype=jnp.int32)[None, :]).astype(jnp.int32)
    csum = jnp.cumsum(onehot, axis=0)
    counts = csum[-1]
    rank = jnp.take_along_axis(csum, cls[:, None], axis=1)[:, 0] - 1
    padded = (counts + MOE_BM - 1) // MOE_BM * MOE_BM
    pad_end = jnp.cumsum(padded)
    pos = (pad_end - padded)[cls] + rank
    blk_cls = jnp.minimum(
        jnp.searchsorted(pad_end, jnp.arange(nblk, dtype=jnp.int32) * MOE_BM, side='right'),
        N_CLASSES - 1).astype(jnp.int32)
    n_used = (pad_end[-1] // MOE_BM).astype(jnp.int32).reshape(1)
    return pos, jnp.asarray(_CLS_EA)[blk_cls], jnp.asarray(_CLS_EB)[blk_cls], n_used


def _rot_cols(w):
    return jnp.concatenate([-w[:, 8:16], w[:, 0:8], -w[:, 24:32], w[:, 16:24]], axis=1)


def _prep_w_in(w):
    kpe = w[:, 3200:3232]
    pad = jnp.zeros((w.shape[0], C_END - C_KPE - 2 * MLA_ROPE), w.dtype)
    return jnp.concatenate([w[:, :3200], w[:, 3232:], kpe, _rot_cols(kpe), pad], axis=1).astype(BF16)


def _prep_mla(w_uq, w_ukv):
    odd = (jnp.arange(MLA_HEADS) % 2 == 1)[None, :, None]
    qh = w_uq.reshape(MLA_Q_RANK, MLA_HEADS, MLA_NOPE + MLA_ROPE)
    nope, pe = qh[:, :, :MLA_NOPE], qh[:, :, MLA_NOPE:]
    pe_sw = _rot_cols(pe.reshape(MLA_Q_RANK * MLA_HEADS, MLA_ROPE)).reshape(MLA_Q_RANK, MLA_HEADS, MLA_ROPE)
    wq = jnp.where(odd, jnp.concatenate([nope, pe, pe_sw], axis=2), jnp.concatenate([pe, pe_sw, nope], axis=2))
    kvh = w_ukv.reshape(MLA_KV_RANK, MLA_HEADS, MLA_NOPE + MLA_V)
    kn, vh = kvh[:, :, :MLA_NOPE], kvh[:, :, MLA_NOPE:]
    wkv = jnp.where(odd, jnp.concatenate([kn, vh], axis=2), jnp.concatenate([vh, kn], axis=2))
    return (wq.reshape(MLA_Q_RANK, MLA_HEADS * LANE).astype(BF16),
            wkv.reshape(MLA_KV_RANK, MLA_HEADS * LANE).astype(BF16))


def _rope_tables(n_ctx, n_lat):
    rows = n_lat // GRID_W
    rowp = jnp.broadcast_to(jnp.arange(rows, dtype=F32)[:, None], (rows, GRID_W)).reshape(-1)
    colp = jnp.broadcast_to(jnp.arange(GRID_W, dtype=F32)[None, :], (rows, GRID_W)).reshape(-1)
    axis_dim = MLA_ROPE // 2
    inv_freq = ROPE_THETA ** (-jnp.arange(0, axis_dim, 2, dtype=F32) / axis_dim)
    ar, ac = rowp[:, None] * inv_freq, colp[:, None] * inv_freq
    cos32 = jnp.concatenate([jnp.cos(ar), jnp.cos(ar), jnp.cos(ac), jnp.cos(ac)], axis=1)
    sin32 = jnp.concatenate([jnp.sin(ar), jnp.sin(ar), jnp.sin(ac), jnp.sin(ac)], axis=1)
    cos32 = jnp.concatenate([jnp.ones((n_ctx, MLA_ROPE), F32), cos32], axis=0)
    sin32 = jnp.concatenate([jnp.zeros((n_ctx, MLA_ROPE), F32), sin32], axis=0)
    lt = n_ctx + n_lat
    one, zero = jnp.ones((lt, MLA_NOPE), F32), jnp.zeros((lt, MLA_NOPE), F32)
    z32 = jnp.zeros((lt, MLA_ROPE), F32)
    sc = MLA_SCALE * LOG2E
    cqe = jnp.concatenate([cos32, z32, one], axis=1) * sc
    sqe = jnp.concatenate([sin32, z32, zero], axis=1) * sc
    cqo = jnp.concatenate([one, cos32, z32], axis=1) * sc
    sqo = jnp.concatenate([zero, sin32, z32], axis=1) * sc
    kc = jnp.concatenate([cos32, sin32, zero], axis=1)
    return cqe, sqe, cqo, sqo, kc


def _lower_bounds(gamma):
    p = jnp.cumsum(jax.nn.softmax(gamma.astype(F32), axis=0), axis=0)
    return p - p[0:1]


def kernel(x, c, ctx, c_ctx, w_mod, b_mod, norm1_g, norm2_g, w_in, gamma_fwd, gamma_bwd, hg_norm_g,
           q_norm_g, kv_norm_g, w_uq, w_ukv, w_pa, w_pb, w_o, w_router, router_bias,
           w_gate_e, w_up_e, w_down_e, final_g):
    b_, n_lat, d = x.shape
    n_ctx = ctx.shape[1]
    depth = w_mod.shape[0]
    assert n_ctx % TM == 0 and n_lat % TM == 0 and n_lat % GRID_W == 0 and b_ + 1 <= SUB
    nct = n_ctx // TM
    lt = n_ctx + n_lat
    nt = lt // TM

    cc = jnp.concatenate([c, c_ctx[None, :], jnp.zeros((SUB - b_ - 1, d), F32)], axis=0)
    mod = _modulation(cc, w_mod, b_mod).reshape(depth, SUB, 6, d)
    mod_x = mod[:, :b_]
    mod_c = jnp.broadcast_to(mod[:, b_:b_ + 1], mod_x.shape)
    mod_t = jnp.stack([mod_c, mod_x], axis=2)
    mod_t = jnp.concatenate([mod_t, jnp.zeros((depth, b_, 2, SUB - 6, d), F32)], axis=3)

    cqe, sqe, cqo, sqo, kc = _rope_tables(n_ctx, n_lat)
    lbf, lbb = _lower_bounds(gamma_fwd), _lower_bounds(gamma_bwd)
    wr = jnp.concatenate([w_router, jnp.zeros((d, LANE - N_EXPERTS), F32)], axis=1)
    rbb = jnp.broadcast_to(router_bias.astype(F32)[:, None], (N_EXPERTS, TM))

    z = jnp.concatenate([ctx, x], axis=1)
    out = None
    for l in range(depth):
        last = l == depth - 1
        wq, wkv = _prep_mla(w_uq[l], w_ukv[l])
        (q, v, lff, kf, lfb, kb, g, gas, gbs, qq, kk, vv) = _inproj(
            z, mod_t[l], norm1_g[l][None], _prep_w_in(w_in[l]), lbf[l][None], lbb[l][None],
            q_norm_g[l][None], kv_norm_g[l][None], wq, wkv, cqe, sqe, cqo, sqo, kc, nct)
        o_f = _hgrn(q, v, lff, kf, nct, reverse=False)
        a = _hgrn(q, v, lfb, kb, nct, reverse=True, extra=(o_f, g, hg_norm_g[l][None]))
        m = _attention(qq, kk, vv, n_ctx)
        zn, ht, route = _merge(a, m, gas, gbs, z, mod_t[l], w_pa[l].astype(BF16), w_pb[l].astype(BF16),
                               w_o[l].astype(BF16), norm2_g[l][None], wr, rbb, nct)
        pos, blk_ea, blk_eb, n_used = _sort_plan(route[:, 0, :].reshape(-1).astype(jnp.int32))
        pos = pos.reshape(b_, nt, 1, TM)
        hs = _sort_rows(ht, pos.reshape(-1, 1, TM), blk_ea.shape[0] * MOE_BM)
        y = _moe(hs, w_router.T, blk_ea, blk_eb, n_used,
                 w_gate_e[l].astype(BF16), w_up_e[l].astype(BF16), w_down_e[l].astype(BF16))
        if last:
            out = _post(y, pos[:, nct:].reshape(-1, 1, TM), zn, mod_t[l], nct, final_g[None])
        else:
            z = _post(y, pos.reshape(-1, 1, TM), zn, mod_t[l], nct)
    return out
```

```python
import functools

import numpy as np
import jax
import jax.numpy as jnp
from jax import lax
from jax.experimental import pallas as pl
from jax.experimental.pallas import tpu as pltpu

F32 = jnp.float32
BF16 = jnp.bfloat16

EPS = 1e-6
GRID_W = 64
ROPE_THETA = 10000.0
HG_HEADS = 4
HG_DK = 128
HG_WIDTH = 512
MLA_HEADS = 8
MLA_NOPE = 64
MLA_ROPE = 32
MLA_V = 64
MLA_Q_RANK = 384
MLA_KV_RANK = 256
MLA_SCALE = (MLA_NOPE + MLA_ROPE) ** -0.5
LOG2E = 1.4426950408889634
N_EXPERTS = 16
N_GROUPS = 4
EPG = 4
N_CLASSES = N_GROUPS * 6

LANE = 128
SUB = 8
TM = 256
CHUNK = 128
MOE_BM = 256
VMEM_LIMIT = 56 * 1024 * 1024

C_Q, C_I, C_FF, C_FB, C_G, C_DQ, C_DKV, C_GA, C_GB, C_KPE, C_END = (
    0, 512, 1024, 1536, 2048, 2560, 2944, 3200, 4224, 5248, 5376)


def _cparams(n_axes):
    return pltpu.CompilerParams(dimension_semantics=("arbitrary",) * n_axes,
                                vmem_limit_bytes=VMEM_LIMIT)


def _rms(x, g):
    y = x * lax.rsqrt(jnp.mean(x * x, axis=-1, keepdims=True) + EPS)
    return y * g


def _dot(a, b):
    return jnp.dot(a, b, preferred_element_type=F32)


def _dot_nt(a, b):
    return lax.dot_general(a, b, (((1,), (1,)), ((), ())), preferred_element_type=F32)


def _dot_tn(a, b):
    return lax.dot_general(a, b, (((0,), (0,)), ((), ())), preferred_element_type=F32)


def _split3(x):
    hi = x.astype(BF16)
    r1 = x - hi.astype(F32)
    mid = r1.astype(BF16)
    lo = (r1 - mid.astype(F32)).astype(BF16)
    return hi, mid, lo


def _dot_f32(a, b, passes=6):
    a0, a1, a2 = _split3(a)
    b0, b1, b2 = _split3(b)
    out = _dot(a0, b0) + (_dot(a0, b1) + _dot(a1, b0))
    if passes == 6:
        out = out + (_dot(a0, b2) + _dot(a1, b1) + _dot(a2, b0))
    return out


def _mod_kernel(c_ref, w_ref, b_ref, o_ref):
    c = c_ref[...]
    s = c * jax.nn.sigmoid(c)
    o_ref[0] = _dot_f32(s, w_ref[0], passes=3) + b_ref[0]


def _modulation(cc, w_mod, b_mod):
    depth, d, n6 = w_mod.shape
    nb = 1536
    return pl.pallas_call(
        _mod_kernel,
        grid=(depth, n6 // nb),
        in_specs=[pl.BlockSpec((SUB, d), lambda l, j: (0, 0)),
                  pl.BlockSpec((1, d, nb), lambda l, j: (l, 0, j)),
                  pl.BlockSpec((1, 1, nb), lambda l, j: (l, 0, j))],
        out_specs=pl.BlockSpec((1, SUB, nb), lambda l, j: (l, 0, j)),
        out_shape=jax.ShapeDtypeStruct((depth, SUB, n6), F32),
        compiler_params=_cparams(2),
        name="modulation",
    )(cc, w_mod, b_mod.reshape(depth, 1, n6))


def _inproj_kernel(z_ref, mod_ref, g1_ref, w_ref, lbf_ref, lbb_ref, qg_ref, kvg_ref,
                   wq_ref, wkv_ref, cqe_ref, sqe_ref, cqo_ref, sqo_ref, kc_ref,
                   q_o, v_o, lff_o, kf_o, lfb_o, kb_o, g_o, gas_o, gbs_o, qq_o, kk_o, vv_o):
    mod = mod_ref[0, 0]
    h = _rms(z_ref[0], g1_ref[...]) * (1.0 + mod[1:2]) + mod[0:1]
    hb = h.astype(BF16)

    p = _dot(hb, w_ref[...])

    def seg(a, b):
        return p[:, a:b]

    q_o[0] = seg(C_Q, C_I).astype(BF16)
    v_o[0] = seg(C_I, C_FF).astype(BF16)
    for a, lb_ref, lf_o, k_o in ((C_FF, lbf_ref, lff_o, kf_o), (C_FB, lbb_ref, lfb_o, kb_o)):
        lb = lb_ref[...]
        f = lb + (1.0 - lb) * jax.nn.sigmoid(seg(a, a + HG_WIDTH))
        lf_o[0] = jnp.log(f) * LOG2E
        k_o[0] = (1.0 - f).astype(BF16)
    g_o[0] = seg(C_G, C_DQ).astype(BF16)
    gas_o[0] = jax.nn.sigmoid(seg(C_GA, C_GB)).astype(BF16)
    gbs_o[0] = jax.nn.sigmoid(seg(C_GB, C_KPE)).astype(BF16)

    lane = lax.broadcasted_iota(jnp.int32, (TM, LANE), 1)
    qn = _rms(seg(C_DQ, C_DKV), qg_ref[...]).astype(BF16)
    qa = _dot(qn, wq_ref[...])
    tabs = ((cqe_ref[...], sqe_ref[...]), (cqo_ref[...], sqo_ref[...]))
    for hh in range(MLA_HEADS):
        sl = slice(hh * LANE, (hh + 1) * LANE)
        cq, sq = tabs[hh % 2]
        blk = qa[:, sl]
        qq_o[0, :, sl] = (blk * cq + pltpu.roll(blk, LANE - MLA_ROPE, 1) * sq).astype(BF16)
    kvn = _rms(seg(C_DKV, C_GA), kvg_ref[...]).astype(BF16)
    kv = _dot(kvn, wkv_ref[...])
    kp = seg(C_KPE, C_END) * kc_ref[...]
    kpe_e = jnp.where(lane < MLA_ROPE, kp + pltpu.roll(kp, LANE - MLA_ROPE, 1), 0.0)
    kpe_o = pltpu.roll(kpe_e, MLA_NOPE, 1)
    one_e = jnp.where(lane == MLA_V, 1.0, 0.0)
    one_o = jnp.where(lane == 0, 1.0, 0.0)
    for hh in range(MLA_HEADS):
        sl = slice(hh * LANE, (hh + 1) * LANE)
        blk = kv[:, sl]
        if hh % 2 == 0:
            kk_o[0, :, sl] = jnp.where(lane >= MLA_V, blk, kpe_e).astype(BF16)
            vv_o[0, :, sl] = jnp.where(lane < MLA_V, blk, one_e).astype(BF16)
        else:
            kk_o[0, :, sl] = jnp.where(lane < MLA_NOPE, blk, kpe_o).astype(BF16)
            vv_o[0, :, sl] = jnp.where(lane >= MLA_NOPE, blk, one_o).astype(BF16)


def _inproj(z, modl, g1, w, lbf, lbb, qg, kvg, wq, wkv, cqe, sqe, cqo, sqo, kc, nct):
    b_, lt, d = z.shape
    nt = lt // TM
    row = lambda w_: pl.BlockSpec((1, TM, w_), lambda b, i: (b, i, 0))
    full = lambda a: pl.BlockSpec(a.shape, lambda b, i: (0,) * a.ndim)
    tab = pl.BlockSpec((TM, LANE), lambda b, i: (i, 0))
    f32o = lambda w_: jax.ShapeDtypeStruct((b_, lt, w_), F32)
    bf16o = lambda w_: jax.ShapeDtypeStruct((b_, lt, w_), BF16)
    return pl.pallas_call(
        _inproj_kernel,
        grid=(b_, nt),
        in_specs=[row(d),
                  pl.BlockSpec((1, 1, SUB, d), lambda b, i: (b, jnp.minimum(i // nct, 1), 0, 0)),
                  full(g1), full(w), full(lbf), full(lbb), full(qg), full(kvg),
                  full(wq), full(wkv), tab, tab, tab, tab, tab],
        out_specs=[row(512)] * 7 + [row(1024)] * 5,
        out_shape=[bf16o(512), bf16o(512), f32o(512), bf16o(512), f32o(512), bf16o(512), bf16o(512)]
        + [bf16o(1024)] * 5,
        compiler_params=_cparams(2),
        name="inproj",
    )(z, modl, g1, w, lbf, lbb, qg, kvg, wq, wkv, cqe, sqe, cqo, sqo, kc)


N_LEVELS = 7


def _chunk_tables(reverse):
    c = CHUNK
    t = np.arange(c)[:, None]
    u = np.arange(c)[None, :]
    tri = ((u >= t) if reverse else (u <= t)).astype(np.float32)
    lvl = np.full((c, c), N_LEVELS + 1, np.int32)
    lvl[t == u] = 0
    for l in range(N_LEVELS, 0, -1):
        m = 1 << l
        same = (t // m) == (u // m)
        lvl[same & ((u > t) if reverse else (u < t))] = l
    return jnp.asarray(tri, dtype=BF16), jnp.asarray(lvl)


def _level_ref(bs_ref, l, reverse):
    w = bs_ref.shape[1]
    sub = lax.broadcasted_iota(jnp.int32, (SUB, w), 0)
    off = 0 if reverse else -1
    m = 1 << l

    def row(r):
        return jnp.broadcast_to(bs_ref[r:r + 1, :], (SUB, w))

    pieces = []
    for j in range(CHUNK // SUB):
        base = j * SUB
        if m >= SUB:
            pieces.append(row((base // m) * m + m // 2 + off))
        else:
            p = row(base + m // 2 + off)
            for i in range(1, SUB // m):
                p = jnp.where(sub < i * m, p, row(base + i * m + m // 2 + off))
            pieces.append(p)
    return jnp.concatenate(pieces, axis=0)


def _hgrn_kernel(*refs, reverse, readout):
    if readout:
        (q_ref, v_ref, lf_ref, k_ref, tri_ref, lvl_ref, of_ref, g_ref, gain_ref, o_ref,
         s_ref, bs_ref, qs_ref, ks_ref) = refs
    else:
        q_ref, v_ref, lf_ref, k_ref, tri_ref, lvl_ref, o_ref, s_ref, bs_ref, qs_ref, ks_ref = refs

    @pl.when(pl.program_id(1) == 0)
    def _():
        s_ref[...] = jnp.zeros_like(s_ref)

    c = CHUNK
    nch = TM // c
    e = lambda x: jnp.exp2(x).astype(BF16)

    lvl = lvl_ref[...]
    for ci in range(nch):
        cidx = (nch - 1 - ci) if reverse else ci
        rows = slice(cidx * c, (cidx + 1) * c)
        lf = lf_ref[0, rows, :]
        hi = lf.astype(BF16)
        lo = (lf - hi.astype(F32)).astype(BF16)
        bsc = bs_ref.at[ci]
        bsc[...] = _dot(tri_ref[...], hi) + _dot(tri_ref[...], lo)
        b = bsc[...]
        btot = b[0:1] if reverse else b[c - 1:c]
        q = q_ref[0, rows, :]
        k = k_ref[0, rows, :]
        qs_ref[ci, 0] = q * e(b)
        ks_ref[ci, 0] = k * e(btot - b)
        for l in range(1, N_LEVELS + 1):
            d = b - _level_ref(bsc, l, reverse)
            qs_ref[ci, l] = q * e(d)
            ks_ref[ci, l] = k * e(-d)
        for hh in range(HG_HEADS):
            hsl = slice(hh * HG_DK, (hh + 1) * HG_DK)
            ix = (0, rows, hsl)
            v = v_ref[ix]
            st = s_ref[hh]
            att = jnp.where(lvl == 0, _dot_nt(q[:, hsl], k[:, hsl]), 0.0)
            for l in range(1, N_LEVELS + 1):
                att = jnp.where(lvl == l, _dot_nt(qs_ref[ci, l, :, hsl], ks_ref[ci, l, :, hsl]), att)
            o = _dot_nt(qs_ref[ci, 0, :, hsl], st.astype(BF16)) + _dot(att.astype(BF16), v)
            s_ref[hh] = st * jnp.exp2(btot[:, hsl]) + _dot_tn(v, ks_ref[ci, 0, :, hsl])
            if readout:
                o = o + of_ref[ix]
                o = o * lax.rsqrt(jnp.mean(o * o, axis=-1, keepdims=True) + EPS)
                o = o * gain_ref[:, hsl]
                g = g_ref[ix].astype(F32)
                o = o * (g * jax.nn.sigmoid(g))
            o_ref[ix] = o.astype(o_ref.dtype)


def _hgrn(q, v, lf, k, nct, reverse, extra=None):
    b_, lt, w = q.shape
    nt = lt // TM
    if reverse:
        tile = lambda i: jnp.where(i < nct, nct - 1 - i, nt - 1 - (i - nct))
    else:
        tile = lambda i: i
    row = pl.BlockSpec((1, TM, w), lambda b, i: (b, tile(i), 0))
    full = lambda a: pl.BlockSpec(a.shape, lambda b, i: (0,) * a.ndim)
    tri, lvl = _chunk_tables(reverse)
    ins = [q, v, lf, k, tri, lvl]
    specs = [row] * 4 + [full(tri), full(lvl)]
    if extra is not None:
        o_f, g, gain = extra
        ins += [o_f, g, gain]
        specs += [row, row, full(gain)]
    return pl.pallas_call(
        functools.partial(_hgrn_kernel, reverse=reverse, readout=extra is not None),
        grid=(b_, nt),
        in_specs=specs,
        out_specs=row,
        out_shape=jax.ShapeDtypeStruct((b_, lt, w), BF16),
        scratch_shapes=[pltpu.VMEM((HG_HEADS, HG_DK, HG_DK), F32), pltpu.VMEM((TM // CHUNK, CHUNK, w), F32),
                        pltpu.VMEM((TM // CHUNK, N_LEVELS + 1, CHUNK, w), BF16),
                        pltpu.VMEM((TM // CHUNK, N_LEVELS + 1, CHUNK, w), BF16)],
        compiler_params=_cparams(2),
        name="hgrn_bwd" if reverse else "hgrn_fwd",
    )(*ins)


KV_CHUNK = 256


def _attn_kernel(q_ref, k_ref, v_ref, o_ref, *, n_ctx, nct):
    i = pl.program_id(2)
    lane = lax.broadcasted_iota(jnp.int32, (TM, LANE), 1)

    def run(nk):
        outs = []
        for hh in range(2):
            sl = slice(hh * LANE, (hh + 1) * LANE)
            q = q_ref[0, :, sl]
            m = acc = None
            edges = [0] + list(range(n_ctx, nk + 1, KV_CHUNK))
            for c in range(len(edges) - 1):
                ks = slice(edges[c], edges[c + 1])
                s = _dot_nt(q, k_ref[0, ks, sl])
                mc = jnp.max(s, axis=-1, keepdims=True)
                if c == 0:
                    m = mc
                    acc = _dot(jnp.exp2(s - m).astype(BF16), v_ref[0, ks, sl])
                else:
                    m_new = jnp.maximum(m, mc)
                    acc = acc * jnp.exp2(m - m_new) + _dot(jnp.exp2(s - m_new).astype(BF16), v_ref[0, ks, sl])
                    m = m_new
            den = acc[:, MLA_V:MLA_V + 1] if hh == 0 else acc[:, 0:1]
            outs.append(acc / den)
        o_ref[0] = jnp.where(lane < MLA_V, outs[0], outs[1]).astype(o_ref.dtype)

    @pl.when(i < nct)
    def _():
        run(n_ctx)

    @pl.when(i >= nct)
    def _():
        run(k_ref.shape[1])


def _attention(qq, kk, vv, n_ctx):
    b_, lt, _ = qq.shape
    nt = lt // TM
    npair = MLA_HEADS // 2
    return pl.pallas_call(
        functools.partial(_attn_kernel, n_ctx=n_ctx, nct=n_ctx // TM),
        grid=(b_, npair, nt),
        in_specs=[pl.BlockSpec((1, TM, 2 * LANE), lambda b, j, i: (b, i, j)),
                  pl.BlockSpec((1, lt, 2 * LANE), lambda b, j, i: (b, 0, j)),
                  pl.BlockSpec((1, lt, 2 * LANE), lambda b, j, i: (b, 0, j))],
        out_specs=pl.BlockSpec((1, TM, LANE), lambda b, j, i: (b, i, j)),
        out_shape=jax.ShapeDtypeStruct((b_, lt, npair * LANE), BF16),
        compiler_params=_cparams(3),
        name="mla_attention",
    )(qq, kk, vv)


def _route(sel):
    r = lambda x, e: x[e:e + 1]
    best = None
    for g in range(N_GROUPS):
        u = [r(sel, EPG * g + j) for j in range(EPG)]
        gs = None
        for a in range(EPG):
            for b in range(a + 1, EPG):
                pr = u[a] + u[b]
                gs = pr if gs is None else jnp.maximum(gs, pr)
        if best is None:
            best, gi = gs, jnp.zeros_like(gs, dtype=jnp.int32)
        else:
            upd = gs > best
            best = jnp.where(upd, gs, best)
            gi = jnp.where(upd, g, gi)

    def pick(x, j):
        out = r(x, j)
        for g in range(1, N_GROUPS):
            out = jnp.where(gi == g, r(x, EPG * g + j), out)
        return out

    u = [pick(sel, j) for j in range(EPG)]

    def argmax4(vals):
        bv, bi = vals[0], jnp.zeros_like(gi)
        for j in range(1, EPG):
            upd = vals[j] > bv
            bv = jnp.where(upd, vals[j], bv)
            bi = jnp.where(upd, j, bi)
        return bi

    l1 = argmax4(u)
    l2 = argmax4([jnp.where(l1 == j, -jnp.inf, u[j]) for j in range(EPG)])
    lo = jnp.minimum(l1, l2)
    hi = jnp.maximum(l1, l2)
    pair = jnp.where(lo == 0, hi - 1, jnp.where(lo == 1, hi + 1, 5))
    cls = (gi * 6 + pair).astype(F32)
    return jnp.concatenate([cls] + [jnp.zeros_like(cls)] * (SUB - 1), axis=0)


def _merge_kernel(a_ref, m_ref, gas_ref, gbs_ref, z_ref, mod_ref, wpa_ref, wpb_ref, wo_ref,
                  g2_ref, wr_ref, rb_ref, zn_o, ht_o, route_o):
    mod = mod_ref[0, 0]
    pa = _dot(a_ref[0].astype(BF16), wpa_ref[...])
    pb = _dot(m_ref[0].astype(BF16), wpb_ref[...])
    mixed = gas_ref[0] * pa + gbs_ref[0] * pb
    zn = z_ref[0] + mod[2:3] * _dot(mixed.astype(BF16), wo_ref[...])
    zn_o[0] = zn
    h2 = _rms(zn, g2_ref[...]) * (1.0 + mod[4:5]) + mod[3:4]
    _tiles_store(ht_o, h2)
    logits = _dot_f32(h2, wr_ref[...], passes=3)
    sc = jax.nn.sigmoid(logits.T[:N_EXPERTS])
    route_o[0] = _route(sc + rb_ref[...])


def _merge(a, m, gas, gbs, z, modl, wpa, wpb, wo, g2, wr, rbb, nct):
    b_, lt, d = z.shape
    nt = lt // TM
    row = lambda w_: pl.BlockSpec((1, TM, w_), lambda b, i: (b, i, 0))
    full = lambda x: pl.BlockSpec(x.shape, lambda b, i: (0,) * x.ndim)
    return pl.pallas_call(
        _merge_kernel,
        grid=(b_, nt),
        in_specs=[row(512), row(512), row(d), row(d), row(d),
                  pl.BlockSpec((1, 1, SUB, d), lambda b, i: (b, jnp.minimum(i // nct, 1), 0, 0)),
                  full(wpa), full(wpb), full(wo), full(g2), full(wr), full(rbb)],
        out_specs=[row(d), pl.BlockSpec((TM * SUB, LANE), lambda b, i: (b * nt + i, 0)),
                   pl.BlockSpec((1, SUB, TM), lambda b, i: (b, 0, i))],
        out_shape=[jax.ShapeDtypeStruct((b_, lt, d), F32),
                   jax.ShapeDtypeStruct((b_ * lt * SUB, LANE), F32),
                   jax.ShapeDtypeStruct((b_, SUB, lt), F32)],
        compiler_params=_cparams(2),
        name="merge_route",
    )(a, m, gas, gbs, z, modl, wpa, wpb, wo, g2, wr, rbb)


def _tiles_store(ref, x, lead=()):
    n = x.shape[0]
    for j in range(SUB):
        ref[lead + (pl.ds(j, n, stride=SUB), slice(None))] = x[:, j * LANE:(j + 1) * LANE]


def _tiles_load(ref, n, lead=()):
    return jnp.concatenate([ref[lead + (pl.ds(j, n, stride=SUB), slice(None))] for j in range(SUB)], axis=1)


def _tile_copy(src_hbm, idx_ref, buf, sem, slot, r):
    return pltpu.make_async_copy(src_hbm.at[pl.ds(pl.multiple_of(idx_ref[0, 0, r] * SUB, SUB), SUB)],
                                 buf.at[slot, pl.ds(pl.multiple_of(r * SUB, SUB), SUB)], sem.at[slot])


GATHER_UNROLL = 8


def _gather_start(src_hbm, idx_ref, buf, sem, slot, n):
    def body(r8, c):
        for u in range(GATHER_UNROLL):
            _tile_copy(src_hbm, idx_ref, buf, sem, slot, r8 * GATHER_UNROLL + u).start()
        return c
    lax.fori_loop(0, n // GATHER_UNROLL, body, 0)


def _gather_wait(src_hbm, buf, sem, slot):
    pltpu.make_async_copy(src_hbm.at[pl.ds(0, buf.shape[1])], buf.at[slot], sem.at[slot]).wait()


def _sort_kernel(pos_ref, ht_ref, init_hbm, hs_hbm, sem):
    del init_hbm

    def body(r8, c):
        for u in range(GATHER_UNROLL):
            r = r8 * GATHER_UNROLL + u
            pltpu.make_async_copy(
                ht_ref.at[pl.ds(pl.multiple_of(r * SUB, SUB), SUB)],
                hs_hbm.at[pl.ds(pl.multiple_of(pos_ref[0, 0, r] * SUB, SUB), SUB)], sem.at[0]).start()
        return c
    lax.fori_loop(0, TM // GATHER_UNROLL, body, 0)
    pltpu.make_async_copy(ht_ref, hs_hbm.at[pl.ds(0, TM * SUB)], sem.at[0]).wait()


def _sort_rows(ht, pos, n_sorted):
    nstep = pos.shape[0]
    init = jnp.zeros((n_sorted * SUB, LANE), F32)
    return pl.pallas_call(
        _sort_kernel,
        grid=(nstep,),
        in_specs=[pl.BlockSpec((1, 1, TM), lambda t: (t, 0, 0), memory_space=pltpu.SMEM),
                  pl.BlockSpec((TM * SUB, LANE), lambda t: (t, 0)), pl.BlockSpec(memory_space=pl.ANY)],
        out_specs=pl.BlockSpec(memory_space=pl.ANY),
        out_shape=jax.ShapeDtypeStruct(init.shape, F32),
        scratch_shapes=[pltpu.SemaphoreType.DMA((1,))],
        input_output_aliases={2: 0},
        compiler_params=_cparams(1),
        name="class_sort",
    )(pos, ht, init)


def _moe_kernel(ea_ref, eb_ref, nu_ref, hs_ref, wrt_ref,
                wga_ref, wua_ref, wda_ref, wgb_ref, wub_ref, wdb_ref, y_ref):
    j = pl.program_id(0)

    @pl.when(j < nu_ref[0])
    def _():
        x32 = _tiles_load(hs_ref, MOE_BM)
        x = x32.astype(BF16)

        def ffn(wg, wu, wd, e):
            sc = jax.nn.sigmoid(jnp.sum(x32 * wrt_ref[pl.ds(e, 1), :], axis=-1, keepdims=True))
            act = jax.nn.silu(_dot(x, wg[0])) * _dot(x, wu[0])
            return sc, _dot(act.astype(BF16), wd[0])

        s_lo, y_lo = ffn(wga_ref, wua_ref, wda_ref, ea_ref[j])
        s_hi, y_hi = ffn(wgb_ref, wub_ref, wdb_ref, eb_ref[j])
        inv = 1.0 / (s_lo + s_hi)
        _tiles_store(y_ref, (s_lo * inv) * y_lo + (s_hi * inv) * y_hi)

    @pl.when(j >= nu_ref[0])
    def _():
        y_ref[...] = jnp.zeros_like(y_ref)


def _moe(hs, wrt, blk_ea, blk_eb, n_used, wg, wu, wd):
    nblk = blk_ea.shape[0]
    d, de = wg.shape[1:]
    tiles = pl.BlockSpec((MOE_BM * SUB, LANE), lambda j, ea, eb, nu: (j, 0))
    wspec = lambda shp, which: pl.BlockSpec(
        (1,) + shp, (lambda j, ea, eb, nu: (ea[j], 0, 0)) if which == 0 else (lambda j, ea, eb, nu: (eb[j], 0, 0)))
    return pl.pallas_call(
        _moe_kernel,
        grid_spec=pltpu.PrefetchScalarGridSpec(
            num_scalar_prefetch=3,
            grid=(nblk,),
            in_specs=[tiles,
                      pl.BlockSpec(wrt.shape, lambda j, ea, eb, nu: (0, 0)),
                      wspec((d, de), 0), wspec((d, de), 0), wspec((de, d), 0),
                      wspec((d, de), 1), wspec((d, de), 1), wspec((de, d), 1)],
            out_specs=tiles),
        out_shape=jax.ShapeDtypeStruct(hs.shape, F32),
        compiler_params=_cparams(1),
        name="moe_ffn",
    )(blk_ea, blk_eb, n_used, hs, wrt, wg, wu, wd, wg, wu, wd)


def _post_kernel(*refs, final):
    if final:
        idx_ref, idxn_ref, y_hbm, zn_ref, mod_ref, fg_ref, o_ref, buf, sem = refs
    else:
        idx_ref, idxn_ref, y_hbm, zn_ref, mod_ref, o_ref, buf, sem = refs
    t = pl.program_id(0)
    nstep = pl.num_programs(0)
    slot = t % 2

    @pl.when(t == 0)
    def _():
        _gather_start(y_hbm, idx_ref, buf, sem, 0, TM)

    @pl.when(t + 1 < nstep)
    def _():
        _gather_start(y_hbm, idxn_ref, buf, sem, 1 - slot, TM)

    _gather_wait(y_hbm, buf, sem, slot)
    z = zn_ref[0] + mod_ref[0, 0][5:6] * _tiles_load(buf, TM, (slot,))
    if final:
        z = _rms(z, fg_ref[...])
    o_ref[0] = z


def _post(y, pos, zn, modl, nct, final_g=None):
    b_, lt, d = zn.shape
    nt = lt // TM
    final = final_g is not None
    t0 = nct if final else 0
    ntl = nt - t0
    nstep = b_ * ntl
    idx_spec = lambda f: pl.BlockSpec((1, 1, TM), f, memory_space=pltpu.SMEM)
    ins = [pos, pos, y, zn, modl]
    specs = [idx_spec(lambda t: (t, 0, 0)),
             idx_spec(lambda t: (jnp.minimum(t + 1, nstep - 1), 0, 0)),
             pl.BlockSpec(memory_space=pl.ANY),
             pl.BlockSpec((1, TM, d), lambda t: (t // ntl, t % ntl + t0, 0)),
             pl.BlockSpec((1, 1, SUB, d), lambda t: (t // ntl, jnp.minimum((t % ntl + t0) // nct, 1), 0, 0))]
    if final:
        ins.append(final_g)
        specs.append(pl.BlockSpec(final_g.shape, lambda t: (0, 0)))
    return pl.pallas_call(
        functools.partial(_post_kernel, final=final),
        grid=(nstep,),
        in_specs=specs,
        out_specs=pl.BlockSpec((1, TM, d), lambda t: (t // ntl, t % ntl, 0)),
        out_shape=jax.ShapeDtypeStruct((b_, ntl * TM, d), F32),
        scratch_shapes=[pltpu.VMEM((2, TM * SUB, LANE), F32), pltpu.SemaphoreType.DMA((2,))],
        compiler_params=_cparams(1),
        name="unsort_residual",
    )(*ins)


_PAIRS = [(0, 1), (0, 2), (0, 3), (1, 2), (1, 3), (2, 3)]
_CLS_EA = np.array([EPG * g + p[0] for g in range(N_GROUPS) for p in _PAIRS], np.int32)
_CLS_EB = np.array([EPG * g + p[1] for g in range(N_GROUPS) for p in _PAIRS], np.int32)


def _sort_plan(cls):
    n = cls.shape[0]
    nblk = n // MOE_BM + N_CLASSES
    onehot = (cls[:, None] == jnp.arange(N_CLASSES, dtype=jnp.int32)[None, :]).astype(jnp.int32)
    csum = jnp.cumsum(onehot, axis=0)
    counts = csum[-1]
    rank = jnp.take_along_axis(csum, cls[:, None], axis=1)[:, 0] - 1
    padded = (counts + MOE_BM - 1) // MOE_BM * MOE_BM
    pad_end = jnp.cumsum(padded)
    pos = (pad_end - padded)[cls] + rank
    blk_cls = jnp.minimum(
        jnp.searchsorted(pad_end, jnp.arange(nblk, dtype=jnp.int32) * MOE_BM, side='right'),
        N_CLASSES - 1).astype(jnp.int32)
    n_used = (pad_end[-1] // MOE_BM).astype(jnp.int32).reshape(1)
    return pos, jnp.asarray(_CLS_EA)[blk_cls], jnp.asarray(_CLS_EB)[blk_cls], n_used


def _rot_cols(w):
    return jnp.concatenate([-w[:, 8:16], w[:, 0:8], -w[:, 24:32], w[:, 16:24]], axis=1)


def _prep_w_in(w):
    kpe = w[:, 3200:3232]
    pad = jnp.zeros((w.shape[0], C_END - C_KPE - 2 * MLA_ROPE), w.dtype)
    return jnp.concatenate([w[:, :3200], w[:, 3232:], kpe, _rot_cols(kpe), pad], axis=1).astype(BF16)


def _prep_mla(w_uq, w_ukv):
    odd = (jnp.arange(MLA_HEADS) % 2 == 1)[None, :, None]
    qh = w_uq.reshape(MLA_Q_RANK, MLA_HEADS, MLA_NOPE + MLA_ROPE)
    nope, pe = qh[:, :, :MLA_NOPE], qh[:, :, MLA_NOPE:]
    pe_sw = _rot_cols(pe.reshape(MLA_Q_RANK * MLA_HEADS, MLA_ROPE)).reshape(MLA_Q_RANK, MLA_HEADS, MLA_ROPE)
    wq = jnp.where(odd, jnp.concatenate([nope, pe, pe_sw], axis=2), jnp.concatenate([pe, pe_sw, nope], axis=2))
    kvh = w_ukv.reshape(MLA_KV_RANK, MLA_HEADS, MLA_NOPE + MLA_V)
    kn, vh = kvh[:, :, :MLA_NOPE], kvh[:, :, MLA_NOPE:]
    wkv = jnp.where(odd, jnp.concatenate([kn, vh], axis=2), jnp.concatenate([vh, kn], axis=2))
    return (wq.reshape(MLA_Q_RANK, MLA_HEADS * LANE).astype(BF16),
            wkv.reshape(MLA_KV_RANK, MLA_HEADS * LANE).astype(BF16))


def _rope_tables(n_ctx, n_lat):
    rows = n_lat // GRID_W
    rowp = jnp.broadcast_to(jnp.arange(rows, dtype=F32)[:, None], (rows, GRID_W)).reshape(-1)
    colp = jnp.broadcast_to(jnp.arange(GRID_W, dtype=F32)[None, :], (rows, GRID_W)).reshape(-1)
    axis_dim = MLA_ROPE // 2
    inv_freq = ROPE_THETA ** (-jnp.arange(0, axis_dim, 2, dtype=F32) / axis_dim)
    ar, ac = rowp[:, None] * inv_freq, colp[:, None] * inv_freq
    cos32 = jnp.concatenate([jnp.cos(ar), jnp.cos(ar), jnp.cos(ac), jnp.cos(ac)], axis=1)
    sin32 = jnp.concatenate([jnp.sin(ar), jnp.sin(ar), jnp.sin(ac), jnp.sin(ac)], axis=1)
    cos32 = jnp.concatenate([jnp.ones((n_ctx, MLA_ROPE), F32), cos32], axis=0)
    sin32 = jnp.concatenate([jnp.zeros((n_ctx, MLA_ROPE), F32), sin32], axis=0)
    lt = n_ctx + n_lat
    one, zero = jnp.ones((lt, MLA_NOPE), F32), jnp.zeros((lt, MLA_NOPE), F32)
    z32 = jnp.zeros((lt, MLA_ROPE), F32)
    sc = MLA_SCALE * LOG2E
    cqe = jnp.concatenate([cos32, z32, one], axis=1) * sc
    sqe = jnp.concatenate([sin32, z32, zero], axis=1) * sc
    cqo = jnp.concatenate([one, cos32, z32], axis=1) * sc
    sqo = jnp.concatenate([zero, sin32, z32], axis=1) * sc
    kc = jnp.concatenate([cos32, sin32, zero], axis=1)
    return cqe, sqe, cqo, sqo, kc


def _lower_bounds(gamma):
    p = jnp.cumsum(jax.nn.softmax(gamma.astype(F32), axis=0), axis=0)
    return p - p[0:1]


def kernel(x, c, ctx, c_ctx, w_mod, b_mod, norm1_g, norm2_g, w_in, gamma_fwd, gamma_bwd, hg_norm_g,
           q_norm_g, kv_norm_g, w_uq, w_ukv, w_pa, w_pb, w_o, w_router, router_bias,
           w_gate_e, w_up_e, w_down_e, final_g):
    b_, n_lat, d = x.shape
    n_ctx = ctx.shape[1]
    depth = w_mod.shape[0]
    assert n_ctx % TM == 0 and n_lat % TM == 0 and n_lat % GRID_W == 0 and b_ + 1 <= SUB
    nct = n_ctx // TM
    lt = n_ctx + n_lat
    nt = lt // TM

    cc = jnp.concatenate([c, c_ctx[None, :], jnp.zeros((SUB - b_ - 1, d), F32)], axis=0)
    mod = _modulation(cc, w_mod, b_mod).reshape(depth, SUB, 6, d)
    mod_x = mod[:, :b_]
    mod_c = jnp.broadcast_to(mod[:, b_:b_ + 1], mod_x.shape)
    mod_t = jnp.stack([mod_c, mod_x], axis=2)
    mod_t = jnp.concatenate([mod_t, jnp.zeros((depth, b_, 2, SUB - 6, d), F32)], axis=3)

    cqe, sqe, cqo, sqo, kc = _rope_tables(n_ctx, n_lat)
    lbf, lbb = _lower_bounds(gamma_fwd), _lower_bounds(gamma_bwd)
    wr = jnp.concatenate([w_router, jnp.zeros((d, LANE - N_EXPERTS), F32)], axis=1)
    rbb = jnp.broadcast_to(router_bias.astype(F32)[:, None], (N_EXPERTS, TM))

    z = jnp.concatenate([ctx, x], axis=1)
    out = None
    for l in range(depth):
        last = l == depth - 1
        wq, wkv = _prep_mla(w_uq[l], w_ukv[l])
        (q, v, lff, kf, lfb, kb, g, gas, gbs, qq, kk, vv) = _inproj(
            z, mod_t[l], norm1_g[l][None], _prep_w_in(w_in[l]), lbf[l][None], lbb[l][None],
            q_norm_g[l][None], kv_norm_g[l][None], wq, wkv, cqe, sqe, cqo, sqo, kc, nct)
        o_f = _hgrn(q, v, lff, kf, nct, reverse=False)
        a = _hgrn(q, v, lfb, kb, nct, reverse=True, extra=(o_f, g, hg_norm_g[l][None]))
        m = _attention(qq, kk, vv, n_ctx)
        zn, ht, route = _merge(a, m, gas, gbs, z, mod_t[l], w_pa[l].astype(BF16), w_pb[l].astype(BF16),
                               w_o[l].astype(BF16), norm2_g[l][None], wr, rbb, nct)
        pos, blk_ea, blk_eb, n_used = _sort_plan(route[:, 0, :].reshape(-1).astype(jnp.int32))
        pos = pos.reshape(b_, nt, 1, TM)
        hs = _sort_rows(ht, pos.reshape(-1, 1, TM), blk_ea.shape[0] * MOE_BM)
        y = _moe(hs, w_router.T, blk_ea, blk_eb, n_used,
                 w_gate_e[l].astype(BF16), w_up_e[l].astype(BF16), w_down_e[l].astype(BF16))
        if last:
            out = _post(y, pos[:, nct:].reshape(-1, 1, TM), zn, mod_t[l], nct, final_g[None])
        else:
            z = _post(y, pos.reshape(-1, 1, TM), zn, mod_t[l], nct)
    return out
```

```python
import functools

import numpy as np
import jax
import jax.numpy as jnp
from jax import lax
from jax.experimental import pallas as pl
from jax.experimental.pallas import tpu as pltpu

F32 = jnp.float32
BF16 = jnp.bfloat16

EPS = 1e-6
GRID_W = 64
ROPE_THETA = 10000.0
HG_HEADS = 4
HG_DK = 128
HG_WIDTH = 512
MLA_HEADS = 8
MLA_NOPE = 64
MLA_ROPE = 32
MLA_V = 64
MLA_Q_RANK = 384
MLA_KV_RANK = 256
MLA_SCALE = (MLA_NOPE + MLA_ROPE) ** -0.5
LOG2E = 1.4426950408889634
N_EXPERTS = 16
N_GROUPS = 4
EPG = 4
N_CLASSES = N_GROUPS * 6

LANE = 128
SUB = 8
TM = 256
CHUNK = 128
MOE_BM = 256
VMEM_LIMIT = 56 * 1024 * 1024

C_Q, C_I, C_FF, C_FB, C_G, C_DQ, C_DKV, C_GA, C_GB, C_KPE, C_END = (
    0, 512, 1024, 1536, 2048, 2560, 2944, 3200, 4224, 5248, 5376)


def _cparams(n_axes):
    return pltpu.CompilerParams(dimension_semantics=("arbitrary",) * n_axes,
                                vmem_limit_bytes=VMEM_LIMIT)


def _rms(x, g):
    y = x * lax.rsqrt(jnp.mean(x * x, axis=-1, keepdims=True) + EPS)
    return y * g


def _dot(a, b):
    return jnp.dot(a, b, preferred_element_type=F32)


def _dot_nt(a, b):
    return lax.dot_general(a, b, (((1,), (1,)), ((), ())), preferred_element_type=F32)


def _dot_tn(a, b):
    return lax.dot_general(a, b, (((0,), (0,)), ((), ())), preferred_element_type=F32)


def _split3(x):
    hi = x.astype(BF16)
    r1 = x - hi.astype(F32)
    mid = r1.astype(BF16)
    lo = (r1 - mid.astype(F32)).astype(BF16)
    return hi, mid, lo


def _dot_f32(a, b, passes=6):
    a0, a1, a2 = _split3(a)
    b0, b1, b2 = _split3(b)
    out = _dot(a0, b0) + (_dot(a0, b1) + _dot(a1, b0))
    if passes == 6:
        out = out + (_dot(a0, b2) + _dot(a1, b1) + _dot(a2, b0))
    return out


def _mod_kernel(c_ref, w_ref, b_ref, o_ref):
    c = c_ref[...]
    s = c * jax.nn.sigmoid(c)
    o_ref[0] = _dot_f32(s, w_ref[0], passes=3) + b_ref[0]


def _modulation(cc, w_mod, b_mod):
    depth, d, n6 = w_mod.shape
    nb = 1536
    return pl.pallas_call(
        _mod_kernel,
        grid=(depth, n6 // nb),
        in_specs=[pl.BlockSpec((SUB, d), lambda l, j: (0, 0)),
                  pl.BlockSpec((1, d, nb), lambda l, j: (l, 0, j)),
                  pl.BlockSpec((1, 1, nb), lambda l, j: (l, 0, j))],
        out_specs=pl.BlockSpec((1, SUB, nb), lambda l, j: (l, 0, j)),
        out_shape=jax.ShapeDtypeStruct((depth, SUB, n6), F32),
        compiler_params=_cparams(2),
        name="modulation",
    )(cc, w_mod, b_mod.reshape(depth, 1, n6))


def _inproj_kernel(z_ref, mod_ref, g1_ref, w_ref, lbf_ref, lbb_ref, qg_ref, kvg_ref,
                   wq_ref, wkv_ref, cqe_ref, sqe_ref, cqo_ref, sqo_ref, kc_ref,
                   q_o, v_o, lff_o, kf_o, lfb_o, kb_o, g_o, gas_o, gbs_o, qq_o, kk_o, vv_o):
    mod = mod_ref[0, 0]
    h = _rms(z_ref[0], g1_ref[...]) * (1.0 + mod[1:2]) + mod[0:1]
    hb = h.astype(BF16)

    p = _dot(hb, w_ref[...])

    def seg(a, b):
        return p[:, a:b]

    q_o[0] = seg(C_Q, C_I).astype(BF16)
    v_o[0] = seg(C_I, C_FF).astype(BF16)
    for a, lb_ref, lf_o, k_o in ((C_FF, lbf_ref, lff_o, kf_o), (C_FB, lbb_ref, lfb_o, kb_o)):
        lb = lb_ref[...]
        f = lb + (1.0 - lb) * jax.nn.sigmoid(seg(a, a + HG_WIDTH))
        lf_o[0] = jnp.log(f) * LOG2E
        k_o[0] = (1.0 - f).astype(BF16)
    g_o[0] = seg(C_G, C_DQ).astype(BF16)
    gas_o[0] = jax.nn.sigmoid(seg(C_GA, C_GB)).astype(BF16)
    gbs_o[0] = jax.nn.sigmoid(seg(C_GB, C_KPE)).astype(BF16)

    lane = lax.broadcasted_iota(jnp.int32, (TM, LANE), 1)
    qn = _rms(seg(C_DQ, C_DKV), qg_ref[...]).astype(BF16)
    qa = _dot(qn, wq_ref[...])
    tabs = ((cqe_ref[...], sqe_ref[...]), (cqo_ref[...], sqo_ref[...]))
    for hh in range(MLA_HEADS):
        sl = slice(hh * LANE, (hh + 1) * LANE)
        cq, sq = tabs[hh % 2]
        blk = qa[:, sl]
        qq_o[0, :, sl] = (blk * cq + pltpu.roll(blk, LANE - MLA_ROPE, 1) * sq).astype(BF16)
    kvn = _rms(seg(C_DKV, C_GA), kvg_ref[...]).astype(BF16)
    kv = _dot(kvn, wkv_ref[...])
    kp = seg(C_KPE, C_END) * kc_ref[...]
    kpe_e = jnp.where(lane < MLA_ROPE, kp + pltpu.roll(kp, LANE - MLA_ROPE, 1), 0.0)
    kpe_o = pltpu.roll(kpe_e, MLA_NOPE, 1)
    one_e = jnp.where(lane == MLA_V, 1.0, 0.0)
    one_o = jnp.where(lane == 0, 1.0, 0.0)
    for hh in range(MLA_HEADS):
        sl = slice(hh * LANE, (hh + 1) * LANE)
        blk = kv[:, sl]
        if hh % 2 == 0:
            kk_o[0, :, sl] = jnp.where(lane >= MLA_V, blk, kpe_e).astype(BF16)
            vv_o[0, :, sl] = jnp.where(lane < MLA_V, blk, one_e).astype(BF16)
        else:
            kk_o[0, :, sl] = jnp.where(lane < MLA_NOPE, blk, kpe_o).astype(BF16)
            vv_o[0, :, sl] = jnp.where(lane >= MLA_NOPE, blk, one_o).astype(BF16)


def _inproj(z, modl, g1, w, lbf, lbb, qg, kvg, wq, wkv, cqe, sqe, cqo, sqo, kc, nct):
    b_, lt, d = z.shape
    nt = lt // TM
    row = lambda w_: pl.BlockSpec((1, TM, w_), lambda b, i: (b, i, 0))
    full = lambda a: pl.BlockSpec(a.shape, lambda b, i: (0,) * a.ndim)
    tab = pl.BlockSpec((TM, LANE), lambda b, i: (i, 0))
    f32o = lambda w_: jax.ShapeDtypeStruct((b_, lt, w_), F32)
    bf16o = lambda w_: jax.ShapeDtypeStruct((b_, lt, w_), BF16)
    return pl.pallas_call(
        _inproj_kernel,
        grid=(b_, nt),
        in_specs=[row(d),
                  pl.BlockSpec((1, 1, SUB, d), lambda b, i: (b, jnp.minimum(i // nct, 1), 0, 0)),
                  full(g1), full(w), full(lbf), full(lbb), full(qg), full(kvg),
                  full(wq), full(wkv), tab, tab, tab, tab, tab],
        out_specs=[row(512)] * 7 + [row(1024)] * 5,
        out_shape=[bf16o(512), bf16o(512), f32o(512), bf16o(512), f32o(512), bf16o(512), bf16o(512)]
        + [bf16o(1024)] * 5,
        compiler_params=_cparams(2),
        name="inproj",
    )(z, modl, g1, w, lbf, lbb, qg, kvg, wq, wkv, cqe, sqe, cqo, sqo, kc)


N_LEVELS = 7


def _chunk_tables(reverse):
    c = CHUNK
    t = np.arange(c)[:, None]
    u = np.arange(c)[None, :]
    tri = ((u >= t) if reverse else (u <= t)).astype(np.float32)
    lvl = np.full((c, c), N_LEVELS + 1, np.int32)
    lvl[t == u] = 0
    for l in range(N_LEVELS, 0, -1):
        m = 1 << l
        same = (t // m) == (u // m)
        lvl[same & ((u > t) if reverse else (u < t))] = l
    return jnp.asarray(tri, dtype=BF16), jnp.asarray(lvl)


def _level_ref(bs_ref, l, reverse):
    w = bs_ref.shape[1]
    sub = lax.broadcasted_iota(jnp.int32, (SUB, w), 0)
    off = 0 if reverse else -1
    m = 1 << l

    def row(r):
        return jnp.broadcast_to(bs_ref[r:r + 1, :], (SUB, w))

    pieces = []
    for j in range(CHUNK // SUB):
        base = j * SUB
        if m >= SUB:
            pieces.append(row((base // m) * m + m // 2 + off))
        else:
            p = row(base + m // 2 + off)
            for i in range(1, SUB // m):
                p = jnp.where(sub < i * m, p, row(base + i * m + m // 2 + off))
            pieces.append(p)
    return jnp.concatenate(pieces, axis=0)


def _hgrn_kernel(*refs, reverse, readout):
    if readout:
        (q_ref, v_ref, lf_ref, k_ref, tri_ref, lvl_ref, of_ref, g_ref, gain_ref, o_ref,
         s_ref, bs_ref, qs_ref, ks_ref) = refs
    else:
        q_ref, v_ref, lf_ref, k_ref, tri_ref, lvl_ref, o_ref, s_ref, bs_ref, qs_ref, ks_ref = refs

    @pl.when(pl.program_id(1) == 0)
    def _():
        s_ref[...] = jnp.zeros_like(s_ref)

    c = CHUNK
    nch = TM // c
    e = lambda x: jnp.exp2(x).astype(BF16)

    lvl = lvl_ref[...]
    for ci in range(nch):
        cidx = (nch - 1 - ci) if reverse else ci
        rows = slice(cidx * c, (cidx + 1) * c)
        lf = lf_ref[0, rows, :]
        hi = lf.astype(BF16)
        lo = (lf - hi.astype(F32)).astype(BF16)
        bsc = bs_ref.at[ci]
        bsc[...] = _dot(tri_ref[...], hi) + _dot(tri_ref[...], lo)
        b = bsc[...]
        btot = b[0:1] if reverse else b[c - 1:c]
        q = q_ref[0, rows, :]
        k = k_ref[0, rows, :]
        qs_ref[ci, 0] = q * e(b)
        ks_ref[ci, 0] = k * e(btot - b)
        for l in range(1, N_LEVELS + 1):
            d = b - _level_ref(bsc, l, reverse)
            qs_ref[ci, l] = q * e(d)
            ks_ref[ci, l] = k * e(-d)
        for hh in range(HG_HEADS):
            hsl = slice(hh * HG_DK, (hh + 1) * HG_DK)
            ix = (0, rows, hsl)
            v = v_ref[ix]
            st = s_ref[hh]
            att = jnp.where(lvl == 0, _dot_nt(q[:, hsl], k[:, hsl]), 0.0)
            for l in range(1, N_LEVELS + 1):
                att = jnp.where(lvl == l, _dot_nt(qs_ref[ci, l, :, hsl], ks_ref[ci, l, :, hsl]), att)
            o = _dot_nt(qs_ref[ci, 0, :, hsl], st.astype(BF16)) + _dot(att.astype(BF16), v)
            s_ref[hh] = st * jnp.exp2(btot[:, hsl]) + _dot_tn(v, ks_ref[ci, 0, :, hsl])
            if readout:
                o = o + of_ref[ix]
                o = o * lax.rsqrt(jnp.mean(o * o, axis=-1, keepdims=True) + EPS)
                o = o * gain_ref[:, hsl]
                g = g_ref[ix].astype(F32)
                o = o * (g * jax.nn.sigmoid(g))
            o_ref[ix] = o.astype(o_ref.dtype)


def _hgrn(q, v, lf, k, nct, reverse, extra=None):
    b_, lt, w = q.shape
    nt = lt // TM
    if reverse:
        tile = lambda i: jnp.where(i < nct, nct - 1 - i, nt - 1 - (i - nct))
    else:
        tile = lambda i: i
    row = pl.BlockSpec((1, TM, w), lambda b, i: (b, tile(i), 0))
    full = lambda a: pl.BlockSpec(a.shape, lambda b, i: (0,) * a.ndim)
    tri, lvl = _chunk_tables(reverse)
    ins = [q, v, lf, k, tri, lvl]
    specs = [row] * 4 + [full(tri), full(lvl)]
    if extra is not None:
        o_f, g, gain = extra
        ins += [o_f, g, gain]
        specs += [row, row, full(gain)]
    return pl.pallas_call(
        functools.partial(_hgrn_kernel, reverse=reverse, readout=extra is not None),
        grid=(b_, nt),
        in_specs=specs,
        out_specs=row,
        out_shape=jax.ShapeDtypeStruct((b_, lt, w), BF16),
        scratch_shapes=[pltpu.VMEM((HG_HEADS, HG_DK, HG_DK), F32), pltpu.VMEM((TM // CHUNK, CHUNK, w), F32),
                        pltpu.VMEM((TM // CHUNK, N_LEVELS + 1, CHUNK, w), BF16),
                        pltpu.VMEM((TM // CHUNK, N_LEVELS + 1, CHUNK, w), BF16)],
        compiler_params=_cparams(2),
        name="hgrn_bwd" if reverse else "hgrn_fwd",
    )(*ins)


KV_CHUNK = 256


def _attn_kernel(q_ref, k_ref, v_ref, o_ref, *, n_ctx, nct):
    i = pl.program_id(2)
    lane = lax.broadcasted_iota(jnp.int32, (TM, LANE), 1)

    def run(nk):
        outs = []
        for hh in range(2):
            sl = slice(hh * LANE, (hh + 1) * LANE)
            q = q_ref[0, :, sl]
            m = acc = None
            edges = [0] + list(range(n_ctx, nk + 1, KV_CHUNK))
            for c in range(len(edges) - 1):
                ks = slice(edges[c], edges[c + 1])
                s = _dot_nt(q, k_ref[0, ks, sl])
                mc = jnp.max(s, axis=-1, keepdims=True)
                if c == 0:
                    m = mc
                    acc = _dot(jnp.exp2(s - m).astype(BF16), v_ref[0, ks, sl])
                else:
                    m_new = jnp.maximum(m, mc)
                    acc = acc * jnp.exp2(m - m_new) + _dot(jnp.exp2(s - m_new).astype(BF16), v_ref[0, ks, sl])
                    m = m_new
            den = acc[:, MLA_V:MLA_V + 1] if hh == 0 else acc[:, 0:1]
            outs.append(acc / den)
        o_ref[0] = jnp.where(lane < MLA_V, outs[0], outs[1]).astype(o_ref.dtype)

    @pl.when(i < nct)
    def _():
        run(n_ctx)

    @pl.when(i >= nct)
    def _():
        run(k_ref.shape[1])


def _attention(qq, kk, vv, n_ctx):
    b_, lt, _ = qq.shape
    nt = lt // TM
    npair = MLA_HEADS // 2
    return pl.pallas_call(
        functools.partial(_attn_kernel, n_ctx=n_ctx, nct=n_ctx // TM),
        grid=(b_, npair, nt),
        in_specs=[pl.BlockSpec((1, TM, 2 * LANE), lambda b, j, i: (b, i, j)),
                  pl.BlockSpec((1, lt, 2 * LANE), lambda b, j, i: (b, 0, j)),
                  pl.BlockSpec((1, lt, 2 * LANE), lambda b, j, i: (b, 0, j))],
        out_specs=pl.BlockSpec((1, TM, LANE), lambda b, j, i: (b, i, j)),
        out_shape=jax.ShapeDtypeStruct((b_, lt, npair * LANE), BF16),
        compiler_params=_cparams(3),
        name="mla_attention",
    )(qq, kk, vv)


def _route(sel):
    r = lambda x, e: x[e:e + 1]
    best = None
    for g in range(N_GROUPS):
        u = [r(sel, EPG * g + j) for j in range(EPG)]
        gs = None
        for a in range(EPG):
            for b in range(a + 1, EPG):
                pr = u[a] + u[b]
                gs = pr if gs is None else jnp.maximum(gs, pr)
        if best is None:
            best, gi = gs, jnp.zeros_like(gs, dtype=jnp.int32)
        else:
            upd = gs > best
            best = jnp.where(upd, gs, best)
            gi = jnp.where(upd, g, gi)

    def pick(x, j):
        out = r(x, j)
        for g in range(1, N_GROUPS):
            out = jnp.where(gi == g, r(x, EPG * g + j), out)
        return out

    u = [pick(sel, j) for j in range(EPG)]

    def argmax4(vals):
        bv, bi = vals[0], jnp.zeros_like(gi)
        for j in range(1, EPG):
            upd = vals[j] > bv
            bv = jnp.where(upd, vals[j], bv)
            bi = jnp.where(upd, j, bi)
        return bi

    l1 = argmax4(u)
    l2 = argmax4([jnp.where(l1 == j, -jnp.inf, u[j]) for j in range(EPG)])
    lo = jnp.minimum(l1, l2)
    hi = jnp.maximum(l1, l2)
    pair = jnp.where(lo == 0, hi - 1, jnp.where(lo == 1, hi + 1, 5))
    cls = (gi * 6 + pair).astype(F32)
    return jnp.concatenate([cls] + [jnp.zeros_like(cls)] * (SUB - 1), axis=0)


def _merge_kernel(a_ref, m_ref, gas_ref, gbs_ref, z_ref, mod_ref, wpa_ref, wpb_ref, wo_ref,
                  g2_ref, wr_ref, rb_ref, zn_o, ht_o, route_o):
    mod = mod_ref[0, 0]
    pa = _dot(a_ref[0].astype(BF16), wpa_ref[...])
    pb = _dot(m_ref[0].astype(BF16), wpb_ref[...])
    mixed = gas_ref[0] * pa + gbs_ref[0] * pb
    zn = z_ref[0] + mod[2:3] * _dot(mixed.astype(BF16), wo_ref[...])
    zn_o[0] = zn
    h2 = _rms(zn, g2_ref[...]) * (1.0 + mod[4:5]) + mod[3:4]
    _tiles_store(ht_o, h2)
    logits = _dot_f32(h2, wr_ref[...], passes=3)
    sc = jax.nn.sigmoid(logits.T[:N_EXPERTS])
    route_o[0] = _route(sc + rb_ref[...])


def _merge(a, m, gas, gbs, z, modl, wpa, wpb, wo, g2, wr, rbb, nct):
    b_, lt, d = z.shape
    nt = lt // TM
    row = lambda w_: pl.BlockSpec((1, TM, w_), lambda b, i: (b, i, 0))
    full = lambda x: pl.BlockSpec(x.shape, lambda b, i: (0,) * x.ndim)
    return pl.pallas_call(
        _merge_kernel,
        grid=(b_, nt),
        in_specs=[row(512), row(512), row(d), row(d), row(d),
                  pl.BlockSpec((1, 1, SUB, d), lambda b, i: (b, jnp.minimum(i // nct, 1), 0, 0)),
                  full(wpa), full(wpb), full(wo), full(g2), full(wr), full(rbb)],
        out_specs=[row(d), pl.BlockSpec((TM * SUB, LANE), lambda b, i: (b * nt + i, 0)),
                   pl.BlockSpec((1, SUB, TM), lambda b, i: (b, 0, i))],
        out_shape=[jax.ShapeDtypeStruct((b_, lt, d), F32),
                   jax.ShapeDtypeStruct((b_ * lt * SUB, LANE), F32),
                   jax.ShapeDtypeStruct((b_, SUB, lt), F32)],
        compiler_params=_cparams(2),
        name="merge_route",
    )(a, m, gas, gbs, z, modl, wpa, wpb, wo, g2, wr, rbb)


def _tiles_store(ref, x, lead=()):
    n = x.shape[0]
    for j in range(SUB):
        ref[lead + (pl.ds(j, n, stride=SUB), slice(None))] = x[:, j * LANE:(j + 1) * LANE]


def _tiles_load(ref, n, lead=()):
    return jnp.concatenate([ref[lead + (pl.ds(j, n, stride=SUB), slice(None))] for j in range(SUB)], axis=1)


def _tile_copy(src_hbm, idx_ref, buf, sem, slot, r):
    return pltpu.make_async_copy(src_hbm.at[pl.ds(pl.multiple_of(idx_ref[0, 0, r] * SUB, SUB), SUB)],
                                 buf.at[slot, pl.ds(pl.multiple_of(r * SUB, SUB), SUB)], sem.at[slot])


GATHER_UNROLL = 8


def _gather_start(src_hbm, idx_ref, buf, sem, slot, n):
    def body(r8, c):
        for u in range(GATHER_UNROLL):
            _tile_copy(src_hbm, idx_ref, buf, sem, slot, r8 * GATHER_UNROLL + u).start(priority=u % 2)
        return c
    lax.fori_loop(0, n // GATHER_UNROLL, body, 0)


def _gather_wait(src_hbm, buf, sem, slot):
    pltpu.make_async_copy(src_hbm.at[pl.ds(0, buf.shape[1])], buf.at[slot], sem.at[slot]).wait()


def _sort_kernel(pos_ref, ht_ref, init_hbm, hs_hbm, sem):
    del init_hbm

    def body(r8, c):
        for u in range(GATHER_UNROLL):
            r = r8 * GATHER_UNROLL + u
            pltpu.make_async_copy(
                ht_ref.at[pl.ds(pl.multiple_of(r * SUB, SUB), SUB)],
                hs_hbm.at[pl.ds(pl.multiple_of(pos_ref[0, 0, r] * SUB, SUB), SUB)], sem.at[0]).start(priority=u % 2)
        return c
    lax.fori_loop(0, TM // GATHER_UNROLL, body, 0)
    pltpu.make_async_copy(ht_ref, hs_hbm.at[pl.ds(0, TM * SUB)], sem.at[0]).wait()


def _sort_rows(ht, pos, n_sorted):
    nstep = pos.shape[0]
    init = jnp.zeros((n_sorted * SUB, LANE), F32)
    return pl.pallas_call(
        _sort_kernel,
        grid=(nstep,),
        in_specs=[pl.BlockSpec((1, 1, TM), lambda t: (t, 0, 0), memory_space=pltpu.SMEM),
                  pl.BlockSpec((TM * SUB, LANE), lambda t: (t, 0)), pl.BlockSpec(memory_space=pl.ANY)],
        out_specs=pl.BlockSpec(memory_space=pl.ANY),
        out_shape=jax.ShapeDtypeStruct(init.shape, F32),
        scratch_shapes=[pltpu.SemaphoreType.DMA((1,))],
        input_output_aliases={2: 0},
        compiler_params=_cparams(1),
        name="class_sort",
    )(pos, ht, init)


def _moe_kernel(ea_ref, eb_ref, nu_ref, hs_ref, wrt_ref,
                wga_ref, wua_ref, wda_ref, wgb_ref, wub_ref, wdb_ref, y_ref, gu_a, dn_a, gu_b, dn_b):
    j = pl.program_id(0)
    jp = jnp.maximum(j - 1, 0)

    @pl.when((j == 0) | (ea_ref[j] != ea_ref[jp]))
    def _():
        gu_a[0] = wga_ref[0].astype(BF16)
        gu_a[1] = wua_ref[0].astype(BF16)
        dn_a[...] = wda_ref[0].astype(BF16)

    @pl.when((j == 0) | (eb_ref[j] != eb_ref[jp]))
    def _():
        gu_b[0] = wgb_ref[0].astype(BF16)
        gu_b[1] = wub_ref[0].astype(BF16)
        dn_b[...] = wdb_ref[0].astype(BF16)

    @pl.when(j < nu_ref[0])
    def _():
        x32 = _tiles_load(hs_ref, MOE_BM)
        x = x32.astype(BF16)

        def ffn(gu, dn, e):
            sc = jax.nn.sigmoid(jnp.sum(x32 * wrt_ref[pl.ds(e, 1), :], axis=-1, keepdims=True))
            act = jax.nn.silu(_dot(x, gu[0])) * _dot(x, gu[1])
            return sc, _dot(act.astype(BF16), dn[...])

        s_lo, y_lo = ffn(gu_a, dn_a, ea_ref[j])
        s_hi, y_hi = ffn(gu_b, dn_b, eb_ref[j])
        inv = 1.0 / (s_lo + s_hi)
        _tiles_store(y_ref, (s_lo * inv) * y_lo + (s_hi * inv) * y_hi)

    @pl.when(j >= nu_ref[0])
    def _():
        y_ref[...] = jnp.zeros_like(y_ref)


def _moe(hs, wrt, blk_ea, blk_eb, n_used, wg, wu, wd, layer):
    nblk = blk_ea.shape[0]
    d, de = wg.shape[2:]
    tiles = pl.BlockSpec((MOE_BM * SUB, LANE), lambda j, ea, eb, nu: (j, 0))
    wspec = lambda shp, which: pl.BlockSpec(
        (None, 1) + shp, (lambda j, ea, eb, nu: (layer, ea[j], 0, 0)) if which == 0
        else (lambda j, ea, eb, nu: (layer, eb[j], 0, 0)))
    return pl.pallas_call(
        _moe_kernel,
        grid_spec=pltpu.PrefetchScalarGridSpec(
            num_scalar_prefetch=3,
            grid=(nblk,),
            in_specs=[tiles,
                      pl.BlockSpec(wrt.shape, lambda j, ea, eb, nu: (0, 0)),
                      wspec((d, de), 0), wspec((d, de), 0), wspec((de, d), 0),
                      wspec((d, de), 1), wspec((d, de), 1), wspec((de, d), 1)],
            out_specs=tiles,
            scratch_shapes=[pltpu.VMEM((2, d, de), BF16), pltpu.VMEM((de, d), BF16)] * 2),
        out_shape=jax.ShapeDtypeStruct(hs.shape, F32),
        compiler_params=_cparams(1),
        name="moe_ffn",
    )(blk_ea, blk_eb, n_used, hs, wrt, wg, wu, wd, wg, wu, wd)


def _post_kernel(*refs, final):
    if final:
        idx_ref, idxn_ref, y_hbm, zn_ref, mod_ref, fg_ref, o_ref, buf, sem = refs
    else:
        idx_ref, idxn_ref, y_hbm, zn_ref, mod_ref, o_ref, buf, sem = refs
    t = pl.program_id(0)
    nstep = pl.num_programs(0)
    slot = t % 2

    @pl.when(t == 0)
    def _():
        _gather_start(y_hbm, idx_ref, buf, sem, 0, TM)

    @pl.when(t + 1 < nstep)
    def _():
        _gather_start(y_hbm, idxn_ref, buf, sem, 1 - slot, TM)

    _gather_wait(y_hbm, buf, sem, slot)
    z = zn_ref[0] + mod_ref[0, 0][5:6] * _tiles_load(buf, TM, (slot,))
    if final:
        z = _rms(z, fg_ref[...])
    o_ref[0] = z


def _post(y, pos, zn, modl, nct, final_g=None):
    b_, lt, d = zn.shape
    nt = lt // TM
    final = final_g is not None
    t0 = nct if final else 0
    ntl = nt - t0
    nstep = b_ * ntl
    idx_spec = lambda f: pl.BlockSpec((1, 1, TM), f, memory_space=pltpu.SMEM)
    ins = [pos, pos, y, zn, modl]
    specs = [idx_spec(lambda t: (t, 0, 0)),
             idx_spec(lambda t: (jnp.minimum(t + 1, nstep - 1), 0, 0)),
             pl.BlockSpec(memory_space=pl.ANY),
             pl.BlockSpec((1, TM, d), lambda t: (t // ntl, t % ntl + t0, 0)),
             pl.BlockSpec((1, 1, SUB, d), lambda t: (t // ntl, jnp.minimum((t % ntl + t0) // nct, 1), 0, 0))]
    if final:
        ins.append(final_g)
        specs.append(pl.BlockSpec(final_g.shape, lambda t: (0, 0)))
    return pl.pallas_call(
        functools.partial(_post_kernel, final=final),
        grid=(nstep,),
        in_specs=specs,
        out_specs=pl.BlockSpec((1, TM, d), lambda t: (t // ntl, t % ntl, 0)),
        out_shape=jax.ShapeDtypeStruct((b_, ntl * TM, d), F32),
        scratch_shapes=[pltpu.VMEM((2, TM * SUB, LANE), F32), pltpu.SemaphoreType.DMA((2,))],
        compiler_params=_cparams(1),
        name="unsort_residual",
    )(*ins)


_PAIRS = [(0, 1), (0, 2), (0, 3), (1, 2), (1, 3), (2, 3)]
_CLS_EA = np.array([EPG * g + p[0] for g in range(N_GROUPS) for p in _PAIRS], np.int32)
_CLS_EB = np.array([EPG * g + p[1] for g in range(N_GROUPS) for p in _PAIRS], np.int32)


def _sort_plan(cls):
    n = cls.shape[0]
    nblk = n // MOE_BM + N_CLASSES
    onehot = (cls[:, None] == jnp.arange(N_CLASSES, dtype=jnp.int32)[None, :]).astype(jnp.int32)
    csum = jnp.cumsum(onehot, axis=0)
    counts = csum[-1]
    rank = jnp.take_along_axis(csum, cls[:, None], axis=1)[:, 0] - 1
    padded = (counts + MOE_BM - 1) // MOE_BM * MOE_BM
    pad_end = jnp.cumsum(padded)
    pos = (pad_end - padded)[cls] + rank
    blk_cls = jnp.minimum(
        jnp.searchsorted(pad_end, jnp.arange(nblk, dtype=jnp.int32) * MOE_BM, side='right'),
        N_CLASSES - 1).astype(jnp.int32)
    n_used = (pad_end[-1] // MOE_BM).astype(jnp.int32).reshape(1)
    return pos, jnp.asarray(_CLS_EA)[blk_cls], jnp.asarray(_CLS_EB)[blk_cls], n_used


def _rot_cols(w):
    return jnp.concatenate([-w[:, 8:16], w[:, 0:8], -w[:, 24:32], w[:, 16:24]], axis=1)


def _prep_w_in(w):
    kpe = w[:, 3200:3232]
    pad = jnp.zeros((w.shape[0], C_END - C_KPE - 2 * MLA_ROPE), w.dtype)
    return jnp.concatenate([w[:, :3200], w[:, 3232:], kpe, _rot_cols(kpe), pad], axis=1).astype(BF16)


def _prep_mla(w_uq, w_ukv):
    odd = (jnp.arange(MLA_HEADS) % 2 == 1)[None, :, None]
    qh = w_uq.reshape(MLA_Q_RANK, MLA_HEADS, MLA_NOPE + MLA_ROPE)
    nope, pe = qh[:, :, :MLA_NOPE], qh[:, :, MLA_NOPE:]
    pe_sw = _rot_cols(pe.reshape(MLA_Q_RANK * MLA_HEADS, MLA_ROPE)).reshape(MLA_Q_RANK, MLA_HEADS, MLA_ROPE)
    wq = jnp.where(odd, jnp.concatenate([nope, pe, pe_sw], axis=2), jnp.concatenate([pe, pe_sw, nope], axis=2))
    kvh = w_ukv.reshape(MLA_KV_RANK, MLA_HEADS, MLA_NOPE + MLA_V)
    kn, vh = kvh[:, :, :MLA_NOPE], kvh[:, :, MLA_NOPE:]
    wkv = jnp.where(odd, jnp.concatenate([kn, vh], axis=2), jnp.concatenate([vh, kn], axis=2))
    return (wq.reshape(MLA_Q_RANK, MLA_HEADS * LANE).astype(BF16),
            wkv.reshape(MLA_KV_RANK, MLA_HEADS * LANE).astype(BF16))


def _rope_tables(n_ctx, n_lat):
    rows = n_lat // GRID_W
    rowp = jnp.broadcast_to(jnp.arange(rows, dtype=F32)[:, None], (rows, GRID_W)).reshape(-1)
    colp = jnp.broadcast_to(jnp.arange(GRID_W, dtype=F32)[None, :], (rows, GRID_W)).reshape(-1)
    axis_dim = MLA_ROPE // 2
    inv_freq = ROPE_THETA ** (-jnp.arange(0, axis_dim, 2, dtype=F32) / axis_dim)
    ar, ac = rowp[:, None] * inv_freq, colp[:, None] * inv_freq
    cos32 = jnp.concatenate([jnp.cos(ar), jnp.cos(ar), jnp.cos(ac), jnp.cos(ac)], axis=1)
    sin32 = jnp.concatenate([jnp.sin(ar), jnp.sin(ar), jnp.sin(ac), jnp.sin(ac)], axis=1)
    cos32 = jnp.concatenate([jnp.ones((n_ctx, MLA_ROPE), F32), cos32], axis=0)
    sin32 = jnp.concatenate([jnp.zeros((n_ctx, MLA_ROPE), F32), sin32], axis=0)
    lt = n_ctx + n_lat
    one, zero = jnp.ones((lt, MLA_NOPE), F32), jnp.zeros((lt, MLA_NOPE), F32)
    z32 = jnp.zeros((lt, MLA_ROPE), F32)
    sc = MLA_SCALE * LOG2E
    cqe = jnp.concatenate([cos32, z32, one], axis=1) * sc
    sqe = jnp.concatenate([sin32, z32, zero], axis=1) * sc
    cqo = jnp.concatenate([one, cos32, z32], axis=1) * sc
    sqo = jnp.concatenate([zero, sin32, z32], axis=1) * sc
    kc = jnp.concatenate([cos32, sin32, zero], axis=1)
    return cqe, sqe, cqo, sqo, kc


def _lower_bounds(gamma):
    p = jnp.cumsum(jax.nn.softmax(gamma.astype(F32), axis=0), axis=0)
    return p - p[0:1]


def kernel(x, c, ctx, c_ctx, w_mod, b_mod, norm1_g, norm2_g, w_in, gamma_fwd, gamma_bwd, hg_norm_g,
           q_norm_g, kv_norm_g, w_uq, w_ukv, w_pa, w_pb, w_o, w_router, router_bias,
           w_gate_e, w_up_e, w_down_e, final_g):
    b_, n_lat, d = x.shape
    n_ctx = ctx.shape[1]
    depth = w_mod.shape[0]
    assert n_ctx % TM == 0 and n_lat % TM == 0 and n_lat % GRID_W == 0 and b_ + 1 <= SUB
    nct = n_ctx // TM
    lt = n_ctx + n_lat
    nt = lt // TM

    cc = jnp.concatenate([c, c_ctx[None, :], jnp.zeros((SUB - b_ - 1, d), F32)], axis=0)
    mod = _modulation(cc, w_mod, b_mod).reshape(depth, SUB, 6, d)
    mod_x = mod[:, :b_]
    mod_c = jnp.broadcast_to(mod[:, b_:b_ + 1], mod_x.shape)
    mod_t = jnp.stack([mod_c, mod_x], axis=2)
    mod_t = jnp.concatenate([mod_t, jnp.zeros((depth, b_, 2, SUB - 6, d), F32)], axis=3)

    cqe, sqe, cqo, sqo, kc = _rope_tables(n_ctx, n_lat)
    lbf, lbb = _lower_bounds(gamma_fwd), _lower_bounds(gamma_bwd)
    wr = jnp.concatenate([w_router, jnp.zeros((d, LANE - N_EXPERTS), F32)], axis=1)
    rbb = jnp.broadcast_to(router_bias.astype(F32)[:, None], (N_EXPERTS, TM))

    z = jnp.concatenate([ctx, x], axis=1)
    out = None
    for l in range(depth):
        last = l == depth - 1
        wq, wkv = _prep_mla(w_uq[l], w_ukv[l])
        (q, v, lff, kf, lfb, kb, g, gas, gbs, qq, kk, vv) = _inproj(
            z, mod_t[l], norm1_g[l][None], _prep_w_in(w_in[l]), lbf[l][None], lbb[l][None],
            q_norm_g[l][None], kv_norm_g[l][None], wq, wkv, cqe, sqe, cqo, sqo, kc, nct)
        o_f = _hgrn(q, v, lff, kf, nct, reverse=False)
        a = _hgrn(q, v, lfb, kb, nct, reverse=True, extra=(o_f, g, hg_norm_g[l][None]))
        m = _attention(qq, kk, vv, n_ctx)
        zn, ht, route = _merge(a, m, gas, gbs, z, mod_t[l], w_pa[l].astype(BF16), w_pb[l].astype(BF16),
                               w_o[l].astype(BF16), norm2_g[l][None], wr, rbb, nct)
        pos, blk_ea, blk_eb, n_used = _sort_plan(route[:, 0, :].reshape(-1).astype(jnp.int32))
        pos = pos.reshape(b_, nt, 1, TM)
        hs = _sort_rows(ht, pos.reshape(-1, 1, TM), blk_ea.shape[0] * MOE_BM)
        y = _moe(hs, w_router.T, blk_ea, blk_eb, n_used, w_gate_e, w_up_e, w_down_e, l)
        if last:
            out = _post(y, pos[:, nct:].reshape(-1, 1, TM), zn, mod_t[l], nct, final_g[None])
        else:
            z = _post(y, pos.reshape(-1, 1, TM), zn, mod_t[l], nct)
    return out
```

```python
import functools

import numpy as np
import jax
import jax.numpy as jnp
from jax import lax
from jax.experimental import pallas as pl
from jax.experimental.pallas import tpu as pltpu

F32 = jnp.float32
BF16 = jnp.bfloat16

EPS = 1e-6
GRID_W = 64
ROPE_THETA = 10000.0
HG_HEADS = 4
HG_DK = 128
HG_WIDTH = 512
MLA_HEADS = 8
MLA_NOPE = 64
MLA_ROPE = 32
MLA_V = 64
MLA_Q_RANK = 384
MLA_KV_RANK = 256
MLA_SCALE = (MLA_NOPE + MLA_ROPE) ** -0.5
LOG2E = 1.4426950408889634
N_EXPERTS = 16
N_GROUPS = 4
EPG = 4
N_CLASSES = N_GROUPS * 6

LANE = 128
SUB = 8
TM = 256
CHUNK = 128
MOE_BM = 256
VMEM_LIMIT = 56 * 1024 * 1024

C_Q, C_I, C_FF, C_FB, C_G, C_DQ, C_DKV, C_GA, C_GB, C_KPE, C_END = (
    0, 512, 1024, 1536, 2048, 2560, 2944, 3200, 4224, 5248, 5376)


def _cparams(n_axes):
    return pltpu.CompilerParams(dimension_semantics=("arbitrary",) * n_axes,
                                vmem_limit_bytes=VMEM_LIMIT)


def _rms(x, g):
    y = x * lax.rsqrt(jnp.mean(x * x, axis=-1, keepdims=True) + EPS)
    return y * g


def _dot(a, b):
    return jnp.dot(a, b, preferred_element_type=F32)


def _dot_nt(a, b):
    return lax.dot_general(a, b, (((1,), (1,)), ((), ())), preferred_element_type=F32)


def _dot_tn(a, b):
    return lax.dot_general(a, b, (((0,), (0,)), ((), ())), preferred_element_type=F32)


def _split3(x):
    hi = x.astype(BF16)
    r1 = x - hi.astype(F32)
    mid = r1.astype(BF16)
    lo = (r1 - mid.astype(F32)).astype(BF16)
    return hi, mid, lo


def _dot_f32(a, b, passes=6):
    a0, a1, a2 = _split3(a)
    b0, b1, b2 = _split3(b)
    out = _dot(a0, b0) + (_dot(a0, b1) + _dot(a1, b0))
    if passes == 6:
        out = out + (_dot(a0, b2) + _dot(a1, b1) + _dot(a2, b0))
    return out


def _mod_kernel(c_ref, w_ref, b_ref, o_ref):
    c = c_ref[...]
    s = c * jax.nn.sigmoid(c)
    o_ref[0] = _dot_f32(s, w_ref[0], passes=3) + b_ref[0]


def _modulation(cc, w_mod, b_mod):
    depth, d, n6 = w_mod.shape
    nb = 1536
    return pl.pallas_call(
        _mod_kernel,
        grid=(depth, n6 // nb),
        in_specs=[pl.BlockSpec((SUB, d), lambda l, j: (0, 0)),
                  pl.BlockSpec((1, d, nb), lambda l, j: (l, 0, j)),
                  pl.BlockSpec((1, 1, nb), lambda l, j: (l, 0, j))],
        out_specs=pl.BlockSpec((1, SUB, nb), lambda l, j: (l, 0, j)),
        out_shape=jax.ShapeDtypeStruct((depth, SUB, n6), F32),
        compiler_params=_cparams(2),
        name="modulation",
    )(cc, w_mod, b_mod.reshape(depth, 1, n6))


def _inproj_kernel(z_ref, mod_ref, g1_ref, w_ref, lbf_ref, lbb_ref, qg_ref, kvg_ref,
                   wq_ref, wkv_ref, cqe_ref, sqe_ref, cqo_ref, sqo_ref, kc_ref,
                   q_o, v_o, lff_o, kf_o, lfb_o, kb_o, g_o, gas_o, gbs_o, qq_o, kk_o, vv_o):
    mod = mod_ref[0, 0]
    h = _rms(z_ref[0], g1_ref[...]) * (1.0 + mod[1:2]) + mod[0:1]
    hb = h.astype(BF16)

    p = _dot(hb, w_ref[...])

    def seg(a, b):
        return p[:, a:b]

    q_o[0] = seg(C_Q, C_I).astype(BF16)
    v_o[0] = seg(C_I, C_FF).astype(BF16)
    for a, lb_ref, lf_o, k_o in ((C_FF, lbf_ref, lff_o, kf_o), (C_FB, lbb_ref, lfb_o, kb_o)):
        lb = lb_ref[...]
        f = lb + (1.0 - lb) * jax.nn.sigmoid(seg(a, a + HG_WIDTH))
        lf_o[0] = jnp.log(f) * LOG2E
        k_o[0] = (1.0 - f).astype(BF16)
    g_o[0] = seg(C_G, C_DQ).astype(BF16)
    gas_o[0] = jax.nn.sigmoid(seg(C_GA, C_GB)).astype(BF16)
    gbs_o[0] = jax.nn.sigmoid(seg(C_GB, C_KPE)).astype(BF16)

    lane = lax.broadcasted_iota(jnp.int32, (TM, LANE), 1)
    qn = _rms(seg(C_DQ, C_DKV), qg_ref[...]).astype(BF16)
    qa = _dot(qn, wq_ref[...])
    tabs = ((cqe_ref[...], sqe_ref[...]), (cqo_ref[...], sqo_ref[...]))
    for hh in range(MLA_HEADS):
        sl = slice(hh * LANE, (hh + 1) * LANE)
        cq, sq = tabs[hh % 2]
        blk = qa[:, sl]
        qq_o[0, :, sl] = (blk * cq + pltpu.roll(blk, LANE - MLA_ROPE, 1) * sq).astype(BF16)
    kvn = _rms(seg(C_DKV, C_GA), kvg_ref[...]).astype(BF16)
    kv = _dot(kvn, wkv_ref[...])
    kp = seg(C_KPE, C_END) * kc_ref[...]
    kpe_e = jnp.where(lane < MLA_ROPE, kp + pltpu.roll(kp, LANE - MLA_ROPE, 1), 0.0)
    kpe_o = pltpu.roll(kpe_e, MLA_NOPE, 1)
    one_e = jnp.where(lane == MLA_V, 1.0, 0.0)
    one_o = jnp.where(lane == 0, 1.0, 0.0)
    for hh in range(MLA_HEADS):
        sl = slice(hh * LANE, (hh + 1) * LANE)
        blk = kv[:, sl]
        if hh % 2 == 0:
            kk_o[0, :, sl] = jnp.where(lane >= MLA_V, blk, kpe_e).astype(BF16)
            vv_o[0, :, sl] = jnp.where(lane < MLA_V, blk, one_e).astype(BF16)
        else:
            kk_o[0, :, sl] = jnp.where(lane < MLA_NOPE, blk, kpe_o).astype(BF16)
            vv_o[0, :, sl] = jnp.where(lane >= MLA_NOPE, blk, one_o).astype(BF16)


def _inproj(z, modl, g1, w, lbf, lbb, qg, kvg, wq, wkv, cqe, sqe, cqo, sqo, kc, nct):
    b_, lt, d = z.shape
    nt = lt // TM
    row = lambda w_: pl.BlockSpec((1, TM, w_), lambda b, i: (b, i, 0))
    full = lambda a: pl.BlockSpec(a.shape, lambda b, i: (0,) * a.ndim)
    tab = pl.BlockSpec((TM, LANE), lambda b, i: (i, 0))
    f32o = lambda w_: jax.ShapeDtypeStruct((b_, lt, w_), F32)
    bf16o = lambda w_: jax.ShapeDtypeStruct((b_, lt, w_), BF16)
    return pl.pallas_call(
        _inproj_kernel,
        grid=(b_, nt),
        in_specs=[row(d),
                  pl.BlockSpec((1, 1, SUB, d), lambda b, i: (b, jnp.minimum(i // nct, 1), 0, 0)),
                  full(g1), full(w), full(lbf), full(lbb), full(qg), full(kvg),
                  full(wq), full(wkv), tab, tab, tab, tab, tab],
        out_specs=[row(512)] * 7 + [row(1024)] * 5,
        out_shape=[bf16o(512), bf16o(512), f32o(512), bf16o(512), f32o(512), bf16o(512), bf16o(512)]
        + [bf16o(1024)] * 5,
        compiler_params=_cparams(2),
        name="inproj",
    )(z, modl, g1, w, lbf, lbb, qg, kvg, wq, wkv, cqe, sqe, cqo, sqo, kc)


N_LEVELS = 7


def _chunk_tables(reverse):
    c = CHUNK
    t = np.arange(c)[:, None]
    u = np.arange(c)[None, :]
    tri = ((u >= t) if reverse else (u <= t)).astype(np.float32)
    lvl = np.full((c, c), N_LEVELS + 1, np.int32)
    lvl[t == u] = 0
    for l in range(N_LEVELS, 0, -1):
        m = 1 << l
        same = (t // m) == (u // m)
        lvl[same & ((u > t) if reverse else (u < t))] = l
    return jnp.asarray(tri, dtype=BF16), jnp.asarray(lvl)


def _level_ref(bs_ref, hsl, l, reverse):
    w = HG_DK
    sub = lax.broadcasted_iota(jnp.int32, (SUB, w), 0)
    off = 0 if reverse else -1
    m = 1 << l

    def row(r):
        return jnp.broadcast_to(bs_ref[r:r + 1, hsl], (SUB, w))

    pieces = []
    for j in range(CHUNK // SUB):
        base = j * SUB
        if m >= SUB:
            pieces.append(row((base // m) * m + m // 2 + off))
        else:
            p = row(base + m // 2 + off)
            for i in range(1, SUB // m):
                p = jnp.where(sub < i * m, p, row(base + i * m + m // 2 + off))
            pieces.append(p)
    return jnp.concatenate(pieces, axis=0)


def _hgrn_kernel(*refs, reverse, readout):
    if readout:
        q_ref, v_ref, lf_ref, k_ref, tri_ref, lvl_ref, of_ref, g_ref, gain_ref, o_ref, s_ref, bs_ref = refs
    else:
        q_ref, v_ref, lf_ref, k_ref, tri_ref, lvl_ref, o_ref, s_ref, bs_ref = refs

    @pl.when(pl.program_id(1) == 0)
    def _():
        s_ref[...] = jnp.zeros_like(s_ref)

    c = CHUNK
    nch = TM // c
    e = lambda x: jnp.exp2(x).astype(BF16)

    lvl = lvl_ref[...]
    for ci in range(nch):
        cidx = (nch - 1 - ci) if reverse else ci
        rows = slice(cidx * c, (cidx + 1) * c)
        lf = lf_ref[0, rows, :]
        hi = lf.astype(BF16)
        lo = (lf - hi.astype(F32)).astype(BF16)
        bsc = bs_ref.at[ci]
        bsc[...] = _dot(tri_ref[...], hi) + _dot(tri_ref[...], lo)
        for hh in range(HG_HEADS):
            hsl = slice(hh * HG_DK, (hh + 1) * HG_DK)
            b = bsc[:, hsl]
            btot = b[0:1] if reverse else b[c - 1:c]
            q = q_ref[0, rows, hsl]
            k = k_ref[0, rows, hsl]
            ix = (0, rows, hsl)
            v = v_ref[ix]
            st = s_ref[hh]
            att = jnp.where(lvl == 0, _dot_nt(q, k), 0.0)
            for l in range(1, N_LEVELS + 1):
                d = b - _level_ref(bsc, hsl, l, reverse)
                att = jnp.where(lvl == l, _dot_nt(q * e(d), k * e(-d)), att)
            o = _dot_nt(q * e(b), st.astype(BF16)) + _dot(att.astype(BF16), v)
            s_ref[hh] = st * jnp.exp2(btot) + _dot_tn(v, k * e(btot - b))
            if readout:
                o = o + of_ref[ix]
                o = o * lax.rsqrt(jnp.mean(o * o, axis=-1, keepdims=True) + EPS)
                o = o * gain_ref[:, hsl]
                g = g_ref[ix].astype(F32)
                o = o * (g * jax.nn.sigmoid(g))
            o_ref[ix] = o.astype(o_ref.dtype)


def _hgrn(q, v, lf, k, nct, reverse, extra=None):
    b_, lt, w = q.shape
    nt = lt // TM
    if reverse:
        tile = lambda i: jnp.where(i < nct, nct - 1 - i, nt - 1 - (i - nct))
    else:
        tile = lambda i: i
    row = pl.BlockSpec((1, TM, w), lambda b, i: (b, tile(i), 0))
    full = lambda a: pl.BlockSpec(a.shape, lambda b, i: (0,) * a.ndim)
    tri, lvl = _chunk_tables(reverse)
    ins = [q, v, lf, k, tri, lvl]
    specs = [row] * 4 + [full(tri), full(lvl)]
    if extra is not None:
        o_f, g, gain = extra
        ins += [o_f, g, gain]
        specs += [row, row, full(gain)]
    return pl.pallas_call(
        functools.partial(_hgrn_kernel, reverse=reverse, readout=extra is not None),
        grid=(b_, nt),
        in_specs=specs,
        out_specs=row,
        out_shape=jax.ShapeDtypeStruct((b_, lt, w), BF16),
        scratch_shapes=[pltpu.VMEM((HG_HEADS, HG_DK, HG_DK), F32), pltpu.VMEM((TM // CHUNK, CHUNK, w), F32)],
        compiler_params=_cparams(2),
        name="hgrn_bwd" if reverse else "hgrn_fwd",
    )(*ins)


KV_CHUNK = 256


def _attn_kernel(q_ref, k_ref, v_ref, o_ref, *, n_ctx, nct):
    i = pl.program_id(2)
    lane = lax.broadcasted_iota(jnp.int32, (TM, LANE), 1)

    def run(nk):
        outs = []
        for hh in range(2):
            sl = slice(hh * LANE, (hh + 1) * LANE)
            q = q_ref[0, :, sl]
            m = acc = None
            edges = [0] + list(range(n_ctx, nk + 1, KV_CHUNK))
            for c in range(len(edges) - 1):
                ks = slice(edges[c], edges[c + 1])
                s = _dot_nt(q, k_ref[0, ks, sl])
                mc = jnp.max(s, axis=-1, keepdims=True)
                if c == 0:
                    m = mc
                    acc = _dot(jnp.exp2(s - m).astype(BF16), v_ref[0, ks, sl])
                else:
                    m_new = jnp.maximum(m, mc)
                    acc = acc * jnp.exp2(m - m_new) + _dot(jnp.exp2(s - m_new).astype(BF16), v_ref[0, ks, sl])
                    m = m_new
            den = acc[:, MLA_V:MLA_V + 1] if hh == 0 else acc[:, 0:1]
            outs.append(acc / den)
        o_ref[0] = jnp.where(lane < MLA_V, outs[0], outs[1]).astype(o_ref.dtype)

    @pl.when(i < nct)
    def _():
        run(n_ctx)

    @pl.when(i >= nct)
    def _():
        run(k_ref.shape[1])


def _attention(qq, kk, vv, n_ctx):
    b_, lt, _ = qq.shape
    nt = lt // TM
    npair = MLA_HEADS // 2
    return pl.pallas_call(
        functools.partial(_attn_kernel, n_ctx=n_ctx, nct=n_ctx // TM),
        grid=(b_, npair, nt),
        in_specs=[pl.BlockSpec((1, TM, 2 * LANE), lambda b, j, i: (b, i, j)),
                  pl.BlockSpec((1, lt, 2 * LANE), lambda b, j, i: (b, 0, j)),
                  pl.BlockSpec((1, lt, 2 * LANE), lambda b, j, i: (b, 0, j))],
        out_specs=pl.BlockSpec((1, TM, LANE), lambda b, j, i: (b, i, j)),
        out_shape=jax.ShapeDtypeStruct((b_, lt, npair * LANE), BF16),
        compiler_params=_cparams(3),
        name="mla_attention",
    )(qq, kk, vv)


def _route(sel):
    r = lambda x, e: x[e:e + 1]
    best = None
    for g in range(N_GROUPS):
        u = [r(sel, EPG * g + j) for j in range(EPG)]
        gs = None
        for a in range(EPG):
            for b in range(a + 1, EPG):
                pr = u[a] + u[b]
                gs = pr if gs is None else jnp.maximum(gs, pr)
        if best is None:
            best, gi = gs, jnp.zeros_like(gs, dtype=jnp.int32)
        else:
            upd = gs > best
            best = jnp.where(upd, gs, best)
            gi = jnp.where(upd, g, gi)

    def pick(x, j):
        out = r(x, j)
        for g in range(1, N_GROUPS):
            out = jnp.where(gi == g, r(x, EPG * g + j), out)
        return out

    u = [pick(sel, j) for j in range(EPG)]

    def argmax4(vals):
        bv, bi = vals[0], jnp.zeros_like(gi)
        for j in range(1, EPG):
            upd = vals[j] > bv
            bv = jnp.where(upd, vals[j], bv)
            bi = jnp.where(upd, j, bi)
        return bi

    l1 = argmax4(u)
    l2 = argmax4([jnp.where(l1 == j, -jnp.inf, u[j]) for j in range(EPG)])
    lo = jnp.minimum(l1, l2)
    hi = jnp.maximum(l1, l2)
    pair = jnp.where(lo == 0, hi - 1, jnp.where(lo == 1, hi + 1, 5))
    cls = (gi * 6 + pair).astype(F32)
    return jnp.concatenate([cls] + [jnp.zeros_like(cls)] * (SUB - 1), axis=0)


def _merge_kernel(a_ref, m_ref, gas_ref, gbs_ref, z_ref, mod_ref, wpa_ref, wpb_ref, wo_ref,
                  g2_ref, wr_ref, rb_ref, zn_o, ht_o, route_o):
    mod = mod_ref[0, 0]
    pa = _dot(a_ref[0].astype(BF16), wpa_ref[...])
    pb = _dot(m_ref[0].astype(BF16), wpb_ref[...])
    mixed = gas_ref[0] * pa + gbs_ref[0] * pb
    zn = z_ref[0] + mod[2:3] * _dot(mixed.astype(BF16), wo_ref[...])
    zn_o[0] = zn
    h2 = _rms(zn, g2_ref[...]) * (1.0 + mod[4:5]) + mod[3:4]
    _tiles_store(ht_o, h2)
    logits = _dot_f32(h2, wr_ref[...], passes=3)
    sc = jax.nn.sigmoid(logits.T[:N_EXPERTS])
    route_o[0] = _route(sc + rb_ref[...])


def _merge(a, m, gas, gbs, z, modl, wpa, wpb, wo, g2, wr, rbb, nct):
    b_, lt, d = z.shape
    nt = lt // TM
    row = lambda w_: pl.BlockSpec((1, TM, w_), lambda b, i: (b, i, 0))
    full = lambda x: pl.BlockSpec(x.shape, lambda b, i: (0,) * x.ndim)
    return pl.pallas_call(
        _merge_kernel,
        grid=(b_, nt),
        in_specs=[row(512), row(512), row(d), row(d), row(d),
                  pl.BlockSpec((1, 1, SUB, d), lambda b, i: (b, jnp.minimum(i // nct, 1), 0, 0)),
                  full(wpa), full(wpb), full(wo), full(g2), full(wr), full(rbb)],
        out_specs=[row(d), pl.BlockSpec((TM * SUB, LANE), lambda b, i: (b * nt + i, 0)),
                   pl.BlockSpec((1, SUB, TM), lambda b, i: (b, 0, i))],
        out_shape=[jax.ShapeDtypeStruct((b_, lt, d), F32),
                   jax.ShapeDtypeStruct((b_ * lt * SUB, LANE), F32),
                   jax.ShapeDtypeStruct((b_, SUB, lt), F32)],
        compiler_params=_cparams(2),
        name="merge_route",
    )(a, m, gas, gbs, z, modl, wpa, wpb, wo, g2, wr, rbb)


def _tiles_store(ref, x, lead=()):
    n = x.shape[0]
    for j in range(SUB):
        ref[lead + (pl.ds(j, n, stride=SUB), slice(None))] = x[:, j * LANE:(j + 1) * LANE]


def _tiles_load(ref, n, lead=()):
    return jnp.concatenate([ref[lead + (pl.ds(j, n, stride=SUB), slice(None))] for j in range(SUB)], axis=1)


def _tile_copy(src_hbm, idx_ref, buf, sem, slot, r):
    return pltpu.make_async_copy(src_hbm.at[pl.ds(pl.multiple_of(idx_ref[0, 0, r] * SUB, SUB), SUB)],
                                 buf.at[slot, pl.ds(pl.multiple_of(r * SUB, SUB), SUB)], sem.at[slot])


GATHER_UNROLL = 8


def _gather_start(src_hbm, idx_ref, buf, sem, slot, n):
    def body(r8, c):
        for u in range(GATHER_UNROLL):
            _tile_copy(src_hbm, idx_ref, buf, sem, slot, r8 * GATHER_UNROLL + u).start(priority=u % 2)
        return c
    lax.fori_loop(0, n // GATHER_UNROLL, body, 0)


def _gather_wait(src_hbm, buf, sem, slot):
    pltpu.make_async_copy(src_hbm.at[pl.ds(0, buf.shape[1])], buf.at[slot], sem.at[slot]).wait()


def _sort_kernel(pos_ref, ht_ref, init_hbm, hs_hbm, sem):
    del init_hbm

    def body(r8, c):
        for u in range(GATHER_UNROLL):
            r = r8 * GATHER_UNROLL + u
            pltpu.make_async_copy(
                ht_ref.at[pl.ds(pl.multiple_of(r * SUB, SUB), SUB)],
                hs_hbm.at[pl.ds(pl.multiple_of(pos_ref[0, 0, r] * SUB, SUB), SUB)], sem.at[0]).start(priority=u % 2)
        return c
    lax.fori_loop(0, TM // GATHER_UNROLL, body, 0)
    pltpu.make_async_copy(ht_ref, hs_hbm.at[pl.ds(0, TM * SUB)], sem.at[0]).wait()


def _sort_rows(ht, pos, n_sorted):
    nstep = pos.shape[0]
    init = jnp.zeros((n_sorted * SUB, LANE), F32)
    return pl.pallas_call(
        _sort_kernel,
        grid=(nstep,),
        in_specs=[pl.BlockSpec((1, 1, TM), lambda t: (t, 0, 0), memory_space=pltpu.SMEM),
                  pl.BlockSpec((TM * SUB, LANE), lambda t: (t, 0)), pl.BlockSpec(memory_space=pl.ANY)],
        out_specs=pl.BlockSpec(memory_space=pl.ANY),
        out_shape=jax.ShapeDtypeStruct(init.shape, F32),
        scratch_shapes=[pltpu.SemaphoreType.DMA((1,))],
        input_output_aliases={2: 0},
        compiler_params=_cparams(1),
        name="class_sort",
    )(pos, ht, init)


def _moe_kernel(ea_ref, eb_ref, nu_ref, hs_ref, wrt_ref,
                wga_ref, wua_ref, wda_ref, wgb_ref, wub_ref, wdb_ref, y_ref, gu_a, dn_a, gu_b, dn_b):
    j = pl.program_id(0)
    jp = jnp.maximum(j - 1, 0)

    @pl.when((j == 0) | (ea_ref[j] != ea_ref[jp]))
    def _():
        gu_a[0] = wga_ref[0].astype(BF16)
        gu_a[1] = wua_ref[0].astype(BF16)
        dn_a[...] = wda_ref[0].astype(BF16)

    @pl.when((j == 0) | (eb_ref[j] != eb_ref[jp]))
    def _():
        gu_b[0] = wgb_ref[0].astype(BF16)
        gu_b[1] = wub_ref[0].astype(BF16)
        dn_b[...] = wdb_ref[0].astype(BF16)

    @pl.when(j < nu_ref[0])
    def _():
        x32 = _tiles_load(hs_ref, MOE_BM)
        x = x32.astype(BF16)

        def ffn(gu, dn, e):
            sc = jax.nn.sigmoid(jnp.sum(x32 * wrt_ref[pl.ds(e, 1), :], axis=-1, keepdims=True))
            act = jax.nn.silu(_dot(x, gu[0])) * _dot(x, gu[1])
            return sc, _dot(act.astype(BF16), dn[...])

        s_lo, y_lo = ffn(gu_a, dn_a, ea_ref[j])
        s_hi, y_hi = ffn(gu_b, dn_b, eb_ref[j])
        inv = 1.0 / (s_lo + s_hi)
        _tiles_store(y_ref, (s_lo * inv) * y_lo + (s_hi * inv) * y_hi)

    @pl.when(j >= nu_ref[0])
    def _():
        y_ref[...] = jnp.zeros_like(y_ref)


def _moe(hs, wrt, blk_ea, blk_eb, n_used, wg, wu, wd, layer):
    nblk = blk_ea.shape[0]
    d, de = wg.shape[2:]
    tiles = pl.BlockSpec((MOE_BM * SUB, LANE), lambda j, ea, eb, nu: (j, 0))
    wspec = lambda shp, which: pl.BlockSpec(
        (None, 1) + shp, (lambda j, ea, eb, nu: (layer, ea[j], 0, 0)) if which == 0
        else (lambda j, ea, eb, nu: (layer, eb[j], 0, 0)))
    return pl.pallas_call(
        _moe_kernel,
        grid_spec=pltpu.PrefetchScalarGridSpec(
            num_scalar_prefetch=3,
            grid=(nblk,),
            in_specs=[tiles,
                      pl.BlockSpec(wrt.shape, lambda j, ea, eb, nu: (0, 0)),
                      wspec((d, de), 0), wspec((d, de), 0), wspec((de, d), 0),
                      wspec((d, de), 1), wspec((d, de), 1), wspec((de, d), 1)],
            out_specs=tiles,
            scratch_shapes=[pltpu.VMEM((2, d, de), BF16), pltpu.VMEM((de, d), BF16)] * 2),
        out_shape=jax.ShapeDtypeStruct(hs.shape, F32),
        compiler_params=_cparams(1),
        name="moe_ffn",
    )(blk_ea, blk_eb, n_used, hs, wrt, wg, wu, wd, wg, wu, wd)


def _post_kernel(*refs, final):
    if final:
        idx_ref, idxn_ref, y_hbm, zn_ref, mod_ref, fg_ref, o_ref, buf, sem = refs
    else:
        idx_ref, idxn_ref, y_hbm, zn_ref, mod_ref, o_ref, buf, sem = refs
    t = pl.program_id(0)
    nstep = pl.num_programs(0)
    slot = t % 2

    @pl.when(t == 0)
    def _():
        _gather_start(y_hbm, idx_ref, buf, sem, 0, TM)

    @pl.when(t + 1 < nstep)
    def _():
        _gather_start(y_hbm, idxn_ref, buf, sem, 1 - slot, TM)

    _gather_wait(y_hbm, buf, sem, slot)
    z = zn_ref[0] + mod_ref[0, 0][5:6] * _tiles_load(buf, TM, (slot,))
    if final:
        z = _rms(z, fg_ref[...])
    o_ref[0] = z


def _post(y, pos, zn, modl, nct, final_g=None):
    b_, lt, d = zn.shape
    nt = lt // TM
    final = final_g is not None
    t0 = nct if final else 0
    ntl = nt - t0
    nstep = b_ * ntl
    idx_spec = lambda f: pl.BlockSpec((1, 1, TM), f, memory_space=pltpu.SMEM)
    ins = [pos, pos, y, zn, modl]
    specs = [idx_spec(lambda t: (t, 0, 0)),
             idx_spec(lambda t: (jnp.minimum(t + 1, nstep - 1), 0, 0)),
             pl.BlockSpec(memory_space=pl.ANY),
             pl.BlockSpec((1, TM, d), lambda t: (t // ntl, t % ntl + t0, 0)),
             pl.BlockSpec((1, 1, SUB, d), lambda t: (t // ntl, jnp.minimum((t % ntl + t0) // nct, 1), 0, 0))]
    if final:
        ins.append(final_g)
        specs.append(pl.BlockSpec(final_g.shape, lambda t: (0, 0)))
    return pl.pallas_call(
        functools.partial(_post_kernel, final=final),
        grid=(nstep,),
        in_specs=specs,
        out_specs=pl.BlockSpec((1, TM, d), lambda t: (t // ntl, t % ntl, 0)),
        out_shape=jax.ShapeDtypeStruct((b_, ntl * TM, d), F32),
        scratch_shapes=[pltpu.VMEM((2, TM * SUB, LANE), F32), pltpu.SemaphoreType.DMA((2,))],
        compiler_params=_cparams(1),
        name="unsort_residual",
    )(*ins)


_PAIRS = [(0, 1), (0, 2), (0, 3), (1, 2), (1, 3), (2, 3)]
_CLS_EA = np.array([EPG * g + p[0] for g in range(N_GROUPS) for p in _PAIRS], np.int32)
_CLS_EB = np.array([EPG * g + p[1] for g in range(N_GROUPS) for p in _PAIRS], np.int32)


def _sort_plan(cls):
    n = cls.shape[0]
    nblk = n // MOE_BM + N_CLASSES
    onehot = (cls[:, None] == jnp.arange(N_CLASSES, dtype=jnp.int32)[None, :]).astype(jnp.int32)
    csum = jnp.cumsum(onehot, axis=0)
    counts = csum[-1]
    rank = jnp.take_along_axis(csum, cls[:, None], axis=1)[:, 0] - 1
    padded = (counts + MOE_BM - 1) // MOE_BM * MOE_BM
    pad_end = jnp.cumsum(padded)
    pos = (pad_end - padded)[cls] + rank
    blk_cls = jnp.minimum(
        jnp.searchsorted(pad_end, jnp.arange(nblk, dtype=jnp.int32) * MOE_BM, side='right'),
        N_CLASSES - 1).astype(jnp.int32)
    n_used = (pad_end[-1] // MOE_BM).astype(jnp.int32).reshape(1)
    return pos, jnp.asarray(_CLS_EA)[blk_cls], jnp.asarray(_CLS_EB)[blk_cls], n_used


def _rot_cols(w):
    return jnp.concatenate([-w[:, 8:16], w[:, 0:8], -w[:, 24:32], w[:, 16:24]], axis=1)


def _prep_w_in(w):
    kpe = w[:, 3200:3232]
    pad = jnp.zeros((w.shape[0], C_END - C_KPE - 2 * MLA_ROPE), w.dtype)
    return jnp.concatenate([w[:, :3200], w[:, 3232:], kpe, _rot_cols(kpe), pad], axis=1).astype(BF16)


def _prep_mla(w_uq, w_ukv):
    odd = (jnp.arange(MLA_HEADS) % 2 == 1)[None, :, None]
    qh = w_uq.reshape(MLA_Q_RANK, MLA_HEADS, MLA_NOPE + MLA_ROPE)
    nope, pe = qh[:, :, :MLA_NOPE], qh[:, :, MLA_NOPE:]
    pe_sw = _rot_cols(pe.reshape(MLA_Q_RANK * MLA_HEADS, MLA_ROPE)).reshape(MLA_Q_RANK, MLA_HEADS, MLA_ROPE)
    wq = jnp.where(odd, jnp.concatenate([nope, pe, pe_sw], axis=2), jnp.concatenate([pe, pe_sw, nope], axis=2))
    kvh = w_ukv.reshape(MLA_KV_RANK, MLA_HEADS, MLA_NOPE + MLA_V)
    kn, vh = kvh[:, :, :MLA_NOPE], kvh[:, :, MLA_NOPE:]
    wkv = jnp.where(odd, jnp.concatenate([kn, vh], axis=2), jnp.concatenate([vh, kn], axis=2))
    return (wq.reshape(MLA_Q_RANK, MLA_HEADS * LANE).astype(BF16),
            wkv.reshape(MLA_KV_RANK, MLA_HEADS * LANE).astype(BF16))


def _rope_tables(n_ctx, n_lat):
    rows = n_lat // GRID_W
    rowp = jnp.broadcast_to(jnp.arange(rows, dtype=F32)[:, None], (rows, GRID_W)).reshape(-1)
    colp = jnp.broadcast_to(jnp.arange(GRID_W, dtype=F32)[None, :], (rows, GRID_W)).reshape(-1)
    axis_dim = MLA_ROPE // 2
    inv_freq = ROPE_THETA ** (-jnp.arange(0, axis_dim, 2, dtype=F32) / axis_dim)
    ar, ac = rowp[:, None] * inv_freq, colp[:, None] * inv_freq
    cos32 = jnp.concatenate([jnp.cos(ar), jnp.cos(ar), jnp.cos(ac), jnp.cos(ac)], axis=1)
    sin32 = jnp.concatenate([jnp.sin(ar), jnp.sin(ar), jnp.sin(ac), jnp.sin(ac)], axis=1)
    cos32 = jnp.concatenate([jnp.ones((n_ctx, MLA_ROPE), F32), cos32], axis=0)
    sin32 = jnp.concatenate([jnp.zeros((n_ctx, MLA_ROPE), F32), sin32], axis=0)
    lt = n_ctx + n_lat
    one, zero = jnp.ones((lt, MLA_NOPE), F32), jnp.zeros((lt, MLA_NOPE), F32)
    z32 = jnp.zeros((lt, MLA_ROPE), F32)
    sc = MLA_SCALE * LOG2E
    cqe = jnp.concatenate([cos32, z32, one], axis=1) * sc
    sqe = jnp.concatenate([sin32, z32, zero], axis=1) * sc
    cqo = jnp.concatenate([one, cos32, z32], axis=1) * sc
    sqo = jnp.concatenate([zero, sin32, z32], axis=1) * sc
    kc = jnp.concatenate([cos32, sin32, zero], axis=1)
    return cqe, sqe, cqo, sqo, kc


def _lower_bounds(gamma):
    p = jnp.cumsum(jax.nn.softmax(gamma.astype(F32), axis=0), axis=0)
    return p - p[0:1]


def kernel(x, c, ctx, c_ctx, w_mod, b_mod, norm1_g, norm2_g, w_in, gamma_fwd, gamma_bwd, hg_norm_g,
           q_norm_g, kv_norm_g, w_uq, w_ukv, w_pa, w_pb, w_o, w_router, router_bias,
           w_gate_e, w_up_e, w_down_e, final_g):
    b_, n_lat, d = x.shape
    n_ctx = ctx.shape[1]
    depth = w_mod.shape[0]
    assert n_ctx % TM == 0 and n_lat % TM == 0 and n_lat % GRID_W == 0 and b_ + 1 <= SUB
    nct = n_ctx // TM
    lt = n_ctx + n_lat
    nt = lt // TM

    cc = jnp.concatenate([c, c_ctx[None, :], jnp.zeros((SUB - b_ - 1, d), F32)], axis=0)
    mod = _modulation(cc, w_mod, b_mod).reshape(depth, SUB, 6, d)
    mod_x = mod[:, :b_]
    mod_c = jnp.broadcast_to(mod[:, b_:b_ + 1], mod_x.shape)
    mod_t = jnp.stack([mod_c, mod_x], axis=2)
    mod_t = jnp.concatenate([mod_t, jnp.zeros((depth, b_, 2, SUB - 6, d), F32)], axis=3)

    cqe, sqe, cqo, sqo, kc = _rope_tables(n_ctx, n_lat)
    lbf, lbb = _lower_bounds(gamma_fwd), _lower_bounds(gamma_bwd)
    wr = jnp.concatenate([w_router, jnp.zeros((d, LANE - N_EXPERTS), F32)], axis=1)
    rbb = jnp.broadcast_to(router_bias.astype(F32)[:, None], (N_EXPERTS, TM))

    z = jnp.concatenate([ctx, x], axis=1)
    out = None
    for l in range(depth):
        last = l == depth - 1
        wq, wkv = _prep_mla(w_uq[l], w_ukv[l])
        (q, v, lff, kf, lfb, kb, g, gas, gbs, qq, kk, vv) = _inproj(
            z, mod_t[l], norm1_g[l][None], _prep_w_in(w_in[l]), lbf[l][None], lbb[l][None],
            q_norm_g[l][None], kv_norm_g[l][None], wq, wkv, cqe, sqe, cqo, sqo, kc, nct)
        o_f = _hgrn(q, v, lff, kf, nct, reverse=False)
        a = _hgrn(q, v, lfb, kb, nct, reverse=True, extra=(o_f, g, hg_norm_g[l][None]))
        m = _attention(qq, kk, vv, n_ctx)
        zn, ht, route = _merge(a, m, gas, gbs, z, mod_t[l], w_pa[l].astype(BF16), w_pb[l].astype(BF16),
                               w_o[l].astype(BF16), norm2_g[l][None], wr, rbb, nct)
        pos, blk_ea, blk_eb, n_used = _sort_plan(route[:, 0, :].reshape(-1).astype(jnp.int32))
        pos = pos.reshape(b_, nt, 1, TM)
        hs = _sort_rows(ht, pos.reshape(-1, 1, TM), blk_ea.shape[0] * MOE_BM)
        y = _moe(hs, w_router.T, blk_ea, blk_eb, n_used, w_gate_e, w_up_e, w_down_e, l)
        if last:
            out = _post(y, pos[:, nct:].reshape(-1, 1, TM), zn, mod_t[l], nct, final_g[None])
        else:
            z = _post(y, pos.reshape(-1, 1, TM), zn, mod_t[l], nct)
    return out
```

```python
import functools

import numpy as np
import jax
import jax.numpy as jnp
from jax import lax
from jax.experimental import pallas as pl
from jax.experimental.pallas import tpu as pltpu

F32 = jnp.float32
BF16 = jnp.bfloat16

EPS = 1e-6
GRID_W = 64
ROPE_THETA = 10000.0
HG_HEADS = 4
HG_DK = 128
HG_WIDTH = 512
MLA_HEADS = 8
MLA_NOPE = 64
MLA_ROPE = 32
MLA_V = 64
MLA_Q_RANK = 384
MLA_KV_RANK = 256
MLA_SCALE = (MLA_NOPE + MLA_ROPE) ** -0.5
LOG2E = 1.4426950408889634
N_EXPERTS = 16
N_GROUPS = 4
EPG = 4
N_CLASSES = N_GROUPS * 6

LANE = 128
SUB = 8
TM = 256
CHUNK = 128
MOE_BM = 256
VMEM_LIMIT = 56 * 1024 * 1024

C_Q, C_I, C_FF, C_FB, C_G, C_DQ, C_DKV, C_GA, C_GB, C_KPE, C_END = (
    0, 512, 1024, 1536, 2048, 2560, 2944, 3200, 4224, 5248, 5376)


def _cparams(n_axes):
    return pltpu.CompilerParams(dimension_semantics=("arbitrary",) * n_axes,
                                vmem_limit_bytes=VMEM_LIMIT)


def _rms(x, g):
    y = x * lax.rsqrt(jnp.mean(x * x, axis=-1, keepdims=True) + EPS)
    return y * g


def _dot(a, b):
    return jnp.dot(a, b, preferred_element_type=F32)


def _dot_nt(a, b):
    return lax.dot_general(a, b, (((1,), (1,)), ((), ())), preferred_element_type=F32)


def _dot_tn(a, b):
    return lax.dot_general(a, b, (((0,), (0,)), ((), ())), preferred_element_type=F32)


def _split3(x):
    hi = x.astype(BF16)
    r1 = x - hi.astype(F32)
    mid = r1.astype(BF16)
    lo = (r1 - mid.astype(F32)).astype(BF16)
    return hi, mid, lo


def _dot_f32(a, b, passes=6):
    a0, a1, a2 = _split3(a)
    b0, b1, b2 = _split3(b)
    out = _dot(a0, b0) + (_dot(a0, b1) + _dot(a1, b0))
    if passes == 6:
        out = out + (_dot(a0, b2) + _dot(a1, b1) + _dot(a2, b0))
    return out


def _mod_kernel(c_ref, w_ref, b_ref, o_ref):
    c = c_ref[...]
    s = c * jax.nn.sigmoid(c)
    o_ref[0] = _dot_f32(s, w_ref[0], passes=3) + b_ref[0]


def _modulation(cc, w_mod, b_mod):
    depth, d, n6 = w_mod.shape
    nb = 1536
    return pl.pallas_call(
        _mod_kernel,
        grid=(depth, n6 // nb),
        in_specs=[pl.BlockSpec((SUB, d), lambda l, j: (0, 0)),
                  pl.BlockSpec((1, d, nb), lambda l, j: (l, 0, j)),
                  pl.BlockSpec((1, 1, nb), lambda l, j: (l, 0, j))],
        out_specs=pl.BlockSpec((1, SUB, nb), lambda l, j: (l, 0, j)),
        out_shape=jax.ShapeDtypeStruct((depth, SUB, n6), F32),
        compiler_params=_cparams(2),
        name="modulation",
    )(cc, w_mod, b_mod.reshape(depth, 1, n6))


def _inproj_kernel(z_ref, mod_ref, g1_ref, w_ref, lbf_ref, lbb_ref, qg_ref, kvg_ref,
                   wq_ref, wkv_ref, cqe_ref, sqe_ref, cqo_ref, sqo_ref, kc_ref,
                   q_o, v_o, lff_o, kf_o, lfb_o, kb_o, g_o, gas_o, gbs_o, qq_o, kk_o, vv_o):
    mod = mod_ref[0, 0]
    h = _rms(z_ref[0], g1_ref[...]) * (1.0 + mod[1:2]) + mod[0:1]
    hb = h.astype(BF16)

    p = _dot(hb, w_ref[...])

    def seg(a, b):
        return p[:, a:b]

    q_o[0] = seg(C_Q, C_I).astype(BF16)
    v_o[0] = seg(C_I, C_FF).astype(BF16)
    for a, lb_ref, lf_o, k_o in ((C_FF, lbf_ref, lff_o, kf_o), (C_FB, lbb_ref, lfb_o, kb_o)):
        lb = lb_ref[...]
        f = lb + (1.0 - lb) * jax.nn.sigmoid(seg(a, a + HG_WIDTH))
        lf_o[0] = jnp.log(f) * LOG2E
        k_o[0] = (1.0 - f).astype(BF16)
    g_o[0] = seg(C_G, C_DQ).astype(BF16)
    gas_o[0] = jax.nn.sigmoid(seg(C_GA, C_GB)).astype(BF16)
    gbs_o[0] = jax.nn.sigmoid(seg(C_GB, C_KPE)).astype(BF16)

    lane = lax.broadcasted_iota(jnp.int32, (TM, LANE), 1)
    qn = _rms(seg(C_DQ, C_DKV), qg_ref[...]).astype(BF16)
    qa = _dot(qn, wq_ref[...])
    tabs = ((cqe_ref[...], sqe_ref[...]), (cqo_ref[...], sqo_ref[...]))
    for hh in range(MLA_HEADS):
        sl = slice(hh * LANE, (hh + 1) * LANE)
        cq, sq = tabs[hh % 2]
        blk = qa[:, sl]
        qq_o[0, :, sl] = (blk * cq + pltpu.roll(blk, LANE - MLA_ROPE, 1) * sq).astype(BF16)
    kvn = _rms(seg(C_DKV, C_GA), kvg_ref[...]).astype(BF16)
    kv = _dot(kvn, wkv_ref[...])
    kp = seg(C_KPE, C_END) * kc_ref[...]
    kpe_e = jnp.where(lane < MLA_ROPE, kp + pltpu.roll(kp, LANE - MLA_ROPE, 1), 0.0)
    kpe_o = pltpu.roll(kpe_e, MLA_NOPE, 1)
    one_e = jnp.where(lane == MLA_V, 1.0, 0.0)
    one_o = jnp.where(lane == 0, 1.0, 0.0)
    for hh in range(MLA_HEADS):
        sl = slice(hh * LANE, (hh + 1) * LANE)
        blk = kv[:, sl]
        if hh % 2 == 0:
            kk_o[0, sl, :] = jnp.where(lane >= MLA_V, blk, kpe_e).T.astype(BF16)
            vv_o[0, :, sl] = jnp.where(lane < MLA_V, blk, one_e).astype(BF16)
        else:
            kk_o[0, sl, :] = jnp.where(lane < MLA_NOPE, blk, kpe_o).T.astype(BF16)
            vv_o[0, :, sl] = jnp.where(lane >= MLA_NOPE, blk, one_o).astype(BF16)


def _inproj(z, modl, g1, w, lbf, lbb, qg, kvg, wq, wkv, cqe, sqe, cqo, sqo, kc, nct):
    b_, lt, d = z.shape
    nt = lt // TM
    row = lambda w_: pl.BlockSpec((1, TM, w_), lambda b, i: (b, i, 0))
    full = lambda a: pl.BlockSpec(a.shape, lambda b, i: (0,) * a.ndim)
    tab = pl.BlockSpec((TM, LANE), lambda b, i: (i, 0))
    f32o = lambda w_: jax.ShapeDtypeStruct((b_, lt, w_), F32)
    bf16o = lambda w_: jax.ShapeDtypeStruct((b_, lt, w_), BF16)
    return pl.pallas_call(
        _inproj_kernel,
        grid=(b_, nt),
        in_specs=[row(d),
                  pl.BlockSpec((1, 1, SUB, d), lambda b, i: (b, jnp.minimum(i // nct, 1), 0, 0)),
                  full(g1), full(w), full(lbf), full(lbb), full(qg), full(kvg),
                  full(wq), full(wkv), tab, tab, tab, tab, tab],
        out_specs=[row(512)] * 7 + [row(1024)] * 3
        + [pl.BlockSpec((1, MLA_HEADS * LANE, TM), lambda b, i: (b, 0, i)), row(1024)],
        out_shape=[bf16o(512), bf16o(512), f32o(512), bf16o(512), f32o(512), bf16o(512), bf16o(512)]
        + [bf16o(1024)] * 3 + [jax.ShapeDtypeStruct((b_, MLA_HEADS * LANE, lt), BF16), bf16o(1024)],
        compiler_params=_cparams(2),
        name="inproj",
    )(z, modl, g1, w, lbf, lbb, qg, kvg, wq, wkv, cqe, sqe, cqo, sqo, kc)


N_LEVELS = 7


def _chunk_tables(reverse):
    c = CHUNK
    t = np.arange(c)[:, None]
    u = np.arange(c)[None, :]
    tri = ((u >= t) if reverse else (u <= t)).astype(np.float32)
    lvl = np.full((c, c), N_LEVELS + 1, np.int32)
    lvl[t == u] = 0
    for l in range(N_LEVELS, 0, -1):
        m = 1 << l
        same = (t // m) == (u // m)
        lvl[same & ((u > t) if reverse else (u < t))] = l
    return jnp.asarray(tri, dtype=BF16), jnp.asarray(lvl)


def _level_ref(bs_ref, hsl, l, reverse):
    w = HG_DK
    sub = lax.broadcasted_iota(jnp.int32, (SUB, w), 0)
    off = 0 if reverse else -1
    m = 1 << l

    def row(r):
        return jnp.broadcast_to(bs_ref[r:r + 1, hsl], (SUB, w))

    pieces = []
    for j in range(CHUNK // SUB):
        base = j * SUB
        if m >= SUB:
            pieces.append(row((base // m) * m + m // 2 + off))
        else:
            p = row(base + m // 2 + off)
            for i in range(1, SUB // m):
                p = jnp.where(sub < i * m, p, row(base + i * m + m // 2 + off))
            pieces.append(p)
    return jnp.concatenate(pieces, axis=0)


def _hgrn_kernel(*refs, reverse, readout):
    if readout:
        q_ref, v_ref, lf_ref, k_ref, tri_ref, lvl_ref, of_ref, g_ref, gain_ref, o_ref, s_ref, bs_ref = refs
    else:
        q_ref, v_ref, lf_ref, k_ref, tri_ref, lvl_ref, o_ref, s_ref, bs_ref = refs

    @pl.when(pl.program_id(1) == 0)
    def _():
        s_ref[...] = jnp.zeros_like(s_ref)

    c = CHUNK
    nch = TM // c
    e = lambda x: jnp.exp2(x).astype(BF16)

    lvl = lvl_ref[...]
    for ci in range(nch):
        cidx = (nch - 1 - ci) if reverse else ci
        rows = slice(cidx * c, (cidx + 1) * c)
        lf = lf_ref[0, rows, :]
        hi = lf.astype(BF16)
        lo = (lf - hi.astype(F32)).astype(BF16)
        bsc = bs_ref.at[ci]
        bsc[...] = _dot(tri_ref[...], hi) + _dot(tri_ref[...], lo)
        for hh in range(HG_HEADS):
            hsl = slice(hh * HG_DK, (hh + 1) * HG_DK)
            b = bsc[:, hsl]
            btot = b[0:1] if reverse else b[c - 1:c]
            q = q_ref[0, rows, hsl]
            k = k_ref[0, rows, hsl]
            ix = (0, rows, hsl)
            v = v_ref[ix]
            st = s_ref[hh]
            att = jnp.where(lvl == 0, _dot_nt(q, k), 0.0)
            for l in range(1, N_LEVELS + 1):
                d = b - _level_ref(bsc, hsl, l, reverse)
                att = jnp.where(lvl == l, _dot_nt(q * e(d), k * e(-d)), att)
            o = _dot_nt(q * e(b), st.astype(BF16)) + _dot(att.astype(BF16), v)
            s_ref[hh] = st * jnp.exp2(btot) + _dot_tn(v, k * e(btot - b))
            if readout:
                o = o + of_ref[ix]
                o = o * lax.rsqrt(jnp.mean(o * o, axis=-1, keepdims=True) + EPS)
                o = o * gain_ref[:, hsl]
                g = g_ref[ix].astype(F32)
                o = o * (g * jax.nn.sigmoid(g))
            o_ref[ix] = o.astype(o_ref.dtype)


def _hgrn(q, v, lf, k, nct, reverse, extra=None):
    b_, lt, w = q.shape
    nt = lt // TM
    if reverse:
        tile = lambda i: jnp.where(i < nct, nct - 1 - i, nt - 1 - (i - nct))
    else:
        tile = lambda i: i
    row = pl.BlockSpec((1, TM, w), lambda b, i: (b, tile(i), 0))
    full = lambda a: pl.BlockSpec(a.shape, lambda b, i: (0,) * a.ndim)
    tri, lvl = _chunk_tables(reverse)
    ins = [q, v, lf, k, tri, lvl]
    specs = [row] * 4 + [full(tri), full(lvl)]
    if extra is not None:
        o_f, g, gain = extra
        ins += [o_f, g, gain]
        specs += [row, row, full(gain)]
    return pl.pallas_call(
        functools.partial(_hgrn_kernel, reverse=reverse, readout=extra is not None),
        grid=(b_, nt),
        in_specs=specs,
        out_specs=row,
        out_shape=jax.ShapeDtypeStruct((b_, lt, w), BF16),
        scratch_shapes=[pltpu.VMEM((HG_HEADS, HG_DK, HG_DK), F32), pltpu.VMEM((TM // CHUNK, CHUNK, w), F32)],
        compiler_params=_cparams(2),
        name="hgrn_bwd" if reverse else "hgrn_fwd",
    )(*ins)


KV_CHUNK = 256


def _attn_kernel(q_ref, k_ref, v_ref, o_ref, *, n_ctx, nct):
    i = pl.program_id(2)
    lane = lax.broadcasted_iota(jnp.int32, (TM, LANE), 1)

    def run(nk):
        outs = []
        for hh in range(2):
            sl = slice(hh * LANE, (hh + 1) * LANE)
            q = q_ref[0, :, sl]
            m = acc = None
            edges = [0] + list(range(n_ctx, nk + 1, KV_CHUNK))
            for c in range(len(edges) - 1):
                ks = slice(edges[c], edges[c + 1])
                s = _dot(q, k_ref[0, sl, ks])
                mc = jnp.max(s, axis=-1, keepdims=True)
                if c == 0:
                    m = mc
                    acc = _dot(jnp.exp2(s - m).astype(BF16), v_ref[0, ks, sl])
                else:
                    m_new = jnp.maximum(m, mc)
                    acc = acc * jnp.exp2(m - m_new) + _dot(jnp.exp2(s - m_new).astype(BF16), v_ref[0, ks, sl])
                    m = m_new
            den = acc[:, MLA_V:MLA_V + 1] if hh == 0 else acc[:, 0:1]
            outs.append(acc / den)
        o_ref[0] = jnp.where(lane < MLA_V, outs[0], outs[1]).astype(o_ref.dtype)

    @pl.when(i < nct)
    def _():
        run(n_ctx)

    @pl.when(i >= nct)
    def _():
        run(k_ref.shape[2])


def _attention(qq, kk, vv, n_ctx):
    b_, lt, _ = qq.shape
    nt = lt // TM
    npair = MLA_HEADS // 2
    return pl.pallas_call(
        functools.partial(_attn_kernel, n_ctx=n_ctx, nct=n_ctx // TM),
        grid=(b_, npair, nt),
        in_specs=[pl.BlockSpec((1, TM, 2 * LANE), lambda b, j, i: (b, i, j)),
                  pl.BlockSpec((1, 2 * LANE, lt), lambda b, j, i: (b, j, 0)),
                  pl.BlockSpec((1, lt, 2 * LANE), lambda b, j, i: (b, 0, j))],
        out_specs=pl.BlockSpec((1, TM, LANE), lambda b, j, i: (b, i, j)),
        out_shape=jax.ShapeDtypeStruct((b_, lt, npair * LANE), BF16),
        compiler_params=_cparams(3),
        name="mla_attention",
    )(qq, kk, vv)


def _route(sel):
    r = lambda x, e: x[e:e + 1]
    best = None
    for g in range(N_GROUPS):
        u = [r(sel, EPG * g + j) for j in range(EPG)]
        gs = None
        for a in range(EPG):
            for b in range(a + 1, EPG):
                pr = u[a] + u[b]
                gs = pr if gs is None else jnp.maximum(gs, pr)
        if best is None:
            best, gi = gs, jnp.zeros_like(gs, dtype=jnp.int32)
        else:
            upd = gs > best
            best = jnp.where(upd, gs, best)
            gi = jnp.where(upd, g, gi)

    def pick(x, j):
        out = r(x, j)
        for g in range(1, N_GROUPS):
            out = jnp.where(gi == g, r(x, EPG * g + j), out)
        return out

    u = [pick(sel, j) for j in range(EPG)]

    def argmax4(vals):
        bv, bi = vals[0], jnp.zeros_like(gi)
        for j in range(1, EPG):
            upd = vals[j] > bv
            bv = jnp.where(upd, vals[j], bv)
            bi = jnp.where(upd, j, bi)
        return bi

    l1 = argmax4(u)
    l2 = argmax4([jnp.where(l1 == j, -jnp.inf, u[j]) for j in range(EPG)])
    lo = jnp.minimum(l1, l2)
    hi = jnp.maximum(l1, l2)
    pair = jnp.where(lo == 0, hi - 1, jnp.where(lo == 1, hi + 1, 5))
    cls = (gi * 6 + pair).astype(F32)
    return jnp.concatenate([cls] + [jnp.zeros_like(cls)] * (SUB - 1), axis=0)


def _merge_kernel(a_ref, m_ref, gas_ref, gbs_ref, z_ref, mod_ref, wpa_ref, wpb_ref, wo_ref,
                  g2_ref, wr_ref, rb_ref, zn_o, ht_o, route_o):
    mod = mod_ref[0, 0]
    pa = _dot(a_ref[0].astype(BF16), wpa_ref[...])
    pb = _dot(m_ref[0].astype(BF16), wpb_ref[...])
    mixed = gas_ref[0] * pa + gbs_ref[0] * pb
    zn = z_ref[0] + mod[2:3] * _dot(mixed.astype(BF16), wo_ref[...])
    zn_o[0] = zn
    h2 = _rms(zn, g2_ref[...]) * (1.0 + mod[4:5]) + mod[3:4]
    _tiles_store(ht_o, h2)
    logits = _dot_f32(h2, wr_ref[...], passes=3)
    sc = jax.nn.sigmoid(logits.T[:N_EXPERTS])
    route_o[0] = _route(sc + rb_ref[...])


def _merge(a, m, gas, gbs, z, modl, wpa, wpb, wo, g2, wr, rbb, nct):
    b_, lt, d = z.shape
    nt = lt // TM
    row = lambda w_: pl.BlockSpec((1, TM, w_), lambda b, i: (b, i, 0))
    full = lambda x: pl.BlockSpec(x.shape, lambda b, i: (0,) * x.ndim)
    return pl.pallas_call(
        _merge_kernel,
        grid=(b_, nt),
        in_specs=[row(512), row(512), row(d), row(d), row(d),
                  pl.BlockSpec((1, 1, SUB, d), lambda b, i: (b, jnp.minimum(i // nct, 1), 0, 0)),
                  full(wpa), full(wpb), full(wo), full(g2), full(wr), full(rbb)],
        out_specs=[row(d), pl.BlockSpec((TM * SUB, LANE), lambda b, i: (b * nt + i, 0)),
                   pl.BlockSpec((1, SUB, TM), lambda b, i: (b, 0, i))],
        out_shape=[jax.ShapeDtypeStruct((b_, lt, d), F32),
                   jax.ShapeDtypeStruct((b_ * lt * SUB, LANE), F32),
                   jax.ShapeDtypeStruct((b_, SUB, lt), F32)],
        compiler_params=_cparams(2),
        name="merge_route",
    )(a, m, gas, gbs, z, modl, wpa, wpb, wo, g2, wr, rbb)


def _tiles_store(ref, x, lead=()):
    n = x.shape[0]
    for j in range(SUB):
        ref[lead + (pl.ds(j, n, stride=SUB), slice(None))] = x[:, j * LANE:(j + 1) * LANE]


def _tiles_load(ref, n, lead=()):
    return jnp.concatenate([ref[lead + (pl.ds(j, n, stride=SUB), slice(None))] for j in range(SUB)], axis=1)


def _tile_copy(src_hbm, idx_ref, buf, sem, slot, r):
    return pltpu.make_async_copy(src_hbm.at[pl.ds(pl.multiple_of(idx_ref[0, 0, r] * SUB, SUB), SUB)],
                                 buf.at[slot, pl.ds(pl.multiple_of(r * SUB, SUB), SUB)], sem.at[slot])


GATHER_UNROLL = 8


def _gather_start(src_hbm, idx_ref, buf, sem, slot, n):
    def body(r8, c):
        for u in range(GATHER_UNROLL):
            _tile_copy(src_hbm, idx_ref, buf, sem, slot, r8 * GATHER_UNROLL + u).start(priority=u % 2)
        return c
    lax.fori_loop(0, n // GATHER_UNROLL, body, 0)


def _gather_wait(src_hbm, buf, sem, slot):
    pltpu.make_async_copy(src_hbm.at[pl.ds(0, buf.shape[1])], buf.at[slot], sem.at[slot]).wait()


def _sort_kernel(pos_ref, ht_ref, init_hbm, hs_hbm, sem):
    del init_hbm

    def body(r8, c):
        for u in range(GATHER_UNROLL):
            r = r8 * GATHER_UNROLL + u
            pltpu.make_async_copy(
                ht_ref.at[pl.ds(pl.multiple_of(r * SUB, SUB), SUB)],
                hs_hbm.at[pl.ds(pl.multiple_of(pos_ref[0, 0, r] * SUB, SUB), SUB)], sem.at[0]).start(priority=u % 2)
        return c
    lax.fori_loop(0, TM // GATHER_UNROLL, body, 0)
    pltpu.make_async_copy(ht_ref, hs_hbm.at[pl.ds(0, TM * SUB)], sem.at[0]).wait()


def _sort_rows(ht, pos, n_sorted):
    nstep = pos.shape[0]
    init = jnp.zeros((n_sorted * SUB, LANE), F32)
    return pl.pallas_call(
        _sort_kernel,
        grid=(nstep,),
        in_specs=[pl.BlockSpec((1, 1, TM), lambda t: (t, 0, 0), memory_space=pltpu.SMEM),
                  pl.BlockSpec((TM * SUB, LANE), lambda t: (t, 0)), pl.BlockSpec(memory_space=pl.ANY)],
        out_specs=pl.BlockSpec(memory_space=pl.ANY),
        out_shape=jax.ShapeDtypeStruct(init.shape, F32),
        scratch_shapes=[pltpu.SemaphoreType.DMA((1,))],
        input_output_aliases={2: 0},
        compiler_params=_cparams(1),
        name="class_sort",
    )(pos, ht, init)


def _moe_kernel(ea_ref, eb_ref, nu_ref, hs_ref, wrt_ref,
                wga_ref, wua_ref, wda_ref, wgb_ref, wub_ref, wdb_ref, y_ref, gu_a, dn_a, gu_b, dn_b):
    j = pl.program_id(0)
    jp = jnp.maximum(j - 1, 0)

    @pl.when((j == 0) | (ea_ref[j] != ea_ref[jp]))
    def _():
        gu_a[0] = wga_ref[0].astype(BF16)
        gu_a[1] = wua_ref[0].astype(BF16)
        dn_a[...] = wda_ref[0].astype(BF16)

    @pl.when((j == 0) | (eb_ref[j] != eb_ref[jp]))
    def _():
        gu_b[0] = wgb_ref[0].astype(BF16)
        gu_b[1] = wub_ref[0].astype(BF16)
        dn_b[...] = wdb_ref[0].astype(BF16)

    @pl.when(j < nu_ref[0])
    def _():
        x32 = _tiles_load(hs_ref, MOE_BM)
        x = x32.astype(BF16)

        def ffn(gu, dn, e):
            sc = jax.nn.sigmoid(jnp.sum(x32 * wrt_ref[pl.ds(e, 1), :], axis=-1, keepdims=True))
            act = jax.nn.silu(_dot(x, gu[0])) * _dot(x, gu[1])
            return sc, _dot(act.astype(BF16), dn[...])

        s_lo, y_lo = ffn(gu_a, dn_a, ea_ref[j])
        s_hi, y_hi = ffn(gu_b, dn_b, eb_ref[j])
        inv = 1.0 / (s_lo + s_hi)
        _tiles_store(y_ref, (s_lo * inv) * y_lo + (s_hi * inv) * y_hi)

    @pl.when(j >= nu_ref[0])
    def _():
        y_ref[...] = jnp.zeros_like(y_ref)


def _moe(hs, wrt, blk_ea, blk_eb, n_used, wg, wu, wd, layer):
    nblk = blk_ea.shape[0]
    d, de = wg.shape[2:]
    tiles = pl.BlockSpec((MOE_BM * SUB, LANE), lambda j, ea, eb, nu: (j, 0))
    wspec = lambda shp, which: pl.BlockSpec(
        (None, 1) + shp, (lambda j, ea, eb, nu: (layer, ea[j], 0, 0)) if which == 0
        else (lambda j, ea, eb, nu: (layer, eb[j], 0, 0)))
    return pl.pallas_call(
        _moe_kernel,
        grid_spec=pltpu.PrefetchScalarGridSpec(
            num_scalar_prefetch=3,
            grid=(nblk,),
            in_specs=[tiles,
                      pl.BlockSpec(wrt.shape, lambda j, ea, eb, nu: (0, 0)),
                      wspec((d, de), 0), wspec((d, de), 0), wspec((de, d), 0),
                      wspec((d, de), 1), wspec((d, de), 1), wspec((de, d), 1)],
            out_specs=tiles,
            scratch_shapes=[pltpu.VMEM((2, d, de), BF16), pltpu.VMEM((de, d), BF16)] * 2),
        out_shape=jax.ShapeDtypeStruct(hs.shape, F32),
        compiler_params=_cparams(1),
        name="moe_ffn",
    )(blk_ea, blk_eb, n_used, hs, wrt, wg, wu, wd, wg, wu, wd)


def _post_kernel(*refs, final):
    if final:
        idx_ref, idxn_ref, y_hbm, zn_ref, mod_ref, fg_ref, o_ref, buf, sem = refs
    else:
        idx_ref, idxn_ref, y_hbm, zn_ref, mod_ref, o_ref, buf, sem = refs
    t = pl.program_id(0)
    nstep = pl.num_programs(0)
    slot = t % 2

    @pl.when(t == 0)
    def _():
        _gather_start(y_hbm, idx_ref, buf, sem, 0, TM)

    @pl.when(t + 1 < nstep)
    def _():
        _gather_start(y_hbm, idxn_ref, buf, sem, 1 - slot, TM)

    _gather_wait(y_hbm, buf, sem, slot)
    z = zn_ref[0] + mod_ref[0, 0][5:6] * _tiles_load(buf, TM, (slot,))
    if final:
        z = _rms(z, fg_ref[...])
    o_ref[0] = z


def _post(y, pos, zn, modl, nct, final_g=None):
    b_, lt, d = zn.shape
    nt = lt // TM
    final = final_g is not None
    t0 = nct if final else 0
    ntl = nt - t0
    nstep = b_ * ntl
    idx_spec = lambda f: pl.BlockSpec((1, 1, TM), f, memory_space=pltpu.SMEM)
    ins = [pos, pos, y, zn, modl]
    specs = [idx_spec(lambda t: (t, 0, 0)),
             idx_spec(lambda t: (jnp.minimum(t + 1, nstep - 1), 0, 0)),
             pl.BlockSpec(memory_space=pl.ANY),
             pl.BlockSpec((1, TM, d), lambda t: (t // ntl, t % ntl + t0, 0)),
             pl.BlockSpec((1, 1, SUB, d), lambda t: (t // ntl, jnp.minimum((t % ntl + t0) // nct, 1), 0, 0))]
    if final:
        ins.append(final_g)
        specs.append(pl.BlockSpec(final_g.shape, lambda t: (0, 0)))
    return pl.pallas_call(
        functools.partial(_post_kernel, final=final),
        grid=(nstep,),
        in_specs=specs,
        out_specs=pl.BlockSpec((1, TM, d), lambda t: (t // ntl, t % ntl, 0)),
        out_shape=jax.ShapeDtypeStruct((b_, ntl * TM, d), F32),
        scratch_shapes=[pltpu.VMEM((2, TM * SUB, LANE), F32), pltpu.SemaphoreType.DMA((2,))],
        compiler_params=_cparams(1),
        name="unsort_residual",
    )(*ins)


_PAIRS = [(0, 1), (0, 2), (0, 3), (1, 2), (1, 3), (2, 3)]
_CLS_EA = np.array([EPG * g + p[0] for g in range(N_GROUPS) for p in _PAIRS], np.int32)
_CLS_EB = np.array([EPG * g + p[1] for g in range(N_GROUPS) for p in _PAIRS], np.int32)


def _sort_plan(cls):
    n = cls.shape[0]
    nblk = n // MOE_BM + N_CLASSES
    onehot = (cls[:, None] == jnp.arange(N_CLASSES, dtype=jnp.int32)[None, :]).astype(jnp.int32)
    csum = jnp.cumsum(onehot, axis=0)
    counts = csum[-1]
    rank = jnp.take_along_axis(csum, cls[:, None], axis=1)[:, 0] - 1
    padded = (counts + MOE_BM - 1) // MOE_BM * MOE_BM
    pad_end = jnp.cumsum(padded)
    pos = (pad_end - padded)[cls] + rank
    blk_cls = jnp.minimum(
        jnp.searchsorted(pad_end, jnp.arange(nblk, dtype=jnp.int32) * MOE_BM, side='right'),
        N_CLASSES - 1).astype(jnp.int32)
    n_used = (pad_end[-1] // MOE_BM).astype(jnp.int32).reshape(1)
    return pos, jnp.asarray(_CLS_EA)[blk_cls], jnp.asarray(_CLS_EB)[blk_cls], n_used


def _rot_cols(w):
    return jnp.concatenate([-w[:, 8:16], w[:, 0:8], -w[:, 24:32], w[:, 16:24]], axis=1)


def _prep_w_in(w):
    kpe = w[:, 3200:3232]
    pad = jnp.zeros((w.shape[0], C_END - C_KPE - 2 * MLA_ROPE), w.dtype)
    return jnp.concatenate([w[:, :3200], w[:, 3232:], kpe, _rot_cols(kpe), pad], axis=1).astype(BF16)


def _prep_mla(w_uq, w_ukv):
    odd = (jnp.arange(MLA_HEADS) % 2 == 1)[None, :, None]
    qh = w_uq.reshape(MLA_Q_RANK, MLA_HEADS, MLA_NOPE + MLA_ROPE)
    nope, pe = qh[:, :, :MLA_NOPE], qh[:, :, MLA_NOPE:]
    pe_sw = _rot_cols(pe.reshape(MLA_Q_RANK * MLA_HEADS, MLA_ROPE)).reshape(MLA_Q_RANK, MLA_HEADS, MLA_ROPE)
    wq = jnp.where(odd, jnp.concatenate([nope, pe, pe_sw], axis=2), jnp.concatenate([pe, pe_sw, nope], axis=2))
    kvh = w_ukv.reshape(MLA_KV_RANK, MLA_HEADS, MLA_NOPE + MLA_V)
    kn, vh = kvh[:, :, :MLA_NOPE], kvh[:, :, MLA_NOPE:]
    wkv = jnp.where(odd, jnp.concatenate([kn, vh], axis=2), jnp.concatenate([vh, kn], axis=2))
    return (wq.reshape(MLA_Q_RANK, MLA_HEADS * LANE).astype(BF16),
            wkv.reshape(MLA_KV_RANK, MLA_HEADS * LANE).astype(BF16))


def _rope_tables(n_ctx, n_lat):
    rows = n_lat // GRID_W
    rowp = jnp.broadcast_to(jnp.arange(rows, dtype=F32)[:, None], (rows, GRID_W)).reshape(-1)
    colp = jnp.broadcast_to(jnp.arange(GRID_W, dtype=F32)[None, :], (rows, GRID_W)).reshape(-1)
    axis_dim = MLA_ROPE // 2
    inv_freq = ROPE_THETA ** (-jnp.arange(0, axis_dim, 2, dtype=F32) / axis_dim)
    ar, ac = rowp[:, None] * inv_freq, colp[:, None] * inv_freq
    cos32 = jnp.concatenate([jnp.cos(ar), jnp.cos(ar), jnp.cos(ac), jnp.cos(ac)], axis=1)
    sin32 = jnp.concatenate([jnp.sin(ar), jnp.sin(ar), jnp.sin(ac), jnp.sin(ac)], axis=1)
    cos32 = jnp.concatenate([jnp.ones((n_ctx, MLA_ROPE), F32), cos32], axis=0)
    sin32 = jnp.concatenate([jnp.zeros((n_ctx, MLA_ROPE), F32), sin32], axis=0)
    lt = n_ctx + n_lat
    one, zero = jnp.ones((lt, MLA_NOPE), F32), jnp.zeros((lt, MLA_NOPE), F32)
    z32 = jnp.zeros((lt, MLA_ROPE), F32)
    sc = MLA_SCALE * LOG2E
    cqe = jnp.concatenate([cos32, z32, one], axis=1) * sc
    sqe = jnp.concatenate([sin32, z32, zero], axis=1) * sc
    cqo = jnp.concatenate([one, cos32, z32], axis=1) * sc
    sqo = jnp.concatenate([zero, sin32, z32], axis=1) * sc
    kc = jnp.concatenate([cos32, sin32, zero], axis=1)
    return cqe, sqe, cqo, sqo, kc


def _lower_bounds(gamma):
    p = jnp.cumsum(jax.nn.softmax(gamma.astype(F32), axis=0), axis=0)
    return p - p[0:1]


def kernel(x, c, ctx, c_ctx, w_mod, b_mod, norm1_g, norm2_g, w_in, gamma_fwd, gamma_bwd, hg_norm_g,
           q_norm_g, kv_norm_g, w_uq, w_ukv, w_pa, w_pb, w_o, w_router, router_bias,
           w_gate_e, w_up_e, w_down_e, final_g):
    b_, n_lat, d = x.shape
    n_ctx = ctx.shape[1]
    depth = w_mod.shape[0]
    assert n_ctx % TM == 0 and n_lat % TM == 0 and n_lat % GRID_W == 0 and b_ + 1 <= SUB
    nct = n_ctx // TM
    lt = n_ctx + n_lat
    nt = lt // TM

    cc = jnp.concatenate([c, c_ctx[None, :], jnp.zeros((SUB - b_ - 1, d), F32)], axis=0)
    mod = _modulation(cc, w_mod, b_mod).reshape(depth, SUB, 6, d)
    mod_x = mod[:, :b_]
    mod_c = jnp.broadcast_to(mod[:, b_:b_ + 1], mod_x.shape)
    mod_t = jnp.stack([mod_c, mod_x], axis=2)
    mod_t = jnp.concatenate([mod_t, jnp.zeros((depth, b_, 2, SUB - 6, d), F32)], axis=3)

    cqe, sqe, cqo, sqo, kc = _rope_tables(n_ctx, n_lat)
    lbf, lbb = _lower_bounds(gamma_fwd), _lower_bounds(gamma_bwd)
    wr = jnp.concatenate([w_router, jnp.zeros((d, LANE - N_EXPERTS), F32)], axis=1)
    rbb = jnp.broadcast_to(router_bias.astype(F32)[:, None], (N_EXPERTS, TM))

    z = jnp.concatenate([ctx, x], axis=1)
    out = None
    for l in range(depth):
        last = l == depth - 1
        wq, wkv = _prep_mla(w_uq[l], w_ukv[l])
        (q, v, lff, kf, lfb, kb, g, gas, gbs, qq, kk, vv) = _inproj(
            z, mod_t[l], norm1_g[l][None], _prep_w_in(w_in[l]), lbf[l][None], lbb[l][None],
            q_norm_g[l][None], kv_norm_g[l][None], wq, wkv, cqe, sqe, cqo, sqo, kc, nct)
        o_f = _hgrn(q, v, lff, kf, nct, reverse=False)
        a = _hgrn(q, v, lfb, kb, nct, reverse=True, extra=(o_f, g, hg_norm_g[l][None]))
        m = _attention(qq, kk, vv, n_ctx)
        zn, ht, route = _merge(a, m, gas, gbs, z, mod_t[l], w_pa[l].astype(BF16), w_pb[l].astype(BF16),
                               w_o[l].astype(BF16), norm2_g[l][None], wr, rbb, nct)
        pos, blk_ea, blk_eb, n_used = _sort_plan(route[:, 0, :].reshape(-1).astype(jnp.int32))
        pos = pos.reshape(b_, nt, 1, TM)
        hs = _sort_rows(ht, pos.reshape(-1, 1, TM), blk_ea.shape[0] * MOE_BM)
        y = _moe(hs, w_router.T, blk_ea, blk_eb, n_used, w_gate_e, w_up_e, w_down_e, l)
        if last:
            out = _post(y, pos[:, nct:].reshape(-1, 1, TM), zn, mod_t[l], nct, final_g[None])
        else:
            z = _post(y, pos.reshape(-1, 1, TM), zn, mod_t[l], nct)
    return out
```

```python
import functools

import numpy as np
import jax
import jax.numpy as jnp
from jax import lax
from jax.experimental import pallas as pl
from jax.experimental.pallas import tpu as pltpu

F32 = jnp.float32
BF16 = jnp.bfloat16

EPS = 1e-6
GRID_W = 64
ROPE_THETA = 10000.0
HG_HEADS = 4
HG_DK = 128
HG_WIDTH = 512
MLA_HEADS = 8
MLA_NOPE = 64
MLA_ROPE = 32
MLA_V = 64
MLA_Q_RANK = 384
MLA_KV_RANK = 256
MLA_SCALE = (MLA_NOPE + MLA_ROPE) ** -0.5
LOG2E = 1.4426950408889634
N_EXPERTS = 16
N_GROUPS = 4
EPG = 4
N_CLASSES = N_GROUPS * 6

LANE = 128
SUB = 8
TM = 256
CHUNK = 128
MOE_BM = 256
VMEM_LIMIT = 56 * 1024 * 1024

C_Q, C_I, C_FF, C_FB, C_G, C_DQ, C_DKV, C_GA, C_GB, C_KPE, C_END = (
    0, 512, 1024, 1536, 2048, 2560, 2944, 3200, 4224, 5248, 5376)


def _cparams(n_axes):
    return pltpu.CompilerParams(dimension_semantics=("arbitrary",) * n_axes,
                                vmem_limit_bytes=VMEM_LIMIT)


def _rms(x, g):
    y = x * lax.rsqrt(jnp.mean(x * x, axis=-1, keepdims=True) + EPS)
    return y * g


def _dot(a, b):
    return jnp.dot(a, b, preferred_element_type=F32)


def _dot_nt(a, b):
    return lax.dot_general(a, b, (((1,), (1,)), ((), ())), preferred_element_type=F32)


def _dot_tn(a, b):
    return lax.dot_general(a, b, (((0,), (0,)), ((), ())), preferred_element_type=F32)


def _split3(x):
    hi = x.astype(BF16)
    r1 = x - hi.astype(F32)
    mid = r1.astype(BF16)
    lo = (r1 - mid.astype(F32)).astype(BF16)
    return hi, mid, lo


def _dot_f32(a, b, passes=6):
    a0, a1, a2 = _split3(a)
    b0, b1, b2 = _split3(b)
    out = _dot(a0, b0) + (_dot(a0, b1) + _dot(a1, b0))
    if passes == 6:
        out = out + (_dot(a0, b2) + _dot(a1, b1) + _dot(a2, b0))
    return out


def _mod_kernel(c_ref, w_ref, b_ref, o_ref):
    c = c_ref[...]
    s = c * jax.nn.sigmoid(c)
    o_ref[0] = _dot_f32(s, w_ref[0], passes=3) + b_ref[0]


def _modulation(cc, w_mod, b_mod):
    depth, d, n6 = w_mod.shape
    nb = 1536
    return pl.pallas_call(
        _mod_kernel,
        grid=(depth, n6 // nb),
        in_specs=[pl.BlockSpec((SUB, d), lambda l, j: (0, 0)),
                  pl.BlockSpec((1, d, nb), lambda l, j: (l, 0, j)),
                  pl.BlockSpec((1, 1, nb), lambda l, j: (l, 0, j))],
        out_specs=pl.BlockSpec((1, SUB, nb), lambda l, j: (l, 0, j)),
        out_shape=jax.ShapeDtypeStruct((depth, SUB, n6), F32),
        compiler_params=_cparams(2),
        name="modulation",
    )(cc, w_mod, b_mod.reshape(depth, 1, n6))


def _inproj_kernel(z_ref, mod_ref, g1_ref, w_ref, lbf_ref, lbb_ref, qg_ref, kvg_ref,
                   wq_ref, wkv_ref, cqe_ref, sqe_ref, cqo_ref, sqo_ref, kc_ref,
                   q_o, v_o, lff_o, kf_o, lfb_o, kb_o, g_o, gas_o, gbs_o, qq_o, kk_o, vv_o):
    mod = mod_ref[0, 0]
    h = _rms(z_ref[0], g1_ref[...]) * (1.0 + mod[1:2]) + mod[0:1]
    hb = h.astype(BF16)

    p = _dot(hb, w_ref[...])

    def seg(a, b):
        return p[:, a:b]

    q_o[0] = seg(C_Q, C_I).astype(BF16)
    v_o[0] = seg(C_I, C_FF).astype(BF16)
    for a, lb_ref, lf_o, k_o in ((C_FF, lbf_ref, lff_o, kf_o), (C_FB, lbb_ref, lfb_o, kb_o)):
        lb = lb_ref[...]
        f = lb + (1.0 - lb) * jax.nn.sigmoid(seg(a, a + HG_WIDTH))
        lf_o[0] = jnp.log(f) * LOG2E
        k_o[0] = (1.0 - f).astype(BF16)
    g_o[0] = seg(C_G, C_DQ).astype(BF16)
    gas_o[0] = jax.nn.sigmoid(seg(C_GA, C_GB)).astype(BF16)
    gbs_o[0] = jax.nn.sigmoid(seg(C_GB, C_KPE)).astype(BF16)

    lane = lax.broadcasted_iota(jnp.int32, (TM, LANE), 1)
    qn = _rms(seg(C_DQ, C_DKV), qg_ref[...]).astype(BF16)
    qa = _dot(qn, wq_ref[...])
    tabs = ((cqe_ref[...], sqe_ref[...]), (cqo_ref[...], sqo_ref[...]))
    for hh in range(MLA_HEADS):
        sl = slice(hh * LANE, (hh + 1) * LANE)
        cq, sq = tabs[hh % 2]
        blk = qa[:, sl]
        qq_o[0, :, sl] = (blk * cq + pltpu.roll(blk, LANE - MLA_ROPE, 1) * sq).astype(BF16)
    kvn = _rms(seg(C_DKV, C_GA), kvg_ref[...]).astype(BF16)
    kv = _dot(kvn, wkv_ref[...])
    kp = seg(C_KPE, C_END) * kc_ref[...]
    kpe_e = jnp.where(lane < MLA_ROPE, kp + pltpu.roll(kp, LANE - MLA_ROPE, 1), 0.0)
    kpe_o = pltpu.roll(kpe_e, MLA_NOPE, 1)
    one_e = jnp.where(lane == MLA_V, 1.0, 0.0)
    one_o = jnp.where(lane == 0, 1.0, 0.0)
    for hh in range(MLA_HEADS):
        sl = slice(hh * LANE, (hh + 1) * LANE)
        blk = kv[:, sl]
        if hh % 2 == 0:
            kk_o[0, sl, :] = jnp.where(lane >= MLA_V, blk, kpe_e).T.astype(BF16)
            vv_o[0, :, sl] = jnp.where(lane < MLA_V, blk, one_e).astype(BF16)
        else:
            kk_o[0, sl, :] = jnp.where(lane < MLA_NOPE, blk, kpe_o).T.astype(BF16)
            vv_o[0, :, sl] = jnp.where(lane >= MLA_NOPE, blk, one_o).astype(BF16)


def _inproj(z, modl, g1, w, lbf, lbb, qg, kvg, wq, wkv, cqe, sqe, cqo, sqo, kc, nct):
    b_, lt, d = z.shape
    nt = lt // TM
    row = lambda w_: pl.BlockSpec((1, TM, w_), lambda b, i: (b, i, 0))
    full = lambda a: pl.BlockSpec(a.shape, lambda b, i: (0,) * a.ndim)
    tab = pl.BlockSpec((TM, LANE), lambda b, i: (i, 0))
    f32o = lambda w_: jax.ShapeDtypeStruct((b_, lt, w_), F32)
    bf16o = lambda w_: jax.ShapeDtypeStruct((b_, lt, w_), BF16)
    return pl.pallas_call(
        _inproj_kernel,
        grid=(b_, nt),
        in_specs=[row(d),
                  pl.BlockSpec((1, 1, SUB, d), lambda b, i: (b, jnp.minimum(i // nct, 1), 0, 0)),
                  full(g1), full(w), full(lbf), full(lbb), full(qg), full(kvg),
                  full(wq), full(wkv), tab, tab, tab, tab, tab],
        out_specs=[row(512)] * 7 + [row(1024)] * 3
        + [pl.BlockSpec((1, MLA_HEADS * LANE, TM), lambda b, i: (b, 0, i)), row(1024)],
        out_shape=[bf16o(512), bf16o(512), f32o(512), bf16o(512), f32o(512), bf16o(512), bf16o(512)]
        + [bf16o(1024)] * 3 + [jax.ShapeDtypeStruct((b_, MLA_HEADS * LANE, lt), BF16), bf16o(1024)],
        compiler_params=_cparams(2),
        name="inproj",
    )(z, modl, g1, w, lbf, lbb, qg, kvg, wq, wkv, cqe, sqe, cqo, sqo, kc)


N_LEVELS = 7


def _chunk_tables(reverse):
    c = CHUNK
    t = np.arange(c)[:, None]
    u = np.arange(c)[None, :]
    tri = ((u >= t) if reverse else (u <= t)).astype(np.float32)
    lvl = np.full((c, c), N_LEVELS + 1, np.int32)
    lvl[t == u] = 0
    for l in range(N_LEVELS, 0, -1):
        m = 1 << l
        same = (t // m) == (u // m)
        lvl[same & ((u > t) if reverse else (u < t))] = l
    return jnp.asarray(tri, dtype=BF16), jnp.asarray(lvl)


def _level_ref(bs_ref, hsl, l, reverse):
    w = HG_DK
    sub = lax.broadcasted_iota(jnp.int32, (SUB, w), 0)
    off = 0 if reverse else -1
    m = 1 << l

    def row(r):
        return jnp.broadcast_to(bs_ref[r:r + 1, hsl], (SUB, w))

    pieces = []
    for j in range(CHUNK // SUB):
        base = j * SUB
        if m >= SUB:
            pieces.append(row((base // m) * m + m // 2 + off))
        else:
            p = row(base + m // 2 + off)
            for i in range(1, SUB // m):
                p = jnp.where(sub < i * m, p, row(base + i * m + m // 2 + off))
            pieces.append(p)
    return jnp.concatenate(pieces, axis=0)


def _hgrn_kernel(*refs, reverse, readout):
    if readout:
        q_ref, v_ref, lf_ref, k_ref, tri_ref, lvl_ref, of_ref, g_ref, gain_ref, o_ref, s_ref, bs_ref = refs
    else:
        q_ref, v_ref, lf_ref, k_ref, tri_ref, lvl_ref, o_ref, s_ref, bs_ref = refs

    @pl.when(pl.program_id(1) == 0)
    def _():
        s_ref[...] = jnp.zeros_like(s_ref)

    c = CHUNK
    nch = TM // c
    e = lambda x: jnp.exp2(x).astype(BF16)

    lvl = lvl_ref[...]
    for ci in range(nch):
        cidx = (nch - 1 - ci) if reverse else ci
        rows = slice(cidx * c, (cidx + 1) * c)
        lf = lf_ref[0, rows, :]
        hi = lf.astype(BF16)
        lo = (lf - hi.astype(F32)).astype(BF16)
        bsc = bs_ref.at[ci]
        bsc[...] = _dot(tri_ref[...], hi) + _dot(tri_ref[...], lo)
        for hh in range(HG_HEADS):
            hsl = slice(hh * HG_DK, (hh + 1) * HG_DK)
            b = bsc[:, hsl]
            btot = b[0:1] if reverse else b[c - 1:c]
            q = q_ref[0, rows, hsl]
            k = k_ref[0, rows, hsl]
            ix = (0, rows, hsl)
            v = v_ref[ix]
            st = s_ref[hh]
            att = jnp.where(lvl == 0, _dot_nt(q, k), 0.0)
            for l in range(1, N_LEVELS + 1):
                f = e(-jnp.abs(b - _level_ref(bsc, hsl, l, reverse)))
                att = jnp.where(lvl == l, _dot_nt(q * f, k * f), att)
            o = _dot_nt(q * e(b), st.astype(BF16)) + _dot(att.astype(BF16), v)
            s_ref[hh] = st * jnp.exp2(btot) + _dot_tn(v, k * e(btot - b))
            if readout:
                o = o + of_ref[ix]
                o = o * lax.rsqrt(jnp.mean(o * o, axis=-1, keepdims=True) + EPS)
                o = o * gain_ref[:, hsl]
                g = g_ref[ix].astype(F32)
                o = o * (g * jax.nn.sigmoid(g))
            o_ref[ix] = o.astype(o_ref.dtype)


def _hgrn(q, v, lf, k, nct, reverse, extra=None):
    b_, lt, w = q.shape
    nt = lt // TM
    if reverse:
        tile = lambda i: jnp.where(i < nct, nct - 1 - i, nt - 1 - (i - nct))
    else:
        tile = lambda i: i
    row = pl.BlockSpec((1, TM, w), lambda b, i: (b, tile(i), 0))
    full = lambda a: pl.BlockSpec(a.shape, lambda b, i: (0,) * a.ndim)
    tri, lvl = _chunk_tables(reverse)
    ins = [q, v, lf, k, tri, lvl]
    specs = [row] * 4 + [full(tri), full(lvl)]
    if extra is not None:
        o_f, g, gain = extra
        ins += [o_f, g, gain]
        specs += [row, row, full(gain)]
    return pl.pallas_call(
        functools.partial(_hgrn_kernel, reverse=reverse, readout=extra is not None),
        grid=(b_, nt),
        in_specs=specs,
        out_specs=row,
        out_shape=jax.ShapeDtypeStruct((b_, lt, w), BF16),
        scratch_shapes=[pltpu.VMEM((HG_HEADS, HG_DK, HG_DK), F32), pltpu.VMEM((TM // CHUNK, CHUNK, w), F32)],
        compiler_params=_cparams(2),
        name="hgrn_bwd" if reverse else "hgrn_fwd",
    )(*ins)


KV_CHUNK = 256


def _attn_kernel(q_ref, k_ref, v_ref, o_ref, *, n_ctx, nct):
    i = pl.program_id(2)
    lane = lax.broadcasted_iota(jnp.int32, (TM, LANE), 1)

    def run(nk):
        outs = []
        for hh in range(2):
            sl = slice(hh * LANE, (hh + 1) * LANE)
            q = q_ref[0, :, sl]
            m = acc = None
            edges = [0] + list(range(n_ctx, nk + 1, KV_CHUNK))
            for c in range(len(edges) - 1):
                ks = slice(edges[c], edges[c + 1])
                s = _dot(q, k_ref[0, sl, ks])
                mc = jnp.max(s, axis=-1, keepdims=True)
                if c == 0:
                    m = mc
                    acc = _dot(jnp.exp2(s - m).astype(BF16), v_ref[0, ks, sl])
                else:
                    m_new = jnp.maximum(m, mc)
                    acc = acc * jnp.exp2(m - m_new) + _dot(jnp.exp2(s - m_new).astype(BF16), v_ref[0, ks, sl])
                    m = m_new
            den = acc[:, MLA_V:MLA_V + 1] if hh == 0 else acc[:, 0:1]
            outs.append(acc / den)
        o_ref[0] = jnp.where(lane < MLA_V, outs[0], outs[1]).astype(o_ref.dtype)

    @pl.when(i < nct)
    def _():
        run(n_ctx)

    @pl.when(i >= nct)
    def _():
        run(k_ref.shape[2])


def _attention(qq, kk, vv, n_ctx):
    b_, lt, _ = qq.shape
    nt = lt // TM
    npair = MLA_HEADS // 2
    return pl.pallas_call(
        functools.partial(_attn_kernel, n_ctx=n_ctx, nct=n_ctx // TM),
        grid=(b_, npair, nt),
        in_specs=[pl.BlockSpec((1, TM, 2 * LANE), lambda b, j, i: (b, i, j)),
                  pl.BlockSpec((1, 2 * LANE, lt), lambda b, j, i: (b, j, 0)),
                  pl.BlockSpec((1, lt, 2 * LANE), lambda b, j, i: (b, 0, j))],
        out_specs=pl.BlockSpec((1, TM, LANE), lambda b, j, i: (b, i, j)),
        out_shape=jax.ShapeDtypeStruct((b_, lt, npair * LANE), BF16),
        compiler_params=_cparams(3),
        name="mla_attention",
    )(qq, kk, vv)


def _route(sel):
    r = lambda x, e: x[e:e + 1]
    best = None
    for g in range(N_GROUPS):
        u = [r(sel, EPG * g + j) for j in range(EPG)]
        gs = None
        for a in range(EPG):
            for b in range(a + 1, EPG):
                pr = u[a] + u[b]
                gs = pr if gs is None else jnp.maximum(gs, pr)
        if best is None:
            best, gi = gs, jnp.zeros_like(gs, dtype=jnp.int32)
        else:
            upd = gs > best
            best = jnp.where(upd, gs, best)
            gi = jnp.where(upd, g, gi)

    def pick(x, j):
        out = r(x, j)
        for g in range(1, N_GROUPS):
            out = jnp.where(gi == g, r(x, EPG * g + j), out)
        return out

    u = [pick(sel, j) for j in range(EPG)]

    def argmax4(vals):
        bv, bi = vals[0], jnp.zeros_like(gi)
        for j in range(1, EPG):
            upd = vals[j] > bv
            bv = jnp.where(upd, vals[j], bv)
            bi = jnp.where(upd, j, bi)
        return bi

    l1 = argmax4(u)
    l2 = argmax4([jnp.where(l1 == j, -jnp.inf, u[j]) for j in range(EPG)])
    lo = jnp.minimum(l1, l2)
    hi = jnp.maximum(l1, l2)
    pair = jnp.where(lo == 0, hi - 1, jnp.where(lo == 1, hi + 1, 5))
    return (gi * 6 + pair).astype(F32)


CLS_ROWS = 32


def _merge_kernel(a_ref, m_ref, gas_ref, gbs_ref, z_ref, mod_ref, wpa_ref, wpb_ref, wo_ref,
                  g2_ref, wr_ref, rb_ref, triu_ref, zn_o, ht_o, route_o, cnt_o, run_ref):
    @pl.when((pl.program_id(0) == 0) & (pl.program_id(1) == 0))
    def _():
        run_ref[...] = jnp.zeros_like(run_ref)

    mod = mod_ref[0, 0]
    pa = _dot(a_ref[0].astype(BF16), wpa_ref[...])
    pb = _dot(m_ref[0].astype(BF16), wpb_ref[...])
    mixed = gas_ref[0] * pa + gbs_ref[0] * pb
    zn = z_ref[0] + mod[2:3] * _dot(mixed.astype(BF16), wo_ref[...])
    zn_o[0] = zn
    h2 = _rms(zn, g2_ref[...]) * (1.0 + mod[4:5]) + mod[3:4]
    _tiles_store(ht_o, h2)
    logits = _dot_f32(h2, wr_ref[...], passes=3)
    sc = jax.nn.sigmoid(logits.T[:N_EXPERTS])
    cls = _route(sc + rb_ref[...])
    onehot = (lax.broadcasted_iota(jnp.int32, (CLS_ROWS, TM), 0) == cls.astype(jnp.int32))
    pref = _dot(onehot.astype(BF16), triu_ref[...])
    rank = jnp.sum(jnp.where(onehot, pref + run_ref[:, 0:1], 0.0), axis=0, keepdims=True) - 1.0
    run_ref[...] = run_ref[...] + pref[:, TM - 1:TM]
    cnt_o[...] = run_ref[...]
    route_o[0] = jnp.concatenate([cls, rank] + [jnp.zeros_like(cls)] * (SUB - 2), axis=0)


def _merge(a, m, gas, gbs, z, modl, wpa, wpb, wo, g2, wr, rbb, nct):
    b_, lt, d = z.shape
    nt = lt // TM
    row = lambda w_: pl.BlockSpec((1, TM, w_), lambda b, i: (b, i, 0))
    full = lambda x: pl.BlockSpec(x.shape, lambda b, i: (0,) * x.ndim)
    triu = jnp.asarray(np.triu(np.ones((TM, TM), np.float32)), dtype=BF16)
    return pl.pallas_call(
        _merge_kernel,
        grid=(b_, nt),
        in_specs=[row(512), row(512), row(d), row(d), row(d),
                  pl.BlockSpec((1, 1, SUB, d), lambda b, i: (b, jnp.minimum(i // nct, 1), 0, 0)),
                  full(wpa), full(wpb), full(wo), full(g2), full(wr), full(rbb), full(triu)],
        out_specs=[row(d), pl.BlockSpec((TM * SUB, LANE), lambda b, i: (b * nt + i, 0)),
                   pl.BlockSpec((1, SUB, TM), lambda b, i: (b, 0, i)),
                   pl.BlockSpec((CLS_ROWS, LANE), lambda b, i: (0, 0))],
        out_shape=[jax.ShapeDtypeStruct((b_, lt, d), F32),
                   jax.ShapeDtypeStruct((b_ * lt * SUB, LANE), F32),
                   jax.ShapeDtypeStruct((b_, SUB, lt), F32),
                   jax.ShapeDtypeStruct((CLS_ROWS, LANE), F32)],
        scratch_shapes=[pltpu.VMEM((CLS_ROWS, LANE), F32)],
        compiler_params=_cparams(2),
        name="merge_route",
    )(a, m, gas, gbs, z, modl, wpa, wpb, wo, g2, wr, rbb, triu)


def _tiles_store(ref, x, lead=()):
    n = x.shape[0]
    for j in range(SUB):
        ref[lead + (pl.ds(j, n, stride=SUB), slice(None))] = x[:, j * LANE:(j + 1) * LANE]


def _tiles_load(ref, n, lead=()):
    return jnp.concatenate([ref[lead + (pl.ds(j, n, stride=SUB), slice(None))] for j in range(SUB)], axis=1)


def _tile_copy(src_hbm, idx_ref, buf, sem, slot, r):
    return pltpu.make_async_copy(src_hbm.at[pl.ds(pl.multiple_of(idx_ref[0, 0, r] * SUB, SUB), SUB)],
                                 buf.at[slot, pl.ds(pl.multiple_of(r * SUB, SUB), SUB)], sem.at[slot])


GATHER_UNROLL = 8


def _gather_start(src_hbm, idx_ref, buf, sem, slot, n):
    def body(r8, c):
        for u in range(GATHER_UNROLL):
            _tile_copy(src_hbm, idx_ref, buf, sem, slot, r8 * GATHER_UNROLL + u).start(priority=u % 2)
        return c
    lax.fori_loop(0, n // GATHER_UNROLL, body, 0)


def _gather_wait(src_hbm, buf, sem, slot):
    pltpu.make_async_copy(src_hbm.at[pl.ds(0, buf.shape[1])], buf.at[slot], sem.at[slot]).wait()


def _sort_kernel(pos_ref, ht_ref, init_hbm, hs_hbm, sem):
    del init_hbm

    def body(r8, c):
        for u in range(GATHER_UNROLL):
            r = r8 * GATHER_UNROLL + u
            pltpu.make_async_copy(
                ht_ref.at[pl.ds(pl.multiple_of(r * SUB, SUB), SUB)],
                hs_hbm.at[pl.ds(pl.multiple_of(pos_ref[0, 0, r] * SUB, SUB), SUB)], sem.at[0]).start(priority=u % 2)
        return c
    lax.fori_loop(0, TM // GATHER_UNROLL, body, 0)
    pltpu.make_async_copy(ht_ref, hs_hbm.at[pl.ds(0, TM * SUB)], sem.at[0]).wait()


def _sort_rows(ht, pos, n_sorted):
    nstep = pos.shape[0]
    init = jnp.zeros((n_sorted * SUB, LANE), F32)
    return pl.pallas_call(
        _sort_kernel,
        grid=(nstep,),
        in_specs=[pl.BlockSpec((1, 1, TM), lambda t: (t, 0, 0), memory_space=pltpu.SMEM),
                  pl.BlockSpec((TM * SUB, LANE), lambda t: (t, 0)), pl.BlockSpec(memory_space=pl.ANY)],
        out_specs=pl.BlockSpec(memory_space=pl.ANY),
        out_shape=jax.ShapeDtypeStruct(init.shape, F32),
        scratch_shapes=[pltpu.SemaphoreType.DMA((1,))],
        input_output_aliases={2: 0},
        compiler_params=_cparams(1),
        name="class_sort",
    )(pos, ht, init)


def _moe_kernel(ea_ref, eb_ref, nu_ref, hs_ref, wrt_ref,
                wga_ref, wua_ref, wda_ref, wgb_ref, wub_ref, wdb_ref, y_ref, gu_a, dn_a, gu_b, dn_b):
    j = pl.program_id(0)
    jp = jnp.maximum(j - 1, 0)

    @pl.when((j == 0) | (ea_ref[j] != ea_ref[jp]))
    def _():
        gu_a[0] = wga_ref[0].astype(BF16)
        gu_a[1] = wua_ref[0].astype(BF16)
        dn_a[...] = wda_ref[0].astype(BF16)

    @pl.when((j == 0) | (eb_ref[j] != eb_ref[jp]))
    def _():
        gu_b[0] = wgb_ref[0].astype(BF16)
        gu_b[1] = wub_ref[0].astype(BF16)
        dn_b[...] = wdb_ref[0].astype(BF16)

    @pl.when(j < nu_ref[0])
    def _():
        x32 = _tiles_load(hs_ref, MOE_BM)
        x = x32.astype(BF16)

        def ffn(gu, dn, e):
            sc = jax.nn.sigmoid(jnp.sum(x32 * wrt_ref[pl.ds(e, 1), :], axis=-1, keepdims=True))
            act = jax.nn.silu(_dot(x, gu[0])) * _dot(x, gu[1])
            return sc, _dot(act.astype(BF16), dn[...])

        s_lo, y_lo = ffn(gu_a, dn_a, ea_ref[j])
        s_hi, y_hi = ffn(gu_b, dn_b, eb_ref[j])
        inv = 1.0 / (s_lo + s_hi)
        _tiles_store(y_ref, (s_lo * inv) * y_lo + (s_hi * inv) * y_hi)

    @pl.when(j >= nu_ref[0])
    def _():
        y_ref[...] = jnp.zeros_like(y_ref)


def _moe(hs, wrt, blk_ea, blk_eb, n_used, wg, wu, wd, layer):
    nblk = blk_ea.shape[0]
    d, de = wg.shape[2:]
    tiles = pl.BlockSpec((MOE_BM * SUB, LANE), lambda j, ea, eb, nu: (j, 0))
    wspec = lambda shp, which: pl.BlockSpec(
        (None, 1) + shp, (lambda j, ea, eb, nu: (layer, ea[j], 0, 0)) if which == 0
        else (lambda j, ea, eb, nu: (layer, eb[j], 0, 0)))
    return pl.pallas_call(
        _moe_kernel,
        grid_spec=pltpu.PrefetchScalarGridSpec(
            num_scalar_prefetch=3,
            grid=(nblk,),
            in_specs=[tiles,
                      pl.BlockSpec(wrt.shape, lambda j, ea, eb, nu: (0, 0)),
                      wspec((d, de), 0), wspec((d, de), 0), wspec((de, d), 0),
                      wspec((d, de), 1), wspec((d, de), 1), wspec((de, d), 1)],
            out_specs=tiles,
            scratch_shapes=[pltpu.VMEM((2, d, de), BF16), pltpu.VMEM((de, d), BF16)] * 2),
        out_shape=jax.ShapeDtypeStruct(hs.shape, F32),
        compiler_params=_cparams(1),
        name="moe_ffn",
    )(blk_ea, blk_eb, n_used, hs, wrt, wg, wu, wd, wg, wu, wd)


def _post_kernel(*refs, final):
    if final:
        idx_ref, idxn_ref, y_hbm, zn_ref, mod_ref, fg_ref, o_ref, buf, sem = refs
    else:
        idx_ref, idxn_ref, y_hbm, zn_ref, mod_ref, o_ref, buf, sem = refs
    t = pl.program_id(0)
    nstep = pl.num_programs(0)
    slot = t % 2

    @pl.when(t == 0)
    def _():
        _gather_start(y_hbm, idx_ref, buf, sem, 0, TM)

    @pl.when(t + 1 < nstep)
    def _():
        _gather_start(y_hbm, idxn_ref, buf, sem, 1 - slot, TM)

    _gather_wait(y_hbm, buf, sem, slot)
    z = zn_ref[0] + mod_ref[0, 0][5:6] * _tiles_load(buf, TM, (slot,))
    if final:
        z = _rms(z, fg_ref[...])
    o_ref[0] = z


def _post(y, pos, zn, modl, nct, final_g=None):
    b_, lt, d = zn.shape
    nt = lt // TM
    final = final_g is not None
    t0 = nct if final else 0
    ntl = nt - t0
    nstep = b_ * ntl
    idx_spec = lambda f: pl.BlockSpec((1, 1, TM), f, memory_space=pltpu.SMEM)
    ins = [pos, pos, y, zn, modl]
    specs = [idx_spec(lambda t: (t, 0, 0)),
             idx_spec(lambda t: (jnp.minimum(t + 1, nstep - 1), 0, 0)),
             pl.BlockSpec(memory_space=pl.ANY),
             pl.BlockSpec((1, TM, d), lambda t: (t // ntl, t % ntl + t0, 0)),
             pl.BlockSpec((1, 1, SUB, d), lambda t: (t // ntl, jnp.minimum((t % ntl + t0) // nct, 1), 0, 0))]
    if final:
        ins.append(final_g)
        specs.append(pl.BlockSpec(final_g.shape, lambda t: (0, 0)))
    return pl.pallas_call(
        functools.partial(_post_kernel, final=final),
        grid=(nstep,),
        in_specs=specs,
        out_specs=pl.BlockSpec((1, TM, d), lambda t: (t // ntl, t % ntl, 0)),
        out_shape=jax.ShapeDtypeStruct((b_, ntl * TM, d), F32),
        scratch_shapes=[pltpu.VMEM((2, TM * SUB, LANE), F32), pltpu.SemaphoreType.DMA((2,))],
        compiler_params=_cparams(1),
        name="unsort_residual",
    )(*ins)


_PAIRS = [(0, 1), (0, 2), (0, 3), (1, 2), (1, 3), (2, 3)]
_CLS_EA = np.array([EPG * g + p[0] for g in range(N_GROUPS) for p in _PAIRS], np.int32)
_CLS_EB = np.array([EPG * g + p[1] for g in range(N_GROUPS) for p in _PAIRS], np.int32)


def _sort_plan(cls, rank, counts):
    n = cls.shape[0]
    nblk = n // MOE_BM + N_CLASSES
    padded = (counts + MOE_BM - 1) // MOE_BM * MOE_BM
    pad_end = jnp.cumsum(padded)
    pos = (pad_end - padded)[cls] + rank
    blk_cls = jnp.minimum(
        jnp.searchsorted(pad_end, jnp.arange(nblk, dtype=jnp.int32) * MOE_BM, side='right'),
        N_CLASSES - 1).astype(jnp.int32)
    n_used = (pad_end[-1] // MOE_BM).astype(jnp.int32).reshape(1)
    return pos, jnp.asarray(_CLS_EA)[blk_cls], jnp.asarray(_CLS_EB)[blk_cls], n_used


def _rot_cols(w):
    return jnp.concatenate([-w[:, 8:16], w[:, 0:8], -w[:, 24:32], w[:, 16:24]], axis=1)


def _prep_w_in(w):
    kpe = w[:, 3200:3232]
    pad = jnp.zeros((w.shape[0], C_END - C_KPE - 2 * MLA_ROPE), w.dtype)
    return jnp.concatenate([w[:, :3200], w[:, 3232:], kpe, _rot_cols(kpe), pad], axis=1).astype(BF16)


def _prep_mla(w_uq, w_ukv):
    odd = (jnp.arange(MLA_HEADS) % 2 == 1)[None, :, None]
    qh = w_uq.reshape(MLA_Q_RANK, MLA_HEADS, MLA_NOPE + MLA_ROPE)
    nope, pe = qh[:, :, :MLA_NOPE], qh[:, :, MLA_NOPE:]
    pe_sw = _rot_cols(pe.reshape(MLA_Q_RANK * MLA_HEADS, MLA_ROPE)).reshape(MLA_Q_RANK, MLA_HEADS, MLA_ROPE)
    wq = jnp.where(odd, jnp.concatenate([nope, pe, pe_sw], axis=2), jnp.concatenate([pe, pe_sw, nope], axis=2))
    kvh = w_ukv.reshape(MLA_KV_RANK, MLA_HEADS, MLA_NOPE + MLA_V)
    kn, vh = kvh[:, :, :MLA_NOPE], kvh[:, :, MLA_NOPE:]
    wkv = jnp.where(odd, jnp.concatenate([kn, vh], axis=2), jnp.concatenate([vh, kn], axis=2))
    return (wq.reshape(MLA_Q_RANK, MLA_HEADS * LANE).astype(BF16),
            wkv.reshape(MLA_KV_RANK, MLA_HEADS * LANE).astype(BF16))


def _rope_tables(n_ctx, n_lat):
    rows = n_lat // GRID_W
    rowp = jnp.broadcast_to(jnp.arange(rows, dtype=F32)[:, None], (rows, GRID_W)).reshape(-1)
    colp = jnp.broadcast_to(jnp.arange(GRID_W, dtype=F32)[None, :], (rows, GRID_W)).reshape(-1)
    axis_dim = MLA_ROPE // 2
    inv_freq = ROPE_THETA ** (-jnp.arange(0, axis_dim, 2, dtype=F32) / axis_dim)
    ar, ac = rowp[:, None] * inv_freq, colp[:, None] * inv_freq
    cos32 = jnp.concatenate([jnp.cos(ar), jnp.cos(ar), jnp.cos(ac), jnp.cos(ac)], axis=1)
    sin32 = jnp.concatenate([jnp.sin(ar), jnp.sin(ar), jnp.sin(ac), jnp.sin(ac)], axis=1)
    cos32 = jnp.concatenate([jnp.ones((n_ctx, MLA_ROPE), F32), cos32], axis=0)
    sin32 = jnp.concatenate([jnp.zeros((n_ctx, MLA_ROPE), F32), sin32], axis=0)
    lt = n_ctx + n_lat
    one, zero = jnp.ones((lt, MLA_NOPE), F32), jnp.zeros((lt, MLA_NOPE), F32)
    z32 = jnp.zeros((lt, MLA_ROPE), F32)
    sc = MLA_SCALE * LOG2E
    cqe = jnp.concatenate([cos32, z32, one], axis=1) * sc
    sqe = jnp.concatenate([sin32, z32, zero], axis=1) * sc
    cqo = jnp.concatenate([one, cos32, z32], axis=1) * sc
    sqo = jnp.concatenate([zero, sin32, z32], axis=1) * sc
    kc = jnp.concatenate([cos32, sin32, zero], axis=1)
    return cqe, sqe, cqo, sqo, kc


def _lower_bounds(gamma):
    p = jnp.cumsum(jax.nn.softmax(gamma.astype(F32), axis=0), axis=0)
    return p - p[0:1]


def kernel(x, c, ctx, c_ctx, w_mod, b_mod, norm1_g, norm2_g, w_in, gamma_fwd, gamma_bwd, hg_norm_g,
           q_norm_g, kv_norm_g, w_uq, w_ukv, w_pa, w_pb, w_o, w_router, router_bias,
           w_gate_e, w_up_e, w_down_e, final_g):
    b_, n_lat, d = x.shape
    n_ctx = ctx.shape[1]
    depth = w_mod.shape[0]
    assert n_ctx % TM == 0 and n_lat % TM == 0 and n_lat % GRID_W == 0 and b_ + 1 <= SUB
    nct = n_ctx // TM
    lt = n_ctx + n_lat
    nt = lt // TM

    cc = jnp.concatenate([c, c_ctx[None, :], jnp.zeros((SUB - b_ - 1, d), F32)], axis=0)
    mod = _modulation(cc, w_mod, b_mod).reshape(depth, SUB, 6, d)
    mod_x = mod[:, :b_]
    mod_c = jnp.broadcast_to(mod[:, b_:b_ + 1], mod_x.shape)
    mod_t = jnp.stack([mod_c, mod_x], axis=2)
    mod_t = jnp.concatenate([mod_t, jnp.zeros((depth, b_, 2, SUB - 6, d), F32)], axis=3)

    cqe, sqe, cqo, sqo, kc = _rope_tables(n_ctx, n_lat)
    lbf, lbb = _lower_bounds(gamma_fwd), _lower_bounds(gamma_bwd)
    wr = jnp.concatenate([w_router, jnp.zeros((d, LANE - N_EXPERTS), F32)], axis=1)
    rbb = jnp.broadcast_to(router_bias.astype(F32)[:, None], (N_EXPERTS, TM))

    z = jnp.concatenate([ctx, x], axis=1)
    out = None
    for l in range(depth):
        last = l == depth - 1
        wq, wkv = _prep_mla(w_uq[l], w_ukv[l])
        (q, v, lff, kf, lfb, kb, g, gas, gbs, qq, kk, vv) = _inproj(
            z, mod_t[l], norm1_g[l][None], _prep_w_in(w_in[l]), lbf[l][None], lbb[l][None],
            q_norm_g[l][None], kv_norm_g[l][None], wq, wkv, cqe, sqe, cqo, sqo, kc, nct)
        o_f = _hgrn(q, v, lff, kf, nct, reverse=False)
        a = _hgrn(q, v, lfb, kb, nct, reverse=True, extra=(o_f, g, hg_norm_g[l][None]))
        m = _attention(qq, kk, vv, n_ctx)
        zn, ht, route, cnt = _merge(a, m, gas, gbs, z, mod_t[l], w_pa[l].astype(BF16), w_pb[l].astype(BF16),
                               w_o[l].astype(BF16), norm2_g[l][None], wr, rbb, nct)
        route = route[:, 0:2, :].astype(jnp.int32)
        pos, blk_ea, blk_eb, n_used = _sort_plan(route[:, 0].reshape(-1), route[:, 1].reshape(-1),
                                                 cnt[:N_CLASSES, 0].astype(jnp.int32))
        pos = pos.reshape(b_, nt, 1, TM)
        hs = _sort_rows(ht, pos.reshape(-1, 1, TM), blk_ea.shape[0] * MOE_BM)
        y = _moe(hs, w_router.T, blk_ea, blk_eb, n_used, w_gate_e, w_up_e, w_down_e, l)
        if last:
            out = _post(y, pos[:, nct:].reshape(-1, 1, TM), zn, mod_t[l], nct, final_g[None])
        else:
            z = _post(y, pos.reshape(-1, 1, TM), zn, mod_t[l], nct)
    return out
```

```python
import functools

import numpy as np
import jax
import jax.numpy as jnp
from jax import lax
from jax.experimental import pallas as pl
from jax.experimental.pallas import tpu as pltpu

F32 = jnp.float32
BF16 = jnp.bfloat16

EPS = 1e-6
GRID_W = 64
ROPE_THETA = 10000.0
HG_HEADS = 4
HG_DK = 128
HG_WIDTH = 512
MLA_HEADS = 8
MLA_NOPE = 64
MLA_ROPE = 32
MLA_V = 64
MLA_Q_RANK = 384
MLA_KV_RANK = 256
MLA_SCALE = (MLA_NOPE + MLA_ROPE) ** -0.5
LOG2E = 1.4426950408889634
N_EXPERTS = 16
N_GROUPS = 4
EPG = 4
N_CLASSES = N_GROUPS * 6

LANE = 128
SUB = 8
TM = 256
CHUNK = 128
MOE_BM = 256
VMEM_LIMIT = 56 * 1024 * 1024

C_Q, C_I, C_FF, C_FB, C_G, C_DQ, C_DKV, C_GA, C_GB, C_KPE, C_END = (
    0, 512, 1024, 1536, 2048, 2560, 2944, 3200, 4224, 5248, 5376)


def _cparams(n_axes):
    return pltpu.CompilerParams(dimension_semantics=("arbitrary",) * n_axes,
                                vmem_limit_bytes=VMEM_LIMIT)


def _rms(x, g):
    y = x * lax.rsqrt(jnp.mean(x * x, axis=-1, keepdims=True) + EPS)
    return y * g


def _dot(a, b):
    return jnp.dot(a, b, preferred_element_type=F32)


def _dot_nt(a, b):
    return lax.dot_general(a, b, (((1,), (1,)), ((), ())), preferred_element_type=F32)


def _dot_tn(a, b):
    return lax.dot_general(a, b, (((0,), (0,)), ((), ())), preferred_element_type=F32)


def _split3(x):
    hi = x.astype(BF16)
    r1 = x - hi.astype(F32)
    mid = r1.astype(BF16)
    lo = (r1 - mid.astype(F32)).astype(BF16)
    return hi, mid, lo


def _dot_f32(a, b, passes=6):
    a0, a1, a2 = _split3(a)
    b0, b1, b2 = _split3(b)
    out = _dot(a0, b0) + (_dot(a0, b1) + _dot(a1, b0))
    if passes == 6:
        out = out + (_dot(a0, b2) + _dot(a1, b1) + _dot(a2, b0))
    return out


def _mod_kernel(c_ref, w_ref, b_ref, o_ref):
    c = c_ref[...]
    s = c * jax.nn.sigmoid(c)
    o_ref[0] = _dot_f32(s, w_ref[0], passes=3) + b_ref[0]


def _modulation(cc, w_mod, b_mod):
    depth, d, n6 = w_mod.shape
    nb = 1536
    return pl.pallas_call(
        _mod_kernel,
        grid=(depth, n6 // nb),
        in_specs=[pl.BlockSpec((SUB, d), lambda l, j: (0, 0)),
                  pl.BlockSpec((1, d, nb), lambda l, j: (l, 0, j)),
                  pl.BlockSpec((1, 1, nb), lambda l, j: (l, 0, j))],
        out_specs=pl.BlockSpec((1, SUB, nb), lambda l, j: (l, 0, j)),
        out_shape=jax.ShapeDtypeStruct((depth, SUB, n6), F32),
        compiler_params=_cparams(2),
        name="modulation",
    )(cc, w_mod, b_mod.reshape(depth, 1, n6))


def _inproj_kernel(z_ref, mod_ref, g1_ref, w_ref, lbf_ref, lbb_ref, qg_ref, kvg_ref,
                   wq_ref, wkv_ref, cqe_ref, sqe_ref, cqo_ref, sqo_ref, kc_ref,
                   q_o, v_o, lff_o, kf_o, lfb_o, kb_o, g_o, gas_o, gbs_o, qq_o, kk_o, vv_o):
    mod = mod_ref[0, 0]
    h = _rms(z_ref[0], g1_ref[...]) * (1.0 + mod[1:2]) + mod[0:1]
    hb = h.astype(BF16)

    p = _dot(hb, w_ref[...])

    def seg(a, b):
        return p[:, a:b]

    q_o[0] = seg(C_Q, C_I).astype(BF16)
    v_o[0] = seg(C_I, C_FF).astype(BF16)
    for a, lb_ref, lf_o, k_o in ((C_FF, lbf_ref, lff_o, kf_o), (C_FB, lbb_ref, lfb_o, kb_o)):
        lb = lb_ref[...]
        f = lb + (1.0 - lb) * jax.nn.sigmoid(seg(a, a + HG_WIDTH))
        lf_o[0] = jnp.log(f) * LOG2E
        k_o[0] = (1.0 - f).astype(BF16)
    g_o[0] = seg(C_G, C_DQ).astype(BF16)
    gas_o[0] = jax.nn.sigmoid(seg(C_GA, C_GB)).astype(BF16)
    gbs_o[0] = jax.nn.sigmoid(seg(C_GB, C_KPE)).astype(BF16)

    lane = lax.broadcasted_iota(jnp.int32, (TM, LANE), 1)
    qn = _rms(seg(C_DQ, C_DKV), qg_ref[...]).astype(BF16)
    qa = _dot(qn, wq_ref[...])
    tabs = ((cqe_ref[...], sqe_ref[...]), (cqo_ref[...], sqo_ref[...]))
    for hh in range(MLA_HEADS):
        sl = slice(hh * LANE, (hh + 1) * LANE)
        cq, sq = tabs[hh % 2]
        blk = qa[:, sl]
        qq_o[0, :, sl] = (blk * cq + pltpu.roll(blk, LANE - MLA_ROPE, 1) * sq).astype(BF16)
    kvn = _rms(seg(C_DKV, C_GA), kvg_ref[...]).astype(BF16)
    kv = _dot(kvn, wkv_ref[...])
    kp = seg(C_KPE, C_END) * kc_ref[...]
    kpe_e = jnp.where(lane < MLA_ROPE, kp + pltpu.roll(kp, LANE - MLA_ROPE, 1), 0.0)
    kpe_o = pltpu.roll(kpe_e, MLA_NOPE, 1)
    one_e = jnp.where(lane == MLA_V, 1.0, 0.0)
    one_o = jnp.where(lane == 0, 1.0, 0.0)
    for hh in range(MLA_HEADS):
        sl = slice(hh * LANE, (hh + 1) * LANE)
        blk = kv[:, sl]
        if hh % 2 == 0:
            kk_o[0, sl, :] = jnp.where(lane >= MLA_V, blk, kpe_e).T.astype(BF16)
            vv_o[0, :, sl] = jnp.where(lane < MLA_V, blk, one_e).astype(BF16)
        else:
            kk_o[0, sl, :] = jnp.where(lane < MLA_NOPE, blk, kpe_o).T.astype(BF16)
            vv_o[0, :, sl] = jnp.where(lane >= MLA_NOPE, blk, one_o).astype(BF16)


def _inproj(z, modl, g1, w, lbf, lbb, qg, kvg, wq, wkv, cqe, sqe, cqo, sqo, kc, nct):
    b_, lt, d = z.shape
    nt = lt // TM
    row = lambda w_: pl.BlockSpec((1, TM, w_), lambda b, i: (b, i, 0))
    full = lambda a: pl.BlockSpec(a.shape, lambda b, i: (0,) * a.ndim)
    tab = pl.BlockSpec((TM, LANE), lambda b, i: (i, 0))
    f32o = lambda w_: jax.ShapeDtypeStruct((b_, lt, w_), F32)
    bf16o = lambda w_: jax.ShapeDtypeStruct((b_, lt, w_), BF16)
    return pl.pallas_call(
        _inproj_kernel,
        grid=(b_, nt),
        in_specs=[row(d),
                  pl.BlockSpec((1, 1, SUB, d), lambda b, i: (b, jnp.minimum(i // nct, 1), 0, 0)),
                  full(g1), full(w), full(lbf), full(lbb), full(qg), full(kvg),
                  full(wq), full(wkv), tab, tab, tab, tab, tab],
        out_specs=[row(512)] * 7 + [row(1024)] * 3
        + [pl.BlockSpec((1, MLA_HEADS * LANE, TM), lambda b, i: (b, 0, i)), row(1024)],
        out_shape=[bf16o(512), bf16o(512), f32o(512), bf16o(512), f32o(512), bf16o(512), bf16o(512)]
        + [bf16o(1024)] * 3 + [jax.ShapeDtypeStruct((b_, MLA_HEADS * LANE, lt), BF16), bf16o(1024)],
        compiler_params=_cparams(2),
        name="inproj",
    )(z, modl, g1, w, lbf, lbb, qg, kvg, wq, wkv, cqe, sqe, cqo, sqo, kc)


N_LEVELS = 7


def _chunk_tables(reverse):
    c = CHUNK
    t = np.arange(c)[:, None]
    u = np.arange(c)[None, :]
    tri = ((u >= t) if reverse else (u <= t)).astype(np.float32)
    lvl = np.full((c, c), N_LEVELS + 1, np.int32)
    lvl[t == u] = 0
    for l in range(N_LEVELS, 0, -1):
        m = 1 << l
        same = (t // m) == (u // m)
        lvl[same & ((u > t) if reverse else (u < t))] = l
    return jnp.asarray(tri, dtype=BF16), jnp.asarray(lvl)


def _level_ref(bs_ref, hsl, l, reverse):
    w = HG_DK
    sub = lax.broadcasted_iota(jnp.int32, (SUB, w), 0)
    off = 0 if reverse else -1
    m = 1 << l

    def row(r):
        return jnp.broadcast_to(bs_ref[r:r + 1, hsl], (SUB, w))

    pieces = []
    for j in range(CHUNK // SUB):
        base = j * SUB
        if m >= SUB:
            pieces.append(row((base // m) * m + m // 2 + off))
        else:
            p = row(base + m // 2 + off)
            for i in range(1, SUB // m):
                p = jnp.where(sub < i * m, p, row(base + i * m + m // 2 + off))
            pieces.append(p)
    return jnp.concatenate(pieces, axis=0)


def _hgrn_kernel(*refs, reverse, readout):
    if readout:
        q_ref, v_ref, lf_ref, k_ref, tri_ref, lvl_ref, of_ref, g_ref, gain_ref, o_ref, s_ref, bs_ref = refs
    else:
        q_ref, v_ref, lf_ref, k_ref, tri_ref, lvl_ref, o_ref, s_ref, bs_ref = refs

    @pl.when(pl.program_id(1) == 0)
    def _():
        s_ref[...] = jnp.zeros_like(s_ref)

    c = CHUNK
    nch = TM // c
    e = lambda x: jnp.exp2(x).astype(BF16)

    lvl = lvl_ref[...]
    for ci in range(nch):
        cidx = (nch - 1 - ci) if reverse else ci
        rows = slice(cidx * c, (cidx + 1) * c)
        lf = lf_ref[0, rows, :]
        hi = lf.astype(BF16)
        lo = (lf - hi.astype(F32)).astype(BF16)
        bsc = bs_ref.at[ci]
        bsc[...] = _dot(tri_ref[...], hi) + _dot(tri_ref[...], lo)
        for hh in range(HG_HEADS):
            hsl = slice(hh * HG_DK, (hh + 1) * HG_DK)
            b = bsc[:, hsl]
            btot = b[0:1] if reverse else b[c - 1:c]
            q = q_ref[0, rows, hsl]
            k = k_ref[0, rows, hsl]
            ix = (0, rows, hsl)
            v = v_ref[ix]
            st = s_ref[hh]
            att = jnp.where(lvl == 0, _dot_nt(q, k), 0.0)
            for l in range(1, N_LEVELS + 1):
                f = e(-jnp.abs(b - _level_ref(bsc, hsl, l, reverse)))
                att = jnp.where(lvl == l, _dot_nt(q * f, k * f), att)
            o = _dot_nt(q * e(b), st.astype(BF16)) + _dot(att.astype(BF16), v)
            s_ref[hh] = st * jnp.exp2(btot) + _dot_tn(v, k * e(btot - b))
            if readout:
                o = o + of_ref[ix]
                o = o * lax.rsqrt(jnp.mean(o * o, axis=-1, keepdims=True) + EPS)
                o = o * gain_ref[:, hsl]
                g = g_ref[ix].astype(F32)
                o = o * (g * jax.nn.sigmoid(g))
            o_ref[ix] = o.astype(o_ref.dtype)


def _hgrn(q, v, lf, k, nct, reverse, extra=None):
    b_, lt, w = q.shape
    nt = lt // TM
    if reverse:
        tile = lambda i: jnp.where(i < nct, nct - 1 - i, nt - 1 - (i - nct))
    else:
        tile = lambda i: i
    row = pl.BlockSpec((1, TM, w), lambda b, i: (b, tile(i), 0))
    full = lambda a: pl.BlockSpec(a.shape, lambda b, i: (0,) * a.ndim)
    tri, lvl = _chunk_tables(reverse)
    ins = [q, v, lf, k, tri, lvl]
    specs = [row] * 4 + [full(tri), full(lvl)]
    if extra is not None:
        o_f, g, gain = extra
        ins += [o_f, g, gain]
        specs += [row, row, full(gain)]
    return pl.pallas_call(
        functools.partial(_hgrn_kernel, reverse=reverse, readout=extra is not None),
        grid=(b_, nt),
        in_specs=specs,
        out_specs=row,
        out_shape=jax.ShapeDtypeStruct((b_, lt, w), BF16),
        scratch_shapes=[pltpu.VMEM((HG_HEADS, HG_DK, HG_DK), F32), pltpu.VMEM((TM // CHUNK, CHUNK, w), F32)],
        compiler_params=_cparams(2),
        name="hgrn_bwd" if reverse else "hgrn_fwd",
    )(*ins)


KV_CHUNK = 256


def _attn_kernel(q_ref, k_ref, v_ref, o_ref, *, n_ctx, nct):
    i = pl.program_id(2)
    lane = lax.broadcasted_iota(jnp.int32, (TM, LANE), 1)

    def run(nk):
        outs = []
        for hh in range(2):
            sl = slice(hh * LANE, (hh + 1) * LANE)
            q = q_ref[0, :, sl]
            m = acc = None
            edges = [0] + list(range(n_ctx, nk + 1, KV_CHUNK))
            for c in range(len(edges) - 1):
                ks = slice(edges[c], edges[c + 1])
                s = _dot(q, k_ref[0, sl, ks])
                mc = jnp.max(s, axis=-1, keepdims=True)
                if c == 0:
                    m = mc
                    acc = _dot(jnp.exp2(s - m).astype(BF16), v_ref[0, ks, sl])
                else:
                    m_new = jnp.maximum(m, mc)
                    acc = acc * jnp.exp2(m - m_new) + _dot(jnp.exp2(s - m_new).astype(BF16), v_ref[0, ks, sl])
                    m = m_new
            den = acc[:, MLA_V:MLA_V + 1] if hh == 0 else acc[:, 0:1]
            outs.append(acc / den)
        o_ref[0] = jnp.where(lane < MLA_V, outs[0], outs[1]).astype(o_ref.dtype)

    @pl.when(i < nct)
    def _():
        run(n_ctx)

    @pl.when(i >= nct)
    def _():
        run(k_ref.shape[2])


def _attention(qq, kk, vv, n_ctx):
    b_, lt, _ = qq.shape
    nt = lt // TM
    npair = MLA_HEADS // 2
    return pl.pallas_call(
        functools.partial(_attn_kernel, n_ctx=n_ctx, nct=n_ctx // TM),
        grid=(b_, npair, nt),
        in_specs=[pl.BlockSpec((1, TM, 2 * LANE), lambda b, j, i: (b, i, j)),
                  pl.BlockSpec((1, 2 * LANE, lt), lambda b, j, i: (b, j, 0)),
                  pl.BlockSpec((1, lt, 2 * LANE), lambda b, j, i: (b, 0, j))],
        out_specs=pl.BlockSpec((1, TM, LANE), lambda b, j, i: (b, i, j)),
        out_shape=jax.ShapeDtypeStruct((b_, lt, npair * LANE), BF16),
        compiler_params=_cparams(3),
        name="mla_attention",
    )(qq, kk, vv)


def _route(sel):
    r = lambda x, e: x[e:e + 1]
    best = None
    for g in range(N_GROUPS):
        u = [r(sel, EPG * g + j) for j in range(EPG)]
        gs = None
        for a in range(EPG):
            for b in range(a + 1, EPG):
                pr = u[a] + u[b]
                gs = pr if gs is None else jnp.maximum(gs, pr)
        if best is None:
            best, gi = gs, jnp.zeros_like(gs, dtype=jnp.int32)
        else:
            upd = gs > best
            best = jnp.where(upd, gs, best)
            gi = jnp.where(upd, g, gi)

    def pick(x, j):
        out = r(x, j)
        for g in range(1, N_GROUPS):
            out = jnp.where(gi == g, r(x, EPG * g + j), out)
        return out

    u = [pick(sel, j) for j in range(EPG)]

    def argmax4(vals):
        bv, bi = vals[0], jnp.zeros_like(gi)
        for j in range(1, EPG):
            upd = vals[j] > bv
            bv = jnp.where(upd, vals[j], bv)
            bi = jnp.where(upd, j, bi)
        return bi

    l1 = argmax4(u)
    l2 = argmax4([jnp.where(l1 == j, -jnp.inf, u[j]) for j in range(EPG)])
    lo = jnp.minimum(l1, l2)
    hi = jnp.maximum(l1, l2)
    pair = jnp.where(lo == 0, hi - 1, jnp.where(lo == 1, hi + 1, 5))
    return (gi * 6 + pair).astype(F32)


CLS_ROWS = 32


def _merge_kernel(a_ref, m_ref, gas_ref, gbs_ref, z_ref, mod_ref, wpa_ref, wpb_ref, wo_ref,
                  g2_ref, wr_ref, rb_ref, triu_ref, zn_o, ht_o, route_o, cnt_o, run_ref):
    @pl.when((pl.program_id(0) == 0) & (pl.program_id(1) == 0))
    def _():
        run_ref[...] = jnp.zeros_like(run_ref)

    mod = mod_ref[0, 0]
    pa = _dot(a_ref[0].astype(BF16), wpa_ref[...])
    pb = _dot(m_ref[0].astype(BF16), wpb_ref[...])
    mixed = gas_ref[0] * pa + gbs_ref[0] * pb
    zn = z_ref[0] + mod[2:3] * _dot(mixed.astype(BF16), wo_ref[...])
    zn_o[0] = zn
    h2 = _rms(zn, g2_ref[...]) * (1.0 + mod[4:5]) + mod[3:4]
    _tiles_store(ht_o, h2)
    logits = _dot_f32(h2, wr_ref[...], passes=3)
    sc = jax.nn.sigmoid(logits.T[:N_EXPERTS])
    cls = _route(sc + rb_ref[...])
    onehot = (lax.broadcasted_iota(jnp.int32, (CLS_ROWS, TM), 0) == cls.astype(jnp.int32))
    pref = _dot(onehot.astype(BF16), triu_ref[...])
    rank = jnp.sum(jnp.where(onehot, pref + run_ref[:, 0:1], 0.0), axis=0, keepdims=True) - 1.0
    run_ref[...] = run_ref[...] + pref[:, TM - 1:TM]
    cnt_o[...] = run_ref[...]
    route_o[0] = jnp.concatenate([cls, rank] + [jnp.zeros_like(cls)] * (SUB - 2), axis=0)


def _merge(a, m, gas, gbs, z, modl, wpa, wpb, wo, g2, wr, rbb, nct):
    b_, lt, d = z.shape
    nt = lt // TM
    row = lambda w_: pl.BlockSpec((1, TM, w_), lambda b, i: (b, i, 0))
    full = lambda x: pl.BlockSpec(x.shape, lambda b, i: (0,) * x.ndim)
    triu = jnp.asarray(np.triu(np.ones((TM, TM), np.float32)), dtype=BF16)
    return pl.pallas_call(
        _merge_kernel,
        grid=(b_, nt),
        in_specs=[row(512), row(512), row(d), row(d), row(d),
                  pl.BlockSpec((1, 1, SUB, d), lambda b, i: (b, jnp.minimum(i // nct, 1), 0, 0)),
                  full(wpa), full(wpb), full(wo), full(g2), full(wr), full(rbb), full(triu)],
        out_specs=[row(d), pl.BlockSpec((TM * SUB, LANE), lambda b, i: (b * nt + i, 0)),
                   pl.BlockSpec((1, SUB, TM), lambda b, i: (b, 0, i)),
                   pl.BlockSpec((CLS_ROWS, LANE), lambda b, i: (0, 0))],
        out_shape=[jax.ShapeDtypeStruct((b_, lt, d), F32),
                   jax.ShapeDtypeStruct((b_ * lt * SUB, LANE), F32),
                   jax.ShapeDtypeStruct((b_, SUB, lt), F32),
                   jax.ShapeDtypeStruct((CLS_ROWS, LANE), F32)],
        scratch_shapes=[pltpu.VMEM((CLS_ROWS, LANE), F32)],
        compiler_params=_cparams(2),
        name="merge_route",
    )(a, m, gas, gbs, z, modl, wpa, wpb, wo, g2, wr, rbb, triu)


def _tiles_store(ref, x, lead=()):
    n = x.shape[0]
    for j in range(SUB):
        ref[lead + (pl.ds(j, n, stride=SUB), slice(None))] = x[:, j * LANE:(j + 1) * LANE]


def _tiles_load(ref, n, lead=()):
    return jnp.concatenate([ref[lead + (pl.ds(j, n, stride=SUB), slice(None))] for j in range(SUB)], axis=1)


def _tile_copy(src_hbm, idx_ref, buf, sem, slot, r):
    return pltpu.make_async_copy(src_hbm.at[pl.ds(pl.multiple_of(idx_ref[0, 0, r] * SUB, SUB), SUB)],
                                 buf.at[slot, pl.ds(pl.multiple_of(r * SUB, SUB), SUB)], sem.at[slot])


GATHER_UNROLL = 8


def _gather_start(src_hbm, idx_ref, buf, sem, slot, n):
    def body(r8, c):
        for u in range(GATHER_UNROLL):
            _tile_copy(src_hbm, idx_ref, buf, sem, slot, r8 * GATHER_UNROLL + u).start(priority=u % 2)
        return c
    lax.fori_loop(0, n // GATHER_UNROLL, body, 0)


def _gather_wait(src_hbm, buf, sem, slot):
    pltpu.make_async_copy(src_hbm.at[pl.ds(0, buf.shape[1])], buf.at[slot], sem.at[slot]).wait()


def _sort_kernel(pos_ref, ht_ref, init_hbm, hs_hbm, sem):
    del init_hbm

    def body(r8, c):
        for u in range(GATHER_UNROLL):
            r = r8 * GATHER_UNROLL + u
            pltpu.make_async_copy(
                ht_ref.at[pl.ds(pl.multiple_of(r * SUB, SUB), SUB)],
                hs_hbm.at[pl.ds(pl.multiple_of(pos_ref[0, 0, r] * SUB, SUB), SUB)], sem.at[0]).start(priority=u % 2)
        return c
    lax.fori_loop(0, TM // GATHER_UNROLL, body, 0)
    pltpu.make_async_copy(ht_ref, hs_hbm.at[pl.ds(0, TM * SUB)], sem.at[0]).wait()


def _sort_rows(ht, pos, n_sorted, init=None):
    nstep = pos.shape[0]
    if init is None:
        init = jnp.zeros((n_sorted * SUB, LANE), F32)
    return pl.pallas_call(
        _sort_kernel,
        grid=(nstep,),
        in_specs=[pl.BlockSpec((1, 1, TM), lambda t: (t, 0, 0), memory_space=pltpu.SMEM),
                  pl.BlockSpec((TM * SUB, LANE), lambda t: (t, 0)), pl.BlockSpec(memory_space=pl.ANY)],
        out_specs=pl.BlockSpec(memory_space=pl.ANY),
        out_shape=jax.ShapeDtypeStruct(init.shape, F32),
        scratch_shapes=[pltpu.SemaphoreType.DMA((1,))],
        input_output_aliases={2: 0},
        compiler_params=_cparams(1),
        name="class_sort",
    )(pos, ht, init)


def _moe_kernel(ea_ref, eb_ref, nu_ref, hs_ref, wrt_ref,
                wga_ref, wua_ref, wda_ref, wgb_ref, wub_ref, wdb_ref, y_ref, gu_a, dn_a, gu_b, dn_b):
    j = pl.program_id(0)
    jp = jnp.maximum(j - 1, 0)

    @pl.when((j == 0) | (ea_ref[j] != ea_ref[jp]))
    def _():
        gu_a[0] = wga_ref[0].astype(BF16)
        gu_a[1] = wua_ref[0].astype(BF16)
        dn_a[...] = wda_ref[0].astype(BF16)

    @pl.when((j == 0) | (eb_ref[j] != eb_ref[jp]))
    def _():
        gu_b[0] = wgb_ref[0].astype(BF16)
        gu_b[1] = wub_ref[0].astype(BF16)
        dn_b[...] = wdb_ref[0].astype(BF16)

    @pl.when(j < nu_ref[0])
    def _():
        x32 = _tiles_load(hs_ref, MOE_BM)
        x = x32.astype(BF16)

        def ffn(gu, dn, e):
            sc = jax.nn.sigmoid(jnp.sum(x32 * wrt_ref[pl.ds(e, 1), :], axis=-1, keepdims=True))
            act = jax.nn.silu(_dot(x, gu[0])) * _dot(x, gu[1])
            return sc, _dot(act.astype(BF16), dn[...])

        s_lo, y_lo = ffn(gu_a, dn_a, ea_ref[j])
        s_hi, y_hi = ffn(gu_b, dn_b, eb_ref[j])
        inv = 1.0 / (s_lo + s_hi)
        _tiles_store(y_ref, (s_lo * inv) * y_lo + (s_hi * inv) * y_hi)

    @pl.when(j >= nu_ref[0])
    def _():
        y_ref[...] = jnp.zeros_like(y_ref)


def _moe(hs, wrt, blk_ea, blk_eb, n_used, wg, wu, wd, layer):
    nblk = blk_ea.shape[0]
    d, de = wg.shape[2:]
    tiles = pl.BlockSpec((MOE_BM * SUB, LANE), lambda j, ea, eb, nu: (j, 0))
    wspec = lambda shp, which: pl.BlockSpec(
        (None, 1) + shp, (lambda j, ea, eb, nu: (layer, ea[j], 0, 0)) if which == 0
        else (lambda j, ea, eb, nu: (layer, eb[j], 0, 0)))
    return pl.pallas_call(
        _moe_kernel,
        grid_spec=pltpu.PrefetchScalarGridSpec(
            num_scalar_prefetch=3,
            grid=(nblk,),
            in_specs=[tiles,
                      pl.BlockSpec(wrt.shape, lambda j, ea, eb, nu: (0, 0)),
                      wspec((d, de), 0), wspec((d, de), 0), wspec((de, d), 0),
                      wspec((d, de), 1), wspec((d, de), 1), wspec((de, d), 1)],
            out_specs=tiles,
            scratch_shapes=[pltpu.VMEM((2, d, de), BF16), pltpu.VMEM((de, d), BF16)] * 2),
        out_shape=jax.ShapeDtypeStruct(hs.shape, F32),
        compiler_params=_cparams(1),
        name="moe_ffn",
    )(blk_ea, blk_eb, n_used, hs, wrt, wg, wu, wd, wg, wu, wd)


def _post_kernel(*refs, final):
    if final:
        idx_ref, idxn_ref, y_hbm, zn_ref, mod_ref, fg_ref, o_ref, buf, sem = refs
    else:
        idx_ref, idxn_ref, y_hbm, zn_ref, mod_ref, o_ref, buf, sem = refs
    t = pl.program_id(0)
    nstep = pl.num_programs(0)
    slot = t % 2

    @pl.when(t == 0)
    def _():
        _gather_start(y_hbm, idx_ref, buf, sem, 0, TM)

    @pl.when(t + 1 < nstep)
    def _():
        _gather_start(y_hbm, idxn_ref, buf, sem, 1 - slot, TM)

    _gather_wait(y_hbm, buf, sem, slot)
    z = zn_ref[0] + mod_ref[0, 0][5:6] * _tiles_load(buf, TM, (slot,))
    if final:
        z = _rms(z, fg_ref[...])
    o_ref[0] = z


def _post(y, pos, zn, modl, nct, final_g=None):
    b_, lt, d = zn.shape
    nt = lt // TM
    final = final_g is not None
    t0 = nct if final else 0
    ntl = nt - t0
    nstep = b_ * ntl
    idx_spec = lambda f: pl.BlockSpec((1, 1, TM), f, memory_space=pltpu.SMEM)
    ins = [pos, pos, y, zn, modl]
    specs = [idx_spec(lambda t: (t, 0, 0)),
             idx_spec(lambda t: (jnp.minimum(t + 1, nstep - 1), 0, 0)),
             pl.BlockSpec(memory_space=pl.ANY),
             pl.BlockSpec((1, TM, d), lambda t: (t // ntl, t % ntl + t0, 0)),
             pl.BlockSpec((1, 1, SUB, d), lambda t: (t // ntl, jnp.minimum((t % ntl + t0) // nct, 1), 0, 0))]
    if final:
        ins.append(final_g)
        specs.append(pl.BlockSpec(final_g.shape, lambda t: (0, 0)))
    return pl.pallas_call(
        functools.partial(_post_kernel, final=final),
        grid=(nstep,),
        in_specs=specs,
        out_specs=pl.BlockSpec((1, TM, d), lambda t: (t // ntl, t % ntl, 0)),
        out_shape=jax.ShapeDtypeStruct((b_, ntl * TM, d), F32),
        scratch_shapes=[pltpu.VMEM((2, TM * SUB, LANE), F32), pltpu.SemaphoreType.DMA((2,))],
        compiler_params=_cparams(1),
        name="unsort_residual",
    )(*ins)


_PAIRS = [(0, 1), (0, 2), (0, 3), (1, 2), (1, 3), (2, 3)]
_CLS_EA = np.array([EPG * g + p[0] for g in range(N_GROUPS) for p in _PAIRS], np.int32)
_CLS_EB = np.array([EPG * g + p[1] for g in range(N_GROUPS) for p in _PAIRS], np.int32)


def _sort_plan(cls, rank, counts):
    n = cls.shape[0]
    nblk = n // MOE_BM + N_CLASSES
    padded = (counts + MOE_BM - 1) // MOE_BM * MOE_BM
    pad_end = jnp.cumsum(padded)
    pos = (pad_end - padded)[cls] + rank
    starts = jnp.arange(nblk, dtype=jnp.int32)[:, None] * MOE_BM
    blk_cls = jnp.minimum(jnp.sum((pad_end[None, :] <= starts).astype(jnp.int32), axis=1), N_CLASSES - 1)
    n_used = (pad_end[-1] // MOE_BM).astype(jnp.int32).reshape(1)
    return pos, jnp.asarray(_CLS_EA)[blk_cls], jnp.asarray(_CLS_EB)[blk_cls], n_used


def _rot_cols(w):
    return jnp.concatenate([-w[:, 8:16], w[:, 0:8], -w[:, 24:32], w[:, 16:24]], axis=1)


def _prep_w_in(w):
    kpe = w[:, 3200:3232]
    pad = jnp.zeros((w.shape[0], C_END - C_KPE - 2 * MLA_ROPE), w.dtype)
    return jnp.concatenate([w[:, :3200], w[:, 3232:], kpe, _rot_cols(kpe), pad], axis=1).astype(BF16)


def _prep_mla(w_uq, w_ukv):
    odd = (jnp.arange(MLA_HEADS) % 2 == 1)[None, :, None]
    qh = w_uq.reshape(MLA_Q_RANK, MLA_HEADS, MLA_NOPE + MLA_ROPE)
    nope, pe = qh[:, :, :MLA_NOPE], qh[:, :, MLA_NOPE:]
    pe_sw = _rot_cols(pe.reshape(MLA_Q_RANK * MLA_HEADS, MLA_ROPE)).reshape(MLA_Q_RANK, MLA_HEADS, MLA_ROPE)
    wq = jnp.where(odd, jnp.concatenate([nope, pe, pe_sw], axis=2), jnp.concatenate([pe, pe_sw, nope], axis=2))
    kvh = w_ukv.reshape(MLA_KV_RANK, MLA_HEADS, MLA_NOPE + MLA_V)
    kn, vh = kvh[:, :, :MLA_NOPE], kvh[:, :, MLA_NOPE:]
    wkv = jnp.where(odd, jnp.concatenate([kn, vh], axis=2), jnp.concatenate([vh, kn], axis=2))
    return (wq.reshape(MLA_Q_RANK, MLA_HEADS * LANE).astype(BF16),
            wkv.reshape(MLA_KV_RANK, MLA_HEADS * LANE).astype(BF16))


def _rope_tables(n_ctx, n_lat):
    rows = n_lat // GRID_W
    rowp = jnp.broadcast_to(jnp.arange(rows, dtype=F32)[:, None], (rows, GRID_W)).reshape(-1)
    colp = jnp.broadcast_to(jnp.arange(GRID_W, dtype=F32)[None, :], (rows, GRID_W)).reshape(-1)
    axis_dim = MLA_ROPE // 2
    inv_freq = ROPE_THETA ** (-jnp.arange(0, axis_dim, 2, dtype=F32) / axis_dim)
    ar, ac = rowp[:, None] * inv_freq, colp[:, None] * inv_freq
    cos32 = jnp.concatenate([jnp.cos(ar), jnp.cos(ar), jnp.cos(ac), jnp.cos(ac)], axis=1)
    sin32 = jnp.concatenate([jnp.sin(ar), jnp.sin(ar), jnp.sin(ac), jnp.sin(ac)], axis=1)
    cos32 = jnp.concatenate([jnp.ones((n_ctx, MLA_ROPE), F32), cos32], axis=0)
    sin32 = jnp.concatenate([jnp.zeros((n_ctx, MLA_ROPE), F32), sin32], axis=0)
    lt = n_ctx + n_lat
    one, zero = jnp.ones((lt, MLA_NOPE), F32), jnp.zeros((lt, MLA_NOPE), F32)
    z32 = jnp.zeros((lt, MLA_ROPE), F32)
    sc = MLA_SCALE * LOG2E
    cqe = jnp.concatenate([cos32, z32, one], axis=1) * sc
    sqe = jnp.concatenate([sin32, z32, zero], axis=1) * sc
    cqo = jnp.concatenate([one, cos32, z32], axis=1) * sc
    sqo = jnp.concatenate([zero, sin32, z32], axis=1) * sc
    kc = jnp.concatenate([cos32, sin32, zero], axis=1)
    return cqe, sqe, cqo, sqo, kc


def _lower_bounds(gamma):
    p = jnp.cumsum(jax.nn.softmax(gamma.astype(F32), axis=0), axis=0)
    return p - p[0:1]


def kernel(x, c, ctx, c_ctx, w_mod, b_mod, norm1_g, norm2_g, w_in, gamma_fwd, gamma_bwd, hg_norm_g,
           q_norm_g, kv_norm_g, w_uq, w_ukv, w_pa, w_pb, w_o, w_router, router_bias,
           w_gate_e, w_up_e, w_down_e, final_g):
    b_, n_lat, d = x.shape
    n_ctx = ctx.shape[1]
    depth = w_mod.shape[0]
    assert n_ctx % TM == 0 and n_lat % TM == 0 and n_lat % GRID_W == 0 and b_ + 1 <= SUB
    nct = n_ctx // TM
    lt = n_ctx + n_lat
    nt = lt // TM

    cc = jnp.concatenate([c, c_ctx[None, :], jnp.zeros((SUB - b_ - 1, d), F32)], axis=0)
    mod = _modulation(cc, w_mod, b_mod).reshape(depth, SUB, 6, d)
    mod_x = mod[:, :b_]
    mod_c = jnp.broadcast_to(mod[:, b_:b_ + 1], mod_x.shape)
    mod_t = jnp.stack([mod_c, mod_x], axis=2)
    mod_t = jnp.concatenate([mod_t, jnp.zeros((depth, b_, 2, SUB - 6, d), F32)], axis=3)

    cqe, sqe, cqo, sqo, kc = _rope_tables(n_ctx, n_lat)
    lbf, lbb = _lower_bounds(gamma_fwd), _lower_bounds(gamma_bwd)
    wr = jnp.concatenate([w_router, jnp.zeros((d, LANE - N_EXPERTS), F32)], axis=1)
    rbb = jnp.broadcast_to(router_bias.astype(F32)[:, None], (N_EXPERTS, TM))

    z = jnp.concatenate([ctx, x], axis=1)
    out = hs = None
    for l in range(depth):
        last = l == depth - 1
        wq, wkv = _prep_mla(w_uq[l], w_ukv[l])
        (q, v, lff, kf, lfb, kb, g, gas, gbs, qq, kk, vv) = _inproj(
            z, mod_t[l], norm1_g[l][None], _prep_w_in(w_in[l]), lbf[l][None], lbb[l][None],
            q_norm_g[l][None], kv_norm_g[l][None], wq, wkv, cqe, sqe, cqo, sqo, kc, nct)
        o_f = _hgrn(q, v, lff, kf, nct, reverse=False)
        a = _hgrn(q, v, lfb, kb, nct, reverse=True, extra=(o_f, g, hg_norm_g[l][None]))
        m = _attention(qq, kk, vv, n_ctx)
        zn, ht, route, cnt = _merge(a, m, gas, gbs, z, mod_t[l], w_pa[l].astype(BF16), w_pb[l].astype(BF16),
                               w_o[l].astype(BF16), norm2_g[l][None], wr, rbb, nct)
        route = route[:, 0:2, :].astype(jnp.int32)
        pos, blk_ea, blk_eb, n_used = _sort_plan(route[:, 0].reshape(-1), route[:, 1].reshape(-1),
                                                 cnt[:N_CLASSES, 0].astype(jnp.int32))
        pos = pos.reshape(b_, nt, 1, TM)
        hs = _sort_rows(ht, pos.reshape(-1, 1, TM), blk_ea.shape[0] * MOE_BM, hs)
        y = _moe(hs, w_router.T, blk_ea, blk_eb, n_used, w_gate_e, w_up_e, w_down_e, l)
        if last:
            out = _post(y, pos[:, nct:].reshape(-1, 1, TM), zn, mod_t[l], nct, final_g[None])
        else:
            z = _post(y, pos.reshape(-1, 1, TM), zn, mod_t[l], nct)
    return out
```

```python
import functools

import numpy as np
import jax
import jax.numpy as jnp
from jax import lax
from jax.experimental import pallas as pl
from jax.experimental.pallas import tpu as pltpu

F32 = jnp.float32
BF16 = jnp.bfloat16

EPS = 1e-6
GRID_W = 64
ROPE_THETA = 10000.0
HG_HEADS = 4
HG_DK = 128
HG_WIDTH = 512
MLA_HEADS = 8
MLA_NOPE = 64
MLA_ROPE = 32
MLA_V = 64
MLA_Q_RANK = 384
MLA_KV_RANK = 256
MLA_SCALE = (MLA_NOPE + MLA_ROPE) ** -0.5
LOG2E = 1.4426950408889634
N_EXPERTS = 16
N_GROUPS = 4
EPG = 4
N_CLASSES = N_GROUPS * 6

LANE = 128
SUB = 8
TM = 256
CHUNK = 128
MOE_BM = 256
VMEM_LIMIT = 56 * 1024 * 1024

C_Q, C_I, C_FF, C_FB, C_G, C_DQ, C_DKV, C_GA, C_GB, C_KPE, C_END = (
    0, 512, 1024, 1536, 2048, 2560, 2944, 3200, 4224, 5248, 5376)


def _cparams(n_axes):
    return pltpu.CompilerParams(dimension_semantics=("arbitrary",) * n_axes,
                                vmem_limit_bytes=VMEM_LIMIT)


def _rms(x, g):
    y = x * lax.rsqrt(jnp.mean(x * x, axis=-1, keepdims=True) + EPS)
    return y * g


def _dot(a, b):
    return jnp.dot(a, b, preferred_element_type=F32)


def _dot_nt(a, b):
    return lax.dot_general(a, b, (((1,), (1,)), ((), ())), preferred_element_type=F32)


def _dot_tn(a, b):
    return lax.dot_general(a, b, (((0,), (0,)), ((), ())), preferred_element_type=F32)


def _split3(x):
    hi = x.astype(BF16)
    r1 = x - hi.astype(F32)
    mid = r1.astype(BF16)
    lo = (r1 - mid.astype(F32)).astype(BF16)
    return hi, mid, lo


def _dot_f32(a, b, passes=6):
    a0, a1, a2 = _split3(a)
    b0, b1, b2 = _split3(b)
    out = _dot(a0, b0) + (_dot(a0, b1) + _dot(a1, b0))
    if passes == 6:
        out = out + (_dot(a0, b2) + _dot(a1, b1) + _dot(a2, b0))
    return out


def _mod_kernel(c_ref, w_ref, b_ref, o_ref):
    c = c_ref[...]
    s = c * jax.nn.sigmoid(c)
    o_ref[0] = _dot_f32(s, w_ref[0], passes=3) + b_ref[0]


def _modulation(cc, w_mod, b_mod):
    depth, d, n6 = w_mod.shape
    nb = 1536
    return pl.pallas_call(
        _mod_kernel,
        grid=(depth, n6 // nb),
        in_specs=[pl.BlockSpec((SUB, d), lambda l, j: (0, 0)),
                  pl.BlockSpec((1, d, nb), lambda l, j: (l, 0, j)),
                  pl.BlockSpec((1, 1, nb), lambda l, j: (l, 0, j))],
        out_specs=pl.BlockSpec((1, SUB, nb), lambda l, j: (l, 0, j)),
        out_shape=jax.ShapeDtypeStruct((depth, SUB, n6), F32),
        compiler_params=_cparams(2),
        name="modulation",
    )(cc, w_mod, b_mod.reshape(depth, 1, n6))


def _inproj_kernel(z_ref, mod_ref, g1_ref, w_ref, lbf_ref, lbb_ref, qg_ref, kvg_ref,
                   wq_ref, wkv_ref, cqe_ref, sqe_ref, cqo_ref, sqo_ref, kc_ref,
                   q_o, v_o, lff_o, kf_o, lfb_o, kb_o, g_o, gas_o, gbs_o, qq_o, kk_o, vv_o):
    mod = mod_ref[0, 0]
    h = _rms(z_ref[0], g1_ref[...]) * (1.0 + mod[1:2]) + mod[0:1]
    hb = h.astype(BF16)

    p = _dot(hb, w_ref[...])

    def seg(a, b):
        return p[:, a:b]

    q_o[0] = seg(C_Q, C_I).astype(BF16)
    v_o[0] = seg(C_I, C_FF).astype(BF16)
    for a, lb_ref, lf_o, k_o in ((C_FF, lbf_ref, lff_o, kf_o), (C_FB, lbb_ref, lfb_o, kb_o)):
        lb = lb_ref[...]
        f = lb + (1.0 - lb) * jax.nn.sigmoid(seg(a, a + HG_WIDTH))
        lf_o[0] = jnp.log(f) * LOG2E
        k_o[0] = (1.0 - f).astype(BF16)
    g_o[0] = seg(C_G, C_DQ).astype(BF16)
    gas_o[0] = jax.nn.sigmoid(seg(C_GA, C_GB)).astype(BF16)
    gbs_o[0] = jax.nn.sigmoid(seg(C_GB, C_KPE)).astype(BF16)

    lane = lax.broadcasted_iota(jnp.int32, (TM, LANE), 1)
    qn = _rms(seg(C_DQ, C_DKV), qg_ref[...]).astype(BF16)
    qa = _dot(qn, wq_ref[...])
    tabs = ((cqe_ref[...], sqe_ref[...]), (cqo_ref[...], sqo_ref[...]))
    for hh in range(MLA_HEADS):
        sl = slice(hh * LANE, (hh + 1) * LANE)
        cq, sq = tabs[hh % 2]
        blk = qa[:, sl]
        qq_o[0, :, sl] = (blk * cq + pltpu.roll(blk, LANE - MLA_ROPE, 1) * sq).astype(BF16)
    kvn = _rms(seg(C_DKV, C_GA), kvg_ref[...]).astype(BF16)
    kv = _dot(kvn, wkv_ref[...])
    kp = seg(C_KPE, C_END) * kc_ref[...]
    kpe_e = jnp.where(lane < MLA_ROPE, kp + pltpu.roll(kp, LANE - MLA_ROPE, 1), 0.0)
    kpe_o = pltpu.roll(kpe_e, MLA_NOPE, 1)
    one_e = jnp.where(lane == MLA_V, 1.0, 0.0)
    one_o = jnp.where(lane == 0, 1.0, 0.0)
    for hh in range(MLA_HEADS):
        sl = slice(hh * LANE, (hh + 1) * LANE)
        blk = kv[:, sl]
        if hh % 2 == 0:
            kk_o[0, sl, :] = jnp.where(lane >= MLA_V, blk, kpe_e).T.astype(BF16)
            vv_o[0, :, sl] = jnp.where(lane < MLA_V, blk, one_e).astype(BF16)
        else:
            kk_o[0, sl, :] = jnp.where(lane < MLA_NOPE, blk, kpe_o).T.astype(BF16)
            vv_o[0, :, sl] = jnp.where(lane >= MLA_NOPE, blk, one_o).astype(BF16)


def _inproj(z, modl, g1, w, lbf, lbb, qg, kvg, wq, wkv, cqe, sqe, cqo, sqo, kc, nct):
    b_, lt, d = z.shape
    nt = lt // TM
    row = lambda w_: pl.BlockSpec((1, TM, w_), lambda b, i: (b, i, 0))
    full = lambda a: pl.BlockSpec(a.shape, lambda b, i: (0,) * a.ndim)
    tab = pl.BlockSpec((TM, LANE), lambda b, i: (i, 0))
    f32o = lambda w_: jax.ShapeDtypeStruct((b_, lt, w_), F32)
    bf16o = lambda w_: jax.ShapeDtypeStruct((b_, lt, w_), BF16)
    return pl.pallas_call(
        _inproj_kernel,
        grid=(b_, nt),
        in_specs=[row(d),
                  pl.BlockSpec((1, 1, SUB, d), lambda b, i: (b, jnp.minimum(i // nct, 1), 0, 0)),
                  full(g1), full(w), full(lbf), full(lbb), full(qg), full(kvg),
                  full(wq), full(wkv), tab, tab, tab, tab, tab],
        out_specs=[row(512)] * 7 + [row(1024)] * 3
        + [pl.BlockSpec((1, MLA_HEADS * LANE, TM), lambda b, i: (b, 0, i)), row(1024)],
        out_shape=[bf16o(512), bf16o(512), f32o(512), bf16o(512), f32o(512), bf16o(512), bf16o(512)]
        + [bf16o(1024)] * 3 + [jax.ShapeDtypeStruct((b_, MLA_HEADS * LANE, lt), BF16), bf16o(1024)],
        compiler_params=_cparams(2),
        name="inproj",
    )(z, modl, g1, w, lbf, lbb, qg, kvg, wq, wkv, cqe, sqe, cqo, sqo, kc)


N_LEVELS = 7


def _chunk_tables(reverse):
    c = CHUNK
    t = np.arange(c)[:, None]
    u = np.arange(c)[None, :]
    tri = ((u >= t) if reverse else (u <= t)).astype(np.float32)
    lvl = np.full((c, c), N_LEVELS + 1, np.int32)
    lvl[t == u] = 0
    for l in range(N_LEVELS, 0, -1):
        m = 1 << l
        same = (t // m) == (u // m)
        lvl[same & ((u > t) if reverse else (u < t))] = l
    return jnp.asarray(tri, dtype=BF16), jnp.asarray(lvl)


def _level_ref(bs_ref, hsl, l, reverse):
    w = HG_DK
    sub = lax.broadcasted_iota(jnp.int32, (SUB, w), 0)
    off = 0 if reverse else -1
    m = 1 << l

    def row(r):
        return jnp.broadcast_to(bs_ref[r:r + 1, hsl], (SUB, w))

    pieces = []
    for j in range(CHUNK // SUB):
        base = j * SUB
        if m >= SUB:
            pieces.append(row((base // m) * m + m // 2 + off))
        else:
            p = row(base + m // 2 + off)
            for i in range(1, SUB // m):
                p = jnp.where(sub < i * m, p, row(base + i * m + m // 2 + off))
            pieces.append(p)
    return jnp.concatenate(pieces, axis=0)


def _hgrn_kernel(*refs, reverse, readout):
    if readout:
        q_ref, v_ref, lf_ref, k_ref, tri_ref, lvl_ref, of_ref, g_ref, gain_ref, o_ref, s_ref, bs_ref = refs
    else:
        q_ref, v_ref, lf_ref, k_ref, tri_ref, lvl_ref, o_ref, s_ref, bs_ref = refs

    @pl.when(pl.program_id(1) == 0)
    def _():
        s_ref[...] = jnp.zeros_like(s_ref)

    c = CHUNK
    nch = TM // c
    e = lambda x: jnp.exp2(x).astype(BF16)

    lvl = lvl_ref[...]
    for ci in range(nch):
        cidx = (nch - 1 - ci) if reverse else ci
        rows = slice(cidx * c, (cidx + 1) * c)
        lf = lf_ref[0, rows, :]
        hi = lf.astype(BF16)
        lo = (lf - hi.astype(F32)).astype(BF16)
        bsc = bs_ref.at[ci]
        bsc[...] = _dot(tri_ref[...], hi) + _dot(tri_ref[...], lo)
        for hh in range(HG_HEADS):
            hsl = slice(hh * HG_DK, (hh + 1) * HG_DK)
            b = bsc[:, hsl]
            btot = b[0:1] if reverse else b[c - 1:c]
            q = q_ref[0, rows, hsl]
            k = k_ref[0, rows, hsl]
            ix = (0, rows, hsl)
            v = v_ref[ix]
            st = s_ref[hh]
            att = jnp.where(lvl == 0, _dot_nt(q, k), 0.0)
            for l in range(1, N_LEVELS + 1):
                f = e(-jnp.abs(b - _level_ref(bsc, hsl, l, reverse)))
                att = jnp.where(lvl == l, _dot_nt(q * f, k * f), att)
            o = _dot_nt(q * e(b), st.astype(BF16)) + _dot(att.astype(BF16), v)
            s_ref[hh] = st * jnp.exp2(btot) + _dot_tn(v, k * e(btot - b))
            if readout:
                o = o + of_ref[ix]
                o = o * lax.rsqrt(jnp.mean(o * o, axis=-1, keepdims=True) + EPS)
                o = o * gain_ref[:, hsl]
                g = g_ref[ix].astype(F32)
                o = o * (g * jax.nn.sigmoid(g))
            o_ref[ix] = o.astype(o_ref.dtype)


def _hgrn(q, v, lf, k, nct, reverse, extra=None):
    b_, lt, w = q.shape
    nt = lt // TM
    if reverse:
        tile = lambda i: jnp.where(i < nct, nct - 1 - i, nt - 1 - (i - nct))
    else:
        tile = lambda i: i
    row = pl.BlockSpec((1, TM, w), lambda b, i: (b, tile(i), 0))
    full = lambda a: pl.BlockSpec(a.shape, lambda b, i: (0,) * a.ndim)
    tri, lvl = _chunk_tables(reverse)
    ins = [q, v, lf, k, tri, lvl]
    specs = [row] * 4 + [full(tri), full(lvl)]
    if extra is not None:
        o_f, g, gain = extra
        ins += [o_f, g, gain]
        specs += [row, row, full(gain)]
    return pl.pallas_call(
        functools.partial(_hgrn_kernel, reverse=reverse, readout=extra is not None),
        grid=(b_, nt),
        in_specs=specs,
        out_specs=row,
        out_shape=jax.ShapeDtypeStruct((b_, lt, w), BF16),
        scratch_shapes=[pltpu.VMEM((HG_HEADS, HG_DK, HG_DK), F32), pltpu.VMEM((TM // CHUNK, CHUNK, w), F32)],
        compiler_params=_cparams(2),
        name="hgrn_bwd" if reverse else "hgrn_fwd",
    )(*ins)


KV_CHUNK = 256


def _attn_kernel(q_ref, k_ref, v_ref, o_ref, *, n_ctx, nct):
    i = pl.program_id(2)
    lane = lax.broadcasted_iota(jnp.int32, (TM, LANE), 1)

    def run(nk):
        outs = []
        for hh in range(2):
            sl = slice(hh * LANE, (hh + 1) * LANE)
            q = q_ref[0, :, sl]
            m = acc = None
            edges = [0] + list(range(n_ctx, nk + 1, KV_CHUNK))
            for c in range(len(edges) - 1):
                ks = slice(edges[c], edges[c + 1])
                s = _dot(q, k_ref[0, sl, ks])
                mc = jnp.max(s, axis=-1, keepdims=True)
                if c == 0:
                    m = mc
                    acc = _dot(jnp.exp2(s - m).astype(BF16), v_ref[0, ks, sl])
                else:
                    m_new = jnp.maximum(m, mc)
                    acc = acc * jnp.exp2(m - m_new) + _dot(jnp.exp2(s - m_new).astype(BF16), v_ref[0, ks, sl])
                    m = m_new
            den = acc[:, MLA_V:MLA_V + 1] if hh == 0 else acc[:, 0:1]
            outs.append(acc / den)
        o_ref[0] = jnp.where(lane < MLA_V, outs[0], outs[1]).astype(o_ref.dtype)

    @pl.when(i < nct)
    def _():
        run(n_ctx)

    @pl.when(i >= nct)
    def _():
        run(k_ref.shape[2])


def _attention(qq, kk, vv, n_ctx):
    b_, lt, _ = qq.shape
    nt = lt // TM
    npair = MLA_HEADS // 2
    return pl.pallas_call(
        functools.partial(_attn_kernel, n_ctx=n_ctx, nct=n_ctx // TM),
        grid=(b_, npair, nt),
        in_specs=[pl.BlockSpec((1, TM, 2 * LANE), lambda b, j, i: (b, i, j)),
                  pl.BlockSpec((1, 2 * LANE, lt), lambda b, j, i: (b, j, 0)),
                  pl.BlockSpec((1, lt, 2 * LANE), lambda b, j, i: (b, 0, j))],
        out_specs=pl.BlockSpec((1, TM, LANE), lambda b, j, i: (b, i, j)),
        out_shape=jax.ShapeDtypeStruct((b_, lt, npair * LANE), BF16),
        compiler_params=_cparams(3),
        name="mla_attention",
    )(qq, kk, vv)


def _route(sel):
    r = lambda x, e: x[e:e + 1]
    best = None
    for g in range(N_GROUPS):
        u = [r(sel, EPG * g + j) for j in range(EPG)]
        gs = None
        for a in range(EPG):
            for b in range(a + 1, EPG):
                pr = u[a] + u[b]
                gs = pr if gs is None else jnp.maximum(gs, pr)
        if best is None:
            best, gi = gs, jnp.zeros_like(gs, dtype=jnp.int32)
        else:
            upd = gs > best
            best = jnp.where(upd, gs, best)
            gi = jnp.where(upd, g, gi)

    def pick(x, j):
        out = r(x, j)
        for g in range(1, N_GROUPS):
            out = jnp.where(gi == g, r(x, EPG * g + j), out)
        return out

    u = [pick(sel, j) for j in range(EPG)]

    def argmax4(vals):
        bv, bi = vals[0], jnp.zeros_like(gi)
        for j in range(1, EPG):
            upd = vals[j] > bv
            bv = jnp.where(upd, vals[j], bv)
            bi = jnp.where(upd, j, bi)
        return bi

    l1 = argmax4(u)
    l2 = argmax4([jnp.where(l1 == j, -jnp.inf, u[j]) for j in range(EPG)])
    lo = jnp.minimum(l1, l2)
    hi = jnp.maximum(l1, l2)
    pair = jnp.where(lo == 0, jnp.where(hi == 1, 0, hi), jnp.where(lo == 1, jnp.where(hi == 2, 1, 4), 5))
    return (gi * 6 + pair).astype(F32)


CLS_ROWS = 32


def _merge_kernel(a_ref, m_ref, gas_ref, gbs_ref, z_ref, mod_ref, wpa_ref, wpb_ref, wo_ref,
                  g2_ref, wr_ref, rb_ref, triu_ref, zn_o, ht_o, route_o, cnt_o, run_ref):
    @pl.when((pl.program_id(0) == 0) & (pl.program_id(1) == 0))
    def _():
        run_ref[...] = jnp.zeros_like(run_ref)

    mod = mod_ref[0, 0]
    pa = _dot(a_ref[0].astype(BF16), wpa_ref[...])
    pb = _dot(m_ref[0].astype(BF16), wpb_ref[...])
    mixed = gas_ref[0] * pa + gbs_ref[0] * pb
    zn = z_ref[0] + mod[2:3] * _dot(mixed.astype(BF16), wo_ref[...])
    zn_o[0] = zn
    h2 = _rms(zn, g2_ref[...]) * (1.0 + mod[4:5]) + mod[3:4]
    _tiles_store(ht_o, h2)
    logits = _dot_f32(h2, wr_ref[...], passes=3)
    sc = jax.nn.sigmoid(logits.T[:N_EXPERTS])
    cls = _route(sc + rb_ref[...])
    onehot = (lax.broadcasted_iota(jnp.int32, (CLS_ROWS, TM), 0) == cls.astype(jnp.int32))
    pref = _dot(onehot.astype(BF16), triu_ref[...])
    rank = jnp.sum(jnp.where(onehot, pref + run_ref[:, 0:1], 0.0), axis=0, keepdims=True) - 1.0
    run_ref[...] = run_ref[...] + pref[:, TM - 1:TM]
    cnt_o[...] = run_ref[...]
    route_o[0] = jnp.concatenate([cls, rank] + [jnp.zeros_like(cls)] * (SUB - 2), axis=0)


def _merge(a, m, gas, gbs, z, modl, wpa, wpb, wo, g2, wr, rbb, nct):
    b_, lt, d = z.shape
    nt = lt // TM
    row = lambda w_: pl.BlockSpec((1, TM, w_), lambda b, i: (b, i, 0))
    full = lambda x: pl.BlockSpec(x.shape, lambda b, i: (0,) * x.ndim)
    triu = jnp.asarray(np.triu(np.ones((TM, TM), np.float32)), dtype=BF16)
    return pl.pallas_call(
        _merge_kernel,
        grid=(b_, nt),
        in_specs=[row(512), row(512), row(d), row(d), row(d),
                  pl.BlockSpec((1, 1, SUB, d), lambda b, i: (b, jnp.minimum(i // nct, 1), 0, 0)),
                  full(wpa), full(wpb), full(wo), full(g2), full(wr), full(rbb), full(triu)],
        out_specs=[row(d), pl.BlockSpec((TM * SUB, LANE), lambda b, i: (b * nt + i, 0)),
                   pl.BlockSpec((1, SUB, TM), lambda b, i: (b, 0, i)),
                   pl.BlockSpec((CLS_ROWS, LANE), lambda b, i: (0, 0))],
        out_shape=[jax.ShapeDtypeStruct((b_, lt, d), F32),
                   jax.ShapeDtypeStruct((b_ * lt * SUB, LANE), F32),
                   jax.ShapeDtypeStruct((b_, SUB, lt), F32),
                   jax.ShapeDtypeStruct((CLS_ROWS, LANE), F32)],
        scratch_shapes=[pltpu.VMEM((CLS_ROWS, LANE), F32)],
        compiler_params=_cparams(2),
        name="merge_route",
    )(a, m, gas, gbs, z, modl, wpa, wpb, wo, g2, wr, rbb, triu)


def _tiles_store(ref, x, lead=()):
    n = x.shape[0]
    for j in range(SUB):
        ref[lead + (pl.ds(j, n, stride=SUB), slice(None))] = x[:, j * LANE:(j + 1) * LANE]


def _tiles_load(ref, n, lead=()):
    return jnp.concatenate([ref[lead + (pl.ds(j, n, stride=SUB), slice(None))] for j in range(SUB)], axis=1)


def _tile_copy(src_hbm, idx_ref, buf, sem, slot, r):
    return pltpu.make_async_copy(src_hbm.at[pl.ds(pl.multiple_of(idx_ref[0, 0, r] * SUB, SUB), SUB)],
                                 buf.at[slot, pl.ds(pl.multiple_of(r * SUB, SUB), SUB)], sem.at[slot])


GATHER_UNROLL = 8


def _gather_start(src_hbm, idx_ref, buf, sem, slot, n):
    def body(r8, c):
        for u in range(GATHER_UNROLL):
            _tile_copy(src_hbm, idx_ref, buf, sem, slot, r8 * GATHER_UNROLL + u).start(priority=u % 2)
        return c
    lax.fori_loop(0, n // GATHER_UNROLL, body, 0)


def _gather_wait(src_hbm, buf, sem, slot):
    pltpu.make_async_copy(src_hbm.at[pl.ds(0, buf.shape[1])], buf.at[slot], sem.at[slot]).wait()


def _sort_kernel(pos_ref, ht_ref, init_hbm, hs_hbm, sem):
    del init_hbm

    def body(r8, c):
        for u in range(GATHER_UNROLL):
            r = r8 * GATHER_UNROLL + u
            pltpu.make_async_copy(
                ht_ref.at[pl.ds(pl.multiple_of(r * SUB, SUB), SUB)],
                hs_hbm.at[pl.ds(pl.multiple_of(pos_ref[0, 0, r] * SUB, SUB), SUB)], sem.at[0]).start(priority=u % 2)
        return c
    lax.fori_loop(0, TM // GATHER_UNROLL, body, 0)
    pltpu.make_async_copy(ht_ref, hs_hbm.at[pl.ds(0, TM * SUB)], sem.at[0]).wait()


def _sort_rows(ht, pos, n_sorted, init=None):
    nstep = pos.shape[0]
    if init is None:
        init = jnp.zeros((n_sorted * SUB, LANE), F32)
    return pl.pallas_call(
        _sort_kernel,
        grid=(nstep,),
        in_specs=[pl.BlockSpec((1, 1, TM), lambda t: (t, 0, 0), memory_space=pltpu.SMEM),
                  pl.BlockSpec((TM * SUB, LANE), lambda t: (t, 0)), pl.BlockSpec(memory_space=pl.ANY)],
        out_specs=pl.BlockSpec(memory_space=pl.ANY),
        out_shape=jax.ShapeDtypeStruct(init.shape, F32),
        scratch_shapes=[pltpu.SemaphoreType.DMA((1,))],
        input_output_aliases={2: 0},
        compiler_params=_cparams(1),
        name="class_sort",
    )(pos, ht, init)


def _moe_kernel(ea_ref, eb_ref, nu_ref, hs_ref, wrt_ref,
                wga_ref, wua_ref, wda_ref, wgb_ref, wub_ref, wdb_ref, y_ref, gu_a, dn_a, gu_b, dn_b):
    j = pl.program_id(0)
    jp = jnp.maximum(j - 1, 0)

    @pl.when((j == 0) | (ea_ref[j] != ea_ref[jp]))
    def _():
        gu_a[0] = wga_ref[0].astype(BF16)
        gu_a[1] = wua_ref[0].astype(BF16)
        dn_a[...] = wda_ref[0].astype(BF16)

    @pl.when((j == 0) | (eb_ref[j] != eb_ref[jp]))
    def _():
        gu_b[0] = wgb_ref[0].astype(BF16)
        gu_b[1] = wub_ref[0].astype(BF16)
        dn_b[...] = wdb_ref[0].astype(BF16)

    @pl.when(j < nu_ref[0])
    def _():
        x32 = _tiles_load(hs_ref, MOE_BM)
        x = x32.astype(BF16)

        def ffn(gu, dn, e):
            sc = jax.nn.sigmoid(jnp.sum(x32 * wrt_ref[pl.ds(e, 1), :], axis=-1, keepdims=True))
            act = jax.nn.silu(_dot(x, gu[0])) * _dot(x, gu[1])
            return sc, _dot(act.astype(BF16), dn[...])

        s_a, y_a = ffn(gu_a, dn_a, ea_ref[j])
        s_b, y_b = ffn(gu_b, dn_b, eb_ref[j])
        inv = 1.0 / (s_a + s_b)
        _tiles_store(y_ref, (s_a * inv) * y_a + (s_b * inv) * y_b)

    @pl.when(j >= nu_ref[0])
    def _():
        y_ref[...] = jnp.zeros_like(y_ref)


def _moe(hs, wrt, blk_ea, blk_eb, n_used, wg, wu, wd, layer):
    nblk = blk_ea.shape[0]
    d, de = wg.shape[2:]
    tiles = pl.BlockSpec((MOE_BM * SUB, LANE), lambda j, ea, eb, nu: (j, 0))
    wspec = lambda shp, which: pl.BlockSpec(
        (None, 1) + shp, (lambda j, ea, eb, nu: (layer, ea[j], 0, 0)) if which == 0
        else (lambda j, ea, eb, nu: (layer, eb[j], 0, 0)))
    return pl.pallas_call(
        _moe_kernel,
        grid_spec=pltpu.PrefetchScalarGridSpec(
            num_scalar_prefetch=3,
            grid=(nblk,),
            in_specs=[tiles,
                      pl.BlockSpec(wrt.shape, lambda j, ea, eb, nu: (0, 0)),
                      wspec((d, de), 0), wspec((d, de), 0), wspec((de, d), 0),
                      wspec((d, de), 1), wspec((d, de), 1), wspec((de, d), 1)],
            out_specs=tiles,
            scratch_shapes=[pltpu.VMEM((2, d, de), BF16), pltpu.VMEM((de, d), BF16)] * 2),
        out_shape=jax.ShapeDtypeStruct(hs.shape, F32),
        compiler_params=_cparams(1),
        name="moe_ffn",
    )(blk_ea, blk_eb, n_used, hs, wrt, wg, wu, wd, wg, wu, wd)


def _post_kernel(*refs, final):
    if final:
        idx_ref, idxn_ref, y_hbm, zn_ref, mod_ref, fg_ref, o_ref, buf, sem = refs
    else:
        idx_ref, idxn_ref, y_hbm, zn_ref, mod_ref, o_ref, buf, sem = refs
    t = pl.program_id(0)
    nstep = pl.num_programs(0)
    slot = t % 2

    @pl.when(t == 0)
    def _():
        _gather_start(y_hbm, idx_ref, buf, sem, 0, TM)

    @pl.when(t + 1 < nstep)
    def _():
        _gather_start(y_hbm, idxn_ref, buf, sem, 1 - slot, TM)

    _gather_wait(y_hbm, buf, sem, slot)
    z = zn_ref[0] + mod_ref[0, 0][5:6] * _tiles_load(buf, TM, (slot,))
    if final:
        z = _rms(z, fg_ref[...])
    o_ref[0] = z


def _post(y, pos, zn, modl, nct, final_g=None):
    b_, lt, d = zn.shape
    nt = lt // TM
    final = final_g is not None
    t0 = nct if final else 0
    ntl = nt - t0
    nstep = b_ * ntl
    idx_spec = lambda f: pl.BlockSpec((1, 1, TM), f, memory_space=pltpu.SMEM)
    ins = [pos, pos, y, zn, modl]
    specs = [idx_spec(lambda t: (t, 0, 0)),
             idx_spec(lambda t: (jnp.minimum(t + 1, nstep - 1), 0, 0)),
             pl.BlockSpec(memory_space=pl.ANY),
             pl.BlockSpec((1, TM, d), lambda t: (t // ntl, t % ntl + t0, 0)),
             pl.BlockSpec((1, 1, SUB, d), lambda t: (t // ntl, jnp.minimum((t % ntl + t0) // nct, 1), 0, 0))]
    if final:
        ins.append(final_g)
        specs.append(pl.BlockSpec(final_g.shape, lambda t: (0, 0)))
    return pl.pallas_call(
        functools.partial(_post_kernel, final=final),
        grid=(nstep,),
        in_specs=specs,
        out_specs=pl.BlockSpec((1, TM, d), lambda t: (t // ntl, t % ntl, 0)),
        out_shape=jax.ShapeDtypeStruct((b_, ntl * TM, d), F32),
        scratch_shapes=[pltpu.VMEM((2, TM * SUB, LANE), F32), pltpu.SemaphoreType.DMA((2,))],
        compiler_params=_cparams(1),
        name="unsort_residual",
    )(*ins)


_PAIR_SLOTS = [(0, 1), (2, 1), (2, 0), (3, 0), (3, 1), (3, 2)]
_CLS_EA = np.array([EPG * g + p[0] for g in range(N_GROUPS) for p in _PAIR_SLOTS], np.int32)
_CLS_EB = np.array([EPG * g + p[1] for g in range(N_GROUPS) for p in _PAIR_SLOTS], np.int32)


def _sort_plan(cls, rank, counts):
    n = cls.shape[0]
    nblk = n // MOE_BM + N_CLASSES
    padded = (counts + MOE_BM - 1) // MOE_BM * MOE_BM
    pad_end = jnp.cumsum(padded)
    pos = (pad_end - padded)[cls] + rank
    starts = jnp.arange(nblk, dtype=jnp.int32)[:, None] * MOE_BM
    blk_cls = jnp.minimum(jnp.sum((pad_end[None, :] <= starts).astype(jnp.int32), axis=1), N_CLASSES - 1)
    n_used = (pad_end[-1] // MOE_BM).astype(jnp.int32).reshape(1)
    return pos, jnp.asarray(_CLS_EA)[blk_cls], jnp.asarray(_CLS_EB)[blk_cls], n_used


def _rot_cols(w):
    return jnp.concatenate([-w[:, 8:16], w[:, 0:8], -w[:, 24:32], w[:, 16:24]], axis=1)


def _prep_w_in(w):
    kpe = w[:, 3200:3232]
    pad = jnp.zeros((w.shape[0], C_END - C_KPE - 2 * MLA_ROPE), w.dtype)
    return jnp.concatenate([w[:, :3200], w[:, 3232:], kpe, _rot_cols(kpe), pad], axis=1).astype(BF16)


def _prep_mla(w_uq, w_ukv):
    odd = (jnp.arange(MLA_HEADS) % 2 == 1)[None, :, None]
    qh = w_uq.reshape(MLA_Q_RANK, MLA_HEADS, MLA_NOPE + MLA_ROPE)
    nope, pe = qh[:, :, :MLA_NOPE], qh[:, :, MLA_NOPE:]
    pe_sw = _rot_cols(pe.reshape(MLA_Q_RANK * MLA_HEADS, MLA_ROPE)).reshape(MLA_Q_RANK, MLA_HEADS, MLA_ROPE)
    wq = jnp.where(odd, jnp.concatenate([nope, pe, pe_sw], axis=2), jnp.concatenate([pe, pe_sw, nope], axis=2))
    kvh = w_ukv.reshape(MLA_KV_RANK, MLA_HEADS, MLA_NOPE + MLA_V)
    kn, vh = kvh[:, :, :MLA_NOPE], kvh[:, :, MLA_NOPE:]
    wkv = jnp.where(odd, jnp.concatenate([kn, vh], axis=2), jnp.concatenate([vh, kn], axis=2))
    return (wq.reshape(MLA_Q_RANK, MLA_HEADS * LANE).astype(BF16),
            wkv.reshape(MLA_KV_RANK, MLA_HEADS * LANE).astype(BF16))


def _rope_tables(n_ctx, n_lat):
    rows = n_lat // GRID_W
    rowp = jnp.broadcast_to(jnp.arange(rows, dtype=F32)[:, None], (rows, GRID_W)).reshape(-1)
    colp = jnp.broadcast_to(jnp.arange(GRID_W, dtype=F32)[None, :], (rows, GRID_W)).reshape(-1)
    axis_dim = MLA_ROPE // 2
    inv_freq = ROPE_THETA ** (-jnp.arange(0, axis_dim, 2, dtype=F32) / axis_dim)
    ar, ac = rowp[:, None] * inv_freq, colp[:, None] * inv_freq
    cos32 = jnp.concatenate([jnp.cos(ar), jnp.cos(ar), jnp.cos(ac), jnp.cos(ac)], axis=1)
    sin32 = jnp.concatenate([jnp.sin(ar), jnp.sin(ar), jnp.sin(ac), jnp.sin(ac)], axis=1)
    cos32 = jnp.concatenate([jnp.ones((n_ctx, MLA_ROPE), F32), cos32], axis=0)
    sin32 = jnp.concatenate([jnp.zeros((n_ctx, MLA_ROPE), F32), sin32], axis=0)
    lt = n_ctx + n_lat
    one, zero = jnp.ones((lt, MLA_NOPE), F32), jnp.zeros((lt, MLA_NOPE), F32)
    z32 = jnp.zeros((lt, MLA_ROPE), F32)
    sc = MLA_SCALE * LOG2E
    cqe = jnp.concatenate([cos32, z32, one], axis=1) * sc
    sqe = jnp.concatenate([sin32, z32, zero], axis=1) * sc
    cqo = jnp.concatenate([one, cos32, z32], axis=1) * sc
    sqo = jnp.concatenate([zero, sin32, z32], axis=1) * sc
    kc = jnp.concatenate([cos32, sin32, zero], axis=1)
    return cqe, sqe, cqo, sqo, kc


def _lower_bounds(gamma):
    p = jnp.cumsum(jax.nn.softmax(gamma.astype(F32), axis=0), axis=0)
    return p - p[0:1]


def kernel(x, c, ctx, c_ctx, w_mod, b_mod, norm1_g, norm2_g, w_in, gamma_fwd, gamma_bwd, hg_norm_g,
           q_norm_g, kv_norm_g, w_uq, w_ukv, w_pa, w_pb, w_o, w_router, router_bias,
           w_gate_e, w_up_e, w_down_e, final_g):
    b_, n_lat, d = x.shape
    n_ctx = ctx.shape[1]
    depth = w_mod.shape[0]
    assert n_ctx % TM == 0 and n_lat % TM == 0 and n_lat % GRID_W == 0 and b_ + 1 <= SUB
    nct = n_ctx // TM
    lt = n_ctx + n_lat
    nt = lt // TM

    cc = jnp.concatenate([c, c_ctx[None, :], jnp.zeros((SUB - b_ - 1, d), F32)], axis=0)
    mod = _modulation(cc, w_mod, b_mod).reshape(depth, SUB, 6, d)
    mod_x = mod[:, :b_]
    mod_c = jnp.broadcast_to(mod[:, b_:b_ + 1], mod_x.shape)
    mod_t = jnp.stack([mod_c, mod_x], axis=2)
    mod_t = jnp.concatenate([mod_t, jnp.zeros((depth, b_, 2, SUB - 6, d), F32)], axis=3)

    cqe, sqe, cqo, sqo, kc = _rope_tables(n_ctx, n_lat)
    lbf, lbb = _lower_bounds(gamma_fwd), _lower_bounds(gamma_bwd)
    wr = jnp.concatenate([w_router, jnp.zeros((d, LANE - N_EXPERTS), F32)], axis=1)
    rbb = jnp.broadcast_to(router_bias.astype(F32)[:, None], (N_EXPERTS, TM))

    z = jnp.concatenate([ctx, x], axis=1)
    out = hs = None
    for l in range(depth):
        last = l == depth - 1
        wq, wkv = _prep_mla(w_uq[l], w_ukv[l])
        (q, v, lff, kf, lfb, kb, g, gas, gbs, qq, kk, vv) = _inproj(
            z, mod_t[l], norm1_g[l][None], _prep_w_in(w_in[l]), lbf[l][None], lbb[l][None],
            q_norm_g[l][None], kv_norm_g[l][None], wq, wkv, cqe, sqe, cqo, sqo, kc, nct)
        o_f = _hgrn(q, v, lff, kf, nct, reverse=False)
        a = _hgrn(q, v, lfb, kb, nct, reverse=True, extra=(o_f, g, hg_norm_g[l][None]))
        m = _attention(qq, kk, vv, n_ctx)
        zn, ht, route, cnt = _merge(a, m, gas, gbs, z, mod_t[l], w_pa[l].astype(BF16), w_pb[l].astype(BF16),
                               w_o[l].astype(BF16), norm2_g[l][None], wr, rbb, nct)
        route = route[:, 0:2, :].astype(jnp.int32)
        pos, blk_ea, blk_eb, n_used = _sort_plan(route[:, 0].reshape(-1), route[:, 1].reshape(-1),
                                                 cnt[:N_CLASSES, 0].astype(jnp.int32))
        pos = pos.reshape(b_, nt, 1, TM)
        hs = _sort_rows(ht, pos.reshape(-1, 1, TM), blk_ea.shape[0] * MOE_BM, hs)
        y = _moe(hs, w_router.T, blk_ea, blk_eb, n_used, w_gate_e, w_up_e, w_down_e, l)
        if last:
            out = _post(y, pos[:, nct:].reshape(-1, 1, TM), zn, mod_t[l], nct, final_g[None])
        else:
            z = _post(y, pos.reshape(-1, 1, TM), zn, mod_t[l], nct)
    return out
```

```python
import functools

import numpy as np
import jax
import jax.numpy as jnp
from jax import lax
from jax.experimental import pallas as pl
from jax.experimental.pallas import tpu as pltpu

F32 = jnp.float32
BF16 = jnp.bfloat16

EPS = 1e-6
GRID_W = 64
ROPE_THETA = 10000.0
HG_HEADS = 4
HG_DK = 128
HG_WIDTH = 512
MLA_HEADS = 8
MLA_NOPE = 64
MLA_ROPE = 32
MLA_V = 64
MLA_Q_RANK = 384
MLA_KV_RANK = 256
MLA_SCALE = (MLA_NOPE + MLA_ROPE) ** -0.5
LOG2E = 1.4426950408889634
N_EXPERTS = 16
N_GROUPS = 4
EPG = 4
N_CLASSES = N_GROUPS * 6

LANE = 128
SUB = 8
TM = 256
CHUNK = 128
MOE_BM = 256
VMEM_LIMIT = 56 * 1024 * 1024

C_Q, C_I, C_FF, C_FB, C_G, C_DQ, C_DKV, C_GA, C_GB, C_KPE, C_END = (
    0, 512, 1024, 1536, 2048, 2560, 2944, 3200, 4224, 5248, 5376)


def _cparams(n_axes):
    return pltpu.CompilerParams(dimension_semantics=("arbitrary",) * n_axes,
                                vmem_limit_bytes=VMEM_LIMIT)


def _rms(x, g):
    y = x * lax.rsqrt(jnp.mean(x * x, axis=-1, keepdims=True) + EPS)
    return y * g


def _dot(a, b):
    return jnp.dot(a, b, preferred_element_type=F32)


def _dot_nt(a, b):
    return lax.dot_general(a, b, (((1,), (1,)), ((), ())), preferred_element_type=F32)


def _dot_tn(a, b):
    return lax.dot_general(a, b, (((0,), (0,)), ((), ())), preferred_element_type=F32)


def _split3(x):
    hi = x.astype(BF16)
    r1 = x - hi.astype(F32)
    mid = r1.astype(BF16)
    lo = (r1 - mid.astype(F32)).astype(BF16)
    return hi, mid, lo


def _dot_f32(a, b, passes=6):
    a0, a1, a2 = _split3(a)
    b0, b1, b2 = _split3(b)
    out = _dot(a0, b0) + (_dot(a0, b1) + _dot(a1, b0))
    if passes == 6:
        out = out + (_dot(a0, b2) + _dot(a1, b1) + _dot(a2, b0))
    return out


def _mod_kernel(c_ref, w_ref, b_ref, o_ref):
    c = c_ref[...]
    s = c * jax.nn.sigmoid(c)
    o_ref[0] = _dot_f32(s, w_ref[0], passes=3) + b_ref[0]


def _modulation(cc, w_mod, b_mod):
    depth, d, n6 = w_mod.shape
    nb = 1536
    return pl.pallas_call(
        _mod_kernel,
        grid=(depth, n6 // nb),
        in_specs=[pl.BlockSpec((SUB, d), lambda l, j: (0, 0)),
                  pl.BlockSpec((1, d, nb), lambda l, j: (l, 0, j)),
                  pl.BlockSpec((1, 1, nb), lambda l, j: (l, 0, j))],
        out_specs=pl.BlockSpec((1, SUB, nb), lambda l, j: (l, 0, j)),
        out_shape=jax.ShapeDtypeStruct((depth, SUB, n6), F32),
        compiler_params=_cparams(2),
        name="modulation",
    )(cc, w_mod, b_mod.reshape(depth, 1, n6))


def _inproj_kernel(z_ref, mod_ref, g1_ref, w_ref, lbf_ref, lbb_ref, qg_ref, kvg_ref,
                   wq_ref, wkv_ref, cqe_ref, sqe_ref, cqo_ref, sqo_ref, kc_ref,
                   q_o, v_o, lff_o, kf_o, lfb_o, kb_o, g_o, gas_o, gbs_o, qq_o, kk_o, vv_o):
    mod = mod_ref[0, 0]
    h = _rms(z_ref[0], g1_ref[...]) * (1.0 + mod[1:2]) + mod[0:1]
    hb = h.astype(BF16)

    p = _dot(hb, w_ref[...])

    def seg(a, b):
        return p[:, a:b]

    q_o[0] = seg(C_Q, C_I).astype(BF16)
    v_o[0] = seg(C_I, C_FF).astype(BF16)
    for a, lb_ref, lf_o, k_o in ((C_FF, lbf_ref, lff_o, kf_o), (C_FB, lbb_ref, lfb_o, kb_o)):
        lb = lb_ref[...]
        f = lb + (1.0 - lb) * jax.nn.sigmoid(seg(a, a + HG_WIDTH))
        lf_o[0] = jnp.log(f) * LOG2E
        k_o[0] = (1.0 - f).astype(BF16)
    g_o[0] = seg(C_G, C_DQ).astype(BF16)
    gas_o[0] = jax.nn.sigmoid(seg(C_GA, C_GB)).astype(BF16)
    gbs_o[0] = jax.nn.sigmoid(seg(C_GB, C_KPE)).astype(BF16)

    lane = lax.broadcasted_iota(jnp.int32, (TM, LANE), 1)
    qn = _rms(seg(C_DQ, C_DKV), qg_ref[...]).astype(BF16)
    qa = _dot(qn, wq_ref[...])
    tabs = ((cqe_ref[...], sqe_ref[...]), (cqo_ref[...], sqo_ref[...]))
    for hh in range(MLA_HEADS):
        sl = slice(hh * LANE, (hh + 1) * LANE)
        cq, sq = tabs[hh % 2]
        blk = qa[:, sl]
        qq_o[0, :, sl] = (blk * cq + pltpu.roll(blk, LANE - MLA_ROPE, 1) * sq).astype(BF16)
    kvn = _rms(seg(C_DKV, C_GA), kvg_ref[...]).astype(BF16)
    kv = _dot(kvn, wkv_ref[...])
    kp = seg(C_KPE, C_END) * kc_ref[...]
    kpe_e = jnp.where(lane < MLA_ROPE, kp + pltpu.roll(kp, LANE - MLA_ROPE, 1), 0.0)
    kpe_o = pltpu.roll(kpe_e, MLA_NOPE, 1)
    one_e = jnp.where(lane == MLA_V, 1.0, 0.0)
    one_o = jnp.where(lane == 0, 1.0, 0.0)
    for hh in range(MLA_HEADS):
        sl = slice(hh * LANE, (hh + 1) * LANE)
        blk = kv[:, sl]
        if hh % 2 == 0:
            kk_o[0, sl, :] = jnp.where(lane >= MLA_V, blk, kpe_e).T.astype(BF16)
            vv_o[0, :, sl] = jnp.where(lane < MLA_V, blk, one_e).astype(BF16)
        else:
            kk_o[0, sl, :] = jnp.where(lane < MLA_NOPE, blk, kpe_o).T.astype(BF16)
            vv_o[0, :, sl] = jnp.where(lane >= MLA_NOPE, blk, one_o).astype(BF16)


def _inproj(z, modl, g1, w, lbf, lbb, qg, kvg, wq, wkv, cqe, sqe, cqo, sqo, kc, nct):
    b_, lt, d = z.shape
    nt = lt // TM
    row = lambda w_: pl.BlockSpec((1, TM, w_), lambda b, i: (b, i, 0))
    full = lambda a: pl.BlockSpec(a.shape, lambda b, i: (0,) * a.ndim)
    tab = pl.BlockSpec((TM, LANE), lambda b, i: (i, 0))
    f32o = lambda w_: jax.ShapeDtypeStruct((b_, lt, w_), F32)
    bf16o = lambda w_: jax.ShapeDtypeStruct((b_, lt, w_), BF16)
    return pl.pallas_call(
        _inproj_kernel,
        grid=(b_, nt),
        in_specs=[row(d),
                  pl.BlockSpec((1, 1, SUB, d), lambda b, i: (b, jnp.minimum(i // nct, 1), 0, 0)),
                  full(g1), full(w), full(lbf), full(lbb), full(qg), full(kvg),
                  full(wq), full(wkv), tab, tab, tab, tab, tab],
        out_specs=[row(512)] * 7 + [row(1024)] * 3
        + [pl.BlockSpec((1, MLA_HEADS * LANE, TM), lambda b, i: (b, 0, i)), row(1024)],
        out_shape=[bf16o(512), bf16o(512), f32o(512), bf16o(512), f32o(512), bf16o(512), bf16o(512)]
        + [bf16o(1024)] * 3 + [jax.ShapeDtypeStruct((b_, MLA_HEADS * LANE, lt), BF16), bf16o(1024)],
        compiler_params=_cparams(2),
        name="inproj",
    )(z, modl, g1, w, lbf, lbb, qg, kvg, wq, wkv, cqe, sqe, cqo, sqo, kc)


N_LEVELS = 7


def _chunk_tables(reverse):
    c = CHUNK
    t = np.arange(c)[:, None]
    u = np.arange(c)[None, :]
    tri = ((u >= t) if reverse else (u <= t)).astype(np.float32)
    lvl = np.full((c, c), N_LEVELS + 1, np.int32)
    lvl[t == u] = 0
    for l in range(N_LEVELS, 0, -1):
        m = 1 << l
        same = (t // m) == (u // m)
        lvl[same & ((u > t) if reverse else (u < t))] = l
    return jnp.asarray(tri, dtype=BF16), jnp.asarray(lvl)


def _level_ref(bs_ref, hsl, l, reverse):
    w = HG_DK
    sub = lax.broadcasted_iota(jnp.int32, (SUB, w), 0)
    off = 0 if reverse else -1
    m = 1 << l

    def row(r):
        return jnp.broadcast_to(bs_ref[r:r + 1, hsl], (SUB, w))

    pieces = []
    for j in range(CHUNK // SUB):
        base = j * SUB
        if m >= SUB:
            pieces.append(row((base // m) * m + m // 2 + off))
        else:
            p = row(base + m // 2 + off)
            for i in range(1, SUB // m):
                p = jnp.where(sub < i * m, p, row(base + i * m + m // 2 + off))
            pieces.append(p)
    return jnp.concatenate(pieces, axis=0)


def _hgrn_kernel(qf_ref, vf_ref, lff_ref, kf_ref, qb_ref, vb_ref, lfb_ref, kb_ref,
                 trif_ref, lvlf_ref, trib_ref, lvlb_ref, of_ref, ob_ref, sf_ref, sb_ref, bsf_ref, bsb_ref):
    @pl.when(pl.program_id(1) == 0)
    def _():
        sf_ref[...] = jnp.zeros_like(sf_ref)
        sb_ref[...] = jnp.zeros_like(sb_ref)

    _hgrn_tile(qf_ref, vf_ref, lff_ref, kf_ref, trif_ref, lvlf_ref, of_ref, sf_ref, bsf_ref, False)
    _hgrn_tile(qb_ref, vb_ref, lfb_ref, kb_ref, trib_ref, lvlb_ref, ob_ref, sb_ref, bsb_ref, True)


def _hgrn_tile(q_ref, v_ref, lf_ref, k_ref, tri_ref, lvl_ref, o_ref, s_ref, bs_ref, reverse):
    c = CHUNK
    nch = TM // c
    e = lambda x: jnp.exp2(x).astype(BF16)

    lvl = lvl_ref[...]
    for ci in range(nch):
        cidx = (nch - 1 - ci) if reverse else ci
        rows = slice(cidx * c, (cidx + 1) * c)
        lf = lf_ref[0, rows, :]
        hi = lf.astype(BF16)
        lo = (lf - hi.astype(F32)).astype(BF16)
        bsc = bs_ref.at[ci]
        bsc[...] = _dot(tri_ref[...], hi) + _dot(tri_ref[...], lo)
        for hh in range(HG_HEADS):
            hsl = slice(hh * HG_DK, (hh + 1) * HG_DK)
            b = bsc[:, hsl]
            btot = b[0:1] if reverse else b[c - 1:c]
            q = q_ref[0, rows, hsl]
            k = k_ref[0, rows, hsl]
            ix = (0, rows, hsl)
            v = v_ref[ix]
            st = s_ref[hh]
            att = jnp.where(lvl == 0, _dot_nt(q, k), 0.0)
            for l in range(1, N_LEVELS + 1):
                f = e(-jnp.abs(b - _level_ref(bsc, hsl, l, reverse)))
                att = jnp.where(lvl == l, _dot_nt(q * f, k * f), att)
            o = _dot_nt(q * e(b), st.astype(BF16)) + _dot(att.astype(BF16), v)
            s_ref[hh] = st * jnp.exp2(btot) + _dot_tn(v, k * e(btot - b))
            o_ref[ix] = o.astype(o_ref.dtype)


def _hgrn(q, v, lf_f, k_f, lf_b, k_b, nct):
    b_, lt, w = q.shape
    nt = lt // TM
    rtile = lambda i: jnp.where(i < nct, nct - 1 - i, nt - 1 - (i - nct))
    rowf = pl.BlockSpec((1, TM, w), lambda b, i: (b, i, 0))
    rowb = pl.BlockSpec((1, TM, w), lambda b, i: (b, rtile(i), 0))
    full = lambda a: pl.BlockSpec(a.shape, lambda b, i: (0,) * a.ndim)
    trif, lvlf = _chunk_tables(False)
    trib, lvlb = _chunk_tables(True)
    o_sh = jax.ShapeDtypeStruct((b_, lt, w), BF16)
    state = [pltpu.VMEM((HG_HEADS, HG_DK, HG_DK), F32)] * 2
    return pl.pallas_call(
        _hgrn_kernel,
        grid=(b_, nt),
        in_specs=[rowf] * 4 + [rowb] * 4 + [full(trif), full(lvlf), full(trib), full(lvlb)],
        out_specs=[rowf, rowb],
        out_shape=[o_sh, o_sh],
        scratch_shapes=state + [pltpu.VMEM((TM // CHUNK, CHUNK, w), F32)] * 2,
        compiler_params=_cparams(2),
        name="hgrn_scan",
    )(q, v, lf_f, k_f, q, v, lf_b, k_b, trif, lvlf, trib, lvlb)


KV_CHUNK = 256


def _attn_kernel(q_ref, k_ref, v_ref, o_ref, *, n_ctx, nct):
    i = pl.program_id(2)
    lane = lax.broadcasted_iota(jnp.int32, (TM, LANE), 1)

    def run(nk):
        outs = []
        for hh in range(2):
            sl = slice(hh * LANE, (hh + 1) * LANE)
            q = q_ref[0, :, sl]
            m = acc = None
            edges = [0] + list(range(n_ctx, nk + 1, KV_CHUNK))
            for c in range(len(edges) - 1):
                ks = slice(edges[c], edges[c + 1])
                s = _dot(q, k_ref[0, sl, ks])
                mc = jnp.max(s, axis=-1, keepdims=True)
                if c == 0:
                    m = mc
                    acc = _dot(jnp.exp2(s - m).astype(BF16), v_ref[0, ks, sl])
                else:
                    m_new = jnp.maximum(m, mc)
                    acc = acc * jnp.exp2(m - m_new) + _dot(jnp.exp2(s - m_new).astype(BF16), v_ref[0, ks, sl])
                    m = m_new
            den = acc[:, MLA_V:MLA_V + 1] if hh == 0 else acc[:, 0:1]
            outs.append(acc / den)
        o_ref[0] = jnp.where(lane < MLA_V, outs[0], outs[1]).astype(o_ref.dtype)

    @pl.when(i < nct)
    def _():
        run(n_ctx)

    @pl.when(i >= nct)
    def _():
        run(k_ref.shape[2])


def _attention(qq, kk, vv, n_ctx):
    b_, lt, _ = qq.shape
    nt = lt // TM
    npair = MLA_HEADS // 2
    return pl.pallas_call(
        functools.partial(_attn_kernel, n_ctx=n_ctx, nct=n_ctx // TM),
        grid=(b_, npair, nt),
        in_specs=[pl.BlockSpec((1, TM, 2 * LANE), lambda b, j, i: (b, i, j)),
                  pl.BlockSpec((1, 2 * LANE, lt), lambda b, j, i: (b, j, 0)),
                  pl.BlockSpec((1, lt, 2 * LANE), lambda b, j, i: (b, 0, j))],
        out_specs=pl.BlockSpec((1, TM, LANE), lambda b, j, i: (b, i, j)),
        out_shape=jax.ShapeDtypeStruct((b_, lt, npair * LANE), BF16),
        compiler_params=_cparams(3),
        name="mla_attention",
    )(qq, kk, vv)


def _route(sel):
    r = lambda x, e: x[e:e + 1]
    best = None
    for g in range(N_GROUPS):
        u = [r(sel, EPG * g + j) for j in range(EPG)]
        gs = None
        for a in range(EPG):
            for b in range(a + 1, EPG):
                pr = u[a] + u[b]
                gs = pr if gs is None else jnp.maximum(gs, pr)
        if best is None:
            best, gi = gs, jnp.zeros_like(gs, dtype=jnp.int32)
        else:
            upd = gs > best
            best = jnp.where(upd, gs, best)
            gi = jnp.where(upd, g, gi)

    def pick(x, j):
        out = r(x, j)
        for g in range(1, N_GROUPS):
            out = jnp.where(gi == g, r(x, EPG * g + j), out)
        return out

    u = [pick(sel, j) for j in range(EPG)]

    def argmax4(vals):
        bv, bi = vals[0], jnp.zeros_like(gi)
        for j in range(1, EPG):
            upd = vals[j] > bv
            bv = jnp.where(upd, vals[j], bv)
            bi = jnp.where(upd, j, bi)
        return bi

    l1 = argmax4(u)
    l2 = argmax4([jnp.where(l1 == j, -jnp.inf, u[j]) for j in range(EPG)])
    lo = jnp.minimum(l1, l2)
    hi = jnp.maximum(l1, l2)
    pair = jnp.where(lo == 0, jnp.where(hi == 1, 0, hi), jnp.where(lo == 1, jnp.where(hi == 2, 1, 4), 5))
    return (gi * 6 + pair).astype(F32)


CLS_ROWS = 32


def _merge_kernel(of_ref, ob_ref, g_ref, gain_ref, m_ref, gas_ref, gbs_ref, z_ref, mod_ref, wpa_ref, wpb_ref,
                  wo_ref, g2_ref, wr_ref, rb_ref, triu_ref, zn_o, ht_o, route_o, cnt_o, run_ref):
    @pl.when((pl.program_id(0) == 0) & (pl.program_id(1) == 0))
    def _():
        run_ref[...] = jnp.zeros_like(run_ref)

    mod = mod_ref[0, 0]
    heads = []
    for hh in range(HG_HEADS):
        hsl = slice(hh * HG_DK, (hh + 1) * HG_DK)
        o = of_ref[0, :, hsl].astype(F32) + ob_ref[0, :, hsl].astype(F32)
        g = g_ref[0, :, hsl].astype(F32)
        heads.append(_rms(o, gain_ref[:, hsl]) * (g * jax.nn.sigmoid(g)))
    pa = _dot(jnp.concatenate(heads, axis=1).astype(BF16), wpa_ref[...])
    pb = _dot(m_ref[0].astype(BF16), wpb_ref[...])
    mixed = gas_ref[0] * pa + gbs_ref[0] * pb
    zn = z_ref[0] + mod[2:3] * _dot(mixed.astype(BF16), wo_ref[...])
    zn_o[0] = zn
    h2 = _rms(zn, g2_ref[...]) * (1.0 + mod[4:5]) + mod[3:4]
    _tiles_store(ht_o, h2)
    logits = _dot_f32(h2, wr_ref[...], passes=3)
    sc = jax.nn.sigmoid(logits.T[:N_EXPERTS])
    cls = _route(sc + rb_ref[...])
    onehot = (lax.broadcasted_iota(jnp.int32, (CLS_ROWS, TM), 0) == cls.astype(jnp.int32))
    pref = _dot(onehot.astype(BF16), triu_ref[...])
    rank = jnp.sum(jnp.where(onehot, pref + run_ref[:, 0:1], 0.0), axis=0, keepdims=True) - 1.0
    run_ref[...] = run_ref[...] + pref[:, TM - 1:TM]
    cnt_o[...] = run_ref[...]
    route_o[0] = jnp.concatenate([cls, rank] + [jnp.zeros_like(cls)] * (SUB - 2), axis=0)


def _merge(o_f, o_b, g, gain, m, gas, gbs, z, modl, wpa, wpb, wo, g2, wr, rbb, nct):
    b_, lt, d = z.shape
    nt = lt // TM
    row = lambda w_: pl.BlockSpec((1, TM, w_), lambda b, i: (b, i, 0))
    full = lambda x: pl.BlockSpec(x.shape, lambda b, i: (0,) * x.ndim)
    triu = jnp.asarray(np.triu(np.ones((TM, TM), np.float32)), dtype=BF16)
    return pl.pallas_call(
        _merge_kernel,
        grid=(b_, nt),
        in_specs=[row(512), row(512), row(512), full(gain), row(512), row(d), row(d), row(d),
                  pl.BlockSpec((1, 1, SUB, d), lambda b, i: (b, jnp.minimum(i // nct, 1), 0, 0)),
                  full(wpa), full(wpb), full(wo), full(g2), full(wr), full(rbb), full(triu)],
        out_specs=[row(d), pl.BlockSpec((TM * SUB, LANE), lambda b, i: (b * nt + i, 0)),
                   pl.BlockSpec((1, SUB, TM), lambda b, i: (b, 0, i)),
                   pl.BlockSpec((CLS_ROWS, LANE), lambda b, i: (0, 0))],
        out_shape=[jax.ShapeDtypeStruct((b_, lt, d), F32),
                   jax.ShapeDtypeStruct((b_ * lt * SUB, LANE), F32),
                   jax.ShapeDtypeStruct((b_, SUB, lt), F32),
                   jax.ShapeDtypeStruct((CLS_ROWS, LANE), F32)],
        scratch_shapes=[pltpu.VMEM((CLS_ROWS, LANE), F32)],
        compiler_params=_cparams(2),
        name="merge_route",
    )(o_f, o_b, g, gain, m, gas, gbs, z, modl, wpa, wpb, wo, g2, wr, rbb, triu)


def _tiles_store(ref, x, lead=()):
    n = x.shape[0]
    for j in range(SUB):
        ref[lead + (pl.ds(j, n, stride=SUB), slice(None))] = x[:, j * LANE:(j + 1) * LANE]


def _tiles_load(ref, n, lead=()):
    return jnp.concatenate([ref[lead + (pl.ds(j, n, stride=SUB), slice(None))] for j in range(SUB)], axis=1)


def _tile_copy(src_hbm, idx_ref, buf, sem, slot, r):
    return pltpu.make_async_copy(src_hbm.at[pl.ds(pl.multiple_of(idx_ref[0, 0, r] * SUB, SUB), SUB)],
                                 buf.at[slot, pl.ds(pl.multiple_of(r * SUB, SUB), SUB)], sem.at[slot])


GATHER_UNROLL = 8


def _gather_start(src_hbm, idx_ref, buf, sem, slot, n):
    def body(r8, c):
        for u in range(GATHER_UNROLL):
            _tile_copy(src_hbm, idx_ref, buf, sem, slot, r8 * GATHER_UNROLL + u).start(priority=u % 2)
        return c
    lax.fori_loop(0, n // GATHER_UNROLL, body, 0)


def _gather_wait(src_hbm, buf, sem, slot):
    pltpu.make_async_copy(src_hbm.at[pl.ds(0, buf.shape[1])], buf.at[slot], sem.at[slot]).wait()


def _sort_kernel(pos_ref, ht_ref, init_hbm, hs_hbm, sem):
    del init_hbm

    def body(r8, c):
        for u in range(GATHER_UNROLL):
            r = r8 * GATHER_UNROLL + u
            pltpu.make_async_copy(
                ht_ref.at[pl.ds(pl.multiple_of(r * SUB, SUB), SUB)],
                hs_hbm.at[pl.ds(pl.multiple_of(pos_ref[0, 0, r] * SUB, SUB), SUB)], sem.at[0]).start(priority=u % 2)
        return c
    lax.fori_loop(0, TM // GATHER_UNROLL, body, 0)
    pltpu.make_async_copy(ht_ref, hs_hbm.at[pl.ds(0, TM * SUB)], sem.at[0]).wait()


def _sort_rows(ht, pos, n_sorted, init=None):
    nstep = pos.shape[0]
    if init is None:
        init = jnp.zeros((n_sorted * SUB, LANE), F32)
    return pl.pallas_call(
        _sort_kernel,
        grid=(nstep,),
        in_specs=[pl.BlockSpec((1, 1, TM), lambda t: (t, 0, 0), memory_space=pltpu.SMEM),
                  pl.BlockSpec((TM * SUB, LANE), lambda t: (t, 0)), pl.BlockSpec(memory_space=pl.ANY)],
        out_specs=pl.BlockSpec(memory_space=pl.ANY),
        out_shape=jax.ShapeDtypeStruct(init.shape, F32),
        scratch_shapes=[pltpu.SemaphoreType.DMA((1,))],
        input_output_aliases={2: 0},
        compiler_params=_cparams(1),
        name="class_sort",
    )(pos, ht, init)


def _moe_kernel(ea_ref, eb_ref, nu_ref, hs_ref, wrt_ref,
                wga_ref, wua_ref, wda_ref, wgb_ref, wub_ref, wdb_ref, y_ref, gu_a, dn_a, gu_b, dn_b):
    j = pl.program_id(0)
    jp = jnp.maximum(j - 1, 0)

    @pl.when((j == 0) | (ea_ref[j] != ea_ref[jp]))
    def _():
        gu_a[0] = wga_ref[0].astype(BF16)
        gu_a[1] = wua_ref[0].astype(BF16)
        dn_a[...] = wda_ref[0].astype(BF16)

    @pl.when((j == 0) | (eb_ref[j] != eb_ref[jp]))
    def _():
        gu_b[0] = wgb_ref[0].astype(BF16)
        gu_b[1] = wub_ref[0].astype(BF16)
        dn_b[...] = wdb_ref[0].astype(BF16)

    @pl.when(j < nu_ref[0])
    def _():
        x32 = _tiles_load(hs_ref, MOE_BM)
        x = x32.astype(BF16)

        def ffn(gu, dn, e):
            sc = jax.nn.sigmoid(jnp.sum(x32 * wrt_ref[pl.ds(e, 1), :], axis=-1, keepdims=True))
            act = jax.nn.silu(_dot(x, gu[0])) * _dot(x, gu[1])
            return sc, _dot(act.astype(BF16), dn[...])

        s_a, y_a = ffn(gu_a, dn_a, ea_ref[j])
        s_b, y_b = ffn(gu_b, dn_b, eb_ref[j])
        inv = 1.0 / (s_a + s_b)
        _tiles_store(y_ref, (s_a * inv) * y_a + (s_b * inv) * y_b)

    @pl.when(j >= nu_ref[0])
    def _():
        y_ref[...] = jnp.zeros_like(y_ref)


def _moe(hs, wrt, blk_ea, blk_eb, n_used, wg, wu, wd, layer):
    nblk = blk_ea.shape[0]
    d, de = wg.shape[2:]
    tiles = pl.BlockSpec((MOE_BM * SUB, LANE), lambda j, ea, eb, nu: (j, 0))
    wspec = lambda shp, which: pl.BlockSpec(
        (None, 1) + shp, (lambda j, ea, eb, nu: (layer, ea[j], 0, 0)) if which == 0
        else (lambda j, ea, eb, nu: (layer, eb[j], 0, 0)))
    return pl.pallas_call(
        _moe_kernel,
        grid_spec=pltpu.PrefetchScalarGridSpec(
            num_scalar_prefetch=3,
            grid=(nblk,),
            in_specs=[tiles,
                      pl.BlockSpec(wrt.shape, lambda j, ea, eb, nu: (0, 0)),
                      wspec((d, de), 0), wspec((d, de), 0), wspec((de, d), 0),
                      wspec((d, de), 1), wspec((d, de), 1), wspec((de, d), 1)],
            out_specs=tiles,
            scratch_shapes=[pltpu.VMEM((2, d, de), BF16), pltpu.VMEM((de, d), BF16)] * 2),
        out_shape=jax.ShapeDtypeStruct(hs.shape, F32),
        compiler_params=_cparams(1),
        name="moe_ffn",
    )(blk_ea, blk_eb, n_used, hs, wrt, wg, wu, wd, wg, wu, wd)


def _post_kernel(*refs, final):
    if final:
        idx_ref, idxn_ref, y_hbm, zn_ref, mod_ref, fg_ref, o_ref, buf, sem = refs
    else:
        idx_ref, idxn_ref, y_hbm, zn_ref, mod_ref, o_ref, buf, sem = refs
    t = pl.program_id(0)
    nstep = pl.num_programs(0)
    slot = t % 2

    @pl.when(t == 0)
    def _():
        _gather_start(y_hbm, idx_ref, buf, sem, 0, TM)

    @pl.when(t + 1 < nstep)
    def _():
        _gather_start(y_hbm, idxn_ref, buf, sem, 1 - slot, TM)

    _gather_wait(y_hbm, buf, sem, slot)
    z = zn_ref[0] + mod_ref[0, 0][5:6] * _tiles_load(buf, TM, (slot,))
    if final:
        z = _rms(z, fg_ref[...])
    o_ref[0] = z


def _post(y, pos, zn, modl, nct, final_g=None):
    b_, lt, d = zn.shape
    nt = lt // TM
    final = final_g is not None
    t0 = nct if final else 0
    ntl = nt - t0
    nstep = b_ * ntl
    idx_spec = lambda f: pl.BlockSpec((1, 1, TM), f, memory_space=pltpu.SMEM)
    ins = [pos, pos, y, zn, modl]
    specs = [idx_spec(lambda t: (t, 0, 0)),
             idx_spec(lambda t: (jnp.minimum(t + 1, nstep - 1), 0, 0)),
             pl.BlockSpec(memory_space=pl.ANY),
             pl.BlockSpec((1, TM, d), lambda t: (t // ntl, t % ntl + t0, 0)),
             pl.BlockSpec((1, 1, SUB, d), lambda t: (t // ntl, jnp.minimum((t % ntl + t0) // nct, 1), 0, 0))]
    if final:
        ins.append(final_g)
        specs.append(pl.BlockSpec(final_g.shape, lambda t: (0, 0)))
    return pl.pallas_call(
        functools.partial(_post_kernel, final=final),
        grid=(nstep,),
        in_specs=specs,
        out_specs=pl.BlockSpec((1, TM, d), lambda t: (t // ntl, t % ntl, 0)),
        out_shape=jax.ShapeDtypeStruct((b_, ntl * TM, d), F32),
        scratch_shapes=[pltpu.VMEM((2, TM * SUB, LANE), F32), pltpu.SemaphoreType.DMA((2,))],
        compiler_params=_cparams(1),
        name="unsort_residual",
    )(*ins)


_PAIR_SLOTS = [(0, 1), (2, 1), (2, 0), (3, 0), (3, 1), (3, 2)]
_CLS_EA = np.array([EPG * g + p[0] for g in range(N_GROUPS) for p in _PAIR_SLOTS], np.int32)
_CLS_EB = np.array([EPG * g + p[1] for g in range(N_GROUPS) for p in _PAIR_SLOTS], np.int32)


def _sort_plan(cls, rank, counts):
    n = cls.shape[0]
    nblk = n // MOE_BM + N_CLASSES
    padded = (counts + MOE_BM - 1) // MOE_BM * MOE_BM
    pad_end = jnp.cumsum(padded)
    pos = (pad_end - padded)[cls] + rank
    starts = jnp.arange(nblk, dtype=jnp.int32)[:, None] * MOE_BM
    blk_cls = jnp.minimum(jnp.sum((pad_end[None, :] <= starts).astype(jnp.int32), axis=1), N_CLASSES - 1)
    n_used = (pad_end[-1] // MOE_BM).astype(jnp.int32).reshape(1)
    return pos, jnp.asarray(_CLS_EA)[blk_cls], jnp.asarray(_CLS_EB)[blk_cls], n_used


def _rot_cols(w):
    return jnp.concatenate([-w[:, 8:16], w[:, 0:8], -w[:, 24:32], w[:, 16:24]], axis=1)


def _prep_w_in(w):
    kpe = w[:, 3200:3232]
    pad = jnp.zeros((w.shape[0], C_END - C_KPE - 2 * MLA_ROPE), w.dtype)
    return jnp.concatenate([w[:, :3200], w[:, 3232:], kpe, _rot_cols(kpe), pad], axis=1).astype(BF16)


def _prep_mla(w_uq, w_ukv):
    odd = (jnp.arange(MLA_HEADS) % 2 == 1)[None, :, None]
    qh = w_uq.reshape(MLA_Q_RANK, MLA_HEADS, MLA_NOPE + MLA_ROPE)
    nope, pe = qh[:, :, :MLA_NOPE], qh[:, :, MLA_NOPE:]
    pe_sw = _rot_cols(pe.reshape(MLA_Q_RANK * MLA_HEADS, MLA_ROPE)).reshape(MLA_Q_RANK, MLA_HEADS, MLA_ROPE)
    wq = jnp.where(odd, jnp.concatenate([nope, pe, pe_sw], axis=2), jnp.concatenate([pe, pe_sw, nope], axis=2))
    kvh = w_ukv.reshape(MLA_KV_RANK, MLA_HEADS, MLA_NOPE + MLA_V)
    kn, vh = kvh[:, :, :MLA_NOPE], kvh[:, :, MLA_NOPE:]
    wkv = jnp.where(odd, jnp.concatenate([kn, vh], axis=2), jnp.concatenate([vh, kn], axis=2))
    return (wq.reshape(MLA_Q_RANK, MLA_HEADS * LANE).astype(BF16),
            wkv.reshape(MLA_KV_RANK, MLA_HEADS * LANE).astype(BF16))


def _rope_tables(n_ctx, n_lat):
    rows = n_lat // GRID_W
    rowp = jnp.broadcast_to(jnp.arange(rows, dtype=F32)[:, None], (rows, GRID_W)).reshape(-1)
    colp = jnp.broadcast_to(jnp.arange(GRID_W, dtype=F32)[None, :], (rows, GRID_W)).reshape(-1)
    axis_dim = MLA_ROPE // 2
    inv_freq = ROPE_THETA ** (-jnp.arange(0, axis_dim, 2, dtype=F32) / axis_dim)
    ar, ac = rowp[:, None] * inv_freq, colp[:, None] * inv_freq
    cos32 = jnp.concatenate([jnp.cos(ar), jnp.cos(ar), jnp.cos(ac), jnp.cos(ac)], axis=1)
    sin32 = jnp.concatenate([jnp.sin(ar), jnp.sin(ar), jnp.sin(ac), jnp.sin(ac)], axis=1)
    cos32 = jnp.concatenate([jnp.ones((n_ctx, MLA_ROPE), F32), cos32], axis=0)
    sin32 = jnp.concatenate([jnp.zeros((n_ctx, MLA_ROPE), F32), sin32], axis=0)
    lt = n_ctx + n_lat
    one, zero = jnp.ones((lt, MLA_NOPE), F32), jnp.zeros((lt, MLA_NOPE), F32)
    z32 = jnp.zeros((lt, MLA_ROPE), F32)
    sc = MLA_SCALE * LOG2E
    cqe = jnp.concatenate([cos32, z32, one], axis=1) * sc
    sqe = jnp.concatenate([sin32, z32, zero], axis=1) * sc
    cqo = jnp.concatenate([one, cos32, z32], axis=1) * sc
    sqo = jnp.concatenate([zero, sin32, z32], axis=1) * sc
    kc = jnp.concatenate([cos32, sin32, zero], axis=1)
    return cqe, sqe, cqo, sqo, kc


def _lower_bounds(gamma):
    p = jnp.cumsum(jax.nn.softmax(gamma.astype(F32), axis=0), axis=0)
    return p - p[0:1]


def kernel(x, c, ctx, c_ctx, w_mod, b_mod, norm1_g, norm2_g, w_in, gamma_fwd, gamma_bwd, hg_norm_g,
           q_norm_g, kv_norm_g, w_uq, w_ukv, w_pa, w_pb, w_o, w_router, router_bias,
           w_gate_e, w_up_e, w_down_e, final_g):
    b_, n_lat, d = x.shape
    n_ctx = ctx.shape[1]
    depth = w_mod.shape[0]
    assert n_ctx % TM == 0 and n_lat % TM == 0 and n_lat % GRID_W == 0 and b_ + 1 <= SUB
    nct = n_ctx // TM
    lt = n_ctx + n_lat
    nt = lt // TM

    cc = jnp.concatenate([c, c_ctx[None, :], jnp.zeros((SUB - b_ - 1, d), F32)], axis=0)
    mod = _modulation(cc, w_mod, b_mod).reshape(depth, SUB, 6, d)
    mod_x = mod[:, :b_]
    mod_c = jnp.broadcast_to(mod[:, b_:b_ + 1], mod_x.shape)
    mod_t = jnp.stack([mod_c, mod_x], axis=2)
    mod_t = jnp.concatenate([mod_t, jnp.zeros((depth, b_, 2, SUB - 6, d), F32)], axis=3)

    cqe, sqe, cqo, sqo, kc = _rope_tables(n_ctx, n_lat)
    lbf, lbb = _lower_bounds(gamma_fwd), _lower_bounds(gamma_bwd)
    wr = jnp.concatenate([w_router, jnp.zeros((d, LANE - N_EXPERTS), F32)], axis=1)
    rbb = jnp.broadcast_to(router_bias.astype(F32)[:, None], (N_EXPERTS, TM))

    z = jnp.concatenate([ctx, x], axis=1)
    out = hs = None
    for l in range(depth):
        last = l == depth - 1
        wq, wkv = _prep_mla(w_uq[l], w_ukv[l])
        (q, v, lff, kf, lfb, kb, g, gas, gbs, qq, kk, vv) = _inproj(
            z, mod_t[l], norm1_g[l][None], _prep_w_in(w_in[l]), lbf[l][None], lbb[l][None],
            q_norm_g[l][None], kv_norm_g[l][None], wq, wkv, cqe, sqe, cqo, sqo, kc, nct)
        o_f, o_b = _hgrn(q, v, lff, kf, lfb, kb, nct)
        m = _attention(qq, kk, vv, n_ctx)
        zn, ht, route, cnt = _merge(o_f, o_b, g, hg_norm_g[l][None], m, gas, gbs, z, mod_t[l],
                                    w_pa[l].astype(BF16), w_pb[l].astype(BF16), w_o[l].astype(BF16),
                                    norm2_g[l][None], wr, rbb, nct)
        route = route[:, 0:2, :].astype(jnp.int32)
        pos, blk_ea, blk_eb, n_used = _sort_plan(route[:, 0].reshape(-1), route[:, 1].reshape(-1),
                                                 cnt[:N_CLASSES, 0].astype(jnp.int32))
        pos = pos.reshape(b_, nt, 1, TM)
        hs = _sort_rows(ht, pos.reshape(-1, 1, TM), blk_ea.shape[0] * MOE_BM, hs)
        y = _moe(hs, w_router.T, blk_ea, blk_eb, n_used, w_gate_e, w_up_e, w_down_e, l)
        if last:
            out = _post(y, pos[:, nct:].reshape(-1, 1, TM), zn, mod_t[l], nct, final_g[None])
        else:
            z = _post(y, pos.reshape(-1, 1, TM), zn, mod_t[l], nct)
    return out
```

```python
import functools

import numpy as np
import jax
import jax.numpy as jnp
from jax import lax
from jax.experimental import pallas as pl
from jax.experimental.pallas import tpu as pltpu

F32 = jnp.float32
BF16 = jnp.bfloat16

EPS = 1e-6
GRID_W = 64
ROPE_THETA = 10000.0
HG_HEADS = 4
HG_DK = 128
HG_WIDTH = 512
MLA_HEADS = 8
MLA_NOPE = 64
MLA_ROPE = 32
MLA_V = 64
MLA_Q_RANK = 384
MLA_KV_RANK = 256
MLA_SCALE = (MLA_NOPE + MLA_ROPE) ** -0.5
LOG2E = 1.4426950408889634
N_EXPERTS = 16
N_GROUPS = 4
EPG = 4
N_CLASSES = N_GROUPS * 6

LANE = 128
SUB = 8
TM = 256
CHUNK = 128
MOE_BM = 256
VMEM_LIMIT = 56 * 1024 * 1024

C_Q, C_I, C_FF, C_FB, C_G, C_DQ, C_DKV, C_GA, C_GB, C_KPE, C_END = (
    0, 512, 1024, 1536, 2048, 2560, 2944, 3200, 4224, 5248, 5376)


def _cparams(n_axes):
    return pltpu.CompilerParams(dimension_semantics=("arbitrary",) * n_axes,
                                vmem_limit_bytes=VMEM_LIMIT)


def _rms(x, g):
    y = x * lax.rsqrt(jnp.mean(x * x, axis=-1, keepdims=True) + EPS)
    return y * g


def _dot(a, b):
    return jnp.dot(a, b, preferred_element_type=F32)


def _dot_nt(a, b):
    return lax.dot_general(a, b, (((1,), (1,)), ((), ())), preferred_element_type=F32)


def _dot_tn(a, b):
    return lax.dot_general(a, b, (((0,), (0,)), ((), ())), preferred_element_type=F32)


def _split3(x):
    hi = x.astype(BF16)
    r1 = x - hi.astype(F32)
    mid = r1.astype(BF16)
    lo = (r1 - mid.astype(F32)).astype(BF16)
    return hi, mid, lo


def _dot_f32(a, b, passes=6):
    a0, a1, a2 = _split3(a)
    b0, b1, b2 = _split3(b)
    out = _dot(a0, b0) + (_dot(a0, b1) + _dot(a1, b0))
    if passes == 6:
        out = out + (_dot(a0, b2) + _dot(a1, b1) + _dot(a2, b0))
    return out


def _mod_kernel(c_ref, w_ref, b_ref, o_ref):
    c = c_ref[...]
    s = c * jax.nn.sigmoid(c)
    o_ref[0] = _dot_f32(s, w_ref[0], passes=3) + b_ref[0]


def _modulation(cc, w_mod, b_mod):
    depth, d, n6 = w_mod.shape
    nb = 1536
    return pl.pallas_call(
        _mod_kernel,
        grid=(depth, n6 // nb),
        in_specs=[pl.BlockSpec((SUB, d), lambda l, j: (0, 0)),
                  pl.BlockSpec((1, d, nb), lambda l, j: (l, 0, j)),
                  pl.BlockSpec((1, 1, nb), lambda l, j: (l, 0, j))],
        out_specs=pl.BlockSpec((1, SUB, nb), lambda l, j: (l, 0, j)),
        out_shape=jax.ShapeDtypeStruct((depth, SUB, n6), F32),
        compiler_params=_cparams(2),
        name="modulation",
    )(cc, w_mod, b_mod.reshape(depth, 1, n6))


def _z_inputs(z, nct):
    if isinstance(z, tuple):
        ctx, x = z
        b_, n_lat, d = x.shape
        specs = [pl.BlockSpec((1, TM, d), lambda b, i: (b, jnp.minimum(i, nct - 1), 0)),
                 pl.BlockSpec((1, TM, d), lambda b, i: (b, jnp.maximum(i - nct, 0), 0))]
        return [ctx, x], specs, b_, ctx.shape[1] + n_lat, d
    b_, lt, d = z.shape
    return [z], [pl.BlockSpec((1, TM, d), lambda b, i: (b, i, 0))], b_, lt, d


def _z_tile(z_refs, nct):
    if len(z_refs) == 2:
        return jnp.where(pl.program_id(1) < nct, z_refs[0][0], z_refs[1][0])
    return z_refs[0][0]


def _inproj_kernel(*refs, nz, nct):
    z_refs = refs[:nz]
    (mod_ref, g1_ref, w_ref, lbf_ref, lbb_ref, qg_ref, kvg_ref,
     wq_ref, wkv_ref, cqe_ref, sqe_ref, cqo_ref, sqo_ref, kc_ref,
     q_o, v_o, lff_o, kf_o, lfb_o, kb_o, g_o, gas_o, gbs_o, qq_o, kk_o, vv_o) = refs[nz:]
    mod = mod_ref[0, 0]
    h = _rms(_z_tile(z_refs, nct), g1_ref[...]) * (1.0 + mod[1:2]) + mod[0:1]
    hb = h.astype(BF16)

    p = _dot(hb, w_ref[...])

    def seg(a, b):
        return p[:, a:b]

    q_o[0] = seg(C_Q, C_I).astype(BF16)
    v_o[0] = seg(C_I, C_FF).astype(BF16)
    for a, lb_ref, lf_o, k_o in ((C_FF, lbf_ref, lff_o, kf_o), (C_FB, lbb_ref, lfb_o, kb_o)):
        lb = lb_ref[...]
        f = lb + (1.0 - lb) * jax.nn.sigmoid(seg(a, a + HG_WIDTH))
        lf_o[0] = jnp.log(f) * LOG2E
        k_o[0] = (1.0 - f).astype(BF16)
    g_o[0] = seg(C_G, C_DQ).astype(BF16)
    gas_o[0] = jax.nn.sigmoid(seg(C_GA, C_GB)).astype(BF16)
    gbs_o[0] = jax.nn.sigmoid(seg(C_GB, C_KPE)).astype(BF16)

    lane = lax.broadcasted_iota(jnp.int32, (TM, LANE), 1)
    qn = _rms(seg(C_DQ, C_DKV), qg_ref[...]).astype(BF16)
    qa = _dot(qn, wq_ref[...])
    tabs = ((cqe_ref[...], sqe_ref[...]), (cqo_ref[...], sqo_ref[...]))
    for hh in range(MLA_HEADS):
        sl = slice(hh * LANE, (hh + 1) * LANE)
        cq, sq = tabs[hh % 2]
        blk = qa[:, sl]
        qq_o[0, :, sl] = (blk * cq + pltpu.roll(blk, LANE - MLA_ROPE, 1) * sq).astype(BF16)
    kvn = _rms(seg(C_DKV, C_GA), kvg_ref[...]).astype(BF16)
    kv = _dot(kvn, wkv_ref[...])
    kp = seg(C_KPE, C_END) * kc_ref[...]
    kpe_e = jnp.where(lane < MLA_ROPE, kp + pltpu.roll(kp, LANE - MLA_ROPE, 1), 0.0)
    kpe_o = pltpu.roll(kpe_e, MLA_NOPE, 1)
    one_e = jnp.where(lane == MLA_V, 1.0, 0.0)
    one_o = jnp.where(lane == 0, 1.0, 0.0)
    for hh in range(MLA_HEADS):
        sl = slice(hh * LANE, (hh + 1) * LANE)
        blk = kv[:, sl]
        if hh % 2 == 0:
            kk_o[0, sl, :] = jnp.where(lane >= MLA_V, blk, kpe_e).T.astype(BF16)
            vv_o[0, :, sl] = jnp.where(lane < MLA_V, blk, one_e).astype(BF16)
        else:
            kk_o[0, sl, :] = jnp.where(lane < MLA_NOPE, blk, kpe_o).T.astype(BF16)
            vv_o[0, :, sl] = jnp.where(lane >= MLA_NOPE, blk, one_o).astype(BF16)


def _inproj(z, modl, g1, w, lbf, lbb, qg, kvg, wq, wkv, cqe, sqe, cqo, sqo, kc, nct):
    z_in, z_specs, b_, lt, d = _z_inputs(z, nct)
    nt = lt // TM
    row = lambda w_: pl.BlockSpec((1, TM, w_), lambda b, i: (b, i, 0))
    full = lambda a: pl.BlockSpec(a.shape, lambda b, i: (0,) * a.ndim)
    tab = pl.BlockSpec((TM, LANE), lambda b, i: (i, 0))
    f32o = lambda w_: jax.ShapeDtypeStruct((b_, lt, w_), F32)
    bf16o = lambda w_: jax.ShapeDtypeStruct((b_, lt, w_), BF16)
    return pl.pallas_call(
        functools.partial(_inproj_kernel, nz=len(z_in), nct=nct),
        grid=(b_, nt),
        in_specs=z_specs + [
                  pl.BlockSpec((1, 1, SUB, d), lambda b, i: (b, jnp.minimum(i // nct, 1), 0, 0)),
                  full(g1), full(w), full(lbf), full(lbb), full(qg), full(kvg),
                  full(wq), full(wkv), tab, tab, tab, tab, tab],
        out_specs=[row(512)] * 7 + [row(1024)] * 3
        + [pl.BlockSpec((1, MLA_HEADS * LANE, TM), lambda b, i: (b, 0, i)), row(1024)],
        out_shape=[bf16o(512), bf16o(512), f32o(512), bf16o(512), f32o(512), bf16o(512), bf16o(512)]
        + [bf16o(1024)] * 3 + [jax.ShapeDtypeStruct((b_, MLA_HEADS * LANE, lt), BF16), bf16o(1024)],
        compiler_params=_cparams(2),
        name="inproj",
    )(*z_in, modl, g1, w, lbf, lbb, qg, kvg, wq, wkv, cqe, sqe, cqo, sqo, kc)


N_LEVELS = 7


def _chunk_tables(reverse):
    c = CHUNK
    t = np.arange(c)[:, None]
    u = np.arange(c)[None, :]
    tri = ((u >= t) if reverse else (u <= t)).astype(np.float32)
    lvl = np.full((c, c), N_LEVELS + 1, np.int32)
    lvl[t == u] = 0
    for l in range(N_LEVELS, 0, -1):
        m = 1 << l
        same = (t // m) == (u // m)
        lvl[same & ((u > t) if reverse else (u < t))] = l
    return jnp.asarray(tri, dtype=BF16), jnp.asarray(lvl)


def _level_ref(bs_ref, hsl, l, reverse):
    w = HG_DK
    sub = lax.broadcasted_iota(jnp.int32, (SUB, w), 0)
    off = 0 if reverse else -1
    m = 1 << l

    def row(r):
        return jnp.broadcast_to(bs_ref[r:r + 1, hsl], (SUB, w))

    pieces = []
    for j in range(CHUNK // SUB):
        base = j * SUB
        if m >= SUB:
            pieces.append(row((base // m) * m + m // 2 + off))
        else:
            p = row(base + m // 2 + off)
            for i in range(1, SUB // m):
                p = jnp.where(sub < i * m, p, row(base + i * m + m // 2 + off))
            pieces.append(p)
    return jnp.concatenate(pieces, axis=0)


def _hgrn_kernel(qf_ref, vf_ref, lff_ref, kf_ref, qb_ref, vb_ref, lfb_ref, kb_ref,
                 trif_ref, lvlf_ref, trib_ref, lvlb_ref, of_ref, ob_ref, sf_ref, sb_ref, bsf_ref, bsb_ref):
    @pl.when(pl.program_id(1) == 0)
    def _():
        sf_ref[...] = jnp.zeros_like(sf_ref)
        sb_ref[...] = jnp.zeros_like(sb_ref)

    _hgrn_tile(qf_ref, vf_ref, lff_ref, kf_ref, trif_ref, lvlf_ref, of_ref, sf_ref, bsf_ref, False)
    _hgrn_tile(qb_ref, vb_ref, lfb_ref, kb_ref, trib_ref, lvlb_ref, ob_ref, sb_ref, bsb_ref, True)


def _hgrn_tile(q_ref, v_ref, lf_ref, k_ref, tri_ref, lvl_ref, o_ref, s_ref, bs_ref, reverse):
    c = CHUNK
    nch = TM // c
    e = lambda x: jnp.exp2(x).astype(BF16)

    lvl = lvl_ref[...]
    for ci in range(nch):
        cidx = (nch - 1 - ci) if reverse else ci
        rows = slice(cidx * c, (cidx + 1) * c)
        lf = lf_ref[0, rows, :]
        hi = lf.astype(BF16)
        lo = (lf - hi.astype(F32)).astype(BF16)
        bsc = bs_ref.at[ci]
        bsc[...] = _dot(tri_ref[...], hi) + _dot(tri_ref[...], lo)
        for hh in range(HG_HEADS):
            hsl = slice(hh * HG_DK, (hh + 1) * HG_DK)
            b = bsc[:, hsl]
            btot = b[0:1] if reverse else b[c - 1:c]
            q = q_ref[0, rows, hsl]
            k = k_ref[0, rows, hsl]
            ix = (0, rows, hsl)
            v = v_ref[ix]
            st = s_ref[hh]
            att = jnp.where(lvl == 0, _dot_nt(q, k), 0.0)
            for l in range(1, N_LEVELS + 1):
                f = e(-jnp.abs(b - _level_ref(bsc, hsl, l, reverse)))
                att = jnp.where(lvl == l, _dot_nt(q * f, k * f), att)
            o = _dot_nt(q * e(b), st.astype(BF16)) + _dot(att.astype(BF16), v)
            s_ref[hh] = st * jnp.exp2(btot) + _dot_tn(v, k * e(btot - b))
            o_ref[ix] = o.astype(o_ref.dtype)


def _hgrn(q, v, lf_f, k_f, lf_b, k_b, nct):
    b_, lt, w = q.shape
    nt = lt // TM
    rtile = lambda i: jnp.where(i < nct, nct - 1 - i, nt - 1 - (i - nct))
    rowf = pl.BlockSpec((1, TM, w), lambda b, i: (b, i, 0))
    rowb = pl.BlockSpec((1, TM, w), lambda b, i: (b, rtile(i), 0))
    full = lambda a: pl.BlockSpec(a.shape, lambda b, i: (0,) * a.ndim)
    trif, lvlf = _chunk_tables(False)
    trib, lvlb = _chunk_tables(True)
    o_sh = jax.ShapeDtypeStruct((b_, lt, w), BF16)
    state = [pltpu.VMEM((HG_HEADS, HG_DK, HG_DK), F32)] * 2
    return pl.pallas_call(
        _hgrn_kernel,
        grid=(b_, nt),
        in_specs=[rowf] * 4 + [rowb] * 4 + [full(trif), full(lvlf), full(trib), full(lvlb)],
        out_specs=[rowf, rowb],
        out_shape=[o_sh, o_sh],
        scratch_shapes=state + [pltpu.VMEM((TM // CHUNK, CHUNK, w), F32)] * 2,
        compiler_params=_cparams(2),
        name="hgrn_scan",
    )(q, v, lf_f, k_f, q, v, lf_b, k_b, trif, lvlf, trib, lvlb)


KV_CHUNK = 256


def _attn_kernel(q_ref, k_ref, v_ref, o_ref, *, n_ctx, nct):
    i = pl.program_id(2)
    lane = lax.broadcasted_iota(jnp.int32, (TM, LANE), 1)

    def run(nk):
        outs = []
        for hh in range(2):
            sl = slice(hh * LANE, (hh + 1) * LANE)
            q = q_ref[0, :, sl]
            m = acc = None
            edges = [0] + list(range(n_ctx, nk + 1, KV_CHUNK))
            for c in range(len(edges) - 1):
                ks = slice(edges[c], edges[c + 1])
                s = _dot(q, k_ref[0, sl, ks])
                mc = jnp.max(s, axis=-1, keepdims=True)
                if c == 0:
                    m = mc
                    acc = _dot(jnp.exp2(s - m).astype(BF16), v_ref[0, ks, sl])
                else:
                    m_new = jnp.maximum(m, mc)
                    acc = acc * jnp.exp2(m - m_new) + _dot(jnp.exp2(s - m_new).astype(BF16), v_ref[0, ks, sl])
                    m = m_new
            den = acc[:, MLA_V:MLA_V + 1] if hh == 0 else acc[:, 0:1]
            outs.append(acc / den)
        o_ref[0] = jnp.where(lane < MLA_V, outs[0], outs[1]).astype(o_ref.dtype)

    @pl.when(i < nct)
    def _():
        run(n_ctx)

    @pl.when(i >= nct)
    def _():
        run(k_ref.shape[2])


def _attention(qq, kk, vv, n_ctx):
    b_, lt, _ = qq.shape
    nt = lt // TM
    npair = MLA_HEADS // 2
    return pl.pallas_call(
        functools.partial(_attn_kernel, n_ctx=n_ctx, nct=n_ctx // TM),
        grid=(b_, npair, nt),
        in_specs=[pl.BlockSpec((1, TM, 2 * LANE), lambda b, j, i: (b, i, j)),
                  pl.BlockSpec((1, 2 * LANE, lt), lambda b, j, i: (b, j, 0)),
                  pl.BlockSpec((1, lt, 2 * LANE), lambda b, j, i: (b, 0, j))],
        out_specs=pl.BlockSpec((1, TM, LANE), lambda b, j, i: (b, i, j)),
        out_shape=jax.ShapeDtypeStruct((b_, lt, npair * LANE), BF16),
        compiler_params=_cparams(3),
        name="mla_attention",
    )(qq, kk, vv)


def _route(sel):
    r = lambda x, e: x[e:e + 1]
    best = None
    for g in range(N_GROUPS):
        u = [r(sel, EPG * g + j) for j in range(EPG)]
        gs = None
        for a in range(EPG):
            for b in range(a + 1, EPG):
                pr = u[a] + u[b]
                gs = pr if gs is None else jnp.maximum(gs, pr)
        if best is None:
            best, gi = gs, jnp.zeros_like(gs, dtype=jnp.int32)
        else:
            upd = gs > best
            best = jnp.where(upd, gs, best)
            gi = jnp.where(upd, g, gi)

    def pick(x, j):
        out = r(x, j)
        for g in range(1, N_GROUPS):
            out = jnp.where(gi == g, r(x, EPG * g + j), out)
        return out

    u = [pick(sel, j) for j in range(EPG)]

    def argmax4(vals):
        bv, bi = vals[0], jnp.zeros_like(gi)
        for j in range(1, EPG):
            upd = vals[j] > bv
            bv = jnp.where(upd, vals[j], bv)
            bi = jnp.where(upd, j, bi)
        return bi

    l1 = argmax4(u)
    l2 = argmax4([jnp.where(l1 == j, -jnp.inf, u[j]) for j in range(EPG)])
    lo = jnp.minimum(l1, l2)
    hi = jnp.maximum(l1, l2)
    pair = jnp.where(lo == 0, jnp.where(hi == 1, 0, hi), jnp.where(lo == 1, jnp.where(hi == 2, 1, 4), 5))
    return (gi * 6 + pair).astype(F32)


CLS_ROWS = 32


def _merge_kernel(*refs, nz, nct):
    z_refs = refs[:nz]
    (of_ref, ob_ref, g_ref, gain_ref, m_ref, gas_ref, gbs_ref, mod_ref, wpa_ref, wpb_ref,
     wo_ref, g2_ref, wr_ref, rb_ref, triu_ref, zn_o, ht_o, route_o, cnt_o, run_ref) = refs[nz:]

    @pl.when((pl.program_id(0) == 0) & (pl.program_id(1) == 0))
    def _():
        run_ref[...] = jnp.zeros_like(run_ref)

    mod = mod_ref[0, 0]
    heads = []
    for hh in range(HG_HEADS):
        hsl = slice(hh * HG_DK, (hh + 1) * HG_DK)
        o = of_ref[0, :, hsl].astype(F32) + ob_ref[0, :, hsl].astype(F32)
        g = g_ref[0, :, hsl].astype(F32)
        heads.append(_rms(o, gain_ref[:, hsl]) * (g * jax.nn.sigmoid(g)))
    pa = _dot(jnp.concatenate(heads, axis=1).astype(BF16), wpa_ref[...])
    pb = _dot(m_ref[0].astype(BF16), wpb_ref[...])
    mixed = gas_ref[0] * pa + gbs_ref[0] * pb
    zn = _z_tile(z_refs, nct) + mod[2:3] * _dot(mixed.astype(BF16), wo_ref[...])
    zn_o[0] = zn
    h2 = _rms(zn, g2_ref[...]) * (1.0 + mod[4:5]) + mod[3:4]
    _tiles_store(ht_o, h2)
    logits = _dot_f32(h2, wr_ref[...], passes=3)
    sc = jax.nn.sigmoid(logits.T[:N_EXPERTS])
    cls = _route(sc + rb_ref[...])
    onehot = (lax.broadcasted_iota(jnp.int32, (CLS_ROWS, TM), 0) == cls.astype(jnp.int32))
    pref = _dot(onehot.astype(BF16), triu_ref[...])
    rank = jnp.sum(jnp.where(onehot, pref + run_ref[:, 0:1], 0.0), axis=0, keepdims=True) - 1.0
    run_ref[...] = run_ref[...] + pref[:, TM - 1:TM]
    cnt_o[...] = run_ref[...]
    route_o[0] = jnp.concatenate([cls, rank] + [jnp.zeros_like(cls)] * (SUB - 2), axis=0)


def _merge(o_f, o_b, g, gain, m, gas, gbs, z, modl, wpa, wpb, wo, g2, wr, rbb, nct):
    z_in, z_specs, b_, lt, d = _z_inputs(z, nct)
    nt = lt // TM
    row = lambda w_: pl.BlockSpec((1, TM, w_), lambda b, i: (b, i, 0))
    full = lambda x: pl.BlockSpec(x.shape, lambda b, i: (0,) * x.ndim)
    triu = jnp.asarray(np.triu(np.ones((TM, TM), np.float32)), dtype=BF16)
    return pl.pallas_call(
        functools.partial(_merge_kernel, nz=len(z_in), nct=nct),
        grid=(b_, nt),
        in_specs=z_specs + [row(512), row(512), row(512), full(gain), row(512), row(d), row(d),
                  pl.BlockSpec((1, 1, SUB, d), lambda b, i: (b, jnp.minimum(i // nct, 1), 0, 0)),
                  full(wpa), full(wpb), full(wo), full(g2), full(wr), full(rbb), full(triu)],
        out_specs=[row(d), pl.BlockSpec((TM * SUB, LANE), lambda b, i: (b * nt + i, 0)),
                   pl.BlockSpec((1, SUB, TM), lambda b, i: (b, 0, i)),
                   pl.BlockSpec((CLS_ROWS, LANE), lambda b, i: (0, 0))],
        out_shape=[jax.ShapeDtypeStruct((b_, lt, d), F32),
                   jax.ShapeDtypeStruct((b_ * lt * SUB, LANE), F32),
                   jax.ShapeDtypeStruct((b_, SUB, lt), F32),
                   jax.ShapeDtypeStruct((CLS_ROWS, LANE), F32)],
        scratch_shapes=[pltpu.VMEM((CLS_ROWS, LANE), F32)],
        compiler_params=_cparams(2),
        name="merge_route",
    )(*z_in, o_f, o_b, g, gain, m, gas, gbs, modl, wpa, wpb, wo, g2, wr, rbb, triu)


def _tiles_store(ref, x, lead=()):
    n = x.shape[0]
    for j in range(SUB):
        ref[lead + (pl.ds(j, n, stride=SUB), slice(None))] = x[:, j * LANE:(j + 1) * LANE]


def _tiles_load(ref, n, lead=()):
    return jnp.concatenate([ref[lead + (pl.ds(j, n, stride=SUB), slice(None))] for j in range(SUB)], axis=1)


def _tile_copy(src_hbm, idx_ref, buf, sem, slot, r):
    return pltpu.make_async_copy(src_hbm.at[pl.ds(pl.multiple_of(idx_ref[0, 0, r] * SUB, SUB), SUB)],
                                 buf.at[slot, pl.ds(pl.multiple_of(r * SUB, SUB), SUB)], sem.at[slot])


GATHER_UNROLL = 8


def _gather_start(src_hbm, idx_ref, buf, sem, slot, n):
    def body(r8, c):
        for u in range(GATHER_UNROLL):
            _tile_copy(src_hbm, idx_ref, buf, sem, slot, r8 * GATHER_UNROLL + u).start(priority=u % 2)
        return c
    lax.fori_loop(0, n // GATHER_UNROLL, body, 0)


def _gather_wait(src_hbm, buf, sem, slot):
    pltpu.make_async_copy(src_hbm.at[pl.ds(0, buf.shape[1])], buf.at[slot], sem.at[slot]).wait()


def _sort_kernel(pos_ref, ht_ref, init_hbm, hs_hbm, sem):
    del init_hbm

    def body(r8, c):
        for u in range(GATHER_UNROLL):
            r = r8 * GATHER_UNROLL + u
            pltpu.make_async_copy(
                ht_ref.at[pl.ds(pl.multiple_of(r * SUB, SUB), SUB)],
                hs_hbm.at[pl.ds(pl.multiple_of(pos_ref[0, 0, r] * SUB, SUB), SUB)], sem.at[0]).start(priority=u % 2)
        return c
    lax.fori_loop(0, TM // GATHER_UNROLL, body, 0)
    pltpu.make_async_copy(ht_ref, hs_hbm.at[pl.ds(0, TM * SUB)], sem.at[0]).wait()


def _sort_rows(ht, pos, n_sorted, init=None):
    nstep = pos.shape[0]
    if init is None:
        init = jnp.zeros((n_sorted * SUB, LANE), F32)
    return pl.pallas_call(
        _sort_kernel,
        grid=(nstep,),
        in_specs=[pl.BlockSpec((1, 1, TM), lambda t: (t, 0, 0), memory_space=pltpu.SMEM),
                  pl.BlockSpec((TM * SUB, LANE), lambda t: (t, 0)), pl.BlockSpec(memory_space=pl.ANY)],
        out_specs=pl.BlockSpec(memory_space=pl.ANY),
        out_shape=jax.ShapeDtypeStruct(init.shape, F32),
        scratch_shapes=[pltpu.SemaphoreType.DMA((1,))],
        input_output_aliases={2: 0},
        compiler_params=_cparams(1),
        name="class_sort",
    )(pos, ht, init)


def _moe_kernel(ea_ref, eb_ref, nu_ref, hs_ref, wrt_ref,
                wga_ref, wua_ref, wda_ref, wgb_ref, wub_ref, wdb_ref, y_ref, gu_a, dn_a, gu_b, dn_b):
    j = pl.program_id(0)
    jp = jnp.maximum(j - 1, 0)

    @pl.when((j == 0) | (ea_ref[j] != ea_ref[jp]))
    def _():
        gu_a[0] = wga_ref[0].astype(BF16)
        gu_a[1] = wua_ref[0].astype(BF16)
        dn_a[...] = wda_ref[0].astype(BF16)

    @pl.when((j == 0) | (eb_ref[j] != eb_ref[jp]))
    def _():
        gu_b[0] = wgb_ref[0].astype(BF16)
        gu_b[1] = wub_ref[0].astype(BF16)
        dn_b[...] = wdb_ref[0].astype(BF16)

    @pl.when(j < nu_ref[0])
    def _():
        x32 = _tiles_load(hs_ref, MOE_BM)
        x = x32.astype(BF16)

        def ffn(gu, dn, e):
            sc = jax.nn.sigmoid(jnp.sum(x32 * wrt_ref[pl.ds(e, 1), :], axis=-1, keepdims=True))
            act = jax.nn.silu(_dot(x, gu[0])) * _dot(x, gu[1])
            return sc, _dot(act.astype(BF16), dn[...])

        s_a, y_a = ffn(gu_a, dn_a, ea_ref[j])
        s_b, y_b = ffn(gu_b, dn_b, eb_ref[j])
        inv = 1.0 / (s_a + s_b)
        _tiles_store(y_ref, (s_a * inv) * y_a + (s_b * inv) * y_b)

    @pl.when(j >= nu_ref[0])
    def _():
        y_ref[...] = jnp.zeros_like(y_ref)


def _moe(hs, wrt, blk_ea, blk_eb, n_used, wg, wu, wd, layer):
    nblk = blk_ea.shape[0]
    d, de = wg.shape[2:]
    tiles = pl.BlockSpec((MOE_BM * SUB, LANE), lambda j, ea, eb, nu: (j, 0))
    wspec = lambda shp, which: pl.BlockSpec(
        (None, 1) + shp, (lambda j, ea, eb, nu: (layer, ea[j], 0, 0)) if which == 0
        else (lambda j, ea, eb, nu: (layer, eb[j], 0, 0)))
    return pl.pallas_call(
        _moe_kernel,
        grid_spec=pltpu.PrefetchScalarGridSpec(
            num_scalar_prefetch=3,
            grid=(nblk,),
            in_specs=[tiles,
                      pl.BlockSpec(wrt.shape, lambda j, ea, eb, nu: (0, 0)),
                      wspec((d, de), 0), wspec((d, de), 0), wspec((de, d), 0),
                      wspec((d, de), 1), wspec((d, de), 1), wspec((de, d), 1)],
            out_specs=tiles,
            scratch_shapes=[pltpu.VMEM((2, d, de), BF16), pltpu.VMEM((de, d), BF16)] * 2),
        out_shape=jax.ShapeDtypeStruct(hs.shape, F32),
        compiler_params=_cparams(1),
        name="moe_ffn",
    )(blk_ea, blk_eb, n_used, hs, wrt, wg, wu, wd, wg, wu, wd)


def _post_kernel(*refs, final):
    if final:
        idx_ref, idxn_ref, y_hbm, zn_ref, mod_ref, fg_ref, o_ref, buf, sem = refs
    else:
        idx_ref, idxn_ref, y_hbm, zn_ref, mod_ref, o_ref, buf, sem = refs
    t = pl.program_id(0)
    nstep = pl.num_programs(0)
    slot = t % 2

    @pl.when(t == 0)
    def _():
        _gather_start(y_hbm, idx_ref, buf, sem, 0, TM)

    @pl.when(t + 1 < nstep)
    def _():
        _gather_start(y_hbm, idxn_ref, buf, sem, 1 - slot, TM)

    _gather_wait(y_hbm, buf, sem, slot)
    z = zn_ref[0] + mod_ref[0, 0][5:6] * _tiles_load(buf, TM, (slot,))
    if final:
        z = _rms(z, fg_ref[...])
    o_ref[0] = z


def _post(y, pos, zn, modl, nct, final_g=None):
    b_, lt, d = zn.shape
    nt = lt // TM
    final = final_g is not None
    t0 = nct if final else 0
    ntl = nt - t0
    nstep = b_ * ntl
    idx_spec = lambda f: pl.BlockSpec((1, 1, TM), f, memory_space=pltpu.SMEM)
    ins = [pos, pos, y, zn, modl]
    specs = [idx_spec(lambda t: (t, 0, 0)),
             idx_spec(lambda t: (jnp.minimum(t + 1, nstep - 1), 0, 0)),
             pl.BlockSpec(memory_space=pl.ANY),
             pl.BlockSpec((1, TM, d), lambda t: (t // ntl, t % ntl + t0, 0)),
             pl.BlockSpec((1, 1, SUB, d), lambda t: (t // ntl, jnp.minimum((t % ntl + t0) // nct, 1), 0, 0))]
    if final:
        ins.append(final_g)
        specs.append(pl.BlockSpec(final_g.shape, lambda t: (0, 0)))
    return pl.pallas_call(
        functools.partial(_post_kernel, final=final),
        grid=(nstep,),
        in_specs=specs,
        out_specs=pl.BlockSpec((1, TM, d), lambda t: (t // ntl, t % ntl, 0)),
        out_shape=jax.ShapeDtypeStruct((b_, ntl * TM, d), F32),
        scratch_shapes=[pltpu.VMEM((2, TM * SUB, LANE), F32), pltpu.SemaphoreType.DMA((2,))],
        compiler_params=_cparams(1),
        name="unsort_residual",
    )(*ins)


_PAIR_SLOTS = [(0, 1), (2, 1), (2, 0), (3, 0), (3, 1), (3, 2)]
_CLS_EA = np.array([EPG * g + p[0] for g in range(N_GROUPS) for p in _PAIR_SLOTS], np.int32)
_CLS_EB = np.array([EPG * g + p[1] for g in range(N_GROUPS) for p in _PAIR_SLOTS], np.int32)


def _sort_plan(cls, rank, counts):
    n = cls.shape[0]
    nblk = n // MOE_BM + N_CLASSES
    padded = (counts + MOE_BM - 1) // MOE_BM * MOE_BM
    pad_end = jnp.cumsum(padded)
    pos = (pad_end - padded)[cls] + rank
    starts = jnp.arange(nblk, dtype=jnp.int32)[:, None] * MOE_BM
    blk_cls = jnp.minimum(jnp.sum((pad_end[None, :] <= starts).astype(jnp.int32), axis=1), N_CLASSES - 1)
    n_used = (pad_end[-1] // MOE_BM).astype(jnp.int32).reshape(1)
    return pos, jnp.asarray(_CLS_EA)[blk_cls], jnp.asarray(_CLS_EB)[blk_cls], n_used


def _rot_cols(w):
    return jnp.concatenate([-w[:, 8:16], w[:, 0:8], -w[:, 24:32], w[:, 16:24]], axis=1)


def _prep_w_in(w):
    kpe = w[:, 3200:3232]
    pad = jnp.zeros((w.shape[0], C_END - C_KPE - 2 * MLA_ROPE), w.dtype)
    return jnp.concatenate([w[:, :3200], w[:, 3232:], kpe, _rot_cols(kpe), pad], axis=1).astype(BF16)


def _prep_mla(w_uq, w_ukv):
    odd = (jnp.arange(MLA_HEADS) % 2 == 1)[None, :, None]
    qh = w_uq.reshape(MLA_Q_RANK, MLA_HEADS, MLA_NOPE + MLA_ROPE)
    nope, pe = qh[:, :, :MLA_NOPE], qh[:, :, MLA_NOPE:]
    pe_sw = _rot_cols(pe.reshape(MLA_Q_RANK * MLA_HEADS, MLA_ROPE)).reshape(MLA_Q_RANK, MLA_HEADS, MLA_ROPE)
    wq = jnp.where(odd, jnp.concatenate([nope, pe, pe_sw], axis=2), jnp.concatenate([pe, pe_sw, nope], axis=2))
    kvh = w_ukv.reshape(MLA_KV_RANK, MLA_HEADS, MLA_NOPE + MLA_V)
    kn, vh = kvh[:, :, :MLA_NOPE], kvh[:, :, MLA_NOPE:]
    wkv = jnp.where(odd, jnp.concatenate([kn, vh], axis=2), jnp.concatenate([vh, kn], axis=2))
    return (wq.reshape(MLA_Q_RANK, MLA_HEADS * LANE).astype(BF16),
            wkv.reshape(MLA_KV_RANK, MLA_HEADS * LANE).astype(BF16))


def _rope_tables(n_ctx, n_lat):
    rows = n_lat // GRID_W
    rowp = jnp.broadcast_to(jnp.arange(rows, dtype=F32)[:, None], (rows, GRID_W)).reshape(-1)
    colp = jnp.broadcast_to(jnp.arange(GRID_W, dtype=F32)[None, :], (rows, GRID_W)).reshape(-1)
    axis_dim = MLA_ROPE // 2
    inv_freq = ROPE_THETA ** (-jnp.arange(0, axis_dim, 2, dtype=F32) / axis_dim)
    ar, ac = rowp[:, None] * inv_freq, colp[:, None] * inv_freq
    cos32 = jnp.concatenate([jnp.cos(ar), jnp.cos(ar), jnp.cos(ac), jnp.cos(ac)], axis=1)
    sin32 = jnp.concatenate([jnp.sin(ar), jnp.sin(ar), jnp.sin(ac), jnp.sin(ac)], axis=1)
    cos32 = jnp.concatenate([jnp.ones((n_ctx, MLA_ROPE), F32), cos32], axis=0)
    sin32 = jnp.concatenate([jnp.zeros((n_ctx, MLA_ROPE), F32), sin32], axis=0)
    lt = n_ctx + n_lat
    one, zero = jnp.ones((lt, MLA_NOPE), F32), jnp.zeros((lt, MLA_NOPE), F32)
    z32 = jnp.zeros((lt, MLA_ROPE), F32)
    sc = MLA_SCALE * LOG2E
    cqe = jnp.concatenate([cos32, z32, one], axis=1) * sc
    sqe = jnp.concatenate([sin32, z32, zero], axis=1) * sc
    cqo = jnp.concatenate([one, cos32, z32], axis=1) * sc
    sqo = jnp.concatenate([zero, sin32, z32], axis=1) * sc
    kc = jnp.concatenate([cos32, sin32, zero], axis=1)
    return cqe, sqe, cqo, sqo, kc


def _lower_bounds(gamma):
    p = jnp.cumsum(jax.nn.softmax(gamma.astype(F32), axis=0), axis=0)
    return p - p[0:1]


def kernel(x, c, ctx, c_ctx, w_mod, b_mod, norm1_g, norm2_g, w_in, gamma_fwd, gamma_bwd, hg_norm_g,
           q_norm_g, kv_norm_g, w_uq, w_ukv, w_pa, w_pb, w_o, w_router, router_bias,
           w_gate_e, w_up_e, w_down_e, final_g):
    b_, n_lat, d = x.shape
    n_ctx = ctx.shape[1]
    depth = w_mod.shape[0]
    assert n_ctx % TM == 0 and n_lat % TM == 0 and n_lat % GRID_W == 0 and b_ + 1 <= SUB
    nct = n_ctx // TM
    lt = n_ctx + n_lat
    nt = lt // TM

    cc = jnp.concatenate([c, c_ctx[None, :], jnp.zeros((SUB - b_ - 1, d), F32)], axis=0)
    mod = _modulation(cc, w_mod, b_mod).reshape(depth, SUB, 6, d)
    mod_x = mod[:, :b_]
    mod_c = jnp.broadcast_to(mod[:, b_:b_ + 1], mod_x.shape)
    mod_t = jnp.stack([mod_c, mod_x], axis=2)
    mod_t = jnp.concatenate([mod_t, jnp.zeros((depth, b_, 2, SUB - 6, d), F32)], axis=3)

    cqe, sqe, cqo, sqo, kc = _rope_tables(n_ctx, n_lat)
    lbf, lbb = _lower_bounds(gamma_fwd), _lower_bounds(gamma_bwd)
    wr = jnp.concatenate([w_router, jnp.zeros((d, LANE - N_EXPERTS), F32)], axis=1)
    rbb = jnp.broadcast_to(router_bias.astype(F32)[:, None], (N_EXPERTS, TM))

    z = (ctx, x)
    out = hs = None
    for l in range(depth):
        last = l == depth - 1
        wq, wkv = _prep_mla(w_uq[l], w_ukv[l])
        (q, v, lff, kf, lfb, kb, g, gas, gbs, qq, kk, vv) = _inproj(
            z, mod_t[l], norm1_g[l][None], _prep_w_in(w_in[l]), lbf[l][None], lbb[l][None],
            q_norm_g[l][None], kv_norm_g[l][None], wq, wkv, cqe, sqe, cqo, sqo, kc, nct)
        o_f, o_b = _hgrn(q, v, lff, kf, lfb, kb, nct)
        m = _attention(qq, kk, vv, n_ctx)
        zn, ht, route, cnt = _merge(o_f, o_b, g, hg_norm_g[l][None], m, gas, gbs, z, mod_t[l],
                                    w_pa[l].astype(BF16), w_pb[l].astype(BF16), w_o[l].astype(BF16),
                                    norm2_g[l][None], wr, rbb, nct)
        route = route[:, 0:2, :].astype(jnp.int32)
        pos, blk_ea, blk_eb, n_used = _sort_plan(route[:, 0].reshape(-1), route[:, 1].reshape(-1),
                                                 cnt[:N_CLASSES, 0].astype(jnp.int32))
        pos = pos.reshape(b_, nt, 1, TM)
        hs = _sort_rows(ht, pos.reshape(-1, 1, TM), blk_ea.shape[0] * MOE_BM, hs)
        y = _moe(hs, w_router.T, blk_ea, blk_eb, n_used, w_gate_e, w_up_e, w_down_e, l)
        if last:
            out = _post(y, pos[:, nct:].reshape(-1, 1, TM), zn, mod_t[l], nct, final_g[None])
        else:
            z = _post(y, pos.reshape(-1, 1, TM), zn, mod_t[l], nct)
    return out
```

```python
import functools

import numpy as np
import jax
import jax.numpy as jnp
from jax import lax
from jax.experimental import pallas as pl
from jax.experimental.pallas import tpu as pltpu

F32 = jnp.float32
BF16 = jnp.bfloat16

EPS = 1e-6
GRID_W = 64
ROPE_THETA = 10000.0
HG_HEADS = 4
HG_DK = 128
HG_WIDTH = 512
MLA_HEADS = 8
MLA_NOPE = 64
MLA_ROPE = 32
MLA_V = 64
MLA_Q_RANK = 384
MLA_KV_RANK = 256
MLA_SCALE = (MLA_NOPE + MLA_ROPE) ** -0.5
LOG2E = 1.4426950408889634
N_EXPERTS = 16
N_GROUPS = 4
EPG = 4
N_CLASSES = N_GROUPS * 6

LANE = 128
SUB = 8
TM = 256
CHUNK = 128
MOE_BM = 256
VMEM_LIMIT = 56 * 1024 * 1024

C_Q, C_I, C_FF, C_FB, C_G, C_DQ, C_DKV, C_GA, C_GB, C_KPE, C_END = (
    0, 512, 1024, 1536, 2048, 2560, 2944, 3200, 4224, 5248, 5376)


def _cparams(n_axes):
    return pltpu.CompilerParams(dimension_semantics=("arbitrary",) * n_axes,
                                vmem_limit_bytes=VMEM_LIMIT)


def _rms(x, g):
    y = x * lax.rsqrt(jnp.mean(x * x, axis=-1, keepdims=True) + EPS)
    return y * g


def _dot(a, b):
    return jnp.dot(a, b, preferred_element_type=F32)


def _dot_nt(a, b):
    return lax.dot_general(a, b, (((1,), (1,)), ((), ())), preferred_element_type=F32)


def _dot_tn(a, b):
    return lax.dot_general(a, b, (((0,), (0,)), ((), ())), preferred_element_type=F32)


def _split3(x):
    hi = x.astype(BF16)
    r1 = x - hi.astype(F32)
    mid = r1.astype(BF16)
    lo = (r1 - mid.astype(F32)).astype(BF16)
    return hi, mid, lo


def _dot_f32(a, b, passes=6):
    a0, a1, a2 = _split3(a)
    b0, b1, b2 = _split3(b)
    out = _dot(a0, b0) + (_dot(a0, b1) + _dot(a1, b0))
    if passes == 6:
        out = out + (_dot(a0, b2) + _dot(a1, b1) + _dot(a2, b0))
    return out


def _mod_kernel(c_ref, w_ref, b_ref, o_ref):
    c = c_ref[...]
    s = c * jax.nn.sigmoid(c)
    o_ref[0] = _dot(s.astype(BF16), w_ref[0].astype(BF16)) + b_ref[0]


def _modulation(cc, w_mod, b_mod):
    depth, d, n6 = w_mod.shape
    nb = 1536
    return pl.pallas_call(
        _mod_kernel,
        grid=(depth, n6 // nb),
        in_specs=[pl.BlockSpec((SUB, d), lambda l, j: (0, 0)),
                  pl.BlockSpec((1, d, nb), lambda l, j: (l, 0, j)),
                  pl.BlockSpec((1, 1, nb), lambda l, j: (l, 0, j))],
        out_specs=pl.BlockSpec((1, SUB, nb), lambda l, j: (l, 0, j)),
        out_shape=jax.ShapeDtypeStruct((depth, SUB, n6), F32),
        compiler_params=_cparams(2),
        name="modulation",
    )(cc, w_mod, b_mod.reshape(depth, 1, n6))


def _z_inputs(z, nct):
    if isinstance(z, tuple):
        ctx, x = z
        b_, n_lat, d = x.shape
        specs = [pl.BlockSpec((1, TM, d), lambda b, i: (b, jnp.minimum(i, nct - 1), 0)),
                 pl.BlockSpec((1, TM, d), lambda b, i: (b, jnp.maximum(i - nct, 0), 0))]
        return [ctx, x], specs, b_, ctx.shape[1] + n_lat, d
    b_, lt, d = z.shape
    return [z], [pl.BlockSpec((1, TM, d), lambda b, i: (b, i, 0))], b_, lt, d


def _z_tile(z_refs, nct):
    if len(z_refs) == 2:
        return jnp.where(pl.program_id(1) < nct, z_refs[0][0], z_refs[1][0])
    return z_refs[0][0]


def _inproj_kernel(*refs, nz, nct):
    z_refs = refs[:nz]
    (mod_ref, g1_ref, w_ref, lbf_ref, lbb_ref, qg_ref, kvg_ref,
     wq_ref, wkv_ref, cqe_ref, sqe_ref, cqo_ref, sqo_ref, kc_ref,
     q_o, v_o, lff_o, kf_o, lfb_o, kb_o, g_o, gas_o, gbs_o, qq_o, kk_o, vv_o) = refs[nz:]
    mod = mod_ref[0, 0]
    h = _rms(_z_tile(z_refs, nct), g1_ref[...]) * (1.0 + mod[1:2]) + mod[0:1]
    hb = h.astype(BF16)

    p = _dot(hb, w_ref[...])

    def seg(a, b):
        return p[:, a:b]

    q_o[0] = seg(C_Q, C_I).astype(BF16)
    v_o[0] = seg(C_I, C_FF).astype(BF16)
    for a, lb_ref, lf_o, k_o in ((C_FF, lbf_ref, lff_o, kf_o), (C_FB, lbb_ref, lfb_o, kb_o)):
        lb = lb_ref[...]
        f = lb + (1.0 - lb) * jax.nn.sigmoid(seg(a, a + HG_WIDTH))
        lf_o[0] = jnp.log(f) * LOG2E
        k_o[0] = (1.0 - f).astype(BF16)
    g_o[0] = seg(C_G, C_DQ).astype(BF16)
    gas_o[0] = jax.nn.sigmoid(seg(C_GA, C_GB)).astype(BF16)
    gbs_o[0] = jax.nn.sigmoid(seg(C_GB, C_KPE)).astype(BF16)

    lane = lax.broadcasted_iota(jnp.int32, (TM, LANE), 1)
    qn = _rms(seg(C_DQ, C_DKV), qg_ref[...]).astype(BF16)
    qa = _dot(qn, wq_ref[...])
    tabs = ((cqe_ref[...], sqe_ref[...]), (cqo_ref[...], sqo_ref[...]))
    for hh in range(MLA_HEADS):
        sl = slice(hh * LANE, (hh + 1) * LANE)
        cq, sq = tabs[hh % 2]
        blk = qa[:, sl]
        qq_o[0, :, sl] = (blk * cq + pltpu.roll(blk, LANE - MLA_ROPE, 1) * sq).astype(BF16)
    kvn = _rms(seg(C_DKV, C_GA), kvg_ref[...]).astype(BF16)
    kv = _dot(kvn, wkv_ref[...])
    kp = seg(C_KPE, C_END) * kc_ref[...]
    kpe_e = jnp.where(lane < MLA_ROPE, kp + pltpu.roll(kp, LANE - MLA_ROPE, 1), 0.0)
    kpe_o = pltpu.roll(kpe_e, MLA_NOPE, 1)
    one_e = jnp.where(lane == MLA_V, 1.0, 0.0)
    one_o = jnp.where(lane == 0, 1.0, 0.0)
    for hh in range(MLA_HEADS):
        sl = slice(hh * LANE, (hh + 1) * LANE)
        blk = kv[:, sl]
        if hh % 2 == 0:
            kk_o[0, sl, :] = jnp.where(lane >= MLA_V, blk, kpe_e).T.astype(BF16)
            vv_o[0, :, sl] = jnp.where(lane < MLA_V, blk, one_e).astype(BF16)
        else:
            kk_o[0, sl, :] = jnp.where(lane < MLA_NOPE, blk, kpe_o).T.astype(BF16)
            vv_o[0, :, sl] = jnp.where(lane >= MLA_NOPE, blk, one_o).astype(BF16)


def _inproj(z, modl, g1, w, lbf, lbb, qg, kvg, wq, wkv, cqe, sqe, cqo, sqo, kc, nct):
    z_in, z_specs, b_, lt, d = _z_inputs(z, nct)
    nt = lt // TM
    row = lambda w_: pl.BlockSpec((1, TM, w_), lambda b, i: (b, i, 0))
    full = lambda a: pl.BlockSpec(a.shape, lambda b, i: (0,) * a.ndim)
    tab = pl.BlockSpec((TM, LANE), lambda b, i: (i, 0))
    f32o = lambda w_: jax.ShapeDtypeStruct((b_, lt, w_), F32)
    bf16o = lambda w_: jax.ShapeDtypeStruct((b_, lt, w_), BF16)
    return pl.pallas_call(
        functools.partial(_inproj_kernel, nz=len(z_in), nct=nct),
        grid=(b_, nt),
        in_specs=z_specs + [
                  pl.BlockSpec((1, 1, SUB, d), lambda b, i: (b, jnp.minimum(i // nct, 1), 0, 0)),
                  full(g1), full(w), full(lbf), full(lbb), full(qg), full(kvg),
                  full(wq), full(wkv), tab, tab, tab, tab, tab],
        out_specs=[row(512)] * 7 + [row(1024)] * 3
        + [pl.BlockSpec((1, MLA_HEADS * LANE, TM), lambda b, i: (b, 0, i)), row(1024)],
        out_shape=[bf16o(512), bf16o(512), f32o(512), bf16o(512), f32o(512), bf16o(512), bf16o(512)]
        + [bf16o(1024)] * 3 + [jax.ShapeDtypeStruct((b_, MLA_HEADS * LANE, lt), BF16), bf16o(1024)],
        compiler_params=_cparams(2),
        name="inproj",
    )(*z_in, modl, g1, w, lbf, lbb, qg, kvg, wq, wkv, cqe, sqe, cqo, sqo, kc)


N_LEVELS = 7


def _chunk_tables(reverse):
    c = CHUNK
    t = np.arange(c)[:, None]
    u = np.arange(c)[None, :]
    tri = ((u >= t) if reverse else (u <= t)).astype(np.float32)
    lvl = np.full((c, c), N_LEVELS + 1, np.int32)
    lvl[t == u] = 0
    for l in range(N_LEVELS, 0, -1):
        m = 1 << l
        same = (t // m) == (u // m)
        lvl[same & ((u > t) if reverse else (u < t))] = l
    return jnp.asarray(tri, dtype=BF16), jnp.asarray(lvl)


def _level_ref(bs_ref, hsl, l, reverse):
    w = HG_DK
    sub = lax.broadcasted_iota(jnp.int32, (SUB, w), 0)
    off = 0 if reverse else -1
    m = 1 << l

    def row(r):
        return jnp.broadcast_to(bs_ref[r:r + 1, hsl], (SUB, w))

    pieces = []
    for j in range(CHUNK // SUB):
        base = j * SUB
        if m >= SUB:
            pieces.append(row((base // m) * m + m // 2 + off))
        else:
            p = row(base + m // 2 + off)
            for i in range(1, SUB // m):
                p = jnp.where(sub < i * m, p, row(base + i * m + m // 2 + off))
            pieces.append(p)
    return jnp.concatenate(pieces, axis=0)


def _hgrn_kernel(qf_ref, vf_ref, lff_ref, kf_ref, qb_ref, vb_ref, lfb_ref, kb_ref,
                 trif_ref, lvlf_ref, trib_ref, lvlb_ref, of_ref, ob_ref, sf_ref, sb_ref, bsf_ref, bsb_ref):
    @pl.when(pl.program_id(1) == 0)
    def _():
        sf_ref[...] = jnp.zeros_like(sf_ref)
        sb_ref[...] = jnp.zeros_like(sb_ref)

    _hgrn_tile(qf_ref, vf_ref, lff_ref, kf_ref, trif_ref, lvlf_ref, of_ref, sf_ref, bsf_ref, False)
    _hgrn_tile(qb_ref, vb_ref, lfb_ref, kb_ref, trib_ref, lvlb_ref, ob_ref, sb_ref, bsb_ref, True)


def _hgrn_tile(q_ref, v_ref, lf_ref, k_ref, tri_ref, lvl_ref, o_ref, s_ref, bs_ref, reverse):
    c = CHUNK
    nch = TM // c
    e = lambda x: jnp.exp2(x).astype(BF16)

    lvl = lvl_ref[...]
    for ci in range(nch):
        cidx = (nch - 1 - ci) if reverse else ci
        rows = slice(cidx * c, (cidx + 1) * c)
        lf = lf_ref[0, rows, :]
        hi = lf.astype(BF16)
        lo = (lf - hi.astype(F32)).astype(BF16)
        bsc = bs_ref.at[ci]
        bsc[...] = _dot(tri_ref[...], hi) + _dot(tri_ref[...], lo)
        for hh in range(HG_HEADS):
            hsl = slice(hh * HG_DK, (hh + 1) * HG_DK)
            b = bsc[:, hsl]
            btot = b[0:1] if reverse else b[c - 1:c]
            q = q_ref[0, rows, hsl]
            k = k_ref[0, rows, hsl]
            ix = (0, rows, hsl)
            v = v_ref[ix]
            st = s_ref[hh]
            att = jnp.where(lvl == 0, _dot_nt(q, k), 0.0)
            for l in range(1, N_LEVELS + 1):
                f = e(-jnp.abs(b - _level_ref(bsc, hsl, l, reverse)))
                att = jnp.where(lvl == l, _dot_nt(q * f, k * f), att)
            o = _dot_nt(q * e(b), st.astype(BF16)) + _dot(att.astype(BF16), v)
            s_ref[hh] = st * jnp.exp2(btot) + _dot_tn(v, k * e(btot - b))
            o_ref[ix] = o.astype(o_ref.dtype)


def _hgrn(q, v, lf_f, k_f, lf_b, k_b, nct):
    b_, lt, w = q.shape
    nt = lt // TM
    rtile = lambda i: jnp.where(i < nct, nct - 1 - i, nt - 1 - (i - nct))
    rowf = pl.BlockSpec((1, TM, w), lambda b, i: (b, i, 0))
    rowb = pl.BlockSpec((1, TM, w), lambda b, i: (b, rtile(i), 0))
    full = lambda a: pl.BlockSpec(a.shape, lambda b, i: (0,) * a.ndim)
    trif, lvlf = _chunk_tables(False)
    trib, lvlb = _chunk_tables(True)
    o_sh = jax.ShapeDtypeStruct((b_, lt, w), BF16)
    state = [pltpu.VMEM((HG_HEADS, HG_DK, HG_DK), F32)] * 2
    return pl.pallas_call(
        _hgrn_kernel,
        grid=(b_, nt),
        in_specs=[rowf] * 4 + [rowb] * 4 + [full(trif), full(lvlf), full(trib), full(lvlb)],
        out_specs=[rowf, rowb],
        out_shape=[o_sh, o_sh],
        scratch_shapes=state + [pltpu.VMEM((TM // CHUNK, CHUNK, w), F32)] * 2,
        compiler_params=_cparams(2),
        name="hgrn_scan",
    )(q, v, lf_f, k_f, q, v, lf_b, k_b, trif, lvlf, trib, lvlb)


KV_CHUNK = 256


def _attn_kernel(q_ref, k_ref, v_ref, o_ref, *, n_ctx, nct):
    i = pl.program_id(2)
    lane = lax.broadcasted_iota(jnp.int32, (TM, LANE), 1)

    def run(nk):
        outs = []
        for hh in range(2):
            sl = slice(hh * LANE, (hh + 1) * LANE)
            q = q_ref[0, :, sl]
            m = acc = None
            edges = [0] + list(range(n_ctx, nk + 1, KV_CHUNK))
            for c in range(len(edges) - 1):
                ks = slice(edges[c], edges[c + 1])
                s = _dot(q, k_ref[0, sl, ks])
                mc = jnp.max(s, axis=-1, keepdims=True)
                if c == 0:
                    m = mc
                    acc = _dot(jnp.exp2(s - m).astype(BF16), v_ref[0, ks, sl])
                else:
                    m_new = jnp.maximum(m, mc)
                    acc = acc * jnp.exp2(m - m_new) + _dot(jnp.exp2(s - m_new).astype(BF16), v_ref[0, ks, sl])
                    m = m_new
            den = acc[:, MLA_V:MLA_V + 1] if hh == 0 else acc[:, 0:1]
            outs.append(acc / den)
        o_ref[0] = jnp.where(lane < MLA_V, outs[0], outs[1]).astype(o_ref.dtype)

    @pl.when(i < nct)
    def _():
        run(n_ctx)

    @pl.when(i >= nct)
    def _():
        run(k_ref.shape[2])


def _attention(qq, kk, vv, n_ctx):
    b_, lt, _ = qq.shape
    nt = lt // TM
    npair = MLA_HEADS // 2
    return pl.pallas_call(
        functools.partial(_attn_kernel, n_ctx=n_ctx, nct=n_ctx // TM),
        grid=(b_, npair, nt),
        in_specs=[pl.BlockSpec((1, TM, 2 * LANE), lambda b, j, i: (b, i, j)),
                  pl.BlockSpec((1, 2 * LANE, lt), lambda b, j, i: (b, j, 0)),
                  pl.BlockSpec((1, lt, 2 * LANE), lambda b, j, i: (b, 0, j))],
        out_specs=pl.BlockSpec((1, TM, LANE), lambda b, j, i: (b, i, j)),
        out_shape=jax.ShapeDtypeStruct((b_, lt, npair * LANE), BF16),
        compiler_params=_cparams(3),
        name="mla_attention",
    )(qq, kk, vv)


def _route(sel):
    r = lambda x, e: x[e:e + 1]
    best = None
    for g in range(N_GROUPS):
        u = [r(sel, EPG * g + j) for j in range(EPG)]
        gs = None
        for a in range(EPG):
            for b in range(a + 1, EPG):
                pr = u[a] + u[b]
                gs = pr if gs is None else jnp.maximum(gs, pr)
        if best is None:
            best, gi = gs, jnp.zeros_like(gs, dtype=jnp.int32)
        else:
            upd = gs > best
            best = jnp.where(upd, gs, best)
            gi = jnp.where(upd, g, gi)

    def pick(x, j):
        out = r(x, j)
        for g in range(1, N_GROUPS):
            out = jnp.where(gi == g, r(x, EPG * g + j), out)
        return out

    u = [pick(sel, j) for j in range(EPG)]

    def argmax4(vals):
        bv, bi = vals[0], jnp.zeros_like(gi)
        for j in range(1, EPG):
            upd = vals[j] > bv
            bv = jnp.where(upd, vals[j], bv)
            bi = jnp.where(upd, j, bi)
        return bi

    l1 = argmax4(u)
    l2 = argmax4([jnp.where(l1 == j, -jnp.inf, u[j]) for j in range(EPG)])
    lo = jnp.minimum(l1, l2)
    hi = jnp.maximum(l1, l2)
    pair = jnp.where(lo == 0, jnp.where(hi == 1, 0, hi), jnp.where(lo == 1, jnp.where(hi == 2, 1, 4), 5))
    return (gi * 6 + pair).astype(F32)


CLS_ROWS = 32


def _merge_kernel(*refs, nz, nct):
    z_refs = refs[:nz]
    (of_ref, ob_ref, g_ref, gain_ref, m_ref, gas_ref, gbs_ref, mod_ref, wpa_ref, wpb_ref,
     wo_ref, g2_ref, wr_ref, rb_ref, triu_ref, zn_o, ht_o, route_o, cnt_o, run_ref) = refs[nz:]

    @pl.when((pl.program_id(0) == 0) & (pl.program_id(1) == 0))
    def _():
        run_ref[...] = jnp.zeros_like(run_ref)

    mod = mod_ref[0, 0]
    heads = []
    for hh in range(HG_HEADS):
        hsl = slice(hh * HG_DK, (hh + 1) * HG_DK)
        o = of_ref[0, :, hsl].astype(F32) + ob_ref[0, :, hsl].astype(F32)
        g = g_ref[0, :, hsl].astype(F32)
        heads.append(_rms(o, gain_ref[:, hsl]) * (g * jax.nn.sigmoid(g)))
    pa = _dot(jnp.concatenate(heads, axis=1).astype(BF16), wpa_ref[...])
    pb = _dot(m_ref[0].astype(BF16), wpb_ref[...])
    mixed = gas_ref[0] * pa + gbs_ref[0] * pb
    zn = _z_tile(z_refs, nct) + mod[2:3] * _dot(mixed.astype(BF16), wo_ref[...])
    zn_o[0] = zn
    h2 = _rms(zn, g2_ref[...]) * (1.0 + mod[4:5]) + mod[3:4]
    _tiles_store(ht_o, h2)
    logits = _dot_f32(h2, wr_ref[...], passes=3)
    sc = jax.nn.sigmoid(logits.T[:N_EXPERTS])
    cls = _route(sc + rb_ref[...])
    onehot = (lax.broadcasted_iota(jnp.int32, (CLS_ROWS, TM), 0) == cls.astype(jnp.int32))
    pref = _dot(onehot.astype(BF16), triu_ref[...])
    rank = jnp.sum(jnp.where(onehot, pref + run_ref[:, 0:1], 0.0), axis=0, keepdims=True) - 1.0
    run_ref[...] = run_ref[...] + pref[:, TM - 1:TM]
    cnt_o[...] = run_ref[...]
    route_o[0] = jnp.concatenate([cls, rank] + [jnp.zeros_like(cls)] * (SUB - 2), axis=0)


def _merge(o_f, o_b, g, gain, m, gas, gbs, z, modl, wpa, wpb, wo, g2, wr, rbb, nct):
    z_in, z_specs, b_, lt, d = _z_inputs(z, nct)
    nt = lt // TM
    row = lambda w_: pl.BlockSpec((1, TM, w_), lambda b, i: (b, i, 0))
    full = lambda x: pl.BlockSpec(x.shape, lambda b, i: (0,) * x.ndim)
    triu = jnp.asarray(np.triu(np.ones((TM, TM), np.float32)), dtype=BF16)
    return pl.pallas_call(
        functools.partial(_merge_kernel, nz=len(z_in), nct=nct),
        grid=(b_, nt),
        in_specs=z_specs + [row(512), row(512), row(512), full(gain), row(512), row(d), row(d),
                  pl.BlockSpec((1, 1, SUB, d), lambda b, i: (b, jnp.minimum(i // nct, 1), 0, 0)),
                  full(wpa), full(wpb), full(wo), full(g2), full(wr), full(rbb), full(triu)],
        out_specs=[row(d), pl.BlockSpec((TM * SUB, LANE), lambda b, i: (b * nt + i, 0)),
                   pl.BlockSpec((1, SUB, TM), lambda b, i: (b, 0, i)),
                   pl.BlockSpec((CLS_ROWS, LANE), lambda b, i: (0, 0))],
        out_shape=[jax.ShapeDtypeStruct((b_, lt, d), F32),
                   jax.ShapeDtypeStruct((b_ * lt * SUB, LANE), F32),
                   jax.ShapeDtypeStruct((b_, SUB, lt), F32),
                   jax.ShapeDtypeStruct((CLS_ROWS, LANE), F32)],
        scratch_shapes=[pltpu.VMEM((CLS_ROWS, LANE), F32)],
        compiler_params=_cparams(2),
        name="merge_route",
    )(*z_in, o_f, o_b, g, gain, m, gas, gbs, modl, wpa, wpb, wo, g2, wr, rbb, triu)


def _tiles_store(ref, x, lead=()):
    n = x.shape[0]
    for j in range(SUB):
        ref[lead + (pl.ds(j, n, stride=SUB), slice(None))] = x[:, j * LANE:(j + 1) * LANE]


def _tiles_load(ref, n, lead=()):
    return jnp.concatenate([ref[lead + (pl.ds(j, n, stride=SUB), slice(None))] for j in range(SUB)], axis=1)


def _tile_copy(src_hbm, idx_ref, buf, sem, slot, r):
    return pltpu.make_async_copy(src_hbm.at[pl.ds(pl.multiple_of(idx_ref[0, 0, r] * SUB, SUB), SUB)],
                                 buf.at[slot, pl.ds(pl.multiple_of(r * SUB, SUB), SUB)], sem.at[slot])


GATHER_UNROLL = 8


def _gather_start(src_hbm, idx_ref, buf, sem, slot, n):
    def body(r8, c):
        for u in range(GATHER_UNROLL):
            _tile_copy(src_hbm, idx_ref, buf, sem, slot, r8 * GATHER_UNROLL + u).start(priority=u % 2)
        return c
    lax.fori_loop(0, n // GATHER_UNROLL, body, 0)


def _gather_wait(src_hbm, buf, sem, slot):
    pltpu.make_async_copy(src_hbm.at[pl.ds(0, buf.shape[1])], buf.at[slot], sem.at[slot]).wait()


def _sort_kernel(pos_ref, ht_ref, init_hbm, hs_hbm, sem):
    del init_hbm

    def body(r8, c):
        for u in range(GATHER_UNROLL):
            r = r8 * GATHER_UNROLL + u
            pltpu.make_async_copy(
                ht_ref.at[pl.ds(pl.multiple_of(r * SUB, SUB), SUB)],
                hs_hbm.at[pl.ds(pl.multiple_of(pos_ref[0, 0, r] * SUB, SUB), SUB)], sem.at[0]).start(priority=u % 2)
        return c
    lax.fori_loop(0, TM // GATHER_UNROLL, body, 0)
    pltpu.make_async_copy(ht_ref, hs_hbm.at[pl.ds(0, TM * SUB)], sem.at[0]).wait()


def _sort_rows(ht, pos, n_sorted, init=None):
    nstep = pos.shape[0]
    if init is None:
        init = jnp.zeros((n_sorted * SUB, LANE), F32)
    return pl.pallas_call(
        _sort_kernel,
        grid=(nstep,),
        in_specs=[pl.BlockSpec((1, 1, TM), lambda t: (t, 0, 0), memory_space=pltpu.SMEM),
                  pl.BlockSpec((TM * SUB, LANE), lambda t: (t, 0)), pl.BlockSpec(memory_space=pl.ANY)],
        out_specs=pl.BlockSpec(memory_space=pl.ANY),
        out_shape=jax.ShapeDtypeStruct(init.shape, F32),
        scratch_shapes=[pltpu.SemaphoreType.DMA((1,))],
        input_output_aliases={2: 0},
        compiler_params=_cparams(1),
        name="class_sort",
    )(pos, ht, init)


def _moe_kernel(ea_ref, eb_ref, nu_ref, hs_ref, wrt_ref,
                wga_ref, wua_ref, wda_ref, wgb_ref, wub_ref, wdb_ref, y_ref, gu_a, dn_a, gu_b, dn_b):
    j = pl.program_id(0)
    jp = jnp.maximum(j - 1, 0)

    @pl.when((j == 0) | (ea_ref[j] != ea_ref[jp]))
    def _():
        gu_a[0] = wga_ref[0].astype(BF16)
        gu_a[1] = wua_ref[0].astype(BF16)
        dn_a[...] = wda_ref[0].astype(BF16)

    @pl.when((j == 0) | (eb_ref[j] != eb_ref[jp]))
    def _():
        gu_b[0] = wgb_ref[0].astype(BF16)
        gu_b[1] = wub_ref[0].astype(BF16)
        dn_b[...] = wdb_ref[0].astype(BF16)

    @pl.when(j < nu_ref[0])
    def _():
        x32 = _tiles_load(hs_ref, MOE_BM)
        x = x32.astype(BF16)

        def ffn(gu, dn, e):
            sc = jax.nn.sigmoid(jnp.sum(x32 * wrt_ref[pl.ds(e, 1), :], axis=-1, keepdims=True))
            act = jax.nn.silu(_dot(x, gu[0])) * _dot(x, gu[1])
            return sc, _dot(act.astype(BF16), dn[...])

        s_a, y_a = ffn(gu_a, dn_a, ea_ref[j])
        s_b, y_b = ffn(gu_b, dn_b, eb_ref[j])
        inv = 1.0 / (s_a + s_b)
        _tiles_store(y_ref, (s_a * inv) * y_a + (s_b * inv) * y_b)

    @pl.when(j >= nu_ref[0])
    def _():
        y_ref[...] = jnp.zeros_like(y_ref)


def _moe(hs, wrt, blk_ea, blk_eb, n_used, wg, wu, wd, layer):
    nblk = blk_ea.shape[0]
    d, de = wg.shape[2:]
    tiles = pl.BlockSpec((MOE_BM * SUB, LANE), lambda j, ea, eb, nu: (j, 0))
    wspec = lambda shp, which: pl.BlockSpec(
        (None, 1) + shp, (lambda j, ea, eb, nu: (layer, ea[j], 0, 0)) if which == 0
        else (lambda j, ea, eb, nu: (layer, eb[j], 0, 0)))
    return pl.pallas_call(
        _moe_kernel,
        grid_spec=pltpu.PrefetchScalarGridSpec(
            num_scalar_prefetch=3,
            grid=(nblk,),
            in_specs=[tiles,
                      pl.BlockSpec(wrt.shape, lambda j, ea, eb, nu: (0, 0)),
                      wspec((d, de), 0), wspec((d, de), 0), wspec((de, d), 0),
                      wspec((d, de), 1), wspec((d, de), 1), wspec((de, d), 1)],
            out_specs=tiles,
            scratch_shapes=[pltpu.VMEM((2, d, de), BF16), pltpu.VMEM((de, d), BF16)] * 2),
        out_shape=jax.ShapeDtypeStruct(hs.shape, F32),
        compiler_params=_cparams(1),
        name="moe_ffn",
    )(blk_ea, blk_eb, n_used, hs, wrt, wg, wu, wd, wg, wu, wd)


def _post_kernel(*refs, final):
    if final:
        idx_ref, idxn_ref, y_hbm, zn_ref, mod_ref, fg_ref, o_ref, buf, sem = refs
    else:
        idx_ref, idxn_ref, y_hbm, zn_ref, mod_ref, o_ref, buf, sem = refs
    t = pl.program_id(0)
    nstep = pl.num_programs(0)
    slot = t % 2

    @pl.when(t == 0)
    def _():
        _gather_start(y_hbm, idx_ref, buf, sem, 0, TM)

    @pl.when(t + 1 < nstep)
    def _():
        _gather_start(y_hbm, idxn_ref, buf, sem, 1 - slot, TM)

    _gather_wait(y_hbm, buf, sem, slot)
    z = zn_ref[0] + mod_ref[0, 0][5:6] * _tiles_load(buf, TM, (slot,))
    if final:
        z = _rms(z, fg_ref[...])
    o_ref[0] = z


def _post(y, pos, zn, modl, nct, final_g=None):
    b_, lt, d = zn.shape
    nt = lt // TM
    final = final_g is not None
    t0 = nct if final else 0
    ntl = nt - t0
    nstep = b_ * ntl
    idx_spec = lambda f: pl.BlockSpec((1, 1, TM), f, memory_space=pltpu.SMEM)
    ins = [pos, pos, y, zn, modl]
    specs = [idx_spec(lambda t: (t, 0, 0)),
             idx_spec(lambda t: (jnp.minimum(t + 1, nstep - 1), 0, 0)),
             pl.BlockSpec(memory_space=pl.ANY),
             pl.BlockSpec((1, TM, d), lambda t: (t // ntl, t % ntl + t0, 0)),
             pl.BlockSpec((1, 1, SUB, d), lambda t: (t // ntl, jnp.minimum((t % ntl + t0) // nct, 1), 0, 0))]
    if final:
        ins.append(final_g)
        specs.append(pl.BlockSpec(final_g.shape, lambda t: (0, 0)))
    return pl.pallas_call(
        functools.partial(_post_kernel, final=final),
        grid=(nstep,),
        in_specs=specs,
        out_specs=pl.BlockSpec((1, TM, d), lambda t: (t // ntl, t % ntl, 0)),
        out_shape=jax.ShapeDtypeStruct((b_, ntl * TM, d), F32),
        scratch_shapes=[pltpu.VMEM((2, TM * SUB, LANE), F32), pltpu.SemaphoreType.DMA((2,))],
        compiler_params=_cparams(1),
        name="unsort_residual",
    )(*ins)


_PAIR_SLOTS = [(0, 1), (2, 1), (2, 0), (3, 0), (3, 1), (3, 2)]
_CLS_EA = np.array([EPG * g + p[0] for g in range(N_GROUPS) for p in _PAIR_SLOTS], np.int32)
_CLS_EB = np.array([EPG * g + p[1] for g in range(N_GROUPS) for p in _PAIR_SLOTS], np.int32)


def _sort_plan(cls, rank, counts):
    n = cls.shape[0]
    nblk = n // MOE_BM + N_CLASSES
    padded = (counts + MOE_BM - 1) // MOE_BM * MOE_BM
    pad_end = jnp.cumsum(padded)
    pos = (pad_end - padded)[cls] + rank
    starts = jnp.arange(nblk, dtype=jnp.int32)[:, None] * MOE_BM
    blk_cls = jnp.minimum(jnp.sum((pad_end[None, :] <= starts).astype(jnp.int32), axis=1), N_CLASSES - 1)
    n_used = (pad_end[-1] // MOE_BM).astype(jnp.int32).reshape(1)
    return pos, jnp.asarray(_CLS_EA)[blk_cls], jnp.asarray(_CLS_EB)[blk_cls], n_used


def _rot_cols(w):
    return jnp.concatenate([-w[:, 8:16], w[:, 0:8], -w[:, 24:32], w[:, 16:24]], axis=1)


def _prep_w_in(w):
    kpe = w[:, 3200:3232]
    pad = jnp.zeros((w.shape[0], C_END - C_KPE - 2 * MLA_ROPE), w.dtype)
    return jnp.concatenate([w[:, :3200], w[:, 3232:], kpe, _rot_cols(kpe), pad], axis=1).astype(BF16)


def _prep_mla(w_uq, w_ukv):
    odd = (jnp.arange(MLA_HEADS) % 2 == 1)[None, :, None]
    qh = w_uq.reshape(MLA_Q_RANK, MLA_HEADS, MLA_NOPE + MLA_ROPE)
    nope, pe = qh[:, :, :MLA_NOPE], qh[:, :, MLA_NOPE:]
    pe_sw = _rot_cols(pe.reshape(MLA_Q_RANK * MLA_HEADS, MLA_ROPE)).reshape(MLA_Q_RANK, MLA_HEADS, MLA_ROPE)
    wq = jnp.where(odd, jnp.concatenate([nope, pe, pe_sw], axis=2), jnp.concatenate([pe, pe_sw, nope], axis=2))
    kvh = w_ukv.reshape(MLA_KV_RANK, MLA_HEADS, MLA_NOPE + MLA_V)
    kn, vh = kvh[:, :, :MLA_NOPE], kvh[:, :, MLA_NOPE:]
    wkv = jnp.where(odd, jnp.concatenate([kn, vh], axis=2), jnp.concatenate([vh, kn], axis=2))
    return (wq.reshape(MLA_Q_RANK, MLA_HEADS * LANE).astype(BF16),
            wkv.reshape(MLA_KV_RANK, MLA_HEADS * LANE).astype(BF16))


def _rope_tables(n_ctx, n_lat):
    rows = n_lat // GRID_W
    rowp = jnp.broadcast_to(jnp.arange(rows, dtype=F32)[:, None], (rows, GRID_W)).reshape(-1)
    colp = jnp.broadcast_to(jnp.arange(GRID_W, dtype=F32)[None, :], (rows, GRID_W)).reshape(-1)
    axis_dim = MLA_ROPE // 2
    inv_freq = ROPE_THETA ** (-jnp.arange(0, axis_dim, 2, dtype=F32) / axis_dim)
    ar, ac = rowp[:, None] * inv_freq, colp[:, None] * inv_freq
    cos32 = jnp.concatenate([jnp.cos(ar), jnp.cos(ar), jnp.cos(ac), jnp.cos(ac)], axis=1)
    sin32 = jnp.concatenate([jnp.sin(ar), jnp.sin(ar), jnp.sin(ac), jnp.sin(ac)], axis=1)
    cos32 = jnp.concatenate([jnp.ones((n_ctx, MLA_ROPE), F32), cos32], axis=0)
    sin32 = jnp.concatenate([jnp.zeros((n_ctx, MLA_ROPE), F32), sin32], axis=0)
    lt = n_ctx + n_lat
    one, zero = jnp.ones((lt, MLA_NOPE), F32), jnp.zeros((lt, MLA_NOPE), F32)
    z32 = jnp.zeros((lt, MLA_ROPE), F32)
    sc = MLA_SCALE * LOG2E
    cqe = jnp.concatenate([cos32, z32, one], axis=1) * sc
    sqe = jnp.concatenate([sin32, z32, zero], axis=1) * sc
    cqo = jnp.concatenate([one, cos32, z32], axis=1) * sc
    sqo = jnp.concatenate([zero, sin32, z32], axis=1) * sc
    kc = jnp.concatenate([cos32, sin32, zero], axis=1)
    return cqe, sqe, cqo, sqo, kc


def _lower_bounds(gamma):
    p = jnp.cumsum(jax.nn.softmax(gamma.astype(F32), axis=0), axis=0)
    return p - p[0:1]


def kernel(x, c, ctx, c_ctx, w_mod, b_mod, norm1_g, norm2_g, w_in, gamma_fwd, gamma_bwd, hg_norm_g,
           q_norm_g, kv_norm_g, w_uq, w_ukv, w_pa, w_pb, w_o, w_router, router_bias,
           w_gate_e, w_up_e, w_down_e, final_g):
    b_, n_lat, d = x.shape
    n_ctx = ctx.shape[1]
    depth = w_mod.shape[0]
    assert n_ctx % TM == 0 and n_lat % TM == 0 and n_lat % GRID_W == 0 and b_ + 1 <= SUB
    nct = n_ctx // TM
    lt = n_ctx + n_lat
    nt = lt // TM

    cc = jnp.concatenate([c, c_ctx[None, :], jnp.zeros((SUB - b_ - 1, d), F32)], axis=0)
    mod = _modulation(cc, w_mod, b_mod).reshape(depth, SUB, 6, d)
    mod_x = mod[:, :b_]
    mod_c = jnp.broadcast_to(mod[:, b_:b_ + 1], mod_x.shape)
    mod_t = jnp.stack([mod_c, mod_x], axis=2)
    mod_t = jnp.concatenate([mod_t, jnp.zeros((depth, b_, 2, SUB - 6, d), F32)], axis=3)

    cqe, sqe, cqo, sqo, kc = _rope_tables(n_ctx, n_lat)
    lbf, lbb = _lower_bounds(gamma_fwd), _lower_bounds(gamma_bwd)
    wr = jnp.concatenate([w_router, jnp.zeros((d, LANE - N_EXPERTS), F32)], axis=1)
    rbb = jnp.broadcast_to(router_bias.astype(F32)[:, None], (N_EXPERTS, TM))

    z = (ctx, x)
    out = hs = None
    for l in range(depth):
        last = l == depth - 1
        wq, wkv = _prep_mla(w_uq[l], w_ukv[l])
        (q, v, lff, kf, lfb, kb, g, gas, gbs, qq, kk, vv) = _inproj(
            z, mod_t[l], norm1_g[l][None], _prep_w_in(w_in[l]), lbf[l][None], lbb[l][None],
            q_norm_g[l][None], kv_norm_g[l][None], wq, wkv, cqe, sqe, cqo, sqo, kc, nct)
        o_f, o_b = _hgrn(q, v, lff, kf, lfb, kb, nct)
        m = _attention(qq, kk, vv, n_ctx)
        zn, ht, route, cnt = _merge(o_f, o_b, g, hg_norm_g[l][None], m, gas, gbs, z, mod_t[l],
                                    w_pa[l].astype(BF16), w_pb[l].astype(BF16), w_o[l].astype(BF16),
                                    norm2_g[l][None], wr, rbb, nct)
        route = route[:, 0:2, :].astype(jnp.int32)
        pos, blk_ea, blk_eb, n_used = _sort_plan(route[:, 0].reshape(-1), route[:, 1].reshape(-1),
                                                 cnt[:N_CLASSES, 0].astype(jnp.int32))
        pos = pos.reshape(b_, nt, 1, TM)
        hs = _sort_rows(ht, pos.reshape(-1, 1, TM), blk_ea.shape[0] * MOE_BM, hs)
        y = _moe(hs, w_router.T, blk_ea, blk_eb, n_used, w_gate_e, w_up_e, w_down_e, l)
        if last:
            out = _post(y, pos[:, nct:].reshape(-1, 1, TM), zn, mod_t[l], nct, final_g[None])
        else:
            z = _post(y, pos.reshape(-1, 1, TM), zn, mod_t[l], nct)
    return out
```

```python
import functools

import numpy as np
import jax
import jax.numpy as jnp
from jax import lax
from jax.experimental import pallas as pl
from jax.experimental.pallas import tpu as pltpu

F32 = jnp.float32
BF16 = jnp.bfloat16

EPS = 1e-6
GRID_W = 64
ROPE_THETA = 10000.0
HG_HEADS = 4
HG_DK = 128
HG_WIDTH = 512
MLA_HEADS = 8
MLA_NOPE = 64
MLA_ROPE = 32
MLA_V = 64
MLA_Q_RANK = 384
MLA_KV_RANK = 256
MLA_SCALE = (MLA_NOPE + MLA_ROPE) ** -0.5
LOG2E = 1.4426950408889634
N_EXPERTS = 16
N_GROUPS = 4
EPG = 4
N_CLASSES = N_GROUPS * 6

LANE = 128
SUB = 8
TM = 256
CHUNK = 128
MOE_BM = 512
VMEM_LIMIT = 56 * 1024 * 1024

C_Q, C_I, C_FF, C_FB, C_G, C_DQ, C_DKV, C_GA, C_GB, C_KPE, C_END = (
    0, 512, 1024, 1536, 2048, 2560, 2944, 3200, 4224, 5248, 5376)


def _cparams(n_axes):
    return pltpu.CompilerParams(dimension_semantics=("arbitrary",) * n_axes,
                                vmem_limit_bytes=VMEM_LIMIT)


def _rms(x, g):
    y = x * lax.rsqrt(jnp.mean(x * x, axis=-1, keepdims=True) + EPS)
    return y * g


def _dot(a, b):
    return jnp.dot(a, b, preferred_element_type=F32)


def _dot_nt(a, b):
    return lax.dot_general(a, b, (((1,), (1,)), ((), ())), preferred_element_type=F32)


def _dot_tn(a, b):
    return lax.dot_general(a, b, (((0,), (0,)), ((), ())), preferred_element_type=F32)


def _split3(x):
    hi = x.astype(BF16)
    r1 = x - hi.astype(F32)
    mid = r1.astype(BF16)
    lo = (r1 - mid.astype(F32)).astype(BF16)
    return hi, mid, lo


def _dot_f32(a, b, passes=6):
    a0, a1, a2 = _split3(a)
    b0, b1, b2 = _split3(b)
    out = _dot(a0, b0) + (_dot(a0, b1) + _dot(a1, b0))
    if passes == 6:
        out = out + (_dot(a0, b2) + _dot(a1, b1) + _dot(a2, b0))
    return out


def _mod_kernel(c_ref, w_ref, b_ref, o_ref):
    c = c_ref[...]
    s = c * jax.nn.sigmoid(c)
    o_ref[0] = _dot_f32(s, w_ref[0], passes=3) + b_ref[0]


def _modulation(cc, w_mod, b_mod):
    depth, d, n6 = w_mod.shape
    nb = 1536
    return pl.pallas_call(
        _mod_kernel,
        grid=(depth, n6 // nb),
        in_specs=[pl.BlockSpec((SUB, d), lambda l, j: (0, 0)),
                  pl.BlockSpec((1, d, nb), lambda l, j: (l, 0, j)),
                  pl.BlockSpec((1, 1, nb), lambda l, j: (l, 0, j))],
        out_specs=pl.BlockSpec((1, SUB, nb), lambda l, j: (l, 0, j)),
        out_shape=jax.ShapeDtypeStruct((depth, SUB, n6), F32),
        compiler_params=_cparams(2),
        name="modulation",
    )(cc, w_mod, b_mod.reshape(depth, 1, n6))


def _z_inputs(z, nct):
    if isinstance(z, tuple):
        ctx, x = z
        b_, n_lat, d = x.shape
        specs = [pl.BlockSpec((1, TM, d), lambda b, i: (b, jnp.minimum(i, nct - 1), 0)),
                 pl.BlockSpec((1, TM, d), lambda b, i: (b, jnp.maximum(i - nct, 0), 0))]
        return [ctx, x], specs, b_, ctx.shape[1] + n_lat, d
    b_, lt, d = z.shape
    return [z], [pl.BlockSpec((1, TM, d), lambda b, i: (b, i, 0))], b_, lt, d


def _z_tile(z_refs, nct):
    if len(z_refs) == 2:
        return jnp.where(pl.program_id(1) < nct, z_refs[0][0], z_refs[1][0])
    return z_refs[0][0]


def _inproj_kernel(*refs, nz, nct):
    z_refs = refs[:nz]
    (mod_ref, g1_ref, w_ref, lbf_ref, lbb_ref, qg_ref, kvg_ref,
     wq_ref, wkv_ref, cqe_ref, sqe_ref, cqo_ref, sqo_ref, kc_ref,
     q_o, v_o, lff_o, kf_o, lfb_o, kb_o, g_o, gas_o, gbs_o, qq_o, kk_o, vv_o) = refs[nz:]
    mod = mod_ref[0, 0]
    h = _rms(_z_tile(z_refs, nct), g1_ref[...]) * (1.0 + mod[1:2]) + mod[0:1]
    hb = h.astype(BF16)

    p = _dot(hb, w_ref[...])

    def seg(a, b):
        return p[:, a:b]

    q_o[0] = seg(C_Q, C_I).astype(BF16)
    v_o[0] = seg(C_I, C_FF).astype(BF16)
    for a, lb_ref, lf_o, k_o in ((C_FF, lbf_ref, lff_o, kf_o), (C_FB, lbb_ref, lfb_o, kb_o)):
        lb = lb_ref[...]
        f = lb + (1.0 - lb) * jax.nn.sigmoid(seg(a, a + HG_WIDTH))
        lf_o[0] = jnp.log(f) * LOG2E
        k_o[0] = (1.0 - f).astype(BF16)
    g_o[0] = seg(C_G, C_DQ).astype(BF16)
    gas_o[0] = jax.nn.sigmoid(seg(C_GA, C_GB)).astype(BF16)
    gbs_o[0] = jax.nn.sigmoid(seg(C_GB, C_KPE)).astype(BF16)

    lane = lax.broadcasted_iota(jnp.int32, (TM, LANE), 1)
    qn = _rms(seg(C_DQ, C_DKV), qg_ref[...]).astype(BF16)
    qa = _dot(qn, wq_ref[...])
    tabs = ((cqe_ref[...], sqe_ref[...]), (cqo_ref[...], sqo_ref[...]))
    for hh in range(MLA_HEADS):
        sl = slice(hh * LANE, (hh + 1) * LANE)
        cq, sq = tabs[hh % 2]
        blk = qa[:, sl]
        qq_o[0, :, sl] = (blk * cq + pltpu.roll(blk, LANE - MLA_ROPE, 1) * sq).astype(BF16)
    kvn = _rms(seg(C_DKV, C_GA), kvg_ref[...]).astype(BF16)
    kv = _dot(kvn, wkv_ref[...])
    kp = seg(C_KPE, C_END) * kc_ref[...]
    kpe_e = jnp.where(lane < MLA_ROPE, kp + pltpu.roll(kp, LANE - MLA_ROPE, 1), 0.0)
    kpe_o = pltpu.roll(kpe_e, MLA_NOPE, 1)
    one_e = jnp.where(lane == MLA_V, 1.0, 0.0)
    one_o = jnp.where(lane == 0, 1.0, 0.0)
    for hh in range(MLA_HEADS):
        sl = slice(hh * LANE, (hh + 1) * LANE)
        blk = kv[:, sl]
        if hh % 2 == 0:
            kk_o[0, sl, :] = jnp.where(lane >= MLA_V, blk, kpe_e).T.astype(BF16)
            vv_o[0, :, sl] = jnp.where(lane < MLA_V, blk, one_e).astype(BF16)
        else:
            kk_o[0, sl, :] = jnp.where(lane < MLA_NOPE, blk, kpe_o).T.astype(BF16)
            vv_o[0, :, sl] = jnp.where(lane >= MLA_NOPE, blk, one_o).astype(BF16)


def _inproj(z, modl, g1, w, lbf, lbb, qg, kvg, wq, wkv, cqe, sqe, cqo, sqo, kc, nct):
    z_in, z_specs, b_, lt, d = _z_inputs(z, nct)
    nt = lt // TM
    row = lambda w_: pl.BlockSpec((1, TM, w_), lambda b, i: (b, i, 0))
    full = lambda a: pl.BlockSpec(a.shape, lambda b, i: (0,) * a.ndim)
    tab = pl.BlockSpec((TM, LANE), lambda b, i: (i, 0))
    f32o = lambda w_: jax.ShapeDtypeStruct((b_, lt, w_), F32)
    bf16o = lambda w_: jax.ShapeDtypeStruct((b_, lt, w_), BF16)
    return pl.pallas_call(
        functools.partial(_inproj_kernel, nz=len(z_in), nct=nct),
        grid=(b_, nt),
        in_specs=z_specs + [
                  pl.BlockSpec((1, 1, SUB, d), lambda b, i: (b, jnp.minimum(i // nct, 1), 0, 0)),
                  full(g1), full(w), full(lbf), full(lbb), full(qg), full(kvg),
                  full(wq), full(wkv), tab, tab, tab, tab, tab],
        out_specs=[row(512)] * 7 + [row(1024)] * 3
        + [pl.BlockSpec((1, MLA_HEADS * LANE, TM), lambda b, i: (b, 0, i)), row(1024)],
        out_shape=[bf16o(512), bf16o(512), f32o(512), bf16o(512), f32o(512), bf16o(512), bf16o(512)]
        + [bf16o(1024)] * 3 + [jax.ShapeDtypeStruct((b_, MLA_HEADS * LANE, lt), BF16), bf16o(1024)],
        compiler_params=_cparams(2),
        name="inproj",
    )(*z_in, modl, g1, w, lbf, lbb, qg, kvg, wq, wkv, cqe, sqe, cqo, sqo, kc)


N_LEVELS = 7


def _chunk_tables(reverse):
    c = CHUNK
    t = np.arange(c)[:, None]
    u = np.arange(c)[None, :]
    tri = ((u >= t) if reverse else (u <= t)).astype(np.float32)
    lvl = np.full((c, c), N_LEVELS + 1, np.int32)
    lvl[t == u] = 0
    for l in range(N_LEVELS, 0, -1):
        m = 1 << l
        same = (t // m) == (u // m)
        lvl[same & ((u > t) if reverse else (u < t))] = l
    return jnp.asarray(tri, dtype=BF16), jnp.asarray(lvl)


def _level_ref(bs_ref, hsl, l, reverse):
    w = HG_DK
    sub = lax.broadcasted_iota(jnp.int32, (SUB, w), 0)
    off = 0 if reverse else -1
    m = 1 << l

    def row(r):
        return jnp.broadcast_to(bs_ref[r:r + 1, hsl], (SUB, w))

    pieces = []
    for j in range(CHUNK // SUB):
        base = j * SUB
        if m >= SUB:
            pieces.append(row((base // m) * m + m // 2 + off))
        else:
            p = row(base + m // 2 + off)
            for i in range(1, SUB // m):
                p = jnp.where(sub < i * m, p, row(base + i * m + m // 2 + off))
            pieces.append(p)
    return jnp.concatenate(pieces, axis=0)


def _hgrn_kernel(qf_ref, vf_ref, lff_ref, kf_ref, qb_ref, vb_ref, lfb_ref, kb_ref,
                 trif_ref, lvlf_ref, trib_ref, lvlb_ref, of_ref, ob_ref, sf_ref, sb_ref, bsf_ref, bsb_ref):
    @pl.when(pl.program_id(1) == 0)
    def _():
        sf_ref[...] = jnp.zeros_like(sf_ref)
        sb_ref[...] = jnp.zeros_like(sb_ref)

    _hgrn_tile(qf_ref, vf_ref, lff_ref, kf_ref, trif_ref, lvlf_ref, of_ref, sf_ref, bsf_ref, False)
    _hgrn_tile(qb_ref, vb_ref, lfb_ref, kb_ref, trib_ref, lvlb_ref, ob_ref, sb_ref, bsb_ref, True)


def _hgrn_tile(q_ref, v_ref, lf_ref, k_ref, tri_ref, lvl_ref, o_ref, s_ref, bs_ref, reverse):
    c = CHUNK
    nch = TM // c
    e = lambda x: jnp.exp2(x).astype(BF16)

    lvl = lvl_ref[...]
    for ci in range(nch):
        cidx = (nch - 1 - ci) if reverse else ci
        rows = slice(cidx * c, (cidx + 1) * c)
        lf = lf_ref[0, rows, :]
        hi = lf.astype(BF16)
        lo = (lf - hi.astype(F32)).astype(BF16)
        bsc = bs_ref.at[ci]
        bsc[...] = _dot(tri_ref[...], hi) + _dot(tri_ref[...], lo)
        for hh in range(HG_HEADS):
            hsl = slice(hh * HG_DK, (hh + 1) * HG_DK)
            b = bsc[:, hsl]
            btot = b[0:1] if reverse else b[c - 1:c]
            q = q_ref[0, rows, hsl]
            k = k_ref[0, rows, hsl]
            ix = (0, rows, hsl)
            v = v_ref[ix]
            st = s_ref[hh]
            att = jnp.where(lvl == 0, _dot_nt(q, k), 0.0)
            for l in range(1, N_LEVELS + 1):
                f = e(-jnp.abs(b - _level_ref(bsc, hsl, l, reverse)))
                att = jnp.where(lvl == l, _dot_nt(q * f, k * f), att)
            o = _dot_nt(q * e(b), st.astype(BF16)) + _dot(att.astype(BF16), v)
            s_ref[hh] = st * jnp.exp2(btot) + _dot_tn(v, k * e(btot - b))
            o_ref[ix] = o.astype(o_ref.dtype)


def _hgrn(q, v, lf_f, k_f, lf_b, k_b, nct):
    b_, lt, w = q.shape
    nt = lt // TM
    rtile = lambda i: jnp.where(i < nct, nct - 1 - i, nt - 1 - (i - nct))
    rowf = pl.BlockSpec((1, TM, w), lambda b, i: (b, i, 0))
    rowb = pl.BlockSpec((1, TM, w), lambda b, i: (b, rtile(i), 0))
    full = lambda a: pl.BlockSpec(a.shape, lambda b, i: (0,) * a.ndim)
    trif, lvlf = _chunk_tables(False)
    trib, lvlb = _chunk_tables(True)
    o_sh = jax.ShapeDtypeStruct((b_, lt, w), BF16)
    state = [pltpu.VMEM((HG_HEADS, HG_DK, HG_DK), F32)] * 2
    return pl.pallas_call(
        _hgrn_kernel,
        grid=(b_, nt),
        in_specs=[rowf] * 4 + [rowb] * 4 + [full(trif), full(lvlf), full(trib), full(lvlb)],
        out_specs=[rowf, rowb],
        out_shape=[o_sh, o_sh],
        scratch_shapes=state + [pltpu.VMEM((TM // CHUNK, CHUNK, w), F32)] * 2,
        compiler_params=_cparams(2),
        name="hgrn_scan",
    )(q, v, lf_f, k_f, q, v, lf_b, k_b, trif, lvlf, trib, lvlb)


KV_CHUNK = 256


def _attn_kernel(q_ref, k_ref, v_ref, o_ref, *, n_ctx, nct):
    i = pl.program_id(2)
    lane = lax.broadcasted_iota(jnp.int32, (TM, LANE), 1)

    def run(nk):
        outs = []
        for hh in range(2):
            sl = slice(hh * LANE, (hh + 1) * LANE)
            q = q_ref[0, :, sl]
            m = acc = None
            edges = [0] + list(range(n_ctx, nk + 1, KV_CHUNK))
            for c in range(len(edges) - 1):
                ks = slice(edges[c], edges[c + 1])
                s = _dot(q, k_ref[0, sl, ks])
                mc = jnp.max(s, axis=-1, keepdims=True)
                if c == 0:
                    m = mc
                    acc = _dot(jnp.exp2(s - m).astype(BF16), v_ref[0, ks, sl])
                else:
                    m_new = jnp.maximum(m, mc)
                    acc = acc * jnp.exp2(m - m_new) + _dot(jnp.exp2(s - m_new).astype(BF16), v_ref[0, ks, sl])
                    m = m_new
            den = acc[:, MLA_V:MLA_V + 1] if hh == 0 else acc[:, 0:1]
            outs.append(acc / den)
        o_ref[0] = jnp.where(lane < MLA_V, outs[0], outs[1]).astype(o_ref.dtype)

    @pl.when(i < nct)
    def _():
        run(n_ctx)

    @pl.when(i >= nct)
    def _():
        run(k_ref.shape[2])


def _attention(qq, kk, vv, n_ctx):
    b_, lt, _ = qq.shape
    nt = lt // TM
    npair = MLA_HEADS // 2
    return pl.pallas_call(
        functools.partial(_attn_kernel, n_ctx=n_ctx, nct=n_ctx // TM),
        grid=(b_, npair, nt),
        in_specs=[pl.BlockSpec((1, TM, 2 * LANE), lambda b, j, i: (b, i, j)),
                  pl.BlockSpec((1, 2 * LANE, lt), lambda b, j, i: (b, j, 0)),
                  pl.BlockSpec((1, lt, 2 * LANE), lambda b, j, i: (b, 0, j))],
        out_specs=pl.BlockSpec((1, TM, LANE), lambda b, j, i: (b, i, j)),
        out_shape=jax.ShapeDtypeStruct((b_, lt, npair * LANE), BF16),
        compiler_params=_cparams(3),
        name="mla_attention",
    )(qq, kk, vv)


def _route(sel):
    r = lambda x, e: x[e:e + 1]
    best = None
    for g in range(N_GROUPS):
        u = [r(sel, EPG * g + j) for j in range(EPG)]
        gs = None
        for a in range(EPG):
            for b in range(a + 1, EPG):
                pr = u[a] + u[b]
                gs = pr if gs is None else jnp.maximum(gs, pr)
        if best is None:
            best, gi = gs, jnp.zeros_like(gs, dtype=jnp.int32)
        else:
            upd = gs > best
            best = jnp.where(upd, gs, best)
            gi = jnp.where(upd, g, gi)

    def pick(x, j):
        out = r(x, j)
        for g in range(1, N_GROUPS):
            out = jnp.where(gi == g, r(x, EPG * g + j), out)
        return out

    u = [pick(sel, j) for j in range(EPG)]

    def argmax4(vals):
        bv, bi = vals[0], jnp.zeros_like(gi)
        for j in range(1, EPG):
            upd = vals[j] > bv
            bv = jnp.where(upd, vals[j], bv)
            bi = jnp.where(upd, j, bi)
        return bi

    l1 = argmax4(u)
    l2 = argmax4([jnp.where(l1 == j, -jnp.inf, u[j]) for j in range(EPG)])
    lo = jnp.minimum(l1, l2)
    hi = jnp.maximum(l1, l2)
    pair = jnp.where(lo == 0, jnp.where(hi == 1, 0, hi), jnp.where(lo == 1, jnp.where(hi == 2, 1, 4), 5))
    return (gi * 6 + pair).astype(F32)


CLS_ROWS = 32


def _merge_kernel(*refs, nz, nct):
    z_refs = refs[:nz]
    (of_ref, ob_ref, g_ref, gain_ref, m_ref, gas_ref, gbs_ref, mod_ref, wpa_ref, wpb_ref,
     wo_ref, g2_ref, wr_ref, rb_ref, triu_ref, zn_o, ht_o, route_o, cnt_o, run_ref) = refs[nz:]

    @pl.when((pl.program_id(0) == 0) & (pl.program_id(1) == 0))
    def _():
        run_ref[...] = jnp.zeros_like(run_ref)

    mod = mod_ref[0, 0]
    heads = []
    for hh in range(HG_HEADS):
        hsl = slice(hh * HG_DK, (hh + 1) * HG_DK)
        o = of_ref[0, :, hsl].astype(F32) + ob_ref[0, :, hsl].astype(F32)
        g = g_ref[0, :, hsl].astype(F32)
        heads.append(_rms(o, gain_ref[:, hsl]) * (g * jax.nn.sigmoid(g)))
    pa = _dot(jnp.concatenate(heads, axis=1).astype(BF16), wpa_ref[...])
    pb = _dot(m_ref[0].astype(BF16), wpb_ref[...])
    mixed = gas_ref[0] * pa + gbs_ref[0] * pb
    zn = _z_tile(z_refs, nct) + mod[2:3] * _dot(mixed.astype(BF16), wo_ref[...])
    zn_o[0] = zn
    h2 = _rms(zn, g2_ref[...]) * (1.0 + mod[4:5]) + mod[3:4]
    _tiles_store(ht_o, h2)
    logits = _dot_f32(h2, wr_ref[...], passes=3)
    sc = jax.nn.sigmoid(logits.T[:N_EXPERTS])
    cls = _route(sc + rb_ref[...])
    onehot = (lax.broadcasted_iota(jnp.int32, (CLS_ROWS, TM), 0) == cls.astype(jnp.int32))
    pref = _dot(onehot.astype(BF16), triu_ref[...])
    rank = jnp.sum(jnp.where(onehot, pref + run_ref[:, 0:1], 0.0), axis=0, keepdims=True) - 1.0
    run_ref[...] = run_ref[...] + pref[:, TM - 1:TM]
    cnt_o[...] = run_ref[...]
    route_o[0] = jnp.concatenate([cls, rank] + [jnp.zeros_like(cls)] * (SUB - 2), axis=0)


def _merge(o_f, o_b, g, gain, m, gas, gbs, z, modl, wpa, wpb, wo, g2, wr, rbb, nct):
    z_in, z_specs, b_, lt, d = _z_inputs(z, nct)
    nt = lt // TM
    row = lambda w_: pl.BlockSpec((1, TM, w_), lambda b, i: (b, i, 0))
    full = lambda x: pl.BlockSpec(x.shape, lambda b, i: (0,) * x.ndim)
    triu = jnp.asarray(np.triu(np.ones((TM, TM), np.float32)), dtype=BF16)
    return pl.pallas_call(
        functools.partial(_merge_kernel, nz=len(z_in), nct=nct),
        grid=(b_, nt),
        in_specs=z_specs + [row(512), row(512), row(512), full(gain), row(512), row(d), row(d),
                  pl.BlockSpec((1, 1, SUB, d), lambda b, i: (b, jnp.minimum(i // nct, 1), 0, 0)),
                  full(wpa), full(wpb), full(wo), full(g2), full(wr), full(rbb), full(triu)],
        out_specs=[row(d), pl.BlockSpec((TM * SUB, LANE), lambda b, i: (b * nt + i, 0)),
                   pl.BlockSpec((1, SUB, TM), lambda b, i: (b, 0, i)),
                   pl.BlockSpec((CLS_ROWS, LANE), lambda b, i: (0, 0))],
        out_shape=[jax.ShapeDtypeStruct((b_, lt, d), F32),
                   jax.ShapeDtypeStruct((b_ * lt * SUB, LANE), F32),
                   jax.ShapeDtypeStruct((b_, SUB, lt), F32),
                   jax.ShapeDtypeStruct((CLS_ROWS, LANE), F32)],
        scratch_shapes=[pltpu.VMEM((CLS_ROWS, LANE), F32)],
        compiler_params=_cparams(2),
        name="merge_route",
    )(*z_in, o_f, o_b, g, gain, m, gas, gbs, modl, wpa, wpb, wo, g2, wr, rbb, triu)


def _tiles_store(ref, x, lead=()):
    n = x.shape[0]
    for j in range(SUB):
        ref[lead + (pl.ds(j, n, stride=SUB), slice(None))] = x[:, j * LANE:(j + 1) * LANE]


def _tiles_load(ref, n, lead=()):
    return jnp.concatenate([ref[lead + (pl.ds(j, n, stride=SUB), slice(None))] for j in range(SUB)], axis=1)


def _tile_copy(src_hbm, idx_ref, buf, sem, slot, r):
    return pltpu.make_async_copy(src_hbm.at[pl.ds(pl.multiple_of(idx_ref[0, 0, r] * SUB, SUB), SUB)],
                                 buf.at[slot, pl.ds(pl.multiple_of(r * SUB, SUB), SUB)], sem.at[slot])


GATHER_UNROLL = 8


def _gather_start(src_hbm, idx_ref, buf, sem, slot, n):
    def body(r8, c):
        for u in range(GATHER_UNROLL):
            _tile_copy(src_hbm, idx_ref, buf, sem, slot, r8 * GATHER_UNROLL + u).start(priority=u % 2)
        return c
    lax.fori_loop(0, n // GATHER_UNROLL, body, 0)


def _gather_wait(src_hbm, buf, sem, slot):
    pltpu.make_async_copy(src_hbm.at[pl.ds(0, buf.shape[1])], buf.at[slot], sem.at[slot]).wait()


def _sort_kernel(pos_ref, ht_ref, init_hbm, hs_hbm, sem):
    del init_hbm

    def body(r8, c):
        for u in range(GATHER_UNROLL):
            r = r8 * GATHER_UNROLL + u
            pltpu.make_async_copy(
                ht_ref.at[pl.ds(pl.multiple_of(r * SUB, SUB), SUB)],
                hs_hbm.at[pl.ds(pl.multiple_of(pos_ref[0, 0, r] * SUB, SUB), SUB)], sem.at[0]).start(priority=u % 2)
        return c
    lax.fori_loop(0, TM // GATHER_UNROLL, body, 0)
    pltpu.make_async_copy(ht_ref, hs_hbm.at[pl.ds(0, TM * SUB)], sem.at[0]).wait()


def _sort_rows(ht, pos, n_sorted, init=None):
    nstep = pos.shape[0]
    if init is None:
        init = jnp.zeros((n_sorted * SUB, LANE), F32)
    return pl.pallas_call(
        _sort_kernel,
        grid=(nstep,),
        in_specs=[pl.BlockSpec((1, 1, TM), lambda t: (t, 0, 0), memory_space=pltpu.SMEM),
                  pl.BlockSpec((TM * SUB, LANE), lambda t: (t, 0)), pl.BlockSpec(memory_space=pl.ANY)],
        out_specs=pl.BlockSpec(memory_space=pl.ANY),
        out_shape=jax.ShapeDtypeStruct(init.shape, F32),
        scratch_shapes=[pltpu.SemaphoreType.DMA((1,))],
        input_output_aliases={2: 0},
        compiler_params=_cparams(1),
        name="class_sort",
    )(pos, ht, init)


def _moe_kernel(ea_ref, eb_ref, nu_ref, hs_ref, wrt_ref,
                wga_ref, wua_ref, wda_ref, wgb_ref, wub_ref, wdb_ref, y_ref, gu_a, dn_a, gu_b, dn_b):
    j = pl.program_id(0)
    jp = jnp.maximum(j - 1, 0)

    @pl.when((j == 0) | (ea_ref[j] != ea_ref[jp]))
    def _():
        gu_a[0] = wga_ref[0].astype(BF16)
        gu_a[1] = wua_ref[0].astype(BF16)
        dn_a[...] = wda_ref[0].astype(BF16)

    @pl.when((j == 0) | (eb_ref[j] != eb_ref[jp]))
    def _():
        gu_b[0] = wgb_ref[0].astype(BF16)
        gu_b[1] = wub_ref[0].astype(BF16)
        dn_b[...] = wdb_ref[0].astype(BF16)

    @pl.when(j < nu_ref[0])
    def _():
        x32 = _tiles_load(hs_ref, MOE_BM)
        x = x32.astype(BF16)

        def ffn(gu, dn, e):
            sc = jax.nn.sigmoid(jnp.sum(x32 * wrt_ref[pl.ds(e, 1), :], axis=-1, keepdims=True))
            act = jax.nn.silu(_dot(x, gu[0])) * _dot(x, gu[1])
            return sc, _dot(act.astype(BF16), dn[...])

        s_a, y_a = ffn(gu_a, dn_a, ea_ref[j])
        s_b, y_b = ffn(gu_b, dn_b, eb_ref[j])
        inv = 1.0 / (s_a + s_b)
        _tiles_store(y_ref, (s_a * inv) * y_a + (s_b * inv) * y_b)

    @pl.when(j >= nu_ref[0])
    def _():
        y_ref[...] = jnp.zeros_like(y_ref)


def _moe(hs, wrt, blk_ea, blk_eb, n_used, wg, wu, wd, layer):
    nblk = blk_ea.shape[0]
    d, de = wg.shape[2:]
    tiles = pl.BlockSpec((MOE_BM * SUB, LANE), lambda j, ea, eb, nu: (j, 0))
    wspec = lambda shp, which: pl.BlockSpec(
        (None, 1) + shp, (lambda j, ea, eb, nu: (layer, ea[j], 0, 0)) if which == 0
        else (lambda j, ea, eb, nu: (layer, eb[j], 0, 0)))
    return pl.pallas_call(
        _moe_kernel,
        grid_spec=pltpu.PrefetchScalarGridSpec(
            num_scalar_prefetch=3,
            grid=(nblk,),
            in_specs=[tiles,
                      pl.BlockSpec(wrt.shape, lambda j, ea, eb, nu: (0, 0)),
                      wspec((d, de), 0), wspec((d, de), 0), wspec((de, d), 0),
                      wspec((d, de), 1), wspec((d, de), 1), wspec((de, d), 1)],
            out_specs=tiles,
            scratch_shapes=[pltpu.VMEM((2, d, de), BF16), pltpu.VMEM((de, d), BF16)] * 2),
        out_shape=jax.ShapeDtypeStruct(hs.shape, F32),
        compiler_params=_cparams(1),
        name="moe_ffn",
    )(blk_ea, blk_eb, n_used, hs, wrt, wg, wu, wd, wg, wu, wd)


def _post_kernel(*refs, final):
    if final:
        idx_ref, idxn_ref, y_hbm, zn_ref, mod_ref, fg_ref, o_ref, buf, sem = refs
    else:
        idx_ref, idxn_ref, y_hbm, zn_ref, mod_ref, o_ref, buf, sem = refs
    t = pl.program_id(0)
    nstep = pl.num_programs(0)
    slot = t % 2

    @pl.when(t == 0)
    def _():
        _gather_start(y_hbm, idx_ref, buf, sem, 0, TM)

    @pl.when(t + 1 < nstep)
    def _():
        _gather_start(y_hbm, idxn_ref, buf, sem, 1 - slot, TM)

    _gather_wait(y_hbm, buf, sem, slot)
    z = zn_ref[0] + mod_ref[0, 0][5:6] * _tiles_load(buf, TM, (slot,))
    if final:
        z = _rms(z, fg_ref[...])
    o_ref[0] = z


def _post(y, pos, zn, modl, nct, final_g=None):
    b_, lt, d = zn.shape
    nt = lt // TM
    final = final_g is not None
    t0 = nct if final else 0
    ntl = nt - t0
    nstep = b_ * ntl
    idx_spec = lambda f: pl.BlockSpec((1, 1, TM), f, memory_space=pltpu.SMEM)
    ins = [pos, pos, y, zn, modl]
    specs = [idx_spec(lambda t: (t, 0, 0)),
             idx_spec(lambda t: (jnp.minimum(t + 1, nstep - 1), 0, 0)),
             pl.BlockSpec(memory_space=pl.ANY),
             pl.BlockSpec((1, TM, d), lambda t: (t // ntl, t % ntl + t0, 0)),
             pl.BlockSpec((1, 1, SUB, d), lambda t: (t // ntl, jnp.minimum((t % ntl + t0) // nct, 1), 0, 0))]
    if final:
        ins.append(final_g)
        specs.append(pl.BlockSpec(final_g.shape, lambda t: (0, 0)))
    return pl.pallas_call(
        functools.partial(_post_kernel, final=final),
        grid=(nstep,),
        in_specs=specs,
        out_specs=pl.BlockSpec((1, TM, d), lambda t: (t // ntl, t % ntl, 0)),
        out_shape=jax.ShapeDtypeStruct((b_, ntl * TM, d), F32),
        scratch_shapes=[pltpu.VMEM((2, TM * SUB, LANE), F32), pltpu.SemaphoreType.DMA((2,))],
        compiler_params=_cparams(1),
        name="unsort_residual",
    )(*ins)


_PAIR_SLOTS = [(0, 1), (2, 1), (2, 0), (3, 0), (3, 1), (3, 2)]
_CLS_EA = np.array([EPG * g + p[0] for g in range(N_GROUPS) for p in _PAIR_SLOTS], np.int32)
_CLS_EB = np.array([EPG * g + p[1] for g in range(N_GROUPS) for p in _PAIR_SLOTS], np.int32)


def _sort_plan(cls, rank, counts):
    n = cls.shape[0]
    nblk = n // MOE_BM + N_CLASSES
    padded = (counts + MOE_BM - 1) // MOE_BM * MOE_BM
    pad_end = jnp.cumsum(padded)
    pos = (pad_end - padded)[cls] + rank
    starts = jnp.arange(nblk, dtype=jnp.int32)[:, None] * MOE_BM
    blk_cls = jnp.minimum(jnp.sum((pad_end[None, :] <= starts).astype(jnp.int32), axis=1), N_CLASSES - 1)
    n_used = (pad_end[-1] // MOE_BM).astype(jnp.int32).reshape(1)
    return pos, jnp.asarray(_CLS_EA)[blk_cls], jnp.asarray(_CLS_EB)[blk_cls], n_used


def _rot_cols(w):
    return jnp.concatenate([-w[:, 8:16], w[:, 0:8], -w[:, 24:32], w[:, 16:24]], axis=1)


def _prep_w_in(w):
    kpe = w[:, 3200:3232]
    pad = jnp.zeros((w.shape[0], C_END - C_KPE - 2 * MLA_ROPE), w.dtype)
    return jnp.concatenate([w[:, :3200], w[:, 3232:], kpe, _rot_cols(kpe), pad], axis=1).astype(BF16)


def _prep_mla(w_uq, w_ukv):
    odd = (jnp.arange(MLA_HEADS) % 2 == 1)[None, :, None]
    qh = w_uq.reshape(MLA_Q_RANK, MLA_HEADS, MLA_NOPE + MLA_ROPE)
    nope, pe = qh[:, :, :MLA_NOPE], qh[:, :, MLA_NOPE:]
    pe_sw = _rot_cols(pe.reshape(MLA_Q_RANK * MLA_HEADS, MLA_ROPE)).reshape(MLA_Q_RANK, MLA_HEADS, MLA_ROPE)
    wq = jnp.where(odd, jnp.concatenate([nope, pe, pe_sw], axis=2), jnp.concatenate([pe, pe_sw, nope], axis=2))
    kvh = w_ukv.reshape(MLA_KV_RANK, MLA_HEADS, MLA_NOPE + MLA_V)
    kn, vh = kvh[:, :, :MLA_NOPE], kvh[:, :, MLA_NOPE:]
    wkv = jnp.where(odd, jnp.concatenate([kn, vh], axis=2), jnp.concatenate([vh, kn], axis=2))
    return (wq.reshape(MLA_Q_RANK, MLA_HEADS * LANE).astype(BF16),
            wkv.reshape(MLA_KV_RANK, MLA_HEADS * LANE).astype(BF16))


def _rope_tables(n_ctx, n_lat):
    rows = n_lat // GRID_W
    rowp = jnp.broadcast_to(jnp.arange(rows, dtype=F32)[:, None], (rows, GRID_W)).reshape(-1)
    colp = jnp.broadcast_to(jnp.arange(GRID_W, dtype=F32)[None, :], (rows, GRID_W)).reshape(-1)
    axis_dim = MLA_ROPE // 2
    inv_freq = ROPE_THETA ** (-jnp.arange(0, axis_dim, 2, dtype=F32) / axis_dim)
    ar, ac = rowp[:, None] * inv_freq, colp[:, None] * inv_freq
    cos32 = jnp.concatenate([jnp.cos(ar), jnp.cos(ar), jnp.cos(ac), jnp.cos(ac)], axis=1)
    sin32 = jnp.concatenate([jnp.sin(ar), jnp.sin(ar), jnp.sin(ac), jnp.sin(ac)], axis=1)
    cos32 = jnp.concatenate([jnp.ones((n_ctx, MLA_ROPE), F32), cos32], axis=0)
    sin32 = jnp.concatenate([jnp.zeros((n_ctx, MLA_ROPE), F32), sin32], axis=0)
    lt = n_ctx + n_lat
    one, zero = jnp.ones((lt, MLA_NOPE), F32), jnp.zeros((lt, MLA_NOPE), F32)
    z32 = jnp.zeros((lt, MLA_ROPE), F32)
    sc = MLA_SCALE * LOG2E
    cqe = jnp.concatenate([cos32, z32, one], axis=1) * sc
    sqe = jnp.concatenate([sin32, z32, zero], axis=1) * sc
    cqo = jnp.concatenate([one, cos32, z32], axis=1) * sc
    sqo = jnp.concatenate([zero, sin32, z32], axis=1) * sc
    kc = jnp.concatenate([cos32, sin32, zero], axis=1)
    return cqe, sqe, cqo, sqo, kc


def _lower_bounds(gamma):
    p = jnp.cumsum(jax.nn.softmax(gamma.astype(F32), axis=0), axis=0)
    return p - p[0:1]


def kernel(x, c, ctx, c_ctx, w_mod, b_mod, norm1_g, norm2_g, w_in, gamma_fwd, gamma_bwd, hg_norm_g,
           q_norm_g, kv_norm_g, w_uq, w_ukv, w_pa, w_pb, w_o, w_router, router_bias,
           w_gate_e, w_up_e, w_down_e, final_g):
    b_, n_lat, d = x.shape
    n_ctx = ctx.shape[1]
    depth = w_mod.shape[0]
    assert n_ctx % TM == 0 and n_lat % TM == 0 and n_lat % GRID_W == 0 and b_ + 1 <= SUB
    nct = n_ctx // TM
    lt = n_ctx + n_lat
    nt = lt // TM

    cc = jnp.concatenate([c, c_ctx[None, :], jnp.zeros((SUB - b_ - 1, d), F32)], axis=0)
    mod = _modulation(cc, w_mod, b_mod).reshape(depth, SUB, 6, d)
    mod_x = mod[:, :b_]
    mod_c = jnp.broadcast_to(mod[:, b_:b_ + 1], mod_x.shape)
    mod_t = jnp.stack([mod_c, mod_x], axis=2)
    mod_t = jnp.concatenate([mod_t, jnp.zeros((depth, b_, 2, SUB - 6, d), F32)], axis=3)

    cqe, sqe, cqo, sqo, kc = _rope_tables(n_ctx, n_lat)
    lbf, lbb = _lower_bounds(gamma_fwd), _lower_bounds(gamma_bwd)
    wr = jnp.concatenate([w_router, jnp.zeros((d, LANE - N_EXPERTS), F32)], axis=1)
    rbb = jnp.broadcast_to(router_bias.astype(F32)[:, None], (N_EXPERTS, TM))

    z = (ctx, x)
    out = hs = None
    for l in range(depth):
        last = l == depth - 1
        wq, wkv = _prep_mla(w_uq[l], w_ukv[l])
        (q, v, lff, kf, lfb, kb, g, gas, gbs, qq, kk, vv) = _inproj(
            z, mod_t[l], norm1_g[l][None], _prep_w_in(w_in[l]), lbf[l][None], lbb[l][None],
            q_norm_g[l][None], kv_norm_g[l][None], wq, wkv, cqe, sqe, cqo, sqo, kc, nct)
        o_f, o_b = _hgrn(q, v, lff, kf, lfb, kb, nct)
        m = _attention(qq, kk, vv, n_ctx)
        zn, ht, route, cnt = _merge(o_f, o_b, g, hg_norm_g[l][None], m, gas, gbs, z, mod_t[l],
                                    w_pa[l].astype(BF16), w_pb[l].astype(BF16), w_o[l].astype(BF16),
                                    norm2_g[l][None], wr, rbb, nct)
        route = route[:, 0:2, :].astype(jnp.int32)
        pos, blk_ea, blk_eb, n_used = _sort_plan(route[:, 0].reshape(-1), route[:, 1].reshape(-1),
                                                 cnt[:N_CLASSES, 0].astype(jnp.int32))
        pos = pos.reshape(b_, nt, 1, TM)
        hs = _sort_rows(ht, pos.reshape(-1, 1, TM), blk_ea.shape[0] * MOE_BM, hs)
        y = _moe(hs, w_router.T, blk_ea, blk_eb, n_used, w_gate_e, w_up_e, w_down_e, l)
        if last:
            out = _post(y, pos[:, nct:].reshape(-1, 1, TM), zn, mod_t[l], nct, final_g[None])
        else:
            z = _post(y, pos.reshape(-1, 1, TM), zn, mod_t[l], nct)
    return out
```

```python
import functools

import numpy as np
import jax
import jax.numpy as jnp
from jax import lax
from jax.experimental import pallas as pl
from jax.experimental.pallas import tpu as pltpu

F32 = jnp.float32
BF16 = jnp.bfloat16

EPS = 1e-6
GRID_W = 64
ROPE_THETA = 10000.0
HG_HEADS = 4
HG_DK = 128
HG_WIDTH = 512
MLA_HEADS = 8
MLA_NOPE = 64
MLA_ROPE = 32
MLA_V = 64
MLA_Q_RANK = 384
MLA_KV_RANK = 256
MLA_SCALE = (MLA_NOPE + MLA_ROPE) ** -0.5
LOG2E = 1.4426950408889634
N_EXPERTS = 16
N_GROUPS = 4
EPG = 4
N_CLASSES = N_GROUPS * 6

LANE = 128
SUB = 8
TM = 256
CHUNK = 128
MOE_BM = 512
VMEM_LIMIT = 56 * 1024 * 1024

C_Q, C_I, C_FF, C_FB, C_G, C_DQ, C_DKV, C_GA, C_GB, C_KPE, C_END = (
    0, 512, 1024, 1536, 2048, 2560, 2944, 3200, 4224, 5248, 5376)


def _cparams(n_axes):
    return pltpu.CompilerParams(dimension_semantics=("arbitrary",) * n_axes,
                                vmem_limit_bytes=VMEM_LIMIT)


def _rms(x, g):
    y = x * lax.rsqrt(jnp.mean(x * x, axis=-1, keepdims=True) + EPS)
    return y * g


def _dot(a, b):
    return jnp.dot(a, b, preferred_element_type=F32)


def _dot_nt(a, b):
    return lax.dot_general(a, b, (((1,), (1,)), ((), ())), preferred_element_type=F32)


def _dot_tn(a, b):
    return lax.dot_general(a, b, (((0,), (0,)), ((), ())), preferred_element_type=F32)


def _split2(x):
    hi = x.astype(BF16)
    lo = (x - hi.astype(F32)).astype(BF16)
    return hi, lo


def _dot_f32(a, b):
    a0, a1 = _split2(a)
    b0, b1 = _split2(b)
    return _dot(a0, b0) + (_dot(a0, b1) + _dot(a1, b0))


def _mod_kernel(c_ref, w_ref, b_ref, o_ref):
    c = c_ref[...]
    s = c * jax.nn.sigmoid(c)
    o_ref[0] = _dot_f32(s, w_ref[0]) + b_ref[0]


def _modulation(cc, w_mod, b_mod):
    depth, d, n6 = w_mod.shape
    nb = 1536
    return pl.pallas_call(
        _mod_kernel,
        grid=(depth, n6 // nb),
        in_specs=[pl.BlockSpec((SUB, d), lambda l, j: (0, 0)),
                  pl.BlockSpec((1, d, nb), lambda l, j: (l, 0, j)),
                  pl.BlockSpec((1, 1, nb), lambda l, j: (l, 0, j))],
        out_specs=pl.BlockSpec((1, SUB, nb), lambda l, j: (l, 0, j)),
        out_shape=jax.ShapeDtypeStruct((depth, SUB, n6), F32),
        compiler_params=_cparams(2),
        name="modulation",
    )(cc, w_mod, b_mod.reshape(depth, 1, n6))


def _z_inputs(z, nct):
    if isinstance(z, tuple):
        ctx, x = z
        b_, n_lat, d = x.shape
        specs = [pl.BlockSpec((1, TM, d), lambda b, i: (b, jnp.minimum(i, nct - 1), 0)),
                 pl.BlockSpec((1, TM, d), lambda b, i: (b, jnp.maximum(i - nct, 0), 0))]
        return [ctx, x], specs, b_, ctx.shape[1] + n_lat, d
    b_, lt, d = z.shape
    return [z], [pl.BlockSpec((1, TM, d), lambda b, i: (b, i, 0))], b_, lt, d


def _z_tile(z_refs, nct):
    if len(z_refs) == 2:
        return jnp.where(pl.program_id(1) < nct, z_refs[0][0], z_refs[1][0])
    return z_refs[0][0]


def _inproj_kernel(*refs, nz, nct):
    z_refs = refs[:nz]
    (mod_ref, g1_ref, w_ref, lbf_ref, lbb_ref, qg_ref, kvg_ref,
     wq_ref, wkv_ref, cqe_ref, sqe_ref, cqo_ref, sqo_ref, kc_ref,
     q_o, v_o, lff_o, kf_o, lfb_o, kb_o, g_o, gas_o, gbs_o, qq_o, kk_o, vv_o) = refs[nz:]
    mod = mod_ref[0, 0]
    h = _rms(_z_tile(z_refs, nct), g1_ref[...]) * (1.0 + mod[1:2]) + mod[0:1]
    hb = h.astype(BF16)

    p = _dot(hb, w_ref[...])

    def seg(a, b):
        return p[:, a:b]

    q_o[0] = seg(C_Q, C_I).astype(BF16)
    v_o[0] = seg(C_I, C_FF).astype(BF16)
    for a, lb_ref, lf_o, k_o in ((C_FF, lbf_ref, lff_o, kf_o), (C_FB, lbb_ref, lfb_o, kb_o)):
        lb = lb_ref[...]
        f = lb + (1.0 - lb) * jax.nn.sigmoid(seg(a, a + HG_WIDTH))
        lf_o[0] = jnp.log(f) * LOG2E
        k_o[0] = (1.0 - f).astype(BF16)
    g_o[0] = seg(C_G, C_DQ).astype(BF16)
    gas_o[0] = jax.nn.sigmoid(seg(C_GA, C_GB)).astype(BF16)
    gbs_o[0] = jax.nn.sigmoid(seg(C_GB, C_KPE)).astype(BF16)

    lane = lax.broadcasted_iota(jnp.int32, (TM, LANE), 1)
    qn = _rms(seg(C_DQ, C_DKV), qg_ref[...]).astype(BF16)
    qa = _dot(qn, wq_ref[...])
    tabs = ((cqe_ref[...], sqe_ref[...]), (cqo_ref[...], sqo_ref[...]))
    for hh in range(MLA_HEADS):
        sl = slice(hh * LANE, (hh + 1) * LANE)
        cq, sq = tabs[hh % 2]
        blk = qa[:, sl]
        qq_o[0, :, sl] = (blk * cq + pltpu.roll(blk, LANE - MLA_ROPE, 1) * sq).astype(BF16)
    kvn = _rms(seg(C_DKV, C_GA), kvg_ref[...]).astype(BF16)
    kv = _dot(kvn, wkv_ref[...])
    kp = seg(C_KPE, C_END) * kc_ref[...]
    kpe_e = jnp.where(lane < MLA_ROPE, kp + pltpu.roll(kp, LANE - MLA_ROPE, 1), 0.0)
    kpe_o = pltpu.roll(kpe_e, MLA_NOPE, 1)
    one_e = jnp.where(lane == MLA_V, 1.0, 0.0)
    one_o = jnp.where(lane == 0, 1.0, 0.0)
    for hh in range(MLA_HEADS):
        sl = slice(hh * LANE, (hh + 1) * LANE)
        blk = kv[:, sl]
        if hh % 2 == 0:
            kk_o[0, sl, :] = jnp.where(lane >= MLA_V, blk, kpe_e).T.astype(BF16)
            vv_o[0, :, sl] = jnp.where(lane < MLA_V, blk, one_e).astype(BF16)
        else:
            kk_o[0, sl, :] = jnp.where(lane < MLA_NOPE, blk, kpe_o).T.astype(BF16)
            vv_o[0, :, sl] = jnp.where(lane >= MLA_NOPE, blk, one_o).astype(BF16)


def _inproj(z, modl, g1, w, lbf, lbb, qg, kvg, wq, wkv, cqe, sqe, cqo, sqo, kc, nct):
    z_in, z_specs, b_, lt, d = _z_inputs(z, nct)
    nt = lt // TM
    row = lambda w_: pl.BlockSpec((1, TM, w_), lambda b, i: (b, i, 0))
    full = lambda a: pl.BlockSpec(a.shape, lambda b, i: (0,) * a.ndim)
    tab = pl.BlockSpec((TM, LANE), lambda b, i: (i, 0))
    f32o = lambda w_: jax.ShapeDtypeStruct((b_, lt, w_), F32)
    bf16o = lambda w_: jax.ShapeDtypeStruct((b_, lt, w_), BF16)
    return pl.pallas_call(
        functools.partial(_inproj_kernel, nz=len(z_in), nct=nct),
        grid=(b_, nt),
        in_specs=z_specs + [
                  pl.BlockSpec((1, 1, SUB, d), lambda b, i: (b, jnp.minimum(i // nct, 1), 0, 0)),
                  full(g1), full(w), full(lbf), full(lbb), full(qg), full(kvg),
                  full(wq), full(wkv), tab, tab, tab, tab, tab],
        out_specs=[row(512)] * 7 + [row(1024)] * 3
        + [pl.BlockSpec((1, MLA_HEADS * LANE, TM), lambda b, i: (b, 0, i)), row(1024)],
        out_shape=[bf16o(512), bf16o(512), f32o(512), bf16o(512), f32o(512), bf16o(512), bf16o(512)]
        + [bf16o(1024)] * 3 + [jax.ShapeDtypeStruct((b_, MLA_HEADS * LANE, lt), BF16), bf16o(1024)],
        compiler_params=_cparams(2),
        name="inproj",
    )(*z_in, modl, g1, w, lbf, lbb, qg, kvg, wq, wkv, cqe, sqe, cqo, sqo, kc)


N_LEVELS = 7


def _chunk_tables(reverse):
    c = CHUNK
    t = np.arange(c)[:, None]
    u = np.arange(c)[None, :]
    tri = ((u >= t) if reverse else (u <= t)).astype(np.float32)
    lvl = np.full((c, c), N_LEVELS + 1, np.int32)
    lvl[t == u] = 0
    for l in range(N_LEVELS, 0, -1):
        m = 1 << l
        same = (t // m) == (u // m)
        lvl[same & ((u > t) if reverse else (u < t))] = l
    return jnp.asarray(tri, dtype=BF16), jnp.asarray(lvl)


def _level_ref(bs_ref, hsl, l, reverse):
    w = HG_DK
    sub = lax.broadcasted_iota(jnp.int32, (SUB, w), 0)
    off = 0 if reverse else -1
    m = 1 << l

    def row(r):
        return jnp.broadcast_to(bs_ref[r:r + 1, hsl], (SUB, w))

    pieces = []
    for j in range(CHUNK // SUB):
        base = j * SUB
        if m >= SUB:
            pieces.append(row((base // m) * m + m // 2 + off))
        else:
            p = row(base + m // 2 + off)
            for i in range(1, SUB // m):
                p = jnp.where(sub < i * m, p, row(base + i * m + m // 2 + off))
            pieces.append(p)
    return jnp.concatenate(pieces, axis=0)


def _hgrn_kernel(qf_ref, vf_ref, lff_ref, kf_ref, qb_ref, vb_ref, lfb_ref, kb_ref,
                 trif_ref, lvlf_ref, trib_ref, lvlb_ref, of_ref, ob_ref, sf_ref, sb_ref, bsf_ref, bsb_ref):
    @pl.when(pl.program_id(1) == 0)
    def _():
        sf_ref[...] = jnp.zeros_like(sf_ref)
        sb_ref[...] = jnp.zeros_like(sb_ref)

    _hgrn_tile(qf_ref, vf_ref, lff_ref, kf_ref, trif_ref, lvlf_ref, of_ref, sf_ref, bsf_ref, False)
    _hgrn_tile(qb_ref, vb_ref, lfb_ref, kb_ref, trib_ref, lvlb_ref, ob_ref, sb_ref, bsb_ref, True)


def _hgrn_tile(q_ref, v_ref, lf_ref, k_ref, tri_ref, lvl_ref, o_ref, s_ref, bs_ref, reverse):
    c = CHUNK
    nch = TM // c
    e = lambda x: jnp.exp2(x).astype(BF16)

    lvl = lvl_ref[...]
    for ci in range(nch):
        cidx = (nch - 1 - ci) if reverse else ci
        rows = slice(cidx * c, (cidx + 1) * c)
        lf = lf_ref[0, rows, :]
        hi = lf.astype(BF16)
        lo = (lf - hi.astype(F32)).astype(BF16)
        bsc = bs_ref.at[ci]
        bsc[...] = _dot(tri_ref[...], hi) + _dot(tri_ref[...], lo)
        for hh in range(HG_HEADS):
            hsl = slice(hh * HG_DK, (hh + 1) * HG_DK)
            b = bsc[:, hsl]
            btot = b[0:1] if reverse else b[c - 1:c]
            q = q_ref[0, rows, hsl]
            k = k_ref[0, rows, hsl]
            ix = (0, rows, hsl)
            v = v_ref[ix]
            st = s_ref[hh]
            att = jnp.where(lvl == 0, _dot_nt(q, k), 0.0)
            for l in range(1, N_LEVELS + 1):
                f = e(-jnp.abs(b - _level_ref(bsc, hsl, l, reverse)))
                att = jnp.where(lvl == l, _dot_nt(q * f, k * f), att)
            o = _dot_nt(q * e(b), st.astype(BF16)) + _dot(att.astype(BF16), v)
            s_ref[hh] = st * jnp.exp2(btot) + _dot_tn(v, k * e(btot - b))
            o_ref[ix] = o.astype(o_ref.dtype)


def _hgrn(q, v, lf_f, k_f, lf_b, k_b, nct):
    b_, lt, w = q.shape
    nt = lt // TM
    rtile = lambda i: jnp.where(i < nct, nct - 1 - i, nt - 1 - (i - nct))
    rowf = pl.BlockSpec((1, TM, w), lambda b, i: (b, i, 0))
    rowb = pl.BlockSpec((1, TM, w), lambda b, i: (b, rtile(i), 0))
    full = lambda a: pl.BlockSpec(a.shape, lambda b, i: (0,) * a.ndim)
    trif, lvlf = _chunk_tables(False)
    trib, lvlb = _chunk_tables(True)
    o_sh = jax.ShapeDtypeStruct((b_, lt, w), BF16)
    state = [pltpu.VMEM((HG_HEADS, HG_DK, HG_DK), F32)] * 2
    return pl.pallas_call(
        _hgrn_kernel,
        grid=(b_, nt),
        in_specs=[rowf] * 4 + [rowb] * 4 + [full(trif), full(lvlf), full(trib), full(lvlb)],
        out_specs=[rowf, rowb],
        out_shape=[o_sh, o_sh],
        scratch_shapes=state + [pltpu.VMEM((TM // CHUNK, CHUNK, w), F32)] * 2,
        compiler_params=_cparams(2),
        name="hgrn_scan",
    )(q, v, lf_f, k_f, q, v, lf_b, k_b, trif, lvlf, trib, lvlb)


KV_CHUNK = 256


def _attn_kernel(q_ref, k_ref, v_ref, o_ref, *, n_ctx, nct):
    i = pl.program_id(2)
    lane = lax.broadcasted_iota(jnp.int32, (TM, LANE), 1)

    def run(nk):
        outs = []
        for hh in range(2):
            sl = slice(hh * LANE, (hh + 1) * LANE)
            q = q_ref[0, :, sl]
            m = acc = None
            edges = [0] + list(range(n_ctx, nk + 1, KV_CHUNK))
            for c in range(len(edges) - 1):
                ks = slice(edges[c], edges[c + 1])
                s = _dot(q, k_ref[0, sl, ks])
                mc = jnp.max(s, axis=-1, keepdims=True)
                if c == 0:
                    m = mc
                    acc = _dot(jnp.exp2(s - m).astype(BF16), v_ref[0, ks, sl])
                else:
                    m_new = jnp.maximum(m, mc)
                    acc = acc * jnp.exp2(m - m_new) + _dot(jnp.exp2(s - m_new).astype(BF16), v_ref[0, ks, sl])
                    m = m_new
            den = acc[:, MLA_V:MLA_V + 1] if hh == 0 else acc[:, 0:1]
            outs.append(acc / den)
        o_ref[0] = jnp.where(lane < MLA_V, outs[0], outs[1]).astype(o_ref.dtype)

    @pl.when(i < nct)
    def _():
        run(n_ctx)

    @pl.when(i >= nct)
    def _():
        run(k_ref.shape[2])


def _attention(qq, kk, vv, n_ctx):
    b_, lt, _ = qq.shape
    nt = lt // TM
    npair = MLA_HEADS // 2
    return pl.pallas_call(
        functools.partial(_attn_kernel, n_ctx=n_ctx, nct=n_ctx // TM),
        grid=(b_, npair, nt),
        in_specs=[pl.BlockSpec((1, TM, 2 * LANE), lambda b, j, i: (b, i, j)),
                  pl.BlockSpec((1, 2 * LANE, lt), lambda b, j, i: (b, j, 0)),
                  pl.BlockSpec((1, lt, 2 * LANE), lambda b, j, i: (b, 0, j))],
        out_specs=pl.BlockSpec((1, TM, LANE), lambda b, j, i: (b, i, j)),
        out_shape=jax.ShapeDtypeStruct((b_, lt, npair * LANE), BF16),
        compiler_params=_cparams(3),
        name="mla_attention",
    )(qq, kk, vv)


def _route(sel):
    r = lambda x, e: x[e:e + 1]
    best = None
    for g in range(N_GROUPS):
        u = [r(sel, EPG * g + j) for j in range(EPG)]
        gs = None
        for a in range(EPG):
            for b in range(a + 1, EPG):
                pr = u[a] + u[b]
                gs = pr if gs is None else jnp.maximum(gs, pr)
        if best is None:
            best, gi = gs, jnp.zeros_like(gs, dtype=jnp.int32)
        else:
            upd = gs > best
            best = jnp.where(upd, gs, best)
            gi = jnp.where(upd, g, gi)

    def pick(x, j):
        out = r(x, j)
        for g in range(1, N_GROUPS):
            out = jnp.where(gi == g, r(x, EPG * g + j), out)
        return out

    u = [pick(sel, j) for j in range(EPG)]

    def argmax4(vals):
        bv, bi = vals[0], jnp.zeros_like(gi)
        for j in range(1, EPG):
            upd = vals[j] > bv
            bv = jnp.where(upd, vals[j], bv)
            bi = jnp.where(upd, j, bi)
        return bi

    l1 = argmax4(u)
    l2 = argmax4([jnp.where(l1 == j, -jnp.inf, u[j]) for j in range(EPG)])
    lo = jnp.minimum(l1, l2)
    hi = jnp.maximum(l1, l2)
    pair = jnp.where(lo == 0, jnp.where(hi == 1, 0, hi), jnp.where(lo == 1, jnp.where(hi == 2, 1, 4), 5))
    return (gi * 6 + pair).astype(F32)


CLS_ROWS = 32


def _merge_kernel(*refs, nz, nct):
    z_refs = refs[:nz]
    (of_ref, ob_ref, g_ref, gain_ref, m_ref, gas_ref, gbs_ref, mod_ref, wpa_ref, wpb_ref,
     wo_ref, g2_ref, wr_ref, rb_ref, triu_ref, zn_o, ht_o, route_o, cnt_o, run_ref) = refs[nz:]

    @pl.when((pl.program_id(0) == 0) & (pl.program_id(1) == 0))
    def _():
        run_ref[...] = jnp.zeros_like(run_ref)

    mod = mod_ref[0, 0]
    heads = []
    for hh in range(HG_HEADS):
        hsl = slice(hh * HG_DK, (hh + 1) * HG_DK)
        o = of_ref[0, :, hsl].astype(F32) + ob_ref[0, :, hsl].astype(F32)
        g = g_ref[0, :, hsl].astype(F32)
        heads.append(_rms(o, gain_ref[:, hsl]) * (g * jax.nn.sigmoid(g)))
    pa = _dot(jnp.concatenate(heads, axis=1).astype(BF16), wpa_ref[...])
    pb = _dot(m_ref[0].astype(BF16), wpb_ref[...])
    mixed = gas_ref[0] * pa + gbs_ref[0] * pb
    zn = _z_tile(z_refs, nct) + mod[2:3] * _dot(mixed.astype(BF16), wo_ref[...])
    zn_o[0] = zn
    h2 = _rms(zn, g2_ref[...]) * (1.0 + mod[4:5]) + mod[3:4]
    _tiles_store(ht_o, h2)
    logits = _dot_f32(h2, wr_ref[...])
    sc = jax.nn.sigmoid(logits.T[:N_EXPERTS])
    cls = _route(sc + rb_ref[...])
    onehot = (lax.broadcasted_iota(jnp.int32, (CLS_ROWS, TM), 0) == cls.astype(jnp.int32))
    pref = _dot(onehot.astype(BF16), triu_ref[...])
    rank = jnp.sum(jnp.where(onehot, pref + run_ref[:, 0:1], 0.0), axis=0, keepdims=True) - 1.0
    run_ref[...] = run_ref[...] + pref[:, TM - 1:TM]
    cnt_o[...] = run_ref[...]
    route_o[0] = jnp.concatenate([cls, rank] + [jnp.zeros_like(cls)] * (SUB - 2), axis=0)


def _merge(o_f, o_b, g, gain, m, gas, gbs, z, modl, wpa, wpb, wo, g2, wr, rbb, nct):
    z_in, z_specs, b_, lt, d = _z_inputs(z, nct)
    nt = lt // TM
    row = lambda w_: pl.BlockSpec((1, TM, w_), lambda b, i: (b, i, 0))
    full = lambda x: pl.BlockSpec(x.shape, lambda b, i: (0,) * x.ndim)
    triu = jnp.asarray(np.triu(np.ones((TM, TM), np.float32)), dtype=BF16)
    return pl.pallas_call(
        functools.partial(_merge_kernel, nz=len(z_in), nct=nct),
        grid=(b_, nt),
        in_specs=z_specs + [row(512), row(512), row(512), full(gain), row(512), row(d), row(d),
                  pl.BlockSpec((1, 1, SUB, d), lambda b, i: (b, jnp.minimum(i // nct, 1), 0, 0)),
                  full(wpa), full(wpb), full(wo), full(g2), full(wr), full(rbb), full(triu)],
        out_specs=[row(d), pl.BlockSpec((TM * SUB, LANE), lambda b, i: (b * nt + i, 0)),
                   pl.BlockSpec((1, SUB, TM), lambda b, i: (b, 0, i)),
                   pl.BlockSpec((CLS_ROWS, LANE), lambda b, i: (0, 0))],
        out_shape=[jax.ShapeDtypeStruct((b_, lt, d), F32),
                   jax.ShapeDtypeStruct((b_ * lt * SUB, LANE), F32),
                   jax.ShapeDtypeStruct((b_, SUB, lt), F32),
                   jax.ShapeDtypeStruct((CLS_ROWS, LANE), F32)],
        scratch_shapes=[pltpu.VMEM((CLS_ROWS, LANE), F32)],
        compiler_params=_cparams(2),
        name="merge_route",
    )(*z_in, o_f, o_b, g, gain, m, gas, gbs, modl, wpa, wpb, wo, g2, wr, rbb, triu)


def _tiles_store(ref, x, lead=()):
    n = x.shape[0]
    for j in range(SUB):
        ref[lead + (pl.ds(j, n, stride=SUB), slice(None))] = x[:, j * LANE:(j + 1) * LANE]


def _tiles_load(ref, n, lead=()):
    return jnp.concatenate([ref[lead + (pl.ds(j, n, stride=SUB), slice(None))] for j in range(SUB)], axis=1)


def _tile_copy(src_hbm, idx_ref, buf, sem, slot, r):
    return pltpu.make_async_copy(src_hbm.at[pl.ds(pl.multiple_of(idx_ref[0, 0, r] * SUB, SUB), SUB)],
                                 buf.at[slot, pl.ds(pl.multiple_of(r * SUB, SUB), SUB)], sem.at[slot])


GATHER_UNROLL = 8


def _gather_start(src_hbm, idx_ref, buf, sem, slot, n):
    def body(r8, c):
        for u in range(GATHER_UNROLL):
            _tile_copy(src_hbm, idx_ref, buf, sem, slot, r8 * GATHER_UNROLL + u).start(priority=u % 2)
        return c
    lax.fori_loop(0, n // GATHER_UNROLL, body, 0)


def _gather_wait(src_hbm, buf, sem, slot):
    pltpu.make_async_copy(src_hbm.at[pl.ds(0, buf.shape[1])], buf.at[slot], sem.at[slot]).wait()


def _sort_kernel(pos_ref, ht_ref, init_hbm, hs_hbm, sem):
    del init_hbm

    def body(r8, c):
        for u in range(GATHER_UNROLL):
            r = r8 * GATHER_UNROLL + u
            pltpu.make_async_copy(
                ht_ref.at[pl.ds(pl.multiple_of(r * SUB, SUB), SUB)],
                hs_hbm.at[pl.ds(pl.multiple_of(pos_ref[0, 0, r] * SUB, SUB), SUB)], sem.at[0]).start(priority=u % 2)
        return c
    lax.fori_loop(0, TM // GATHER_UNROLL, body, 0)
    pltpu.make_async_copy(ht_ref, hs_hbm.at[pl.ds(0, TM * SUB)], sem.at[0]).wait()


def _sort_rows(ht, pos, n_sorted, init=None):
    nstep = pos.shape[0]
    if init is None:
        init = jnp.zeros((n_sorted * SUB, LANE), F32)
    return pl.pallas_call(
        _sort_kernel,
        grid=(nstep,),
        in_specs=[pl.BlockSpec((1, 1, TM), lambda t: (t, 0, 0), memory_space=pltpu.SMEM),
                  pl.BlockSpec((TM * SUB, LANE), lambda t: (t, 0)), pl.BlockSpec(memory_space=pl.ANY)],
        out_specs=pl.BlockSpec(memory_space=pl.ANY),
        out_shape=jax.ShapeDtypeStruct(init.shape, F32),
        scratch_shapes=[pltpu.SemaphoreType.DMA((1,))],
        input_output_aliases={2: 0},
        compiler_params=_cparams(1),
        name="class_sort",
    )(pos, ht, init)


def _moe_kernel(ea_ref, eb_ref, nu_ref, hs_ref, wrt_ref,
                wga_ref, wua_ref, wda_ref, wgb_ref, wub_ref, wdb_ref, y_ref, gu_a, dn_a, gu_b, dn_b):
    j = pl.program_id(0)
    jp = jnp.maximum(j - 1, 0)

    @pl.when((j == 0) | (ea_ref[j] != ea_ref[jp]))
    def _():
        gu_a[0] = wga_ref[0].astype(BF16)
        gu_a[1] = wua_ref[0].astype(BF16)
        dn_a[...] = wda_ref[0].astype(BF16)

    @pl.when((j == 0) | (eb_ref[j] != eb_ref[jp]))
    def _():
        gu_b[0] = wgb_ref[0].astype(BF16)
        gu_b[1] = wub_ref[0].astype(BF16)
        dn_b[...] = wdb_ref[0].astype(BF16)

    @pl.when(j < nu_ref[0])
    def _():
        x32 = _tiles_load(hs_ref, MOE_BM)
        x = x32.astype(BF16)

        def ffn(gu, dn, e):
            sc = jax.nn.sigmoid(jnp.sum(x32 * wrt_ref[pl.ds(e, 1), :], axis=-1, keepdims=True))
            act = jax.nn.silu(_dot(x, gu[0])) * _dot(x, gu[1])
            return sc, _dot(act.astype(BF16), dn[...])

        s_a, y_a = ffn(gu_a, dn_a, ea_ref[j])
        s_b, y_b = ffn(gu_b, dn_b, eb_ref[j])
        inv = 1.0 / (s_a + s_b)
        _tiles_store(y_ref, (s_a * inv) * y_a + (s_b * inv) * y_b)

    @pl.when(j >= nu_ref[0])
    def _():
        y_ref[...] = jnp.zeros_like(y_ref)


def _moe(hs, wrt, blk_ea, blk_eb, n_used, wg, wu, wd, layer):
    nblk = blk_ea.shape[0]
    d, de = wg.shape[2:]
    tiles = pl.BlockSpec((MOE_BM * SUB, LANE), lambda j, ea, eb, nu: (j, 0))
    wspec = lambda shp, which: pl.BlockSpec(
        (None, 1) + shp, (lambda j, ea, eb, nu: (layer, ea[j], 0, 0)) if which == 0
        else (lambda j, ea, eb, nu: (layer, eb[j], 0, 0)))
    return pl.pallas_call(
        _moe_kernel,
        grid_spec=pltpu.PrefetchScalarGridSpec(
            num_scalar_prefetch=3,
            grid=(nblk,),
            in_specs=[tiles,
                      pl.BlockSpec(wrt.shape, lambda j, ea, eb, nu: (0, 0)),
                      wspec((d, de), 0), wspec((d, de), 0), wspec((de, d), 0),
                      wspec((d, de), 1), wspec((d, de), 1), wspec((de, d), 1)],
            out_specs=tiles,
            scratch_shapes=[pltpu.VMEM((2, d, de), BF16), pltpu.VMEM((de, d), BF16)] * 2),
        out_shape=jax.ShapeDtypeStruct(hs.shape, F32),
        compiler_params=_cparams(1),
        name="moe_ffn",
    )(blk_ea, blk_eb, n_used, hs, wrt, wg, wu, wd, wg, wu, wd)


def _post_kernel(*refs, final):
    if final:
        idx_ref, idxn_ref, y_hbm, zn_ref, mod_ref, fg_ref, o_ref, buf, sem = refs
    else:
        idx_ref, idxn_ref, y_hbm, zn_ref, mod_ref, o_ref, buf, sem = refs
    t = pl.program_id(0)
    nstep = pl.num_programs(0)
    slot = t % 2

    @pl.when(t == 0)
    def _():
        _gather_start(y_hbm, idx_ref, buf, sem, 0, TM)

    @pl.when(t + 1 < nstep)
    def _():
        _gather_start(y_hbm, idxn_ref, buf, sem, 1 - slot, TM)

    _gather_wait(y_hbm, buf, sem, slot)
    z = zn_ref[0] + mod_ref[0, 0][5:6] * _tiles_load(buf, TM, (slot,))
    if final:
        z = _rms(z, fg_ref[...])
    o_ref[0] = z


def _post(y, pos, zn, modl, nct, final_g=None):
    b_, lt, d = zn.shape
    nt = lt // TM
    final = final_g is not None
    t0 = nct if final else 0
    ntl = nt - t0
    nstep = b_ * ntl
    idx_spec = lambda f: pl.BlockSpec((1, 1, TM), f, memory_space=pltpu.SMEM)
    ins = [pos, pos, y, zn, modl]
    specs = [idx_spec(lambda t: (t, 0, 0)),
             idx_spec(lambda t: (jnp.minimum(t + 1, nstep - 1), 0, 0)),
             pl.BlockSpec(memory_space=pl.ANY),
             pl.BlockSpec((1, TM, d), lambda t: (t // ntl, t % ntl + t0, 0)),
             pl.BlockSpec((1, 1, SUB, d), lambda t: (t // ntl, jnp.minimum((t % ntl + t0) // nct, 1), 0, 0))]
    if final:
        ins.append(final_g)
        specs.append(pl.BlockSpec(final_g.shape, lambda t: (0, 0)))
    return pl.pallas_call(
        functools.partial(_post_kernel, final=final),
        grid=(nstep,),
        in_specs=specs,
        out_specs=pl.BlockSpec((1, TM, d), lambda t: (t // ntl, t % ntl, 0)),
        out_shape=jax.ShapeDtypeStruct((b_, ntl * TM, d), F32),
        scratch_shapes=[pltpu.VMEM((2, TM * SUB, LANE), F32), pltpu.SemaphoreType.DMA((2,))],
        compiler_params=_cparams(1),
        name="unsort_residual",
    )(*ins)


_PAIR_SLOTS = [(0, 1), (2, 1), (2, 0), (3, 0), (3, 1), (3, 2)]
_CLS_EA = np.array([EPG * g + p[0] for g in range(N_GROUPS) for p in _PAIR_SLOTS], np.int32)
_CLS_EB = np.array([EPG * g + p[1] for g in range(N_GROUPS) for p in _PAIR_SLOTS], np.int32)


def _sort_plan(cls, rank, counts):
    n = cls.shape[0]
    nblk = n // MOE_BM + N_CLASSES
    padded = (counts + MOE_BM - 1) // MOE_BM * MOE_BM
    pad_end = jnp.cumsum(padded)
    pos = (pad_end - padded)[cls] + rank
    starts = jnp.arange(nblk, dtype=jnp.int32)[:, None] * MOE_BM
    blk_cls = jnp.minimum(jnp.sum((pad_end[None, :] <= starts).astype(jnp.int32), axis=1), N_CLASSES - 1)
    n_used = (pad_end[-1] // MOE_BM).astype(jnp.int32).reshape(1)
    return pos, jnp.asarray(_CLS_EA)[blk_cls], jnp.asarray(_CLS_EB)[blk_cls], n_used


def _rot_cols(w):
    return jnp.concatenate([-w[:, 8:16], w[:, 0:8], -w[:, 24:32], w[:, 16:24]], axis=1)


def _prep_w_in(w):
    kpe = w[:, 3200:3232]
    pad = jnp.zeros((w.shape[0], C_END - C_KPE - 2 * MLA_ROPE), w.dtype)
    return jnp.concatenate([w[:, :3200], w[:, 3232:], kpe, _rot_cols(kpe), pad], axis=1).astype(BF16)


def _prep_mla(w_uq, w_ukv):
    odd = (jnp.arange(MLA_HEADS) % 2 == 1)[None, :, None]
    qh = w_uq.reshape(MLA_Q_RANK, MLA_HEADS, MLA_NOPE + MLA_ROPE)
    nope, pe = qh[:, :, :MLA_NOPE], qh[:, :, MLA_NOPE:]
    pe_sw = _rot_cols(pe.reshape(MLA_Q_RANK * MLA_HEADS, MLA_ROPE)).reshape(MLA_Q_RANK, MLA_HEADS, MLA_ROPE)
    wq = jnp.where(odd, jnp.concatenate([nope, pe, pe_sw], axis=2), jnp.concatenate([pe, pe_sw, nope], axis=2))
    kvh = w_ukv.reshape(MLA_KV_RANK, MLA_HEADS, MLA_NOPE + MLA_V)
    kn, vh = kvh[:, :, :MLA_NOPE], kvh[:, :, MLA_NOPE:]
    wkv = jnp.where(odd, jnp.concatenate([kn, vh], axis=2), jnp.concatenate([vh, kn], axis=2))
    return (wq.reshape(MLA_Q_RANK, MLA_HEADS * LANE).astype(BF16),
            wkv.reshape(MLA_KV_RANK, MLA_HEADS * LANE).astype(BF16))


def _rope_tables(n_ctx, n_lat):
    rows = n_lat // GRID_W
    rowp = jnp.broadcast_to(jnp.arange(rows, dtype=F32)[:, None], (rows, GRID_W)).reshape(-1)
    colp = jnp.broadcast_to(jnp.arange(GRID_W, dtype=F32)[None, :], (rows, GRID_W)).reshape(-1)
    axis_dim = MLA_ROPE // 2
    inv_freq = ROPE_THETA ** (-jnp.arange(0, axis_dim, 2, dtype=F32) / axis_dim)
    ar, ac = rowp[:, None] * inv_freq, colp[:, None] * inv_freq
    cos32 = jnp.concatenate([jnp.cos(ar), jnp.cos(ar), jnp.cos(ac), jnp.cos(ac)], axis=1)
    sin32 = jnp.concatenate([jnp.sin(ar), jnp.sin(ar), jnp.sin(ac), jnp.sin(ac)], axis=1)
    cos32 = jnp.concatenate([jnp.ones((n_ctx, MLA_ROPE), F32), cos32], axis=0)
    sin32 = jnp.concatenate([jnp.zeros((n_ctx, MLA_ROPE), F32), sin32], axis=0)
    lt = n_ctx + n_lat
    one, zero = jnp.ones((lt, MLA_NOPE), F32), jnp.zeros((lt, MLA_NOPE), F32)
    z32 = jnp.zeros((lt, MLA_ROPE), F32)
    sc = MLA_SCALE * LOG2E
    cqe = jnp.concatenate([cos32, z32, one], axis=1) * sc
    sqe = jnp.concatenate([sin32, z32, zero], axis=1) * sc
    cqo = jnp.concatenate([one, cos32, z32], axis=1) * sc
    sqo = jnp.concatenate([zero, sin32, z32], axis=1) * sc
    kc = jnp.concatenate([cos32, sin32, zero], axis=1)
    return cqe, sqe, cqo, sqo, kc


def _lower_bounds(gamma):
    p = jnp.cumsum(jax.nn.softmax(gamma.astype(F32), axis=0), axis=0)
    return p - p[0:1]


def kernel(x, c, ctx, c_ctx, w_mod, b_mod, norm1_g, norm2_g, w_in, gamma_fwd, gamma_bwd, hg_norm_g,
           q_norm_g, kv_norm_g, w_uq, w_ukv, w_pa, w_pb, w_o, w_router, router_bias,
           w_gate_e, w_up_e, w_down_e, final_g):
    b_, n_lat, d = x.shape
    n_ctx = ctx.shape[1]
    depth = w_mod.shape[0]
    assert n_ctx % TM == 0 and n_lat % TM == 0 and n_lat % GRID_W == 0 and b_ + 1 <= SUB
    nct = n_ctx // TM
    lt = n_ctx + n_lat
    nt = lt // TM

    cc = jnp.concatenate([c, c_ctx[None, :], jnp.zeros((SUB - b_ - 1, d), F32)], axis=0)
    mod = _modulation(cc, w_mod, b_mod).reshape(depth, SUB, 6, d)
    mod_x = mod[:, :b_]
    mod_c = jnp.broadcast_to(mod[:, b_:b_ + 1], mod_x.shape)
    mod_t = jnp.stack([mod_c, mod_x], axis=2)
    mod_t = jnp.concatenate([mod_t, jnp.zeros((depth, b_, 2, SUB - 6, d), F32)], axis=3)

    cqe, sqe, cqo, sqo, kc = _rope_tables(n_ctx, n_lat)
    lbf, lbb = _lower_bounds(gamma_fwd), _lower_bounds(gamma_bwd)
    wr = jnp.concatenate([w_router, jnp.zeros((d, LANE - N_EXPERTS), F32)], axis=1)
    rbb = jnp.broadcast_to(router_bias.astype(F32)[:, None], (N_EXPERTS, TM))

    z = (ctx, x)
    out = hs = None
    for l in range(depth):
        last = l == depth - 1
        wq, wkv = _prep_mla(w_uq[l], w_ukv[l])
        (q, v, lff, kf, lfb, kb, g, gas, gbs, qq, kk, vv) = _inproj(
            z, mod_t[l], norm1_g[l][None], _prep_w_in(w_in[l]), lbf[l][None], lbb[l][None],
            q_norm_g[l][None], kv_norm_g[l][None], wq, wkv, cqe, sqe, cqo, sqo, kc, nct)
        o_f, o_b = _hgrn(q, v, lff, kf, lfb, kb, nct)
        m = _attention(qq, kk, vv, n_ctx)
        zn, ht, route, cnt = _merge(o_f, o_b, g, hg_norm_g[l][None], m, gas, gbs, z, mod_t[l],
                                    w_pa[l].astype(BF16), w_pb[l].astype(BF16), w_o[l].astype(BF16),
                                    norm2_g[l][None], wr, rbb, nct)
        route = route[:, 0:2, :].astype(jnp.int32)
        pos, blk_ea, blk_eb, n_used = _sort_plan(route[:, 0].reshape(-1), route[:, 1].reshape(-1),
                                                 cnt[:N_CLASSES, 0].astype(jnp.int32))
        pos = pos.reshape(b_, nt, 1, TM)
        hs = _sort_rows(ht, pos.reshape(-1, 1, TM), blk_ea.shape[0] * MOE_BM, hs)
        y = _moe(hs, w_router.T, blk_ea, blk_eb, n_used, w_gate_e, w_up_e, w_down_e, l)
        if last:
            out = _post(y, pos[:, nct:].reshape(-1, 1, TM), zn, mod_t[l], nct, final_g[None])
        else:
            z = _post(y, pos.reshape(-1, 1, TM), zn, mod_t[l], nct)
    return out
```
